```python
import math
import jax
import jax.numpy as jnp
from jax import lax
import numpy as np

D_MODEL = 1024
BATCH = 8
SEQ = 4096
DEPTH = 2

HEAD_DIM = 64
ROT_DIM = HEAD_DIM // 4
ROPE_THETA = 500000.0
NORM_EPS = 1e-6
NEG_INF = -1e30
ATTN_SCALE = HEAD_DIM ** -0.5
Q_CHUNK = 32

N_BRANCH = 3
BRANCH_HEADS = 8
BRANCH_WIDTH = BRANCH_HEADS * HEAD_DIM

DIL_PATTERNS = ((128, 1), (512, 4), (2048, 16))
N_GROUPS_A = len(DIL_PATTERNS)

IDX_HEADS = 8
IDX_DIM = 64
IDX_ROT = IDX_DIM // 4
IDX_TOPK_MAX = 256

MOBA_BLOCK = 256
MOBA_TOPK = 3

A_QKV_COLS = 3 * N_GROUPS_A * BRANCH_WIDTH
B_QKV_COLS = 3 * BRANCH_WIDTH
IDX_Q_COLS = IDX_HEADS * IDX_DIM
IDX_K_COLS = IDX_DIM
IDX_W_COLS = IDX_HEADS
C_QKV_COLS = 3 * BRANCH_WIDTH
SILU_GATE_COLS = N_BRANCH * BRANCH_WIDTH
MERGE_GATE_COLS = N_BRANCH * D_MODEL
IN_COLS = (A_QKV_COLS, B_QKV_COLS, IDX_Q_COLS, IDX_K_COLS, IDX_W_COLS,
           C_QKV_COLS, SILU_GATE_COLS, MERGE_GATE_COLS)
N_IN = sum(IN_COLS)
SPLIT_POINTS = tuple(int(c) for c in np.cumsum(IN_COLS)[:-1])

kernel_name = 'hybrid_dilated_dsa_moba_block'


def rms_norm(x, g):
    xf = x.astype(jnp.float32)
    y = xf * lax.rsqrt(jnp.mean(xf * xf, axis=-1, keepdims=True) + NORM_EPS)
    return (y * g.astype(jnp.float32)).astype(x.dtype)


def rope_tables(positions, rot_dim):
    inv = ROPE_THETA ** (-jnp.arange(0, rot_dim, 2, dtype=jnp.float32) / rot_dim)
    ang = positions.astype(jnp.float32)[..., None] * inv
    return jnp.cos(ang)[:, :, None, :], jnp.sin(ang)[:, :, None, :]


def partial_rope(x, cos, sin):
    half = cos.shape[-1]
    xf = x.astype(jnp.float32)
    x1, x2, rest = xf[..., :half], xf[..., half:2 * half], xf[..., 2 * half:]
    out = jnp.concatenate([x1 * cos - x2 * sin, x2 * cos + x1 * sin, rest], axis=-1)
    return out.astype(x.dtype)


def banded_attention(q, k, v, band):
    N, n, dh = q.shape
    nb = -(-n // band)
    pad = nb * band - n
    qb = jnp.pad(q, ((0, 0), (0, pad), (0, 0))).reshape(N, nb, band, dh)
    kp = jnp.pad(k, ((0, 0), (band, pad), (0, 0)))
    vp = jnp.pad(v, ((0, 0), (band, pad), (0, 0)))

    def windows(t):
        return jnp.concatenate([t[:, :-band].reshape(N, nb, band, dh),
                                t[:, band:].reshape(N, nb, band, dh)], axis=2)

    kw, vw = windows(kp), windows(vp)
    s = jnp.einsum('nbqd,nbkd->nbqk', qb, kw, preferred_element_type=jnp.float32) * ATTN_SCALE
    i = jnp.arange(band)[:, None]
    j = jnp.arange(2 * band)[None, :]
    dist = i + band - j
    blk = jnp.arange(nb)[:, None, None]
    valid = (dist >= 0) & (dist <= band) & ((blk > 0) | (j >= band))
    s = jnp.where(valid, s, NEG_INF)
    m = jnp.max(s, axis=-1, keepdims=True)
    p = jnp.exp(s - m)
    den = jnp.sum(p, axis=-1, keepdims=True)
    o = jnp.einsum('nbqk,nbkd->nbqd', (p / den).astype(v.dtype), vw)
    lse = (m + jnp.log(den))[..., 0]
    return o.reshape(N, nb * band, dh)[:, :n], lse.reshape(N, nb * band)[:, :n]


def dilated_attention(q, k, v):
    B, S, _, dh = q.shape
    H = BRANCH_HEADS
    q = q.reshape(B, S, N_GROUPS_A, H, dh)
    k = k.reshape(B, S, N_GROUPS_A, H, dh)
    v = v.reshape(B, S, N_GROUPS_A, H, dh)
    outs, lses = [], []
    for g, (window, dil) in enumerate(DIL_PATTERNS):
        n = S // dil

        def fold(t):
            return t[:, :, g].reshape(B, n, dil, H, dh).transpose(0, 2, 3, 1, 4).reshape(B * dil * H, n, dh)

        o, lse = banded_attention(fold(q), fold(k), fold(v), window // dil)
        outs.append(o.reshape(B, dil, H, n, dh).transpose(0, 3, 1, 2, 4).reshape(B, S, H, dh))
        lses.append(lse.reshape(B, dil, H, n).transpose(0, 3, 1, 2).reshape(B, S, H))
    o = jnp.stack(outs, axis=2)
    wts = jax.nn.softmax(jnp.stack(lses, axis=2), axis=2)
    return jnp.einsum('bsgh,bsghd->bshd', wts.astype(o.dtype), o)


def dsa_attention(q, k, v, q_idx, k_idx, w_idx):
    B, S, H, dh = q.shape
    topk = min(IDX_TOPK_MAX, S // 4)
    b_i = jnp.arange(B)[:, None, None]
    s_pos = jnp.arange(S)
    k_idx32 = k_idx.astype(jnp.float32)

    def chunk(c):
        t0 = c * Q_CHUNK
        qc = lax.dynamic_slice_in_dim(q, t0, Q_CHUNK, axis=1)
        qi = lax.dynamic_slice_in_dim(q_idx, t0, Q_CHUNK, axis=1).astype(jnp.float32)
        wi = lax.dynamic_slice_in_dim(w_idx, t0, Q_CHUNK, axis=1).astype(jnp.float32)
        t = t0 + jnp.arange(Q_CHUNK)
        logits = jnp.einsum('bqhd,bsd->bqhs', qi, k_idx32) * (IDX_DIM ** -0.5)
        score = jnp.einsum('bqh,bqhs->bqs', wi * (IDX_HEADS ** -0.5), jax.nn.relu(logits))
        score = jnp.where(s_pos[None, None, :] <= t[None, :, None], score, NEG_INF)
        _, idx = lax.top_k(score, topk)
        ok = idx <= t[None, :, None]
        ks = k[b_i, idx]
        vs = v[b_i, idx]
        s = jnp.einsum('bqhd,bqkhd->bqhk', qc, ks, preferred_element_type=jnp.float32) * ATTN_SCALE
        s = jnp.where(ok[:, :, None, :], s, NEG_INF)
        p = jax.nn.softmax(s, axis=-1)
        return jnp.einsum('bqhk,bqkhd->bqhd', p.astype(v.dtype), vs)

    out = lax.map(chunk, jnp.arange(S // Q_CHUNK))
    return out.transpose(1, 0, 2, 3, 4).reshape(B, S, H, dh)


def moba_attention(q, k, v):
    B, S, H, dh = q.shape
    nblk = -(-S // MOBA_BLOCK)
    pad = nblk * MOBA_BLOCK - S
    kp = jnp.pad(k, ((0, 0), (0, pad), (0, 0), (0, 0)))
    vp = jnp.pad(v, ((0, 0), (0, pad), (0, 0), (0, 0)))
    kb = kp.reshape(B, nblk, MOBA_BLOCK, H, dh).transpose(0, 3, 1, 2, 4)
    vb = vp.reshape(B, nblk, MOBA_BLOCK, H, dh).transpose(0, 3, 1, 2, 4)
    kmean = jnp.mean(kb.astype(jnp.float32), axis=3)
    topb = min(MOBA_TOPK, nblk - 1)
    b_i = jnp.arange(B)[:, None, None, None]
    h_i = jnp.arange(H)[None, :, None, None]
    blk_ids = jnp.arange(nblk)
    in_blk = jnp.arange(MOBA_BLOCK)

    def chunk(c):
        t0 = c * Q_CHUNK
        qc = lax.dynamic_slice_in_dim(q, t0, Q_CHUNK, axis=1).transpose(0, 2, 1, 3)
        t = t0 + jnp.arange(Q_CHUNK)
        own = t0 // MOBA_BLOCK
        k_own = lax.dynamic_index_in_dim(kb, own, axis=2, keepdims=False)
        v_own = lax.dynamic_index_in_dim(vb, own, axis=2, keepdims=False)
        s_own = jnp.einsum('bhqd,bhkd->bhqk', qc, k_own, preferred_element_type=jnp.float32) * ATTN_SCALE
        causal = (own * MOBA_BLOCK + in_blk)[None, :] <= t[:, None]
        s_own = jnp.where(causal, s_own, NEG_INF)
        if topb > 0:
            gate = jnp.einsum('bhqd,bhnd->bhqn', qc.astype(jnp.float32), kmean)
            gate = jnp.where(blk_ids < own, gate, NEG_INF)
            _, bidx = lax.top_k(gate, topb)
            okb = bidx < own
            ks = kb[b_i, h_i, bidx]
            vs = vb[b_i, h_i, bidx]
            s_sel = jnp.einsum('bhqd,bhqnkd->bhqnk', qc, ks, preferred_element_type=jnp.float32) * ATTN_SCALE
            s_sel = jnp.where(okb[..., None], s_sel, NEG_INF).reshape(B, H, Q_CHUNK, topb * MOBA_BLOCK)
            p = jax.nn.softmax(jnp.concatenate([s_sel, s_own], axis=-1), axis=-1).astype(v.dtype)
            p_sel = p[..., :topb * MOBA_BLOCK].reshape(B, H, Q_CHUNK, topb, MOBA_BLOCK)
            p_own = p[..., topb * MOBA_BLOCK:]
            o = (jnp.einsum('bhqnk,bhqnkd->bhqd', p_sel, vs)
                 + jnp.einsum('bhqk,bhkd->bhqd', p_own, v_own))
        else:
            p_own = jax.nn.softmax(s_own, axis=-1).astype(v.dtype)
            o = jnp.einsum('bhqk,bhkd->bhqd', p_own, v_own)
        return o.transpose(0, 2, 1, 3)

    out = lax.map(chunk, jnp.arange(S // Q_CHUNK))
    return out.transpose(1, 0, 2, 3, 4).reshape(B, S, H, dh)


def hybrid_layer(x, cos, sin, cos_i, sin_i, norm_g, w_in, qk_g, w_br, w_out):
    B, S, _ = x.shape
    h = rms_norm(x, norm_g)
    proj = jnp.einsum('bsd,dc->bsc', h, w_in)
    a_qkv, b_qkv, q_idx, k_idx, w_idx, c_qkv, z, g = jnp.split(proj, SPLIT_POINTS, axis=-1)

    def heads(t, n_heads):
        t = t.reshape(B, S, 3, n_heads, HEAD_DIM)
        return t[:, :, 0], t[:, :, 1], t[:, :, 2]

    def prep_qk(q, k, mixer):
        q = partial_rope(rms_norm(q, qk_g[mixer, 0]), cos, sin)
        k = partial_rope(rms_norm(k, qk_g[mixer, 1]), cos, sin)
        return q, k

    qa, ka, va = heads(a_qkv, N_GROUPS_A * BRANCH_HEADS)
    qa, ka = prep_qk(qa, ka, 0)
    o_a = dilated_attention(qa, ka, va)

    qb, kb, vb = heads(b_qkv, BRANCH_HEADS)
    qb, kb = prep_qk(qb, kb, 1)
    q_idx = partial_rope(q_idx.reshape(B, S, IDX_HEADS, IDX_DIM), cos_i, sin_i)
    k_idx = partial_rope(k_idx[:, :, None, :], cos_i, sin_i)[:, :, 0]
    o_b = dsa_attention(qb, kb, vb, q_idx, k_idx, w_idx)

    qc, kc, vc = heads(c_qkv, BRANCH_HEADS)
    qc, kc = prep_qk(qc, kc, 2)
    o_c = moba_attention(qc, kc, vc)

    o = jnp.stack([o_a, o_b, o_c], axis=2).reshape(B, S, N_BRANCH, BRANCH_WIDTH)
    o = o * jax.nn.silu(z.reshape(B, S, N_BRANCH, BRANCH_WIDTH))
    y = jnp.einsum('bsnc,ncd->bsnd', o, w_br)
    gate = jax.nn.sigmoid(g.reshape(B, S, N_BRANCH, D_MODEL))
    merged = jnp.sum(gate * y, axis=2)
    return x + jnp.einsum('bsd,de->bse', merged, w_out)


def setup_inputs(seed: int = 0) -> dict:
    key = jax.random.key(seed)
    ks = jax.random.split(key, 7)
    x = jax.random.normal(ks[0], (BATCH, SEQ, D_MODEL), jnp.float32)
    offsets = jax.random.randint(ks[1], (BATCH, 1), 0, 2048, dtype=jnp.int32)
    positions = (offsets + jnp.arange(SEQ, dtype=jnp.int32)[None, :]).astype(jnp.int32)
    norm_g = 1.0 + 0.1 * jax.random.normal(ks[2], (DEPTH, D_MODEL), jnp.float32)
    w_in = jax.random.normal(ks[3], (DEPTH, D_MODEL, N_IN), jnp.float32) * (D_MODEL ** -0.5)
    qk_g = 1.0 + 0.1 * jax.random.normal(ks[4], (DEPTH, N_BRANCH, 2, HEAD_DIM), jnp.float32)
    w_br = jax.random.normal(ks[5], (DEPTH, N_BRANCH, BRANCH_WIDTH, D_MODEL), jnp.float32) * (BRANCH_WIDTH ** -0.5)
    w_out = jax.random.normal(ks[6], (DEPTH, D_MODEL, D_MODEL), jnp.float32) * (D_MODEL ** -0.5)
    return {'x': x, 'positions': positions, 'norm_g': norm_g, 'w_in': w_in,
            'qk_g': qk_g, 'w_br': w_br, 'w_out': w_out}


def reference(x, positions, norm_g, w_in, qk_g, w_br, w_out):
    cos, sin = rope_tables(positions, ROT_DIM)
    cos_i, sin_i = rope_tables(positions, IDX_ROT)
    for layer in range(DEPTH):
        x = hybrid_layer(x, cos, sin, cos_i, sin_i, norm_g[layer], w_in[layer],
                         qk_g[layer], w_br[layer], w_out[layer])
    return x
```

```python
import functools

import jax
import jax.numpy as jnp
import numpy as np
from jax import lax
from jax.experimental import pallas as pl
from jax.experimental.pallas import tpu as pltpu

F32 = jnp.float32
BF16 = jnp.bfloat16
I32 = jnp.int32

D_MODEL = 1024
HEAD_DIM = 64
ROT_DIM = HEAD_DIM // 4
ROT_HALF = ROT_DIM // 2
ROPE_THETA = 500000.0
NORM_EPS = 1e-6
NEG_INF = -1e30
N_HEADS = 8
BRANCH_WIDTH = N_HEADS * HEAD_DIM
N_BRANCH = 3
DIL_PATTERNS = ((128, 1), (512, 4), (2048, 16))
BAND = 128
IDX_HEADS = 8
IDX_DIM = 64
IDX_TOPK_MAX = 256
MOBA_BLOCK = 256
MOBA_TOPK = 3
MOBA_SLOTS = 16

LANES = 128
VMEM_LIMIT_BYTES = 56 * 1024 * 1024

TILE = BRANCH_WIDTH
N_FOLD_TILES = 6
T_A = 0
T_B = 3
T_IQ = 6
T_IKW = 7
T_C = 8
T_Z = 11
T_G = 14
N_NAT_TILES = 20
N_TILES = N_FOLD_TILES + N_NAT_TILES
NP = N_NAT_TILES * TILE
A_STEP_TOKENS = 2048

KIND_PLAIN, KIND_NORM_ROPE, KIND_ROPE, KIND_ROPE_LOW, KIND_SILU, KIND_SIGMOID = range(6)

_FLOOR_KEY = int(np.array(-5e29, np.float32).view(np.int32)) ^ 0x7FFFFFFF
_INT_MIN = -2 ** 31
_LOG2_MOBA_BLOCK = MOBA_BLOCK.bit_length() - 1
_LOG2_MOBA_SLOTS = MOBA_SLOTS.bit_length() - 1
_LOG2_HEAD_DIM = HEAD_DIM.bit_length() - 1


def _dot(a, b):
    return jnp.dot(a, b, preferred_element_type=F32)


def _dot_nt(a, b):
    return lax.dot_general(a, b, (((1,), (1,)), ((), ())), preferred_element_type=F32)


def _params(*sem):
    return pltpu.CompilerParams(dimension_semantics=sem, vmem_limit_bytes=VMEM_LIMIT_BYTES)


def _inproj_kernel(kind_ref, x_ref, ng_ref, w_ref, gain_ref, cos_ref, s1_ref, s2_ref, bd_ref,
                   o_ref, f1_ref, f2_ref, h_ref, stage_ref):
    j = pl.program_id(1)
    tm = x_ref.shape[0]
    n_ch = TILE // LANES

    def fold(val, ref, dil):
        for c in range(n_ch):
            stage_ref[c] = val[:, c * LANES:(c + 1) * LANES]
        for r in range(dil):
            parts = [stage_ref[c, pl.ds(r, tm // dil, stride=dil), :] for c in range(n_ch)]
            ref[0, r] = jnp.concatenate(parts, axis=1).astype(ref.dtype)

    def emit(val):
        @pl.when(j >= N_FOLD_TILES)
        def _():
            o_ref[...] = val.astype(o_ref.dtype)

        @pl.when(j < N_FOLD_TILES // 2)
        def _():
            fold(val, f1_ref, DIL_PATTERNS[1][1])

        @pl.when(jnp.logical_and(j >= N_FOLD_TILES // 2, j < N_FOLD_TILES))
        def _():
            fold(val, f2_ref, DIL_PATTERNS[2][1])

    @pl.when(j == 0)
    def _():
        x = x_ref[...]
        ms = jnp.mean(x * x, axis=-1, keepdims=True)
        h_ref[...] = (x * lax.rsqrt(ms + NORM_EPS) * ng_ref[...]).astype(BF16)

    y = _dot(h_ref[...], w_ref[...])
    kind = kind_ref[j]
    reps = TILE // LANES

    def rope(v, low_only):
        c = jnp.tile(cos_ref[...], (1, reps))
        s1 = jnp.tile(s1_ref[...], (1, reps))
        s2 = jnp.tile(s2_ref[...], (1, reps))
        if low_only:
            lane = lax.broadcasted_iota(I32, v.shape, 1)
            low = lane < HEAD_DIM
            c = jnp.where(low, c, 1.0)
            s1 = jnp.where(low, s1, 0.0)
            s2 = jnp.where(low, s2, 0.0)
        return v * c + pltpu.roll(v, TILE - ROT_HALF, 1) * s1 + pltpu.roll(v, ROT_HALF, 1) * s2

    @pl.when(kind == KIND_PLAIN)
    def _():
        emit(y)

    @pl.when(kind == KIND_NORM_ROPE)
    def _():
        ms = _dot((y * y).astype(BF16), bd_ref[...])
        yn = y * lax.rsqrt(ms + NORM_EPS) * gain_ref[0]
        emit(rope(yn, False))

    @pl.when(kind == KIND_ROPE)
    def _():
        o_ref[...] = rope(y * gain_ref[0], False).astype(o_ref.dtype)

    @pl.when(kind == KIND_ROPE_LOW)
    def _():
        o_ref[...] = rope(y, True).astype(o_ref.dtype)

    @pl.when(kind == KIND_SILU)
    def _():
        o_ref[...] = (y / (1.0 + jnp.exp(-y))).astype(o_ref.dtype)

    @pl.when(kind == KIND_SIGMOID)
    def _():
        o_ref[...] = (1.0 / (1.0 + jnp.exp(-y))).astype(o_ref.dtype)


def _inproj(x2, ng, w_r, kinds, gains, cos_t, s1_t, s2_t, bd, tm, B, S):
    T = x2.shape[0]
    grid = (T // tm, N_TILES)
    per_b = S // tm
    d1, d2 = DIL_PATTERNS[1][1], DIL_PATTERNS[2][1]
    half = N_FOLD_TILES // 2
    assert S % tm == 0 and tm % (16 * d2) == 0

    def fold_spec(dil, first):
        return pl.BlockSpec(
            (1, dil, tm // dil, TILE),
            lambda i, j, k: (i // per_b, 0, i % per_b, jnp.clip(j - first, 0, half - 1)))

    return pl.pallas_call(
        _inproj_kernel,
        grid_spec=pltpu.PrefetchScalarGridSpec(
            num_scalar_prefetch=1,
            grid=grid,
            in_specs=[
                pl.BlockSpec((tm, D_MODEL), lambda i, j, k: (i, 0)),
                pl.BlockSpec((1, D_MODEL), lambda i, j, k: (0, 0)),
                pl.BlockSpec((D_MODEL, TILE), lambda i, j, k: (0, j)),
                pl.BlockSpec((1, 1, TILE), lambda i, j, k: (j, 0, 0)),
                pl.BlockSpec((tm, LANES), lambda i, j, k: (i, 0)),
                pl.BlockSpec((tm, LANES), lambda i, j, k: (i, 0)),
                pl.BlockSpec((tm, LANES), lambda i, j, k: (i, 0)),
                pl.BlockSpec((TILE, TILE), lambda i, j, k: (0, 0)),
            ],
            out_specs=[
                pl.BlockSpec((tm, TILE), lambda i, j, k: (i, jnp.maximum(j - N_FOLD_TILES, 0))),
                fold_spec(d1, 0),
                fold_spec(d2, half),
            ],
            scratch_shapes=[pltpu.VMEM((tm, D_MODEL), BF16),
                            pltpu.VMEM((TILE // LANES, tm, LANES), F32)],
        ),
        out_shape=[jax.ShapeDtypeStruct((T, NP), BF16),
                   jax.ShapeDtypeStruct((B, d1, S // d1, half * TILE), BF16),
                   jax.ShapeDtypeStruct((B, d2, S // d2, half * TILE), BF16)],
        compiler_params=_params("arbitrary", "arbitrary"),
        name="inproj",
    )(kinds, x2, ng, w_r, gains, cos_t, s1_t, s2_t, bd)


def _attn_a_kernel(q_ref, kc_ref, kp_ref, vc_ref, vp_ref, o_ref, lse_ref, kcat, vcat, ost, lst,
                   *, tq, dil):
    m = pl.program_id(1)
    nsub = tq // BAND
    n_ch = TILE // LANES
    row = lax.broadcasted_iota(I32, (BAND, BAND), 0)
    col = lax.broadcasted_iota(I32, (BAND, BAND), 1)
    lead = col - row
    mask_cur = col <= row

    def residue(r, carry):
        kcat[0:BAND, :] = kp_ref[r]
        kcat[BAND:, :] = kc_ref[r]
        vcat[0:BAND, :] = vp_ref[r]
        vcat[BAND:, :] = vc_ref[r]

        def body(u, carry):
            r0 = pl.multiple_of(u * BAND, BAND)
            r1 = pl.multiple_of(u * BAND + BAND, BAND)
            q = q_ref[r, pl.ds(r0, BAND), :]
            k_p = kcat[pl.ds(r0, BAND), :]
            k_c = kcat[pl.ds(r1, BAND), :]
            v_p = vcat[pl.ds(r0, BAND), :]
            v_c = vcat[pl.ds(r1, BAND), :]
            mp = lead >= jnp.where(m * nsub + u > 0, 0, BAND)
            outs, lses = [], []
            for h in range(N_HEADS):
                sl = slice(h * HEAD_DIM, (h + 1) * HEAD_DIM)
                s_p = jnp.where(mp, _dot_nt(q[:, sl], k_p[:, sl]), NEG_INF)
                s_c = jnp.where(mask_cur, _dot_nt(q[:, sl], k_c[:, sl]), NEG_INF)
                mx = jnp.maximum(jnp.max(s_p, axis=1, keepdims=True),
                                 jnp.max(s_c, axis=1, keepdims=True))
                p_p = jnp.exp(s_p - mx)
                p_c = jnp.exp(s_c - mx)
                den = jnp.sum(p_p, axis=1, keepdims=True) + jnp.sum(p_c, axis=1, keepdims=True)
                pv = _dot(p_p.astype(BF16), v_p[:, sl]) + _dot(p_c.astype(BF16), v_c[:, sl])
                outs.append(pv / den)
                lses.append(jnp.broadcast_to(mx + jnp.log(den), (BAND, HEAD_DIM)))
            o = jnp.concatenate(outs, axis=1)
            lse = jnp.concatenate(lses, axis=1)
            rows = pl.ds(r + u * (BAND * dil), BAND, stride=dil) if dil > 1 else pl.ds(r0, BAND)
            for c in range(n_ch):
                ost[c, rows, :] = o[:, c * LANES:(c + 1) * LANES]
                lst[c, rows, :] = lse[:, c * LANES:(c + 1) * LANES]
            return carry

        return lax.fori_loop(0, nsub, body, carry)

    lax.fori_loop(0, dil, residue, 0)
    o_ref[0] = jnp.concatenate([ost[c] for c in range(n_ch)], axis=1).astype(o_ref.dtype)
    lse_ref[0] = jnp.concatenate([lst[c] for c in range(n_ch)], axis=1)


def _attn_a(qkv, g, dil, B, S):
    n = S // dil
    ts = min(A_STEP_TOKENS, S)
    tq = ts // dil
    assert S % ts == 0 and tq % BAND == 0
    sub = tq // BAND

    def spec_cur(t):
        return pl.BlockSpec((None, dil, tq, TILE), lambda b, m: (b, 0, m, t))

    def spec_prev(t):
        return pl.BlockSpec((None, dil, BAND, TILE), lambda b, m: (b, 0, jnp.maximum(m * sub - 1, 0), t))

    out_spec = pl.BlockSpec((1, ts, TILE), lambda b, m: (b, m, 0))
    n_ch = TILE // LANES
    return pl.pallas_call(
        functools.partial(_attn_a_kernel, tq=tq, dil=dil),
        grid=(B, S // ts),
        in_specs=[spec_cur(0), spec_cur(1), spec_prev(1), spec_cur(2), spec_prev(2)],
        out_specs=[out_spec, out_spec],
        out_shape=[jax.ShapeDtypeStruct((B, S, TILE), BF16), jax.ShapeDtypeStruct((B, S, TILE), F32)],
        scratch_shapes=[pltpu.VMEM((tq + BAND, TILE), BF16), pltpu.VMEM((tq + BAND, TILE), BF16),
                        pltpu.VMEM((n_ch, ts, LANES), F32), pltpu.VMEM((n_ch, ts, LANES), F32)],
        compiler_params=_params("arbitrary", "arbitrary"),
        name=f"attn_a{g}",
    )(qkv, qkv, qkv, qkv, qkv)


def _flash_init(m_ref, l_ref, acc_ref):
    m_ref[...] = jnp.full(m_ref.shape, NEG_INF, F32)
    l_ref[...] = jnp.zeros(l_ref.shape, F32)
    acc_ref[...] = jnp.zeros(acc_ref.shape, F32)


def _flash_head_step(h, q_h, k_h, v_h, bias, m_ref, l_ref, acc_ref):
    tk = k_h.shape[0]
    sl = slice(h * HEAD_DIM, (h + 1) * HEAD_DIM)
    s = _dot_nt(q_h, k_h) + bias
    m_prev = m_ref[h]
    m_next = jnp.maximum(m_prev, jnp.max(s, axis=1, keepdims=True))
    p = jnp.exp(s - jnp.tile(m_next, (1, tk // LANES)))
    alpha = jnp.exp(m_prev - m_next)
    l_ref[h] = alpha * l_ref[h] + jnp.sum(p, axis=1, keepdims=True)
    acc_ref[:, sl] = acc_ref[:, sl] * alpha[:, :HEAD_DIM] + _dot(p.astype(BF16), v_h)
    m_ref[h] = m_next


def _flash_finish(o_ref, l_ref, acc_ref):
    outs = []
    for h in range(N_HEADS):
        sl = slice(h * HEAD_DIM, (h + 1) * HEAD_DIM)
        outs.append(acc_ref[:, sl] / l_ref[h][:, :HEAD_DIM])
    o_ref[0] = jnp.concatenate(outs, axis=1).astype(o_ref.dtype)


def _dsa_kernel(q_ref, k_ref, v_ref, iq_ref, ikw_q_ref, ikw_ref, o_ref,
                key_ref, thr_ref, cge_ref, m_ref, l_ref, acc_ref, *, tq, tk, topk, rg):
    qi = pl.program_id(1)
    n_kt = qi + 1
    t0 = qi * tq

    iq = iq_ref[0]
    w = ikw_q_ref[0][:, HEAD_DIM:HEAD_DIM + IDX_HEADS].astype(F32) * (IDX_HEADS ** -0.5)
    rows = t0 + lax.broadcasted_iota(I32, (tq, tk), 0)
    cols = lax.broadcasted_iota(I32, (tq, tk), 1)

    def score_body(c, carry):
        c0 = pl.multiple_of(c * tk, tk)
        kx = ikw_ref[0, pl.ds(c0, tk), :][:, :IDX_DIM]
        sc = jnp.zeros((tq, tk), F32)
        for h in range(IDX_HEADS):
            lg = _dot_nt(iq[:, h * IDX_DIM:(h + 1) * IDX_DIM], kx)
            sc = sc + w[:, h:h + 1] * jnp.maximum(lg, 0.0)
        sc = jnp.where(sc == 0.0, 0.0, sc)
        sc = jnp.where(cols + c0 <= rows, sc, NEG_INF)
        bits = pltpu.bitcast(sc, I32)
        key_ref[:, pl.ds(c0, tk)] = jnp.where(bits < 0, bits ^ 0x7FFFFFFF, bits)
        return carry

    lax.fori_loop(0, n_kt, score_body, 0)

    n_ch = n_kt * (tk // LANES)
    n_rg = tq // rg
    thr_ref[...] = jnp.zeros(thr_ref.shape, I32)
    cge_ref[...] = jnp.zeros(cge_ref.shape, I32) + n_kt * tk

    def count_ge(g, cand):
        def cbody(c, acc):
            for s in range(tk // LANES):
                c0 = pl.multiple_of(c * tk + s * LANES, LANES)
                kk = key_ref[g * rg:(g + 1) * rg, pl.ds(c0, LANES)]
                acc = acc + jnp.where(kk >= cand, 1, 0)
            return acc
        acc = lax.fori_loop(0, n_kt, cbody, jnp.zeros((rg, LANES), I32))
        return jnp.sum(acc, axis=1, keepdims=True)

    def bit_body(it, carry):
        bit = lax.shift_left(jnp.int32(1), 31 - it)
        for g in range(n_rg):
            ans = thr_ref[g * rg:(g + 1) * rg, :]
            cand_u = ans | bit
            cnt = count_ge(g, cand_u ^ _INT_MIN)
            ok = cnt >= topk
            thr_ref[g * rg:(g + 1) * rg, :] = jnp.where(ok, cand_u, ans)
            cge_ref[g * rg:(g + 1) * rg, :] = jnp.where(ok, cnt, cge_ref[g * rg:(g + 1) * rg, :])
        return carry

    lax.fori_loop(0, 32, bit_body, 0)
    thr_ref[...] = jnp.maximum(thr_ref[...] ^ _INT_MIN, _FLOOR_KEY)

    tie = jnp.logical_and(cge_ref[...] > topk, thr_ref[...] > _FLOOR_KEY)
    any_tie = jnp.max(jnp.where(tie, 1, 0)) > 0

    @pl.when(any_tie)
    def _():
        def count2(g, thr, pos):
            def cbody(c, acc):
                c0 = pl.multiple_of(c * LANES, LANES)
                kk = key_ref[g * rg:(g + 1) * rg, pl.ds(c0, LANES)]
                cidx = c0 + lax.broadcasted_iota(I32, (rg, LANES), 1)
                gt = acc[0] + jnp.where(kk > thr, 1, 0)
                eq = acc[1] + jnp.where(jnp.logical_and(kk == thr, cidx <= pos), 1, 0)
                return gt, eq
            z = jnp.zeros((rg, LANES), I32)
            gt, eq = lax.fori_loop(0, n_ch, cbody, (z, z))
            return jnp.sum(gt, axis=1, keepdims=True), jnp.sum(eq, axis=1, keepdims=True)

        n_bits = int(np.log2(key_ref.shape[1]))
        for g in range(n_rg):
            thr = thr_ref[g * rg:(g + 1) * rg, :]

            def jbody(it, lo):
                step = lax.shift_left(jnp.int32(1), n_bits - 1 - it)
                cand = lo + step
                gt, eq = count2(g, thr, cand - 1)
                return jnp.where(gt + eq >= topk, lo, cand)

            jcol = lax.fori_loop(0, n_bits, jbody, jnp.zeros((rg, LANES), I32))
            is_tie = tie[g * rg:(g + 1) * rg, :]

            def fix(c, carry):
                c0 = pl.multiple_of(c * LANES, LANES)
                kk = key_ref[g * rg:(g + 1) * rg, pl.ds(c0, LANES)]
                cidx = c0 + lax.broadcasted_iota(I32, (rg, LANES), 1)
                drop = jnp.logical_and(is_tie, jnp.logical_and(kk == thr, cidx > jcol))
                key_ref[g * rg:(g + 1) * rg, pl.ds(c0, LANES)] = jnp.where(drop, _INT_MIN, kk)
                return carry

            lax.fori_loop(0, n_ch, fix, 0)

    _flash_init(m_ref, l_ref, acc_ref)
    q = q_ref[0]
    thr_full = jnp.tile(thr_ref[...], (1, tk // LANES))

    def att_body(j, carry):
        c0 = pl.multiple_of(j * tk, tk)
        bias = jnp.where(key_ref[:, pl.ds(c0, tk)] >= thr_full, 0.0, NEG_INF)
        k = k_ref[0, pl.ds(c0, tk), :]
        v = v_ref[0, pl.ds(c0, tk), :]
        for h in range(N_HEADS):
            sl = slice(h * HEAD_DIM, (h + 1) * HEAD_DIM)
            _flash_head_step(h, q[:, sl], k[:, sl], v[:, sl], bias, m_ref, l_ref, acc_ref)
        return carry

    lax.fori_loop(0, n_kt, att_body, 0)
    _flash_finish(o_ref, l_ref, acc_ref)


def _dsa(P3, B, S):
    tq = tk = 256
    topk = min(IDX_TOPK_MAX, S // 4)
    assert S % tq == 0 and topk <= tq
    rg = 64
    grid = (B, S // tq)

    def qspec(t, width=TILE, per=1):
        return pl.BlockSpec((1, tq, width), lambda b, i: (b, i, t * per))

    def full(t, width=TILE, per=1):
        return pl.BlockSpec((1, S, width), lambda b, i: (b, 0, t * per))

    per = TILE // LANES
    return pl.pallas_call(
        functools.partial(_dsa_kernel, tq=tq, tk=tk, topk=topk, rg=rg),
        grid=grid,
        in_specs=[qspec(T_B), full(T_B + 1), full(T_B + 2), qspec(T_IQ),
                  qspec(T_IKW, LANES, per), full(T_IKW, LANES, per)],
        out_specs=pl.BlockSpec((1, tq, TILE), lambda b, i: (b, i, 0)),
        out_shape=jax.ShapeDtypeStruct((B, S, TILE), BF16),
        scratch_shapes=[pltpu.VMEM((tq, S), I32), pltpu.VMEM((tq, LANES), I32),
                        pltpu.VMEM((tq, LANES), I32),
                        pltpu.VMEM((N_HEADS, tq, LANES), F32), pltpu.VMEM((N_HEADS, tq, LANES), F32),
                        pltpu.VMEM((tq, TILE), F32)],
        compiler_params=_params("arbitrary", "arbitrary"),
        name="dsa",
    )(P3, P3, P3, P3, P3, P3)


def _moba_kernel(q_ref, k_ref, v_ref, o_ref, kmh_ref, kml_ref, selb_ref, m_ref, l_ref, acc_ref,
                 *, tq, nblk, topb):
    qi = pl.program_id(1)
    S = k_ref.shape[1]
    gl = N_HEADS * MOBA_SLOTS

    @pl.when(qi == 0)
    def _():
        blk_row = lax.broadcasted_iota(I32, (MOBA_SLOTS, S), 0)
        blk_col = jnp.right_shift(lax.broadcasted_iota(I32, (MOBA_SLOTS, S), 1), _LOG2_MOBA_BLOCK)
        avg = jnp.where(blk_row == blk_col, 1.0 / MOBA_BLOCK, 0.0).astype(BF16)
        kmean = _dot(avg, k_ref[0])
        km = jnp.tile(kmean, (N_HEADS, 1))
        r_head = jnp.right_shift(lax.broadcasted_iota(I32, (gl, TILE), 0), _LOG2_MOBA_SLOTS)
        c_head = jnp.right_shift(lax.broadcasted_iota(I32, (gl, TILE), 1), _LOG2_HEAD_DIM)
        km = jnp.where(r_head == c_head, km, 0.0)
        hi = km.astype(BF16)
        kmh_ref[...] = hi
        kml_ref[...] = (km - hi.astype(F32)).astype(BF16)

    q = q_ref[0]
    gate = _dot_nt(q, kmh_ref[...]) + _dot_nt(q, kml_ref[...])
    lane = lax.broadcasted_iota(I32, (tq, gl), 1)
    blk = lane & (MOBA_SLOTS - 1)
    past = blk < qi
    gate = jnp.where(past, gate, NEG_INF)
    rank = jnp.zeros((tq, gl), I32)
    for d in range(1, MOBA_SLOTS):
        fwd = pltpu.roll(gate, gl - d, 1)
        back = pltpu.roll(gate, MOBA_SLOTS - d, 1)
        wraps = blk + d >= MOBA_SLOTS
        other = jnp.where(wraps, back, fwd)
        rank = rank + jnp.where(wraps, jnp.where(other >= gate, 1, 0), jnp.where(other > gate, 1, 0))
    sel = jnp.logical_and(past, rank < topb)
    selb_ref[...] = jnp.where(sel, 0.0, NEG_INF)

    _flash_init(m_ref, l_ref, acc_ref)

    def att_body(n, carry):
        c0 = pl.multiple_of(n * MOBA_BLOCK, MOBA_BLOCK)
        k = k_ref[0, pl.ds(c0, MOBA_BLOCK), :]
        v = v_ref[0, pl.ds(c0, MOBA_BLOCK), :]
        sb = selb_ref[...]
        for h in range(N_HEADS):
            sl = slice(h * HEAD_DIM, (h + 1) * HEAD_DIM)
            colb = jnp.max(jnp.where(lane == h * MOBA_SLOTS + n, sb, NEG_INF), axis=1, keepdims=True)
            _flash_head_step(h, q[:, sl], k[:, sl], v[:, sl], colb, m_ref, l_ref, acc_ref)
        return carry

    lax.fori_loop(0, qi, att_body, 0)

    c0 = pl.multiple_of(qi * MOBA_BLOCK, MOBA_BLOCK)
    k = k_ref[0, pl.ds(c0, MOBA_BLOCK), :]
    v = v_ref[0, pl.ds(c0, MOBA_BLOCK), :]
    r = lax.broadcasted_iota(I32, (tq, MOBA_BLOCK), 0)
    c = lax.broadcasted_iota(I32, (tq, MOBA_BLOCK), 1)
    causal = jnp.where(c <= r, 0.0, NEG_INF)
    for h in range(N_HEADS):
        sl = slice(h * HEAD_DIM, (h + 1) * HEAD_DIM)
        _flash_head_step(h, q[:, sl], k[:, sl], v[:, sl], causal, m_ref, l_ref, acc_ref)
    _flash_finish(o_ref, l_ref, acc_ref)


def _moba(P3, B, S):
    tq = MOBA_BLOCK
    nblk = S // MOBA_BLOCK
    assert S % MOBA_BLOCK == 0 and nblk <= MOBA_SLOTS
    topb = min(MOBA_TOPK, nblk - 1)
    gl = N_HEADS * MOBA_SLOTS
    return pl.pallas_call(
        functools.partial(_moba_kernel, tq=tq, nblk=nblk, topb=topb),
        grid=(B, S // tq),
        in_specs=[pl.BlockSpec((1, tq, TILE), lambda b, i: (b, i, T_C)),
                  pl.BlockSpec((1, S, TILE), lambda b, i: (b, 0, T_C + 1)),
                  pl.BlockSpec((1, S, TILE), lambda b, i: (b, 0, T_C + 2))],
        out_specs=pl.BlockSpec((1, tq, TILE), lambda b, i: (b, i, 0)),
        out_shape=jax.ShapeDtypeStruct((B, S, TILE), BF16),
        scratch_shapes=[pltpu.VMEM((gl, TILE), BF16), pltpu.VMEM((gl, TILE), BF16),
                        pltpu.VMEM((tq, gl), F32),
                        pltpu.VMEM((N_HEADS, tq, LANES), F32), pltpu.VMEM((N_HEADS, tq, LANES), F32),
                        pltpu.VMEM((tq, TILE), F32)],
        compiler_params=_params("arbitrary", "arbitrary"),
        name="moba",
    )(P3, P3, P3)


def _post_kernel(x_ref, oa0, oa1, oa2, la0, la1, la2, ob_ref, oc_ref, z0, z1, z2, g0, g1, g2,
                 wbr_ref, wout_ref, out_ref):
    l0, l1, l2 = la0[...], la1[...], la2[...]
    mx = jnp.maximum(jnp.maximum(l0, l1), l2)
    e0, e1, e2 = jnp.exp(l0 - mx), jnp.exp(l1 - mx), jnp.exp(l2 - mx)
    den = e0 + e1 + e2
    o_a = (e0 * oa0[...].astype(F32) + e1 * oa1[...].astype(F32) + e2 * oa2[...].astype(F32)) / den
    branches = (o_a, ob_ref[...].astype(F32), oc_ref[...].astype(F32))
    merged = jnp.zeros(out_ref.shape, F32)
    for n, (o, z, g) in enumerate(zip(branches, (z0, z1, z2), (g0, g1, g2))):
        y = _dot((o * z[...].astype(F32)).astype(BF16), wbr_ref[n])
        merged = merged + g[...].astype(F32) * y
    out_ref[...] = x_ref[...] + _dot(merged.astype(BF16), wout_ref[...])


def _post(x2, oa, la, ob, oc, P2, wbr, wout, tm):
    T = x2.shape[0]
    row = lambda width, t: pl.BlockSpec((tm, width), lambda i: (i, t))
    per_g = D_MODEL // TILE
    in_specs = ([row(D_MODEL, 0)] + [row(TILE, 0)] * 8
                + [row(TILE, T_Z + n) for n in range(N_BRANCH)]
                + [row(D_MODEL, T_G // per_g + n) for n in range(N_BRANCH)]
                + [pl.BlockSpec((N_BRANCH, BRANCH_WIDTH, D_MODEL), lambda i: (0, 0, 0)),
                   pl.BlockSpec((D_MODEL, D_MODEL), lambda i: (0, 0))])
    return pl.pallas_call(
        _post_kernel,
        grid=(T // tm,),
        in_specs=in_specs,
        out_specs=row(D_MODEL, 0),
        out_shape=jax.ShapeDtypeStruct((T, D_MODEL), F32),
        compiler_params=_params("arbitrary"),
        name="post",
    )(x2, oa[0], oa[1], oa[2], la[0], la[1], la[2], ob, oc, P2, P2, P2, P2, P2, P2, wbr, wout)


def _rearrange_w_in(w):
    bw = BRANCH_WIDTH
    a_q, a_k, a_v = w[:, 0:3 * bw], w[:, 3 * bw:6 * bw], w[:, 6 * bw:9 * bw]
    off = 9 * bw
    b_qkv = w[:, off:off + 3 * bw]; off += 3 * bw
    iq = w[:, off:off + IDX_HEADS * IDX_DIM]; off += IDX_HEADS * IDX_DIM
    ik = w[:, off:off + IDX_DIM]; off += IDX_DIM
    iw = w[:, off:off + IDX_HEADS]; off += IDX_HEADS
    c_qkv = w[:, off:off + 3 * bw]; off += 3 * bw
    z = w[:, off:off + 3 * bw]; off += 3 * bw
    g = w[:, off:off + 3 * D_MODEL]; off += 3 * D_MODEL
    assert off == w.shape[1]
    cols = []
    for grp in (1, 2, 0):
        sl = slice(grp * bw, (grp + 1) * bw)
        cols += [a_q[:, sl], a_k[:, sl], a_v[:, sl]]
    pad = jnp.zeros((w.shape[0], TILE - IDX_DIM - IDX_HEADS), w.dtype)
    cols += [b_qkv, iq, ik, iw, pad, c_qkv, z, g]
    out = jnp.concatenate(cols, axis=1).astype(BF16)
    assert out.shape[1] == N_TILES * TILE
    return out


def _tile_tables(qk_g):
    kinds = np.zeros((N_TILES,), np.int32)
    gains = [jnp.ones((TILE,), F32)] * N_TILES
    scale = HEAD_DIM ** -0.5
    nat = N_FOLD_TILES

    def qk(tile, mixer):
        kinds[tile] = KIND_NORM_ROPE
        kinds[tile + 1] = KIND_NORM_ROPE
        gains[tile] = jnp.tile(qk_g[mixer, 0], N_HEADS) * scale
        gains[tile + 1] = jnp.tile(qk_g[mixer, 1], N_HEADS)

    qk(0, 0)
    qk(N_FOLD_TILES // 2, 0)
    qk(nat + T_A, 0)
    qk(nat + T_B, 1)
    qk(nat + T_C, 2)
    kinds[nat + T_IQ] = KIND_ROPE
    gains[nat + T_IQ] = jnp.full((TILE,), IDX_DIM ** -0.5, F32)
    kinds[nat + T_IKW] = KIND_ROPE_LOW
    kinds[nat + T_Z:nat + T_Z + 3] = KIND_SILU
    kinds[nat + T_G:nat + T_G + 6] = KIND_SIGMOID
    return jnp.asarray(kinds), jnp.stack(gains)[:, None, :]


def _rope_tables(positions):
    inv = ROPE_THETA ** (-jnp.arange(0, ROT_DIM, 2, dtype=F32) / ROT_DIM)
    ang = positions.astype(F32).reshape(-1)[:, None] * inv
    cos, sin = jnp.cos(ang), jnp.sin(ang)
    T = cos.shape[0]
    z8 = jnp.zeros((T, ROT_HALF), F32)
    rest1 = jnp.ones((T, HEAD_DIM - ROT_DIM), F32)
    rest0 = jnp.zeros((T, HEAD_DIM - ROT_DIM), F32)
    c = jnp.concatenate([cos, cos, rest1], axis=1)
    s1 = jnp.concatenate([-sin, z8, rest0], axis=1)
    s2 = jnp.concatenate([z8, sin, rest0], axis=1)
    two = lambda t: jnp.concatenate([t, t], axis=1)
    return two(c), two(s1), two(s2)


def _block_diag_mean():
    h = np.arange(TILE) // HEAD_DIM
    return jnp.asarray((h[:, None] == h[None, :]).astype(np.float32) / HEAD_DIM, BF16)


def _layer(x2, B, S, cos_t, s1_t, s2_t, bd, norm_g, w_in, qk_g, w_br, w_out, tm_in, tm_post):
    kinds, gains = _tile_tables(qk_g)
    P2, f1, f2 = _inproj(x2, norm_g[None, :], _rearrange_w_in(w_in), kinds, gains, cos_t, s1_t, s2_t,
                         bd, tm_in, B, S)
    P3 = P2.reshape(B, S, NP)
    oa, la = [], []
    for g, qkv in enumerate((P3.reshape(B, 1, S, NP), f1, f2)):
        o, lse = _attn_a(qkv, g, DIL_PATTERNS[g][1], B, S)
        oa.append(o.reshape(B * S, TILE))
        la.append(lse.reshape(B * S, TILE))
    ob = _dsa(P3, B, S).reshape(B * S, TILE)
    oc = _moba(P3, B, S).reshape(B * S, TILE)
    return _post(x2, oa, la, ob, oc, P2, w_br.astype(BF16), w_out.astype(BF16), tm_post)


def _forward(x, positions, norm_g, w_in, qk_g, w_br, w_out, tm_in=1024, tm_post=512):
    B, S, D = x.shape
    cos_t, s1_t, s2_t = _rope_tables(positions)
    bd = _block_diag_mean()
    x2 = x.reshape(B * S, D)
    for layer in range(norm_g.shape[0]):
        x2 = _layer(x2, B, S, cos_t, s1_t, s2_t, bd, norm_g[layer], w_in[layer], qk_g[layer],
                    w_br[layer], w_out[layer], tm_in, tm_post)
    return x2.reshape(B, S, D)


def kernel(x, positions, norm_g, w_in, qk_g, w_br, w_out):
    return _forward(x, positions, norm_g, w_in, qk_g, w_br, w_out)
```

```python
import functools

import jax
import jax.numpy as jnp
import numpy as np
from jax import lax
from jax.experimental import pallas as pl
from jax.experimental.pallas import tpu as pltpu

F32 = jnp.float32
BF16 = jnp.bfloat16
I32 = jnp.int32

D_MODEL = 1024
HEAD_DIM = 64
ROT_DIM = HEAD_DIM // 4
ROT_HALF = ROT_DIM // 2
ROPE_THETA = 500000.0
NORM_EPS = 1e-6
NEG_INF = -1e30
N_HEADS = 8
BRANCH_WIDTH = N_HEADS * HEAD_DIM
N_BRANCH = 3
DIL_PATTERNS = ((128, 1), (512, 4), (2048, 16))
BAND = 128
IDX_HEADS = 8
IDX_DIM = 64
IDX_TOPK_MAX = 256
MOBA_BLOCK = 256
MOBA_TOPK = 3
MOBA_SLOTS = 16

LANES = 128
SUBLANES = 8
VMEM_LIMIT_BYTES = 56 * 1024 * 1024

TILE = BRANCH_WIDTH
N_FOLD_TILES = 6
T_A = 0
T_KB = 3
T_IKW = 4
T_KC = 5
T_G = 6
T_Z = 12
N_NAT_TILES = 15
TT_QB, TT_VB, TT_IQ, TT_QC, TT_VC = range(5)
N_T_TILES = 5
J_NAT = N_FOLD_TILES
J_T = N_FOLD_TILES + N_NAT_TILES
N_TILES = J_T + N_T_TILES
NP = N_NAT_TILES * TILE
A_STEP_TOKENS = 2048

KIND_PLAIN, KIND_NORM_ROPE, KIND_ROPE, KIND_ROPE_LOW, KIND_SILU, KIND_SIGMOID = range(6)

_FLOOR_KEY = int(np.array(-5e29, np.float32).view(np.int32)) ^ 0x7FFFFFFF
_INT_MIN = -2 ** 31
_LOG2_MOBA_BLOCK = MOBA_BLOCK.bit_length() - 1
_LOG2_MOBA_SLOTS = MOBA_SLOTS.bit_length() - 1
_LOG2_HEAD_DIM = HEAD_DIM.bit_length() - 1


def _dot(a, b):
    return jnp.dot(a, b, preferred_element_type=F32)


def _dot_nt(a, b):
    return lax.dot_general(a, b, (((1,), (1,)), ((), ())), preferred_element_type=F32)


def _params(*sem):
    return pltpu.CompilerParams(dimension_semantics=sem, vmem_limit_bytes=VMEM_LIMIT_BYTES)


def _inproj_kernel(kind_ref, x_ref, ng_ref, w_ref, wt_ref, gain_ref, gain_t_ref,
                   cos_ref, s1_ref, s2_ref, cos_t_ref, s1_t_ref, s2_t_ref, bd_ref,
                   o_ref, f1_ref, f2_ref, tt_ref, h_ref, ht_ref, stage_ref):
    j = pl.program_id(1)
    tm = x_ref.shape[0]
    n_ch = TILE // LANES
    kind = kind_ref[j]

    @pl.when(j == 0)
    def _():
        x = x_ref[...]
        ms = jnp.mean(x * x, axis=-1, keepdims=True)
        h = x * lax.rsqrt(ms + NORM_EPS) * ng_ref[...]
        h_ref[...] = h.astype(BF16)
        ht_ref[...] = jnp.transpose(h).astype(BF16)

    def fold(val, ref, dil):
        for c in range(n_ch):
            stage_ref[c] = val[:, c * LANES:(c + 1) * LANES]
        for r in range(dil):
            parts = [stage_ref[c, pl.ds(r, tm // dil, stride=dil), :] for c in range(n_ch)]
            ref[0, r] = jnp.concatenate(parts, axis=1).astype(ref.dtype)

    def emit(val):
        @pl.when(j >= J_NAT)
        def _():
            o_ref[...] = val.astype(o_ref.dtype)

        @pl.when(j < N_FOLD_TILES // 2)
        def _():
            fold(val, f1_ref, DIL_PATTERNS[1][1])

        @pl.when(jnp.logical_and(j >= N_FOLD_TILES // 2, j < J_NAT))
        def _():
            fold(val, f2_ref, DIL_PATTERNS[2][1])

    @pl.when(j < J_T)
    def _():
        y = _dot(h_ref[...], w_ref[...])

        def rope(v, low_only):
            c = jnp.tile(cos_ref[...], (1, n_ch))
            s1 = jnp.tile(s1_ref[...], (1, n_ch))
            s2 = jnp.tile(s2_ref[...], (1, n_ch))
            if low_only:
                low = lax.broadcasted_iota(I32, v.shape, 1) < HEAD_DIM
                c = jnp.where(low, c, 1.0)
                s1 = jnp.where(low, s1, 0.0)
                s2 = jnp.where(low, s2, 0.0)
            return v * c + pltpu.roll(v, TILE - ROT_HALF, 1) * s1 + pltpu.roll(v, ROT_HALF, 1) * s2

        @pl.when(kind == KIND_PLAIN)
        def _():
            emit(y)

        @pl.when(kind == KIND_NORM_ROPE)
        def _():
            ms = _dot((y * y).astype(BF16), bd_ref[...])
            emit(rope(y * lax.rsqrt(ms + NORM_EPS) * gain_ref[0], False))

        @pl.when(kind == KIND_ROPE_LOW)
        def _():
            o_ref[...] = rope(y, True).astype(o_ref.dtype)

        @pl.when(kind == KIND_SILU)
        def _():
            o_ref[...] = (y / (1.0 + jnp.exp(-y))).astype(o_ref.dtype)

        @pl.when(kind == KIND_SIGMOID)
        def _():
            o_ref[...] = (1.0 / (1.0 + jnp.exp(-y))).astype(o_ref.dtype)

    @pl.when(j >= J_T)
    def _():
        yt = _dot(wt_ref[...], ht_ref[...])
        gain = jnp.tile(gain_t_ref[0], (1, tm // LANES))

        def rope_t(v):
            c = jnp.tile(cos_t_ref[...], (N_HEADS, 1))
            s1 = jnp.tile(s1_t_ref[...], (N_HEADS, 1))
            s2 = jnp.tile(s2_t_ref[...], (N_HEADS, 1))
            return v * c + pltpu.roll(v, TILE - ROT_HALF, 0) * s1 + pltpu.roll(v, ROT_HALF, 0) * s2

        @pl.when(kind == KIND_PLAIN)
        def _():
            tt_ref[...] = yt.astype(tt_ref.dtype)

        @pl.when(kind == KIND_NORM_ROPE)
        def _():
            ms = _dot(bd_ref[...], (yt * yt).astype(BF16))
            tt_ref[...] = rope_t(yt * lax.rsqrt(ms + NORM_EPS) * gain).astype(tt_ref.dtype)

        @pl.when(kind == KIND_ROPE)
        def _():
            tt_ref[...] = rope_t(yt * gain).astype(tt_ref.dtype)


def _inproj(x2, ng, w_nat, w_t, kinds, gains, gains_t, tabs, tabs_t, bd, tm, B, S):
    T = x2.shape[0]
    grid = (T // tm, N_TILES)
    per_b = S // tm
    d1, d2 = DIL_PATTERNS[1][1], DIL_PATTERNS[2][1]
    half = N_FOLD_TILES // 2
    assert S % tm == 0 and tm % (16 * d2) == 0
    n_w = J_T

    def fold_spec(dil, first):
        return pl.BlockSpec(
            (1, dil, tm // dil, TILE),
            lambda i, j, k: (i // per_b, 0, i % per_b, jnp.clip(j - first, 0, half - 1)))

    row_tab = pl.BlockSpec((tm, LANES), lambda i, j, k: (i, 0))
    col_tab = pl.BlockSpec((HEAD_DIM, tm), lambda i, j, k: (0, i))
    t_idx = lambda j: jnp.maximum(j - J_T, 0)
    return pl.pallas_call(
        _inproj_kernel,
        grid_spec=pltpu.PrefetchScalarGridSpec(
            num_scalar_prefetch=1,
            grid=grid,
            in_specs=[
                pl.BlockSpec((tm, D_MODEL), lambda i, j, k: (i, 0)),
                pl.BlockSpec((1, D_MODEL), lambda i, j, k: (0, 0)),
                pl.BlockSpec((D_MODEL, TILE), lambda i, j, k: (0, jnp.minimum(j, n_w - 1))),
                pl.BlockSpec((TILE, D_MODEL), lambda i, j, k: (t_idx(j), 0)),
                pl.BlockSpec((1, 1, TILE), lambda i, j, k: (jnp.minimum(j, n_w - 1), 0, 0)),
                pl.BlockSpec((1, TILE, LANES), lambda i, j, k: (t_idx(j), 0, 0)),
                row_tab, row_tab, row_tab, col_tab, col_tab, col_tab,
                pl.BlockSpec((TILE, TILE), lambda i, j, k: (0, 0)),
            ],
            out_specs=[
                pl.BlockSpec((tm, TILE), lambda i, j, k: (i, jnp.clip(j - J_NAT, 0, N_NAT_TILES - 1))),
                fold_spec(d1, 0),
                fold_spec(d2, half),
                pl.BlockSpec((None, TILE, tm), lambda i, j, k: (i // per_b, t_idx(j), i % per_b)),
            ],
            scratch_shapes=[pltpu.VMEM((tm, D_MODEL), BF16), pltpu.VMEM((D_MODEL, tm), BF16),
                            pltpu.VMEM((TILE // LANES, tm, LANES), F32)],
        ),
        out_shape=[jax.ShapeDtypeStruct((T, NP), BF16),
                   jax.ShapeDtypeStruct((B, d1, S // d1, half * TILE), BF16),
                   jax.ShapeDtypeStruct((B, d2, S // d2, half * TILE), BF16),
                   jax.ShapeDtypeStruct((B, N_T_TILES * TILE, S), BF16)],
        compiler_params=_params("arbitrary", "arbitrary"),
        name="inproj",
    )(kinds, x2, ng, w_nat, w_t, gains, gains_t, *tabs, *tabs_t, bd)


def _attn_a_kernel(q_ref, kc_ref, kp_ref, vc_ref, vp_ref, o_ref, lse_ref, kcat, vcat, ost, lst,
                   *, tq, dil):
    m = pl.program_id(1)
    nsub = tq // BAND
    n_ch = TILE // LANES
    row = lax.broadcasted_iota(I32, (BAND, BAND), 0)
    col = lax.broadcasted_iota(I32, (BAND, BAND), 1)
    lead = col - row
    mask_cur = col <= row

    def residue(r, carry):
        kcat[0:BAND, :] = kp_ref[r]
        kcat[BAND:, :] = kc_ref[r]
        vcat[0:BAND, :] = vp_ref[r]
        vcat[BAND:, :] = vc_ref[r]

        def body(u, carry):
            r0 = pl.multiple_of(u * BAND, BAND)
            r1 = pl.multiple_of(u * BAND + BAND, BAND)
            q = q_ref[r, pl.ds(r0, BAND), :]
            k_p = kcat[pl.ds(r0, BAND), :]
            k_c = kcat[pl.ds(r1, BAND), :]
            v_p = vcat[pl.ds(r0, BAND), :]
            v_c = vcat[pl.ds(r1, BAND), :]
            mp = lead >= jnp.where(m * nsub + u > 0, 0, BAND)
            outs, lses = [], []
            for h in range(N_HEADS):
                sl = slice(h * HEAD_DIM, (h + 1) * HEAD_DIM)
                s_p = jnp.where(mp, _dot_nt(q[:, sl], k_p[:, sl]), NEG_INF)
                s_c = jnp.where(mask_cur, _dot_nt(q[:, sl], k_c[:, sl]), NEG_INF)
                mx = jnp.maximum(jnp.max(s_p, axis=1, keepdims=True),
                                 jnp.max(s_c, axis=1, keepdims=True))
                p_p = jnp.exp(s_p - mx)
                p_c = jnp.exp(s_c - mx)
                den = jnp.sum(p_p, axis=1, keepdims=True) + jnp.sum(p_c, axis=1, keepdims=True)
                pv = _dot(p_p.astype(BF16), v_p[:, sl]) + _dot(p_c.astype(BF16), v_c[:, sl])
                outs.append(pv / den)
                lses.append(jnp.broadcast_to(mx + jnp.log(den), (BAND, HEAD_DIM)))
            o = jnp.concatenate(outs, axis=1)
            lse = jnp.concatenate(lses, axis=1)
            rows = pl.ds(r + u * (BAND * dil), BAND, stride=dil) if dil > 1 else pl.ds(r0, BAND)
            for c in range(n_ch):
                ost[c, rows, :] = o[:, c * LANES:(c + 1) * LANES]
                lst[c, rows, :] = lse[:, c * LANES:(c + 1) * LANES]
            return carry

        return lax.fori_loop(0, nsub, body, carry)

    lax.fori_loop(0, dil, residue, 0)
    o_ref[0] = jnp.concatenate([ost[c] for c in range(n_ch)], axis=1).astype(o_ref.dtype)
    lse_ref[0] = jnp.concatenate([lst[c] for c in range(n_ch)], axis=1)


def _attn_a(qkv, g, dil, B, S):
    ts = min(A_STEP_TOKENS, S)
    tq = ts // dil
    assert S % ts == 0 and tq % BAND == 0
    sub = tq // BAND

    def spec_cur(t):
        return pl.BlockSpec((None, dil, tq, TILE), lambda b, m: (b, 0, m, t))

    def spec_prev(t):
        return pl.BlockSpec((None, dil, BAND, TILE), lambda b, m: (b, 0, jnp.maximum(m * sub - 1, 0), t))

    out_spec = pl.BlockSpec((1, ts, TILE), lambda b, m: (b, m, 0))
    n_ch = TILE // LANES
    return pl.pallas_call(
        functools.partial(_attn_a_kernel, tq=tq, dil=dil),
        grid=(B, S // ts),
        in_specs=[spec_cur(0), spec_cur(1), spec_prev(1), spec_cur(2), spec_prev(2)],
        out_specs=[out_spec, out_spec],
        out_shape=[jax.ShapeDtypeStruct((B, S, TILE), BF16), jax.ShapeDtypeStruct((B, S, TILE), F32)],
        scratch_shapes=[pltpu.VMEM((tq + BAND, TILE), BF16), pltpu.VMEM((tq + BAND, TILE), BF16),
                        pltpu.VMEM((n_ch, ts, LANES), F32), pltpu.VMEM((n_ch, ts, LANES), F32)],
        compiler_params=_params("arbitrary", "arbitrary"),
        name=f"attn_a{g}",
    )(qkv, qkv, qkv, qkv, qkv)


def _pad_q(qt_ref, qpad_ref):
    qpad_ref[...] = jnp.zeros(qpad_ref.shape, qpad_ref.dtype)
    for h in range(N_HEADS):
        r0 = h * LANES + (h % 2) * HEAD_DIM
        qpad_ref[r0:r0 + HEAD_DIM, :] = qt_ref[h * HEAD_DIM:(h + 1) * HEAD_DIM, :]


def _flash_init(m_ref, l_ref, acc_ref):
    m_ref[...] = jnp.full(m_ref.shape, NEG_INF, F32)
    l_ref[...] = jnp.zeros(l_ref.shape, F32)
    acc_ref[...] = jnp.zeros(acc_ref.shape, F32)


def _flash_tile_step(k, vt_of_head, bias_of_head, qpad_ref, s_ref, mn_ref, m_ref, l_ref, acc_ref):
    tk = k.shape[0]
    for h in range(N_HEADS):
        k_pair = k[:, (h // 2) * LANES:(h // 2 + 1) * LANES]
        s = _dot(k_pair, qpad_ref[h * LANES:(h + 1) * LANES, :]) + bias_of_head(h)
        s_ref[h] = s
        mn_ref[h] = jnp.maximum(m_ref[h], jnp.max(s, axis=0, keepdims=True))
    for h in range(N_HEADS):
        m_prev, m_next = m_ref[h], mn_ref[h]
        p = jnp.exp(s_ref[h] - jnp.tile(m_next, (tk // SUBLANES, 1)))
        alpha = jnp.exp(m_prev - m_next)
        l_ref[h] = alpha * l_ref[h] + jnp.sum(p, axis=0, keepdims=True)
        acc_ref[h] = (acc_ref[h] * jnp.tile(alpha, (HEAD_DIM // SUBLANES, 1))
                      + _dot(vt_of_head(h), p.astype(BF16)))
        m_ref[h] = m_next


def _flash_finish(o_ref, l_ref, acc_ref):
    outs = [acc_ref[h] / jnp.tile(l_ref[h], (HEAD_DIM // SUBLANES, 1)) for h in range(N_HEADS)]
    o_ref[...] = jnp.transpose(jnp.concatenate(outs, axis=0)).astype(o_ref.dtype)


def _dsa_kernel(qt_ref, k_ref, vt_ref, iqt_ref, ikw_q_ref, ikw_ref, o_ref,
                key_ref, qpad_ref, bias_ref, s_ref, mn_ref, m_ref, l_ref, acc_ref, *, tq, tk, topk):
    qi = pl.program_id(1)
    n_kt = qi + 1
    t0 = qi * tq
    n_acc = 4
    rows8 = tk // SUBLANES

    w8 = jnp.transpose(ikw_q_ref[...].astype(F32))[HEAD_DIM:HEAD_DIM + IDX_HEADS, :] * (IDX_HEADS ** -0.5)
    krow = lax.broadcasted_iota(I32, (tk, tq), 0)
    qcol = t0 + lax.broadcasted_iota(I32, (tk, tq), 1)

    def score_body(c, carry):
        c0 = pl.multiple_of(c * tk, tk)
        kx = ikw_ref[pl.ds(c0, tk), :][:, :IDX_DIM]
        sc = jnp.zeros((tk, tq), F32)
        for h in range(IDX_HEADS):
            lg = _dot(kx, iqt_ref[h * IDX_DIM:(h + 1) * IDX_DIM, :])
            sc = sc + w8[h:h + 1, :] * jnp.maximum(lg, 0.0)
        sc = jnp.where(sc == 0.0, 0.0, sc)
        sc = jnp.where(krow + c0 <= qcol, sc, NEG_INF)
        bits = pltpu.bitcast(sc, I32)
        key_ref[pl.ds(c0, tk), :] = jnp.where(bits < 0, bits ^ 0x7FFFFFFF, bits)
        return carry

    lax.fori_loop(0, n_kt, score_body, 0)

    def count_rows(fn, n_out):
        def cbody(c, accs):
            accs = [list(a) for a in accs]
            c0 = pl.multiple_of(c * tk, tk)
            kt = key_ref[pl.ds(c0, tk), :]
            for g in range(rows8):
                r0 = c0 + g * SUBLANES
                vals = fn(kt[g * SUBLANES:(g + 1) * SUBLANES], r0)
                for o in range(n_out):
                    accs[o][g % n_acc] = accs[o][g % n_acc] + vals[o]
            return tuple(tuple(a) for a in accs)
        z = jnp.zeros((SUBLANES, tq), I32)
        accs = lax.fori_loop(0, n_kt, cbody, tuple(tuple(z for _ in range(n_acc)) for _ in range(n_out)))
        return [jnp.sum(functools.reduce(lambda a, b: a + b, a), axis=0, keepdims=True) for a in accs]

    def bit_body(it, carry):
        ans, cge = carry
        cand_u = ans | lax.shift_left(jnp.int32(1), 31 - it)
        cand = jnp.broadcast_to(cand_u ^ _INT_MIN, (SUBLANES, tq))
        cnt, = count_rows(lambda kk, r0: (jnp.where(kk >= cand, 1, 0),), 1)
        ok = cnt >= topk
        return jnp.where(ok, cand_u, ans), jnp.where(ok, cnt, cge)

    ans, cge = lax.fori_loop(0, 32, bit_body,
                             (jnp.zeros((1, tq), I32), jnp.zeros((1, tq), I32) + n_kt * tk))
    thr = jnp.maximum(ans ^ _INT_MIN, _FLOOR_KEY)

    tie = jnp.logical_and(cge > topk, thr > _FLOOR_KEY)
    any_tie = jnp.max(jnp.where(tie, 1, 0)) > 0
    thr8 = jnp.broadcast_to(thr, (SUBLANES, tq))

    @pl.when(any_tie)
    def _():
        n_bits = int(np.log2(key_ref.shape[0]))
        sub = lax.broadcasted_iota(I32, (SUBLANES, tq), 0)

        def jbody(it, lo):
            cand = lo + lax.shift_left(jnp.int32(1), n_bits - 1 - it)
            pos = jnp.broadcast_to(cand - 1, (SUBLANES, tq))

            def f(kk, r0):
                eq = jnp.logical_and(kk == thr8, sub + r0 <= pos)
                return jnp.where(kk > thr8, 1, 0), jnp.where(eq, 1, 0)

            gt, eq = count_rows(f, 2)
            return jnp.where(gt + eq >= topk, lo, cand)

        jrow = jnp.broadcast_to(lax.fori_loop(0, n_bits, jbody, jnp.zeros((1, tq), I32)), (SUBLANES, tq))
        tie8 = jnp.broadcast_to(jnp.where(tie, 1, 0), (SUBLANES, tq)) > 0

        def fix(c, carry):
            for g in range(rows8):
                r0 = pl.multiple_of(c * tk + g * SUBLANES, SUBLANES)
                kk = key_ref[pl.ds(r0, SUBLANES), :]
                drop = jnp.logical_and(tie8, jnp.logical_and(kk == thr8, sub + r0 > jrow))
                key_ref[pl.ds(r0, SUBLANES), :] = jnp.where(drop, _INT_MIN, kk)
            return carry

        lax.fori_loop(0, n_kt, fix, 0)

    _pad_q(qt_ref, qpad_ref)
    _flash_init(m_ref, l_ref, acc_ref)

    def att_body(j, carry):
        c0 = pl.multiple_of(j * tk, tk)
        bias_ref[...] = jnp.where(key_ref[pl.ds(c0, tk), :] >= thr, 0.0, NEG_INF)
        _flash_tile_step(k_ref[pl.ds(c0, tk), :],
                         lambda h: vt_ref[h * HEAD_DIM:(h + 1) * HEAD_DIM, pl.ds(c0, tk)],
                         lambda h: bias_ref[...], qpad_ref, s_ref, mn_ref, m_ref, l_ref, acc_ref)
        return carry

    lax.fori_loop(0, n_kt, att_body, 0)
    _flash_finish(o_ref, l_ref, acc_ref)


def _flash_scratch(tk, tq):
    stat = pltpu.VMEM((N_HEADS, SUBLANES, tq), F32)
    return [pltpu.VMEM((N_HEADS, tk, tq), F32), stat, stat, stat, pltpu.VMEM((N_HEADS, HEAD_DIM, tq), F32)]


def _dsa(P3, TT, B, S):
    tq = tk = 256
    topk = min(IDX_TOPK_MAX, S // 4)
    assert S % tq == 0 and topk <= tk
    per = TILE // LANES

    def tt_q(t):
        return pl.BlockSpec((None, TILE, tq), lambda b, i: (b, t, i))

    return pl.pallas_call(
        functools.partial(_dsa_kernel, tq=tq, tk=tk, topk=topk),
        grid=(B, S // tq),
        in_specs=[tt_q(TT_QB),
                  pl.BlockSpec((None, S, TILE), lambda b, i: (b, 0, T_KB)),
                  pl.BlockSpec((None, TILE, S), lambda b, i: (b, TT_VB, 0)),
                  tt_q(TT_IQ),
                  pl.BlockSpec((None, tq, LANES), lambda b, i: (b, i, T_IKW * per)),
                  pl.BlockSpec((None, S, LANES), lambda b, i: (b, 0, T_IKW * per))],
        out_specs=pl.BlockSpec((None, tq, TILE), lambda b, i: (b, i, 0)),
        out_shape=jax.ShapeDtypeStruct((B, S, TILE), BF16),
        scratch_shapes=[pltpu.VMEM((S, tq), I32), pltpu.VMEM((N_HEADS * LANES, tq), BF16),
                        pltpu.VMEM((tk, tq), F32)] + _flash_scratch(tk, tq),
        compiler_params=_params("arbitrary", "arbitrary"),
        name="dsa",
    )(TT, P3, TT, TT, P3, P3)


def _moba_kernel(qt_ref, k_ref, vt_ref, o_ref, kmh_ref, kml_ref, qpad_ref, selb_ref,
                 s_ref, mn_ref, m_ref, l_ref, acc_ref, *, tq, topb):
    qi = pl.program_id(1)
    S = k_ref.shape[0]
    gl = N_HEADS * MOBA_SLOTS

    @pl.when(qi == 0)
    def _():
        blk_row = lax.broadcasted_iota(I32, (MOBA_SLOTS, S), 0)
        blk_col = jnp.right_shift(lax.broadcasted_iota(I32, (MOBA_SLOTS, S), 1), _LOG2_MOBA_BLOCK)
        avg = jnp.where(blk_row == blk_col, 1.0 / MOBA_BLOCK, 0.0).astype(BF16)
        kmean = _dot(avg, k_ref[...])
        km = jnp.tile(kmean, (N_HEADS, 1))
        r_head = jnp.right_shift(lax.broadcasted_iota(I32, (gl, TILE), 0), _LOG2_MOBA_SLOTS)
        c_head = jnp.right_shift(lax.broadcasted_iota(I32, (gl, TILE), 1), _LOG2_HEAD_DIM)
        km = jnp.where(r_head == c_head, km, 0.0)
        hi = km.astype(BF16)
        kmh_ref[...] = hi
        kml_ref[...] = (km - hi.astype(F32)).astype(BF16)

    qt = qt_ref[...]
    gate = _dot(kmh_ref[...], qt) + _dot(kml_ref[...], qt)
    blk = lax.broadcasted_iota(I32, (gl, tq), 0) & (MOBA_SLOTS - 1)
    past = blk < qi
    gate = jnp.where(past, gate, NEG_INF)
    rank = jnp.zeros((gl, tq), I32)
    for d in range(1, MOBA_SLOTS):
        fwd = pltpu.roll(gate, gl - d, 0)
        back = pltpu.roll(gate, MOBA_SLOTS - d, 0)
        wraps = blk + d >= MOBA_SLOTS
        other = jnp.where(wraps, back, fwd)
        rank = rank + jnp.where(wraps, jnp.where(other >= gate, 1, 0), jnp.where(other > gate, 1, 0))
    sel = jnp.logical_and(past, rank < topb)
    selb_ref[...] = jnp.where(sel, 0.0, NEG_INF)

    _pad_q(qt_ref, qpad_ref)
    _flash_init(m_ref, l_ref, acc_ref)

    def block_step(n, bias_of_head):
        c0 = pl.multiple_of(n * MOBA_BLOCK, MOBA_BLOCK)
        _flash_tile_step(k_ref[pl.ds(c0, MOBA_BLOCK), :],
                         lambda h: vt_ref[h * HEAD_DIM:(h + 1) * HEAD_DIM, pl.ds(c0, MOBA_BLOCK)],
                         bias_of_head, qpad_ref, s_ref, mn_ref, m_ref, l_ref, acc_ref)

    def att_body(n, carry):
        block_step(n, lambda h: selb_ref[pl.ds(h * MOBA_SLOTS + n, 1), :])
        return carry

    lax.fori_loop(0, qi, att_body, 0)

    r = lax.broadcasted_iota(I32, (MOBA_BLOCK, tq), 0)
    c = lax.broadcasted_iota(I32, (MOBA_BLOCK, tq), 1)
    causal = jnp.where(r <= c, 0.0, NEG_INF)
    block_step(qi, lambda h: causal)
    _flash_finish(o_ref, l_ref, acc_ref)


def _moba(P3, TT, B, S):
    tq = MOBA_BLOCK
    nblk = S // MOBA_BLOCK
    assert S % MOBA_BLOCK == 0 and nblk <= MOBA_SLOTS
    topb = min(MOBA_TOPK, nblk - 1)
    gl = N_HEADS * MOBA_SLOTS
    return pl.pallas_call(
        functools.partial(_moba_kernel, tq=tq, topb=topb),
        grid=(B, S // tq),
        in_specs=[pl.BlockSpec((None, TILE, tq), lambda b, i: (b, TT_QC, i)),
                  pl.BlockSpec((None, S, TILE), lambda b, i: (b, 0, T_KC)),
                  pl.BlockSpec((None, TILE, S), lambda b, i: (b, TT_VC, 0))],
        out_specs=pl.BlockSpec((None, tq, TILE), lambda b, i: (b, i, 0)),
        out_shape=jax.ShapeDtypeStruct((B, S, TILE), BF16),
        scratch_shapes=[pltpu.VMEM((gl, TILE), BF16), pltpu.VMEM((gl, TILE), BF16),
                        pltpu.VMEM((N_HEADS * LANES, tq), BF16), pltpu.VMEM((gl, tq), F32)]
                       + _flash_scratch(MOBA_BLOCK, tq),
        compiler_params=_params("arbitrary", "arbitrary"),
        name="moba",
    )(TT, P3, TT)


def _post_kernel(x_ref, oa0, oa1, oa2, la0, la1, la2, ob_ref, oc_ref, z0, z1, z2, g0, g1, g2,
                 wbr_ref, wout_ref, out_ref):
    l0, l1, l2 = la0[...], la1[...], la2[...]
    mx = jnp.maximum(jnp.maximum(l0, l1), l2)
    e0, e1, e2 = jnp.exp(l0 - mx), jnp.exp(l1 - mx), jnp.exp(l2 - mx)
    den = e0 + e1 + e2
    o_a = (e0 * oa0[...].astype(F32) + e1 * oa1[...].astype(F32) + e2 * oa2[...].astype(F32)) / den
    branches = (o_a, ob_ref[...].astype(F32), oc_ref[...].astype(F32))
    merged = jnp.zeros(out_ref.shape, F32)
    for n, (o, z, g) in enumerate(zip(branches, (z0, z1, z2), (g0, g1, g2))):
        y = _dot((o * z[...].astype(F32)).astype(BF16), wbr_ref[n])
        merged = merged + g[...].astype(F32) * y
    out_ref[...] = x_ref[...] + _dot(merged.astype(BF16), wout_ref[...])


def _post(x2, oa, la, ob, oc, P2, wbr, wout, tm):
    T = x2.shape[0]
    row = lambda width, t: pl.BlockSpec((tm, width), lambda i: (i, t))
    per_g = D_MODEL // TILE
    assert T_G % per_g == 0
    in_specs = ([row(D_MODEL, 0)] + [row(TILE, 0)] * 8
                + [row(TILE, T_Z + n) for n in range(N_BRANCH)]
                + [row(D_MODEL, T_G // per_g + n) for n in range(N_BRANCH)]
                + [pl.BlockSpec((N_BRANCH, BRANCH_WIDTH, D_MODEL), lambda i: (0, 0, 0)),
                   pl.BlockSpec((D_MODEL, D_MODEL), lambda i: (0, 0))])
    return pl.pallas_call(
        _post_kernel,
        grid=(T // tm,),
        in_specs=in_specs,
        out_specs=row(D_MODEL, 0),
        out_shape=jax.ShapeDtypeStruct((T, D_MODEL), F32),
        compiler_params=_params("arbitrary"),
        name="post",
    )(x2, oa[0], oa[1], oa[2], la[0], la[1], la[2], ob, oc, P2, P2, P2, P2, P2, P2, wbr, wout)


def _rearrange_w_in(w):
    bw = BRANCH_WIDTH
    a_q, a_k, a_v = w[:, 0:3 * bw], w[:, 3 * bw:6 * bw], w[:, 6 * bw:9 * bw]
    off = 9 * bw
    b_q, b_k, b_v = (w[:, off + i * bw:off + (i + 1) * bw] for i in range(3)); off += 3 * bw
    iq = w[:, off:off + IDX_HEADS * IDX_DIM]; off += IDX_HEADS * IDX_DIM
    ik = w[:, off:off + IDX_DIM]; off += IDX_DIM
    iw = w[:, off:off + IDX_HEADS]; off += IDX_HEADS
    c_q, c_k, c_v = (w[:, off + i * bw:off + (i + 1) * bw] for i in range(3)); off += 3 * bw
    z = w[:, off:off + 3 * bw]; off += 3 * bw
    g = w[:, off:off + 3 * D_MODEL]; off += 3 * D_MODEL
    assert off == w.shape[1]
    cols = []
    for grp in (1, 2, 0):
        sl = slice(grp * bw, (grp + 1) * bw)
        cols += [a_q[:, sl], a_k[:, sl], a_v[:, sl]]
    pad = jnp.zeros((w.shape[0], TILE - IDX_DIM - IDX_HEADS), w.dtype)
    cols += [b_k, ik, iw, pad, c_k, g, z]
    w_nat = jnp.concatenate(cols, axis=1).astype(BF16)
    assert w_nat.shape[1] == J_T * TILE
    w_t = jnp.concatenate([b_q, b_v, iq, c_q, c_v], axis=1).T.astype(BF16)
    assert w_t.shape[0] == N_T_TILES * TILE
    return w_nat, w_t


def _tile_tables(qk_g):
    kinds = np.zeros((N_TILES,), np.int32)
    ones = jnp.ones((TILE,), F32)
    gains = [ones] * J_T
    gains_t = [ones] * N_T_TILES
    scale = HEAD_DIM ** -0.5
    head = lambda v: jnp.tile(v, N_HEADS)

    def qk(tile, mixer):
        kinds[tile] = kinds[tile + 1] = KIND_NORM_ROPE
        gains[tile] = head(qk_g[mixer, 0]) * scale
        gains[tile + 1] = head(qk_g[mixer, 1])

    qk(0, 0)
    qk(N_FOLD_TILES // 2, 0)
    qk(J_NAT + T_A, 0)
    for tile, mixer in ((T_KB, 1), (T_KC, 2)):
        kinds[J_NAT + tile] = KIND_NORM_ROPE
        gains[J_NAT + tile] = head(qk_g[mixer, 1])
    kinds[J_NAT + T_IKW] = KIND_ROPE_LOW
    kinds[J_NAT + T_Z:J_NAT + T_Z + 3] = KIND_SILU
    kinds[J_NAT + T_G:J_NAT + T_G + 6] = KIND_SIGMOID
    for tile, mixer in ((TT_QB, 1), (TT_QC, 2)):
        kinds[J_T + tile] = KIND_NORM_ROPE
        gains_t[tile] = head(qk_g[mixer, 0]) * scale
    kinds[J_T + TT_IQ] = KIND_ROPE
    gains_t[TT_IQ] = jnp.full((TILE,), IDX_DIM ** -0.5, F32)
    gains_t = jnp.broadcast_to(jnp.stack(gains_t)[:, :, None], (N_T_TILES, TILE, LANES))
    return jnp.asarray(kinds), jnp.stack(gains)[:, None, :], gains_t


def _rope_tables(positions):
    inv = ROPE_THETA ** (-jnp.arange(0, ROT_DIM, 2, dtype=F32) / ROT_DIM)
    ang = positions.astype(F32).reshape(-1)[:, None] * inv
    cos, sin = jnp.cos(ang), jnp.sin(ang)
    T = cos.shape[0]
    z8 = jnp.zeros((T, ROT_HALF), F32)
    rest1 = jnp.ones((T, HEAD_DIM - ROT_DIM), F32)
    rest0 = jnp.zeros((T, HEAD_DIM - ROT_DIM), F32)
    c = jnp.concatenate([cos, cos, rest1], axis=1)
    s1 = jnp.concatenate([-sin, z8, rest0], axis=1)
    s2 = jnp.concatenate([z8, sin, rest0], axis=1)
    two = lambda t: jnp.concatenate([t, t], axis=1)
    return (two(c), two(s1), two(s2)), (c.T, s1.T, s2.T)


def _block_diag_mean():
    h = np.arange(TILE) // HEAD_DIM
    return jnp.asarray((h[:, None] == h[None, :]).astype(np.float32) / HEAD_DIM, BF16)


def _layer(x2, B, S, tabs, tabs_t, bd, norm_g, w_in, qk_g, w_br, w_out, tm_in, tm_post):
    kinds, gains, gains_t = _tile_tables(qk_g)
    w_nat, w_t = _rearrange_w_in(w_in)
    P2, f1, f2, TT = _inproj(x2, norm_g[None, :], w_nat, w_t, kinds, gains, gains_t, tabs, tabs_t, bd,
                             tm_in, B, S)
    P3 = P2.reshape(B, S, NP)
    oa, la = [], []
    for g, qkv in enumerate((P3.reshape(B, 1, S, NP), f1, f2)):
        o, lse = _attn_a(qkv, g, DIL_PATTERNS[g][1], B, S)
        oa.append(o.reshape(B * S, TILE))
        la.append(lse.reshape(B * S, TILE))
    ob = _dsa(P3, TT, B, S).reshape(B * S, TILE)
    oc = _moba(P3, TT, B, S).reshape(B * S, TILE)
    return _post(x2, oa, la, ob, oc, P2, w_br.astype(BF16), w_out.astype(BF16), tm_post)


def _forward(x, positions, norm_g, w_in, qk_g, w_br, w_out, tm_in=1024, tm_post=512):
    B, S, D = x.shape
    tabs, tabs_t = _rope_tables(positions)
    bd = _block_diag_mean()
    x2 = x.reshape(B * S, D)
    for layer in range(norm_g.shape[0]):
        x2 = _layer(x2, B, S, tabs, tabs_t, bd, norm_g[layer], w_in[layer], qk_g[layer],
                    w_br[layer], w_out[layer], tm_in, tm_post)
    return x2.reshape(B, S, D)


def kernel(x, positions, norm_g, w_in, qk_g, w_br, w_out):
    return _forward(x, positions, norm_g, w_in, qk_g, w_br, w_out)
```

```python
import functools

import jax
import jax.numpy as jnp
import numpy as np
from jax import lax
from jax.experimental import pallas as pl
from jax.experimental.pallas import tpu as pltpu

F32 = jnp.float32
BF16 = jnp.bfloat16
I32 = jnp.int32

D_MODEL = 1024
HEAD_DIM = 64
ROT_DIM = HEAD_DIM // 4
ROT_HALF = ROT_DIM // 2
ROPE_THETA = 500000.0
NORM_EPS = 1e-6
NEG_INF = -1e30
N_HEADS = 8
BRANCH_WIDTH = N_HEADS * HEAD_DIM
N_BRANCH = 3
DIL_PATTERNS = ((128, 1), (512, 4), (2048, 16))
BAND = 128
IDX_HEADS = 8
IDX_DIM = 64
IDX_TOPK_MAX = 256
MOBA_BLOCK = 256
MOBA_TOPK = 3
MOBA_SLOTS = 16

LANES = 128
SUBLANES = 8
MXU_DIM = 256
VMEM_LIMIT_BYTES = 56 * 1024 * 1024

TILE = BRANCH_WIDTH
N_FOLD_TILES = 6
T_A = 0
T_KB = 3
T_IKW = 4
T_KC = 5
T_G = 6
T_Z = 12
N_NAT_TILES = 15
TT_QB, TT_VB, TT_IQ, TT_QC, TT_VC = range(5)
N_T_TILES = 5
J_NAT = N_FOLD_TILES
J_T = N_FOLD_TILES + N_NAT_TILES
N_TILES = J_T + N_T_TILES
NP = N_NAT_TILES * TILE
A_STEP_TOKENS = 2048

KIND_PLAIN, KIND_NORM_ROPE, KIND_ROPE, KIND_ROPE_LOW, KIND_SILU, KIND_SIGMOID = range(6)
DEST_NAT, DEST_F1, DEST_F2, DEST_T = range(4)
_ROUTES = ((KIND_PLAIN, DEST_NAT), (KIND_PLAIN, DEST_F1), (KIND_PLAIN, DEST_F2),
           (KIND_NORM_ROPE, DEST_NAT), (KIND_NORM_ROPE, DEST_F1), (KIND_NORM_ROPE, DEST_F2),
           (KIND_ROPE_LOW, DEST_NAT), (KIND_SILU, DEST_NAT), (KIND_SIGMOID, DEST_NAT),
           (KIND_PLAIN, DEST_T), (KIND_NORM_ROPE, DEST_T), (KIND_ROPE, DEST_T))


def _route_code(kind, dest):
    return kind * 4 + dest

_FLOOR_KEY = int(np.array(-5e29, np.float32).view(np.int32)) ^ 0x7FFFFFFF
_INT_MIN = -2 ** 31
_LOG2_MOBA_BLOCK = MOBA_BLOCK.bit_length() - 1
_LOG2_MOBA_SLOTS = MOBA_SLOTS.bit_length() - 1
_LOG2_HEAD_DIM = HEAD_DIM.bit_length() - 1


def _dot(a, b):
    return jnp.dot(a, b, preferred_element_type=F32)


def _dot_nt(a, b):
    return lax.dot_general(a, b, (((1,), (1,)), ((), ())), preferred_element_type=F32)


def _params(*sem):
    return pltpu.CompilerParams(dimension_semantics=sem, vmem_limit_bytes=VMEM_LIMIT_BYTES)


def _inproj_kernel(route_ref, x_ref, ng_ref, w_ref, wt_ref, gain_ref, gain_t_ref,
                   cos_ref, s1_ref, s2_ref, cos_t_ref, s1_t_ref, s2_t_ref, bd_ref,
                   o_ref, f1_ref, f2_ref, tt_ref, h_ref, ht_ref, stage_ref):
    j = pl.program_id(1)
    tm = x_ref.shape[0]
    n_ch = TILE // LANES

    @pl.when(j == 0)
    def _():
        x = x_ref[...]
        ms = jnp.mean(x * x, axis=-1, keepdims=True)
        h = x * lax.rsqrt(ms + NORM_EPS) * ng_ref[...]
        h_ref[...] = h.astype(BF16)
        ht_ref[...] = jnp.transpose(h).astype(BF16)

    def fold(val, ref, dil):
        for c in range(n_ch):
            stage_ref[c] = val[:, c * LANES:(c + 1) * LANES]
        for r in range(dil):
            parts = [stage_ref[c, pl.ds(r, tm // dil, stride=dil), :] for c in range(n_ch)]
            ref[0, r] = jnp.concatenate(parts, axis=1).astype(ref.dtype)

    def rope(v, low_only):
        c = jnp.tile(cos_ref[...], (1, n_ch))
        s1 = jnp.tile(s1_ref[...], (1, n_ch))
        s2 = jnp.tile(s2_ref[...], (1, n_ch))
        if low_only:
            low = lax.broadcasted_iota(I32, v.shape, 1) < HEAD_DIM
            c = jnp.where(low, c, 1.0)
            s1 = jnp.where(low, s1, 0.0)
            s2 = jnp.where(low, s2, 0.0)
        return v * c + pltpu.roll(v, TILE - ROT_HALF, 1) * s1 + pltpu.roll(v, ROT_HALF, 1) * s2

    def rope_t(v):
        c = jnp.tile(cos_t_ref[...], (N_HEADS, 1))
        s1 = jnp.tile(s1_t_ref[...], (N_HEADS, 1))
        s2 = jnp.tile(s2_t_ref[...], (N_HEADS, 1))
        return v * c + pltpu.roll(v, TILE - ROT_HALF, 0) * s1 + pltpu.roll(v, ROT_HALF, 0) * s2

    def epilogue(kind, y):
        if kind == KIND_PLAIN:
            return y
        if kind == KIND_NORM_ROPE:
            y2 = (y * y).astype(BF16)
            ms = jnp.concatenate([_dot(y2[:, c * MXU_DIM:(c + 1) * MXU_DIM], bd_ref[...])
                                  for c in range(TILE // MXU_DIM)], axis=1)
            return rope(y * lax.rsqrt(ms + NORM_EPS) * gain_ref[0], False)
        if kind == KIND_ROPE_LOW:
            return rope(y, True)
        if kind == KIND_SILU:
            return y / (1.0 + jnp.exp(-y))
        assert kind == KIND_SIGMOID
        return 1.0 / (1.0 + jnp.exp(-y))

    def epilogue_t(kind, yt):
        if kind == KIND_PLAIN:
            return yt
        gain = jnp.tile(gain_t_ref[0], (1, tm // LANES))
        if kind == KIND_NORM_ROPE:
            y2 = (yt * yt).astype(BF16)
            ms = jnp.concatenate([_dot(bd_ref[...], y2[c * MXU_DIM:(c + 1) * MXU_DIM, :])
                                  for c in range(TILE // MXU_DIM)], axis=0)
            return rope_t(yt * lax.rsqrt(ms + NORM_EPS) * gain)
        assert kind == KIND_ROPE
        return rope_t(yt * gain)

    route = route_ref[j]
    for kind, dest in _ROUTES:
        @pl.when(route == _route_code(kind, dest))
        def _(kind=kind, dest=dest):
            if dest == DEST_T:
                tt_ref[...] = epilogue_t(kind, _dot(wt_ref[...], ht_ref[...])).astype(tt_ref.dtype)
                return
            val = epilogue(kind, _dot(h_ref[...], w_ref[...]))
            if dest == DEST_NAT:
                o_ref[...] = val.astype(o_ref.dtype)
            elif dest == DEST_F1:
                fold(val, f1_ref, DIL_PATTERNS[1][1])
            else:
                fold(val, f2_ref, DIL_PATTERNS[2][1])


def _inproj(x2, ng, w_nat, w_t, kinds, gains, gains_t, tabs, tabs_t, bd, tm, B, S):
    T = x2.shape[0]
    grid = (T // tm, N_TILES)
    per_b = S // tm
    d1, d2 = DIL_PATTERNS[1][1], DIL_PATTERNS[2][1]
    half = N_FOLD_TILES // 2
    assert S % tm == 0 and tm % (16 * d2) == 0
    n_w = J_T

    def fold_spec(dil, first):
        return pl.BlockSpec(
            (1, dil, tm // dil, TILE),
            lambda i, j, k: (i // per_b, 0, i % per_b, jnp.clip(j - first, 0, half - 1)))

    row_tab = pl.BlockSpec((tm, LANES), lambda i, j, k: (i, 0))
    col_tab = pl.BlockSpec((HEAD_DIM, tm), lambda i, j, k: (0, i))
    t_idx = lambda j: jnp.maximum(j - J_T, 0)
    return pl.pallas_call(
        _inproj_kernel,
        grid_spec=pltpu.PrefetchScalarGridSpec(
            num_scalar_prefetch=1,
            grid=grid,
            in_specs=[
                pl.BlockSpec((tm, D_MODEL), lambda i, j, k: (i, 0)),
                pl.BlockSpec((1, D_MODEL), lambda i, j, k: (0, 0)),
                pl.BlockSpec((D_MODEL, TILE), lambda i, j, k: (0, jnp.minimum(j, n_w - 1))),
                pl.BlockSpec((TILE, D_MODEL), lambda i, j, k: (t_idx(j), 0)),
                pl.BlockSpec((1, 1, TILE), lambda i, j, k: (jnp.minimum(j, n_w - 1), 0, 0)),
                pl.BlockSpec((1, TILE, LANES), lambda i, j, k: (t_idx(j), 0, 0)),
                row_tab, row_tab, row_tab, col_tab, col_tab, col_tab,
                pl.BlockSpec((MXU_DIM, MXU_DIM), lambda i, j, k: (0, 0)),
            ],
            out_specs=[
                pl.BlockSpec((tm, TILE), lambda i, j, k: (i, jnp.clip(j - J_NAT, 0, N_NAT_TILES - 1))),
                fold_spec(d1, 0),
                fold_spec(d2, half),
                pl.BlockSpec((None, TILE, tm), lambda i, j, k: (i // per_b, t_idx(j), i % per_b)),
            ],
            scratch_shapes=[pltpu.VMEM((tm, D_MODEL), BF16), pltpu.VMEM((D_MODEL, tm), BF16),
                            pltpu.VMEM((TILE // LANES, tm, LANES), F32)],
        ),
        out_shape=[jax.ShapeDtypeStruct((T, NP), BF16),
                   jax.ShapeDtypeStruct((B, d1, S // d1, half * TILE), BF16),
                   jax.ShapeDtypeStruct((B, d2, S // d2, half * TILE), BF16),
                   jax.ShapeDtypeStruct((B, N_T_TILES * TILE, S), BF16)],
        compiler_params=_params("arbitrary", "arbitrary"),
        name="inproj",
    )(kinds, x2, ng, w_nat, w_t, gains, gains_t, *tabs, *tabs_t, bd)


def _attn_a_kernel(q_ref, kc_ref, kp_ref, vc_ref, vp_ref, o_ref, lse_ref, kcat, vtcat, ost, lst,
                   s_ref, *, tq, dil):
    m = pl.program_id(1)
    nsub = tq // BAND
    n_ch = TILE // LANES
    nk = 2 * BAND
    key = lax.broadcasted_iota(I32, (nk, BAND), 0)
    qry = lax.broadcasted_iota(I32, (nk, BAND), 1) + BAND
    in_band = jnp.logical_and(key <= qry, key >= qry - BAND)
    lane = lax.broadcasted_iota(I32, (BAND, LANES), 1)
    half_mask = (jnp.where(lane < HEAD_DIM, 1.0, 0.0).astype(BF16),
                 jnp.where(lane < HEAD_DIM, 0.0, 1.0).astype(BF16))

    def residue(r, carry):
        kcat[0:BAND, :] = kp_ref[r]
        kcat[BAND:, :] = kc_ref[r]
        vtcat[:, 0:BAND] = jnp.transpose(vp_ref[r].astype(F32)).astype(BF16)
        for c in range(nsub):
            chunk = vc_ref[r, c * BAND:(c + 1) * BAND, :].astype(F32)
            vtcat[:, (c + 1) * BAND:(c + 2) * BAND] = jnp.transpose(chunk).astype(BF16)

        def body(u, carry):
            r0 = pl.multiple_of(u * BAND, BAND)
            q = q_ref[r, pl.ds(r0, BAND), :]
            k2 = kcat[pl.ds(r0, nk), :]
            valid = jnp.logical_and(in_band, key >= jnp.where(m * nsub + u > 0, 0, BAND))
            maxes, outs, lses = [], [], []
            for h in range(N_HEADS):
                pair = slice((h // 2) * LANES, (h // 2 + 1) * LANES)
                q_h = q[:, pair] * half_mask[h % 2]
                s = jnp.where(valid, _dot_nt(k2[:, pair], q_h), NEG_INF)
                s_ref[h] = s
                maxes.append(jnp.max(s, axis=0, keepdims=True))
            for h in range(N_HEADS):
                mx = maxes[h]
                p = jnp.exp(s_ref[h] - mx)
                den = jnp.sum(p, axis=0, keepdims=True)
                vt_h = vtcat[h * HEAD_DIM:(h + 1) * HEAD_DIM, pl.ds(r0, nk)]
                outs.append(_dot(vt_h, p.astype(BF16)) / den)
                lses.append(mx + jnp.log(den))
            o = jnp.transpose(jnp.concatenate(outs, axis=0))
            lse8 = jnp.concatenate(lses, axis=0)
            lse = jnp.transpose(jnp.tile(lse8, (LANES // N_HEADS, 1)))
            rows = pl.ds(r + u * (BAND * dil), BAND, stride=dil) if dil > 1 else pl.ds(r0, BAND)
            for c in range(n_ch):
                ost[c, rows, :] = o[:, c * LANES:(c + 1) * LANES]
            lst[rows, :] = lse
            return carry

        return lax.fori_loop(0, nsub, body, carry)

    lax.fori_loop(0, dil, residue, 0)
    o_ref[0] = jnp.concatenate([ost[c] for c in range(n_ch)], axis=1).astype(o_ref.dtype)
    lse_ref[0] = lst[...]


def _attn_a(qkv, g, dil, B, S):
    ts = min(A_STEP_TOKENS, S)
    tq = ts // dil
    assert S % ts == 0 and tq % BAND == 0
    sub = tq // BAND

    def spec_cur(t):
        return pl.BlockSpec((None, dil, tq, TILE), lambda b, m: (b, 0, m, t))

    def spec_prev(t):
        return pl.BlockSpec((None, dil, BAND, TILE), lambda b, m: (b, 0, jnp.maximum(m * sub - 1, 0), t))

    n_ch = TILE // LANES
    return pl.pallas_call(
        functools.partial(_attn_a_kernel, tq=tq, dil=dil),
        grid=(B, S // ts),
        in_specs=[spec_cur(0), spec_cur(1), spec_prev(1), spec_cur(2), spec_prev(2)],
        out_specs=[pl.BlockSpec((1, ts, TILE), lambda b, m: (b, m, 0)),
                   pl.BlockSpec((1, ts, LANES), lambda b, m: (b, m, 0))],
        out_shape=[jax.ShapeDtypeStruct((B, S, TILE), BF16), jax.ShapeDtypeStruct((B, S, LANES), F32)],
        scratch_shapes=[pltpu.VMEM((tq + BAND, TILE), BF16), pltpu.VMEM((TILE, tq + BAND), BF16),
                        pltpu.VMEM((n_ch, ts, LANES), F32), pltpu.VMEM((ts, LANES), F32),
                        pltpu.VMEM((N_HEADS, 2 * BAND, BAND), F32)],
        compiler_params=_params("arbitrary", "arbitrary"),
        name=f"attn_a{g}",
    )(qkv, qkv, qkv, qkv, qkv)


def _pad_q(qt_ref, qpad_ref):
    qpad_ref[...] = jnp.zeros(qpad_ref.shape, qpad_ref.dtype)
    for h in range(N_HEADS):
        r0 = h * LANES + (h % 2) * HEAD_DIM
        qpad_ref[r0:r0 + HEAD_DIM, :] = qt_ref[h * HEAD_DIM:(h + 1) * HEAD_DIM, :]


def _flash_init(m_ref, l_ref, acc_ref):
    m_ref[...] = jnp.full(m_ref.shape, NEG_INF, F32)
    l_ref[...] = jnp.zeros(l_ref.shape, F32)
    acc_ref[...] = jnp.zeros(acc_ref.shape, F32)


def _flash_tile_step(k, vt_of_head, bias_of_head, qpad_ref, s_ref, mn_ref, m_ref, l_ref, acc_ref):
    tk = k.shape[0]
    for h in range(N_HEADS):
        k_pair = k[:, (h // 2) * LANES:(h // 2 + 1) * LANES]
        s = _dot(k_pair, qpad_ref[h * LANES:(h + 1) * LANES, :]) + bias_of_head(h)
        s_ref[h] = s
        mn_ref[h] = jnp.maximum(m_ref[h], jnp.max(s, axis=0, keepdims=True))
    for h in range(N_HEADS):
        m_prev, m_next = m_ref[h], mn_ref[h]
        p = jnp.exp(s_ref[h] - jnp.tile(m_next, (tk // SUBLANES, 1)))
        alpha = jnp.exp(m_prev - m_next)
        l_ref[h] = alpha * l_ref[h] + jnp.sum(p, axis=0, keepdims=True)
        acc_ref[h] = (acc_ref[h] * jnp.tile(alpha, (HEAD_DIM // SUBLANES, 1))
                      + _dot(vt_of_head(h), p.astype(BF16)))
        m_ref[h] = m_next


def _flash_finish(o_ref, l_ref, acc_ref):
    outs = [acc_ref[h] / jnp.tile(l_ref[h], (HEAD_DIM // SUBLANES, 1)) for h in range(N_HEADS)]
    o_ref[...] = jnp.transpose(jnp.concatenate(outs, axis=0)).astype(o_ref.dtype)


def _dsa_kernel(qt_ref, k_ref, vt_ref, iqt_ref, ikw_q_ref, ikw_ref, o_ref,
                key_ref, qpad_ref, bias_ref, s_ref, mn_ref, m_ref, l_ref, acc_ref, *, tq, tk, topk):
    qi = pl.program_id(1)
    n_kt = qi + 1
    t0 = qi * tq
    n_acc = 4
    rows8 = tk // SUBLANES

    w8 = jnp.transpose(ikw_q_ref[...].astype(F32))[HEAD_DIM:HEAD_DIM + IDX_HEADS, :] * (IDX_HEADS ** -0.5)
    krow = lax.broadcasted_iota(I32, (tk, tq), 0)
    qcol = t0 + lax.broadcasted_iota(I32, (tk, tq), 1)

    def score_body(c, carry):
        c0 = pl.multiple_of(c * tk, tk)
        kx = ikw_ref[pl.ds(c0, tk), :][:, :IDX_DIM]
        sc = jnp.zeros((tk, tq), F32)
        for h in range(IDX_HEADS):
            lg = _dot(kx, iqt_ref[h * IDX_DIM:(h + 1) * IDX_DIM, :])
            sc = sc + w8[h:h + 1, :] * jnp.maximum(lg, 0.0)
        sc = jnp.where(sc == 0.0, 0.0, sc)
        sc = jnp.where(krow + c0 <= qcol, sc, NEG_INF)
        bits = pltpu.bitcast(sc, I32)
        key_ref[pl.ds(c0, tk), :] = jnp.where(bits < 0, bits ^ 0x7FFFFFFF, bits)
        return carry

    lax.fori_loop(0, n_kt, score_body, 0)

    def count_rows(fn, n_out):
        def cbody(c, accs):
            accs = [list(a) for a in accs]
            c0 = pl.multiple_of(c * tk, tk)
            kt = key_ref[pl.ds(c0, tk), :]
            for g in range(rows8):
                r0 = c0 + g * SUBLANES
                vals = fn(kt[g * SUBLANES:(g + 1) * SUBLANES], r0)
                for o in range(n_out):
                    accs[o][g % n_acc] = accs[o][g % n_acc] + vals[o]
            return tuple(tuple(a) for a in accs)
        z = jnp.zeros((SUBLANES, tq), I32)
        accs = lax.fori_loop(0, n_kt, cbody, tuple(tuple(z for _ in range(n_acc)) for _ in range(n_out)))
        return [jnp.sum(functools.reduce(lambda a, b: a + b, a), axis=0, keepdims=True) for a in accs]

    def bit_body(it, carry):
        ans, cge = carry
        cand_u = ans | lax.shift_left(jnp.int32(1), 31 - it)
        cand = jnp.broadcast_to(cand_u ^ _INT_MIN, (SUBLANES, tq))
        cnt, = count_rows(lambda kk, r0: (jnp.where(kk >= cand, 1, 0),), 1)
        ok = cnt >= topk
        return jnp.where(ok, cand_u, ans), jnp.where(ok, cnt, cge)

    ans, cge = lax.fori_loop(0, 32, bit_body,
                             (jnp.zeros((1, tq), I32), jnp.zeros((1, tq), I32) + n_kt * tk))
    thr = jnp.maximum(ans ^ _INT_MIN, _FLOOR_KEY)

    tie = jnp.logical_and(cge > topk, thr > _FLOOR_KEY)
    any_tie = jnp.max(jnp.where(tie, 1, 0)) > 0
    thr8 = jnp.broadcast_to(thr, (SUBLANES, tq))

    @pl.when(any_tie)
    def _():
        n_bits = int(np.log2(key_ref.shape[0]))
        sub = lax.broadcasted_iota(I32, (SUBLANES, tq), 0)

        def jbody(it, lo):
            cand = lo + lax.shift_left(jnp.int32(1), n_bits - 1 - it)
            pos = jnp.broadcast_to(cand - 1, (SUBLANES, tq))

            def f(kk, r0):
                eq = jnp.logical_and(kk == thr8, sub + r0 <= pos)
                return jnp.where(kk > thr8, 1, 0), jnp.where(eq, 1, 0)

            gt, eq = count_rows(f, 2)
            return jnp.where(gt + eq >= topk, lo, cand)

        jrow = jnp.broadcast_to(lax.fori_loop(0, n_bits, jbody, jnp.zeros((1, tq), I32)), (SUBLANES, tq))
        tie8 = jnp.broadcast_to(jnp.where(tie, 1, 0), (SUBLANES, tq)) > 0

        def fix(c, carry):
            for g in range(rows8):
                r0 = pl.multiple_of(c * tk + g * SUBLANES, SUBLANES)
                kk = key_ref[pl.ds(r0, SUBLANES), :]
                drop = jnp.logical_and(tie8, jnp.logical_and(kk == thr8, sub + r0 > jrow))
                key_ref[pl.ds(r0, SUBLANES), :] = jnp.where(drop, _INT_MIN, kk)
            return carry

        lax.fori_loop(0, n_kt, fix, 0)

    _pad_q(qt_ref, qpad_ref)
    _flash_init(m_ref, l_ref, acc_ref)

    def att_body(j, carry):
        c0 = pl.multiple_of(j * tk, tk)
        bias_ref[...] = jnp.where(key_ref[pl.ds(c0, tk), :] >= thr, 0.0, NEG_INF)
        _flash_tile_step(k_ref[pl.ds(c0, tk), :],
                         lambda h: vt_ref[h * HEAD_DIM:(h + 1) * HEAD_DIM, pl.ds(c0, tk)],
                         lambda h: bias_ref[...], qpad_ref, s_ref, mn_ref, m_ref, l_ref, acc_ref)
        return carry

    lax.fori_loop(0, n_kt, att_body, 0)
    _flash_finish(o_ref, l_ref, acc_ref)


def _flash_scratch(tk, tq):
    stat = pltpu.VMEM((N_HEADS, SUBLANES, tq), F32)
    return [pltpu.VMEM((N_HEADS, tk, tq), F32), stat, stat, stat, pltpu.VMEM((N_HEADS, HEAD_DIM, tq), F32)]


def _dsa(P3, TT, B, S):
    tq = tk = 256
    topk = min(IDX_TOPK_MAX, S // 4)
    assert S % tq == 0 and topk <= tk
    per = TILE // LANES

    def tt_q(t):
        return pl.BlockSpec((None, TILE, tq), lambda b, i: (b, t, i))

    return pl.pallas_call(
        functools.partial(_dsa_kernel, tq=tq, tk=tk, topk=topk),
        grid=(B, S // tq),
        in_specs=[tt_q(TT_QB),
                  pl.BlockSpec((None, S, TILE), lambda b, i: (b, 0, T_KB)),
                  pl.BlockSpec((None, TILE, S), lambda b, i: (b, TT_VB, 0)),
                  tt_q(TT_IQ),
                  pl.BlockSpec((None, tq, LANES), lambda b, i: (b, i, T_IKW * per)),
                  pl.BlockSpec((None, S, LANES), lambda b, i: (b, 0, T_IKW * per))],
        out_specs=pl.BlockSpec((None, tq, TILE), lambda b, i: (b, i, 0)),
        out_shape=jax.ShapeDtypeStruct((B, S, TILE), BF16),
        scratch_shapes=[pltpu.VMEM((S, tq), I32), pltpu.VMEM((N_HEADS * LANES, tq), BF16),
                        pltpu.VMEM((tk, tq), F32)] + _flash_scratch(tk, tq),
        compiler_params=_params("arbitrary", "arbitrary"),
        name="dsa",
    )(TT, P3, TT, TT, P3, P3)


def _moba_kernel(qt_ref, k_ref, vt_ref, o_ref, kmh_ref, kml_ref, qpad_ref, selb_ref,
                 s_ref, mn_ref, m_ref, l_ref, acc_ref, *, tq, topb):
    qi = pl.program_id(1)
    S = k_ref.shape[0]
    gl = N_HEADS * MOBA_SLOTS

    @pl.when(qi == 0)
    def _():
        blk_row = lax.broadcasted_iota(I32, (MOBA_SLOTS, S), 0)
        blk_col = jnp.right_shift(lax.broadcasted_iota(I32, (MOBA_SLOTS, S), 1), _LOG2_MOBA_BLOCK)
        avg = jnp.where(blk_row == blk_col, 1.0 / MOBA_BLOCK, 0.0).astype(BF16)
        kmean = _dot(avg, k_ref[...])
        km = jnp.tile(kmean, (N_HEADS, 1))
        r_head = jnp.right_shift(lax.broadcasted_iota(I32, (gl, TILE), 0), _LOG2_MOBA_SLOTS)
        c_head = jnp.right_shift(lax.broadcasted_iota(I32, (gl, TILE), 1), _LOG2_HEAD_DIM)
        km = jnp.where(r_head == c_head, km, 0.0)
        hi = km.astype(BF16)
        kmh_ref[...] = hi
        kml_ref[...] = (km - hi.astype(F32)).astype(BF16)

    qt = qt_ref[...]
    gate = _dot(kmh_ref[...], qt) + _dot(kml_ref[...], qt)
    blk = lax.broadcasted_iota(I32, (gl, tq), 0) & (MOBA_SLOTS - 1)
    past = blk < qi
    gate = jnp.where(past, gate, NEG_INF)
    rank = jnp.zeros((gl, tq), I32)
    for d in range(1, MOBA_SLOTS):
        fwd = pltpu.roll(gate, gl - d, 0)
        back = pltpu.roll(gate, MOBA_SLOTS - d, 0)
        wraps = blk + d >= MOBA_SLOTS
        other = jnp.where(wraps, back, fwd)
        rank = rank + jnp.where(wraps, jnp.where(other >= gate, 1, 0), jnp.where(other > gate, 1, 0))
    sel = jnp.logical_and(past, rank < topb)
    selb_ref[...] = jnp.where(sel, 0.0, NEG_INF)

    _pad_q(qt_ref, qpad_ref)
    _flash_init(m_ref, l_ref, acc_ref)

    def block_step(n, bias_of_head):
        c0 = pl.multiple_of(n * MOBA_BLOCK, MOBA_BLOCK)
        _flash_tile_step(k_ref[pl.ds(c0, MOBA_BLOCK), :],
                         lambda h: vt_ref[h * HEAD_DIM:(h + 1) * HEAD_DIM, pl.ds(c0, MOBA_BLOCK)],
                         bias_of_head, qpad_ref, s_ref, mn_ref, m_ref, l_ref, acc_ref)

    def att_body(n, carry):
        block_step(n, lambda h: selb_ref[pl.ds(h * MOBA_SLOTS + n, 1), :])
        return carry

    lax.fori_loop(0, qi, att_body, 0)

    r = lax.broadcasted_iota(I32, (MOBA_BLOCK, tq), 0)
    c = lax.broadcasted_iota(I32, (MOBA_BLOCK, tq), 1)
    causal = jnp.where(r <= c, 0.0, NEG_INF)
    block_step(qi, lambda h: causal)
    _flash_finish(o_ref, l_ref, acc_ref)


def _moba(P3, TT, B, S):
    tq = MOBA_BLOCK
    nblk = S // MOBA_BLOCK
    assert S % MOBA_BLOCK == 0 and nblk <= MOBA_SLOTS
    topb = min(MOBA_TOPK, nblk - 1)
    gl = N_HEADS * MOBA_SLOTS
    return pl.pallas_call(
        functools.partial(_moba_kernel, tq=tq, topb=topb),
        grid=(B, S // tq),
        in_specs=[pl.BlockSpec((None, TILE, tq), lambda b, i: (b, TT_QC, i)),
                  pl.BlockSpec((None, S, TILE), lambda b, i: (b, 0, T_KC)),
                  pl.BlockSpec((None, TILE, S), lambda b, i: (b, TT_VC, 0))],
        out_specs=pl.BlockSpec((None, tq, TILE), lambda b, i: (b, i, 0)),
        out_shape=jax.ShapeDtypeStruct((B, S, TILE), BF16),
        scratch_shapes=[pltpu.VMEM((gl, TILE), BF16), pltpu.VMEM((gl, TILE), BF16),
                        pltpu.VMEM((N_HEADS * LANES, tq), BF16), pltpu.VMEM((gl, tq), F32)]
                       + _flash_scratch(MOBA_BLOCK, tq),
        compiler_params=_params("arbitrary", "arbitrary"),
        name="moba",
    )(TT, P3, TT)


def _post_kernel(x_ref, oa0, oa1, oa2, la0, la1, la2, ob_ref, oc_ref, z0, z1, z2, g0, g1, g2,
                 wbr_ref, wout_ref, expand_ref, out_ref):
    l0, l1, l2 = la0[...], la1[...], la2[...]
    mx = jnp.maximum(jnp.maximum(l0, l1), l2)
    e0, e1, e2 = jnp.exp(l0 - mx), jnp.exp(l1 - mx), jnp.exp(l2 - mx)
    den = e0 + e1 + e2

    def spread(w):
        hi = w.astype(BF16)
        lo = (w - hi.astype(F32)).astype(BF16)
        return _dot(hi, expand_ref[...]) + _dot(lo, expand_ref[...])

    o_a = (spread(e0 / den) * oa0[...].astype(F32) + spread(e1 / den) * oa1[...].astype(F32)
           + spread(e2 / den) * oa2[...].astype(F32))
    branches = (o_a, ob_ref[...].astype(F32), oc_ref[...].astype(F32))
    merged = jnp.zeros(out_ref.shape, F32)
    for n, (o, z, g) in enumerate(zip(branches, (z0, z1, z2), (g0, g1, g2))):
        y = _dot((o * z[...].astype(F32)).astype(BF16), wbr_ref[n])
        merged = merged + g[...].astype(F32) * y
    out_ref[...] = x_ref[...] + _dot(merged.astype(BF16), wout_ref[...])


def _post(x2, oa, la, ob, oc, P2, wbr, wout, tm):
    T = x2.shape[0]
    row = lambda width, t: pl.BlockSpec((tm, width), lambda i: (i, t))
    per_g = D_MODEL // TILE
    assert T_G % per_g == 0
    head_of = np.arange(TILE) // HEAD_DIM
    expand = jnp.asarray((np.arange(LANES)[:, None] == head_of[None, :]).astype(np.float32), BF16)
    in_specs = ([row(D_MODEL, 0)] + [row(TILE, 0)] * 3 + [row(LANES, 0)] * 3 + [row(TILE, 0)] * 2
                + [row(TILE, T_Z + n) for n in range(N_BRANCH)]
                + [row(D_MODEL, T_G // per_g + n) for n in range(N_BRANCH)]
                + [pl.BlockSpec((N_BRANCH, BRANCH_WIDTH, D_MODEL), lambda i: (0, 0, 0)),
                   pl.BlockSpec((D_MODEL, D_MODEL), lambda i: (0, 0)),
                   pl.BlockSpec((LANES, TILE), lambda i: (0, 0))])
    return pl.pallas_call(
        _post_kernel,
        grid=(T // tm,),
        in_specs=in_specs,
        out_specs=row(D_MODEL, 0),
        out_shape=jax.ShapeDtypeStruct((T, D_MODEL), F32),
        compiler_params=_params("arbitrary"),
        name="post",
    )(x2, oa[0], oa[1], oa[2], la[0], la[1], la[2], ob, oc, P2, P2, P2, P2, P2, P2, wbr, wout, expand)


def _rearrange_w_in(w):
    bw = BRANCH_WIDTH
    a_q, a_k, a_v = w[:, 0:3 * bw], w[:, 3 * bw:6 * bw], w[:, 6 * bw:9 * bw]
    off = 9 * bw
    b_q, b_k, b_v = (w[:, off + i * bw:off + (i + 1) * bw] for i in range(3)); off += 3 * bw
    iq = w[:, off:off + IDX_HEADS * IDX_DIM]; off += IDX_HEADS * IDX_DIM
    ik = w[:, off:off + IDX_DIM]; off += IDX_DIM
    iw = w[:, off:off + IDX_HEADS]; off += IDX_HEADS
    c_q, c_k, c_v = (w[:, off + i * bw:off + (i + 1) * bw] for i in range(3)); off += 3 * bw
    z = w[:, off:off + 3 * bw]; off += 3 * bw
    g = w[:, off:off + 3 * D_MODEL]; off += 3 * D_MODEL
    assert off == w.shape[1]
    cols = []
    for grp in (1, 2, 0):
        sl = slice(grp * bw, (grp + 1) * bw)
        cols += [a_q[:, sl], a_k[:, sl], a_v[:, sl]]
    pad = jnp.zeros((w.shape[0], TILE - IDX_DIM - IDX_HEADS), w.dtype)
    cols += [b_k, ik, iw, pad, c_k, g, z]
    w_nat = jnp.concatenate(cols, axis=1).astype(BF16)
    assert w_nat.shape[1] == J_T * TILE
    w_t = jnp.concatenate([b_q, b_v, iq, c_q, c_v], axis=1).T.astype(BF16)
    assert w_t.shape[0] == N_T_TILES * TILE
    return w_nat, w_t


def _tile_tables(qk_g):
    kinds = np.zeros((N_TILES,), np.int32)
    ones = jnp.ones((TILE,), F32)
    gains = [ones] * J_T
    gains_t = [ones] * N_T_TILES
    scale = HEAD_DIM ** -0.5
    head = lambda v: jnp.tile(v, N_HEADS)

    def qk(tile, mixer):
        kinds[tile] = kinds[tile + 1] = KIND_NORM_ROPE
        gains[tile] = head(qk_g[mixer, 0]) * scale
        gains[tile + 1] = head(qk_g[mixer, 1])

    qk(0, 0)
    qk(N_FOLD_TILES // 2, 0)
    qk(J_NAT + T_A, 0)
    for tile, mixer in ((T_KB, 1), (T_KC, 2)):
        kinds[J_NAT + tile] = KIND_NORM_ROPE
        gains[J_NAT + tile] = head(qk_g[mixer, 1])
    kinds[J_NAT + T_IKW] = KIND_ROPE_LOW
    kinds[J_NAT + T_Z:J_NAT + T_Z + 3] = KIND_SILU
    kinds[J_NAT + T_G:J_NAT + T_G + 6] = KIND_SIGMOID
    for tile, mixer in ((TT_QB, 1), (TT_QC, 2)):
        kinds[J_T + tile] = KIND_NORM_ROPE
        gains_t[tile] = head(qk_g[mixer, 0]) * scale
    kinds[J_T + TT_IQ] = KIND_ROPE
    gains_t[TT_IQ] = jnp.full((TILE,), IDX_DIM ** -0.5, F32)
    gains_t = jnp.broadcast_to(jnp.stack(gains_t)[:, :, None], (N_T_TILES, TILE, LANES))
    half = N_FOLD_TILES // 2
    dest = np.array([DEST_F1] * half + [DEST_F2] * half + [DEST_NAT] * N_NAT_TILES + [DEST_T] * N_T_TILES)
    assert all((k, d) in _ROUTES for k, d in zip(kinds.tolist(), dest.tolist()))
    routes = _route_code(kinds, dest).astype(np.int32)
    return jnp.asarray(routes), jnp.stack(gains)[:, None, :], gains_t


def _rope_tables(positions):
    inv = ROPE_THETA ** (-jnp.arange(0, ROT_DIM, 2, dtype=F32) / ROT_DIM)
    ang = positions.astype(F32).reshape(-1)[:, None] * inv
    cos, sin = jnp.cos(ang), jnp.sin(ang)
    T = cos.shape[0]
    z8 = jnp.zeros((T, ROT_HALF), F32)
    rest1 = jnp.ones((T, HEAD_DIM - ROT_DIM), F32)
    rest0 = jnp.zeros((T, HEAD_DIM - ROT_DIM), F32)
    c = jnp.concatenate([cos, cos, rest1], axis=1)
    s1 = jnp.concatenate([-sin, z8, rest0], axis=1)
    s2 = jnp.concatenate([z8, sin, rest0], axis=1)
    two = lambda t: jnp.concatenate([t, t], axis=1)
    return (two(c), two(s1), two(s2)), (c.T, s1.T, s2.T)


def _block_diag_mean():
    h = np.arange(MXU_DIM) // HEAD_DIM
    return jnp.asarray((h[:, None] == h[None, :]).astype(np.float32) / HEAD_DIM, BF16)


def _layer(x2, B, S, tabs, tabs_t, bd, norm_g, w_in, qk_g, w_br, w_out, tm_in, tm_post):
    kinds, gains, gains_t = _tile_tables(qk_g)
    w_nat, w_t = _rearrange_w_in(w_in)
    P2, f1, f2, TT = _inproj(x2, norm_g[None, :], w_nat, w_t, kinds, gains, gains_t, tabs, tabs_t, bd,
                             tm_in, B, S)
    P3 = P2.reshape(B, S, NP)
    oa, la = [], []
    for g, qkv in enumerate((P3.reshape(B, 1, S, NP), f1, f2)):
        o, lse = _attn_a(qkv, g, DIL_PATTERNS[g][1], B, S)
        oa.append(o.reshape(B * S, TILE))
        la.append(lse.reshape(B * S, LANES))
    ob = _dsa(P3, TT, B, S).reshape(B * S, TILE)
    oc = _moba(P3, TT, B, S).reshape(B * S, TILE)
    return _post(x2, oa, la, ob, oc, P2, w_br.astype(BF16), w_out.astype(BF16), tm_post)


def _forward(x, positions, norm_g, w_in, qk_g, w_br, w_out, tm_in=1024, tm_post=512):
    B, S, D = x.shape
    tabs, tabs_t = _rope_tables(positions)
    bd = _block_diag_mean()
    x2 = x.reshape(B * S, D)
    for layer in range(norm_g.shape[0]):
        x2 = _layer(x2, B, S, tabs, tabs_t, bd, norm_g[layer], w_in[layer], qk_g[layer],
                    w_br[layer], w_out[layer], tm_in, tm_post)
    return x2.reshape(B, S, D)


def kernel(x, positions, norm_g, w_in, qk_g, w_br, w_out):
    return _forward(x, positions, norm_g, w_in, qk_g, w_br, w_out)
```

```python
import functools

import jax
import jax.numpy as jnp
import numpy as np
from jax import lax
from jax.experimental import pallas as pl
from jax.experimental.pallas import tpu as pltpu

F32 = jnp.float32
BF16 = jnp.bfloat16
I32 = jnp.int32
I16 = jnp.int16

D_MODEL = 1024
HEAD_DIM = 64
ROT_DIM = HEAD_DIM // 4
ROT_HALF = ROT_DIM // 2
ROPE_THETA = 500000.0
NORM_EPS = 1e-6
NEG_INF = -1e30
LOG2_E = 1.4426950408889634
N_HEADS = 8
BRANCH_WIDTH = N_HEADS * HEAD_DIM
N_BRANCH = 3
DIL_PATTERNS = ((128, 1), (512, 4), (2048, 16))
BAND = 128
IDX_HEADS = 8
IDX_DIM = 64
IDX_TOPK_MAX = 256
MOBA_BLOCK = 256
MOBA_TOPK = 3
MOBA_SLOTS = 16

LANES = 128
SUBLANES = 8
MXU_DIM = 256
VMEM_LIMIT_BYTES = 56 * 1024 * 1024

TILE = BRANCH_WIDTH
N_FOLD_TILES = 6
T_A = 0
T_KB = 3
T_IKW = 4
T_KC = 5
T_G = 6
T_Z = 12
N_NAT_TILES = 15
TT_QB, TT_VB, TT_IQ, TT_QC, TT_VC = range(5)
N_T_TILES = 5
J_NAT = N_FOLD_TILES
J_T = N_FOLD_TILES + N_NAT_TILES
N_TILES = J_T + N_T_TILES
NP = N_NAT_TILES * TILE
A_STEP_TOKENS = 2048

KIND_PLAIN, KIND_NORM_ROPE, KIND_ROPE, KIND_ROPE_LOW, KIND_SILU, KIND_SIGMOID = range(6)
DEST_NAT, DEST_F1, DEST_F2, DEST_T = range(4)
_ROUTES = ((KIND_PLAIN, DEST_NAT), (KIND_PLAIN, DEST_F1), (KIND_PLAIN, DEST_F2),
           (KIND_NORM_ROPE, DEST_NAT), (KIND_NORM_ROPE, DEST_F1), (KIND_NORM_ROPE, DEST_F2),
           (KIND_ROPE_LOW, DEST_NAT), (KIND_SILU, DEST_NAT), (KIND_SIGMOID, DEST_NAT),
           (KIND_PLAIN, DEST_T), (KIND_NORM_ROPE, DEST_T), (KIND_ROPE, DEST_T))


def _route_code(kind, dest):
    return kind * 4 + dest

_FLOOR_KEY = int(np.array(-5e29, np.float32).view(np.int32)) ^ 0x7FFFFFFF
_INT_MIN = -2 ** 31
_HALF16 = 32768
_PACK16 = 16
_LOG2_MOBA_BLOCK = MOBA_BLOCK.bit_length() - 1
_LOG2_MOBA_SLOTS = MOBA_SLOTS.bit_length() - 1
_LOG2_HEAD_DIM = HEAD_DIM.bit_length() - 1


def _dot(a, b):
    return jnp.dot(a, b, preferred_element_type=F32)


def _dot_nt(a, b):
    return lax.dot_general(a, b, (((1,), (1,)), ((), ())), preferred_element_type=F32)


def _params(*sem):
    return pltpu.CompilerParams(dimension_semantics=sem, vmem_limit_bytes=VMEM_LIMIT_BYTES)


def _inproj_kernel(route_ref, x_ref, ng_ref, w_ref, wt_ref, gain_ref, gain_t_ref,
                   cos_ref, s1_ref, s2_ref, cos_t_ref, s1_t_ref, s2_t_ref, bd_ref,
                   o_ref, f1_ref, f2_ref, tt_ref, h_ref, ht_ref, stage_ref):
    j = pl.program_id(1)
    tm = x_ref.shape[0]
    n_ch = TILE // LANES

    @pl.when(j == 0)
    def _():
        x = x_ref[...]
        ms = jnp.mean(x * x, axis=-1, keepdims=True)
        h = x * lax.rsqrt(ms + NORM_EPS) * ng_ref[...]
        h_ref[...] = h.astype(BF16)
        ht_ref[...] = jnp.transpose(h).astype(BF16)

    def fold(val, ref, dil):
        for c in range(n_ch):
            stage_ref[c] = val[:, c * LANES:(c + 1) * LANES]
        for r in range(dil):
            parts = [stage_ref[c, pl.ds(r, tm // dil, stride=dil), :] for c in range(n_ch)]
            ref[0, r] = jnp.concatenate(parts, axis=1).astype(ref.dtype)

    def rope(v, low_only):
        c = jnp.tile(cos_ref[...], (1, n_ch))
        s1 = jnp.tile(s1_ref[...], (1, n_ch))
        s2 = jnp.tile(s2_ref[...], (1, n_ch))
        if low_only:
            low = lax.broadcasted_iota(I32, v.shape, 1) < HEAD_DIM
            c = jnp.where(low, c, 1.0)
            s1 = jnp.where(low, s1, 0.0)
            s2 = jnp.where(low, s2, 0.0)
        return v * c + pltpu.roll(v, TILE - ROT_HALF, 1) * s1 + pltpu.roll(v, ROT_HALF, 1) * s2

    def rope_t(v):
        c = jnp.tile(cos_t_ref[...], (N_HEADS, 1))
        s1 = jnp.tile(s1_t_ref[...], (N_HEADS, 1))
        s2 = jnp.tile(s2_t_ref[...], (N_HEADS, 1))
        return v * c + pltpu.roll(v, TILE - ROT_HALF, 0) * s1 + pltpu.roll(v, ROT_HALF, 0) * s2

    def epilogue(kind, y):
        if kind == KIND_PLAIN:
            return y
        if kind == KIND_NORM_ROPE:
            y2 = (y * y).astype(BF16)
            ms = jnp.concatenate([_dot(y2[:, c * MXU_DIM:(c + 1) * MXU_DIM], bd_ref[...])
                                  for c in range(TILE // MXU_DIM)], axis=1)
            return rope(y * lax.rsqrt(ms + NORM_EPS) * gain_ref[0], False)
        if kind == KIND_ROPE_LOW:
            return rope(y, True)
        if kind == KIND_SILU:
            return y / (1.0 + jnp.exp(-y))
        assert kind == KIND_SIGMOID
        return 1.0 / (1.0 + jnp.exp(-y))

    def epilogue_t(kind, yt):
        if kind == KIND_PLAIN:
            return yt
        gain = jnp.tile(gain_t_ref[0], (1, tm // LANES))
        if kind == KIND_NORM_ROPE:
            y2 = (yt * yt).astype(BF16)
            ms = jnp.concatenate([_dot(bd_ref[...], y2[c * MXU_DIM:(c + 1) * MXU_DIM, :])
                                  for c in range(TILE // MXU_DIM)], axis=0)
            return rope_t(yt * lax.rsqrt(ms + NORM_EPS) * gain)
        assert kind == KIND_ROPE
        return rope_t(yt * gain)

    route = route_ref[j]
    for kind, dest in _ROUTES:
        @pl.when(route == _route_code(kind, dest))
        def _(kind=kind, dest=dest):
            if dest == DEST_T:
                tt_ref[...] = epilogue_t(kind, _dot(wt_ref[...], ht_ref[...])).astype(tt_ref.dtype)
                return
            val = epilogue(kind, _dot(h_ref[...], w_ref[...]))
            if dest == DEST_NAT:
                o_ref[...] = val.astype(o_ref.dtype)
            elif dest == DEST_F1:
                fold(val, f1_ref, DIL_PATTERNS[1][1])
            else:
                fold(val, f2_ref, DIL_PATTERNS[2][1])


def _inproj(x2, ng, w_nat, w_t, kinds, gains, gains_t, tabs, tabs_t, bd, tm, B, S):
    T = x2.shape[0]
    grid = (T // tm, N_TILES)
    per_b = S // tm
    d1, d2 = DIL_PATTERNS[1][1], DIL_PATTERNS[2][1]
    half = N_FOLD_TILES // 2
    assert S % tm == 0 and tm % (16 * d2) == 0
    n_w = J_T

    def fold_spec(dil, first):
        return pl.BlockSpec(
            (1, dil, tm // dil, TILE),
            lambda i, j, k: (i // per_b, 0, i % per_b, jnp.clip(j - first, 0, half - 1)))

    row_tab = pl.BlockSpec((tm, LANES), lambda i, j, k: (i, 0))
    col_tab = pl.BlockSpec((HEAD_DIM, tm), lambda i, j, k: (0, i))
    t_idx = lambda j: jnp.maximum(j - J_T, 0)
    return pl.pallas_call(
        _inproj_kernel,
        grid_spec=pltpu.PrefetchScalarGridSpec(
            num_scalar_prefetch=1,
            grid=grid,
            in_specs=[
                pl.BlockSpec((tm, D_MODEL), lambda i, j, k: (i, 0)),
                pl.BlockSpec((1, D_MODEL), lambda i, j, k: (0, 0)),
                pl.BlockSpec((D_MODEL, TILE), lambda i, j, k: (0, jnp.minimum(j, n_w - 1))),
                pl.BlockSpec((TILE, D_MODEL), lambda i, j, k: (t_idx(j), 0)),
                pl.BlockSpec((1, 1, TILE), lambda i, j, k: (jnp.minimum(j, n_w - 1), 0, 0)),
                pl.BlockSpec((1, TILE, LANES), lambda i, j, k: (t_idx(j), 0, 0)),
                row_tab, row_tab, row_tab, col_tab, col_tab, col_tab,
                pl.BlockSpec((MXU_DIM, MXU_DIM), lambda i, j, k: (0, 0)),
            ],
            out_specs=[
                pl.BlockSpec((tm, TILE), lambda i, j, k: (i, jnp.clip(j - J_NAT, 0, N_NAT_TILES - 1))),
                fold_spec(d1, 0),
                fold_spec(d2, half),
                pl.BlockSpec((None, TILE, tm), lambda i, j, k: (i // per_b, t_idx(j), i % per_b)),
            ],
            scratch_shapes=[pltpu.VMEM((tm, D_MODEL), BF16), pltpu.VMEM((D_MODEL, tm), BF16),
                            pltpu.VMEM((TILE // LANES, tm, LANES), F32)],
        ),
        out_shape=[jax.ShapeDtypeStruct((T, NP), BF16),
                   jax.ShapeDtypeStruct((B, d1, S // d1, half * TILE), BF16),
                   jax.ShapeDtypeStruct((B, d2, S // d2, half * TILE), BF16),
                   jax.ShapeDtypeStruct((B, N_T_TILES * TILE, S), BF16)],
        compiler_params=_params("arbitrary", "arbitrary"),
        name="inproj",
    )(kinds, x2, ng, w_nat, w_t, gains, gains_t, *tabs, *tabs_t, bd)


def _attn_a_kernel(q_ref, kc_ref, kp_ref, vc_ref, vp_ref, o_ref, lse_ref, kcat, vtcat, ost, lst,
                   s_ref, *, tq, dil):
    m = pl.program_id(1)
    nsub = tq // BAND
    n_ch = TILE // LANES
    nk = 2 * BAND
    key = lax.broadcasted_iota(I32, (nk, BAND), 0)
    qry = lax.broadcasted_iota(I32, (nk, BAND), 1) + BAND
    in_band = jnp.logical_and(key <= qry, key >= qry - BAND)
    lane = lax.broadcasted_iota(I32, (BAND, LANES), 1)
    half_mask = (jnp.where(lane < HEAD_DIM, 1.0, 0.0).astype(BF16),
                 jnp.where(lane < HEAD_DIM, 0.0, 1.0).astype(BF16))
    ones = jnp.ones((_PACK16, nk), BF16)

    def residue(r, carry):
        kcat[0:BAND, :] = kp_ref[r]
        kcat[BAND:, :] = kc_ref[r]
        vtcat[:, 0:BAND] = jnp.transpose(vp_ref[r].astype(F32)).astype(BF16)
        for c in range(nsub):
            chunk = vc_ref[r, c * BAND:(c + 1) * BAND, :].astype(F32)
            vtcat[:, (c + 1) * BAND:(c + 2) * BAND] = jnp.transpose(chunk).astype(BF16)

        def body(u, carry):
            r0 = pl.multiple_of(u * BAND, BAND)
            q = q_ref[r, pl.ds(r0, BAND), :]
            k2 = kcat[pl.ds(r0, nk), :]
            valid = jnp.logical_and(in_band, key >= jnp.where(m * nsub + u > 0, 0, BAND))
            maxes, outs, lses = [], [], []
            for h in range(N_HEADS):
                pair = slice((h // 2) * LANES, (h // 2 + 1) * LANES)
                q_h = q[:, pair] * half_mask[h % 2]
                s = jnp.where(valid, _dot_nt(k2[:, pair], q_h), NEG_INF)
                s_ref[h] = s
                maxes.append(jnp.max(s, axis=0, keepdims=True))
            for h in range(N_HEADS):
                mx = maxes[h]
                p = jnp.exp2(s_ref[h] - mx)
                vt_h = vtcat[h * HEAD_DIM:(h + 1) * HEAD_DIM, pl.ds(r0, nk)]
                pv = _dot(jnp.concatenate([vt_h, ones], axis=0), p.astype(BF16))
                den = pv[HEAD_DIM:HEAD_DIM + 1]
                outs.append(pv[:HEAD_DIM] / den)
                lses.append(mx + jnp.log2(den))
            o = jnp.transpose(jnp.concatenate(outs, axis=0))
            lse8 = jnp.concatenate(lses, axis=0)
            lse = jnp.transpose(jnp.tile(lse8, (LANES // N_HEADS, 1)))
            rows = pl.ds(r + u * (BAND * dil), BAND, stride=dil) if dil > 1 else pl.ds(r0, BAND)
            for c in range(n_ch):
                ost[c, rows, :] = o[:, c * LANES:(c + 1) * LANES]
            lst[rows, :] = lse
            return carry

        return lax.fori_loop(0, nsub, body, carry)

    lax.fori_loop(0, dil, residue, 0)
    o_ref[0] = jnp.concatenate([ost[c] for c in range(n_ch)], axis=1).astype(o_ref.dtype)
    lse_ref[0] = lst[...]


def _attn_a(qkv, g, dil, B, S):
    ts = min(A_STEP_TOKENS, S)
    tq = ts // dil
    assert S % ts == 0 and tq % BAND == 0
    sub = tq // BAND

    def spec_cur(t):
        return pl.BlockSpec((None, dil, tq, TILE), lambda b, m: (b, 0, m, t))

    def spec_prev(t):
        return pl.BlockSpec((None, dil, BAND, TILE), lambda b, m: (b, 0, jnp.maximum(m * sub - 1, 0), t))

    n_ch = TILE // LANES
    return pl.pallas_call(
        functools.partial(_attn_a_kernel, tq=tq, dil=dil),
        grid=(B, S // ts),
        in_specs=[spec_cur(0), spec_cur(1), spec_prev(1), spec_cur(2), spec_prev(2)],
        out_specs=[pl.BlockSpec((1, ts, TILE), lambda b, m: (b, m, 0)),
                   pl.BlockSpec((1, ts, LANES), lambda b, m: (b, m, 0))],
        out_shape=[jax.ShapeDtypeStruct((B, S, TILE), BF16), jax.ShapeDtypeStruct((B, S, LANES), F32)],
        scratch_shapes=[pltpu.VMEM((tq + BAND, TILE), BF16), pltpu.VMEM((TILE, tq + BAND), BF16),
                        pltpu.VMEM((n_ch, ts, LANES), F32), pltpu.VMEM((ts, LANES), F32),
                        pltpu.VMEM((N_HEADS, 2 * BAND, BAND), F32)],
        compiler_params=_params("arbitrary", "arbitrary"),
        name=f"attn_a{g}",
    )(qkv, qkv, qkv, qkv, qkv)


def _pad_q(qt_ref, qpad_ref):
    qpad_ref[...] = jnp.zeros(qpad_ref.shape, qpad_ref.dtype)
    for h in range(N_HEADS):
        r0 = h * LANES + (h % 2) * HEAD_DIM
        qpad_ref[r0:r0 + HEAD_DIM, :] = qt_ref[h * HEAD_DIM:(h + 1) * HEAD_DIM, :]


def _flash_init(m_ref, l_ref, acc_ref):
    m_ref[...] = jnp.full(m_ref.shape, NEG_INF, F32)
    l_ref[...] = jnp.zeros(l_ref.shape, F32)
    acc_ref[...] = jnp.zeros(acc_ref.shape, F32)


def _flash_tile_step(k, vt_of_head, bias_of_head, qpad_ref, s_ref, mn_ref, m_ref, l_ref, acc_ref):
    tk = k.shape[0]
    for h in range(N_HEADS):
        k_pair = k[:, (h // 2) * LANES:(h // 2 + 1) * LANES]
        s = _dot(k_pair, qpad_ref[h * LANES:(h + 1) * LANES, :]) + bias_of_head(h)
        s_ref[h] = s
        mn_ref[h] = jnp.maximum(m_ref[h], jnp.max(s, axis=0, keepdims=True))
    ones = jnp.ones((_PACK16, tk), BF16)
    for h in range(N_HEADS):
        m_prev, m_next = m_ref[h], mn_ref[h]
        p = jnp.exp2(s_ref[h] - jnp.tile(m_next, (tk // SUBLANES, 1)))
        alpha = jnp.exp2(m_prev - m_next)
        pv = _dot(jnp.concatenate([vt_of_head(h), ones], axis=0), p.astype(BF16))
        l_ref[h] = alpha * l_ref[h] + pv[HEAD_DIM:HEAD_DIM + SUBLANES]
        acc_ref[h] = acc_ref[h] * jnp.tile(alpha, (HEAD_DIM // SUBLANES, 1)) + pv[:HEAD_DIM]
        m_ref[h] = m_next


def _flash_finish(o_ref, l_ref, acc_ref):
    outs = [acc_ref[h] / jnp.tile(l_ref[h], (HEAD_DIM // SUBLANES, 1)) for h in range(N_HEADS)]
    o_ref[...] = jnp.transpose(jnp.concatenate(outs, axis=0)).astype(o_ref.dtype)


def _dsa_kernel(qt_ref, k_ref, vt_ref, iqt_ref, ikw_q_ref, ikw_ref, o_ref,
                key_ref, hi_ref, lo_ref, qpad_ref, bias_ref, s_ref, mn_ref, m_ref, l_ref, acc_ref,
                *, tq, tk, topk):
    qi = pl.program_id(1)
    n_kt = qi + 1
    t0 = qi * tq
    n_acc = 4
    rows8 = tk // SUBLANES

    w8 = jnp.transpose(ikw_q_ref[...].astype(F32))[HEAD_DIM:HEAD_DIM + IDX_HEADS, :] * (IDX_HEADS ** -0.5)
    krow = lax.broadcasted_iota(I32, (tk, tq), 0)
    qcol = t0 + lax.broadcasted_iota(I32, (tk, tq), 1)

    def score_body(c, carry):
        c0 = pl.multiple_of(c * tk, tk)
        kx = ikw_ref[pl.ds(c0, tk), :][:, :IDX_DIM]
        sc = jnp.zeros((tk, tq), F32)
        for h in range(IDX_HEADS):
            lg = _dot(kx, iqt_ref[h * IDX_DIM:(h + 1) * IDX_DIM, :])
            sc = sc + w8[h:h + 1, :] * jnp.maximum(lg, 0.0)
        sc = jnp.where(sc == 0.0, 0.0, sc)
        sc = jnp.where(krow + c0 <= qcol, sc, NEG_INF)
        bits = pltpu.bitcast(sc, I32)
        key = jnp.where(bits < 0, bits ^ 0x7FFFFFFF, bits)
        key_ref[pl.ds(c0, tk), :] = key
        hi_ref[pl.ds(c0, tk), :] = jnp.right_shift(key, 16).astype(I16)
        lo_ref[pl.ds(c0, tk), :] = ((key & 0xFFFF) - _HALF16).astype(I16)
        return carry

    lax.fori_loop(0, n_kt, score_body, 0)

    rows16 = tk // _PACK16
    one16 = jnp.ones((_PACK16, tq), I16)
    zero16 = jnp.zeros((_PACK16, tq), I16)

    def count16(ref, pred):
        def cbody(c, accs):
            accs = list(accs)
            c0 = pl.multiple_of(c * tk, tk)
            t = ref[pl.ds(c0, tk), :]
            for g in range(rows16):
                hit = jnp.where(pred(t[g * _PACK16:(g + 1) * _PACK16]), one16, zero16)
                accs[g % n_acc] = accs[g % n_acc] + hit
            return tuple(accs)
        accs = lax.fori_loop(0, n_kt, cbody, tuple(zero16 for _ in range(n_acc)))
        return jnp.sum(functools.reduce(lambda a, b: a + b, accs).astype(I32), axis=0, keepdims=True)

    def as16(v):
        return jnp.broadcast_to(v, (_PACK16, tq)).astype(I16)

    def select16(ref, need, cge0):
        def bit_body(it, carry):
            ans, cge = carry
            cand_u = ans | lax.shift_left(jnp.int32(1), 15 - it)
            cand = as16(cand_u - _HALF16)
            cnt = count16(ref, lambda t: t >= cand)
            ok = cnt >= need
            return jnp.where(ok, cand_u, ans), jnp.where(ok, cnt, cge)
        return lax.fori_loop(0, 16, bit_body, (jnp.zeros((1, tq), I32), cge0))

    n_all = jnp.zeros((1, tq), I32) + n_kt * tk
    p_u, cge_hi = select16(hi_ref, topk, n_all)
    p16 = as16(p_u - _HALF16)
    c_gt = count16(hi_ref, lambda t: t > p16)

    def bucket_body(c, carry):
        c0 = pl.multiple_of(c * tk, tk)
        lo_ref[pl.ds(c0, tk), :] = jnp.where(hi_ref[pl.ds(c0, tk), :] == jnp.tile(p16, (rows16, 1)),
                                             lo_ref[pl.ds(c0, tk), :], jnp.int16(-_HALF16))
        return carry

    lax.fori_loop(0, n_kt, bucket_body, 0)
    l_u, cge_lo = select16(lo_ref, topk - c_gt, cge_hi - c_gt)
    cge = c_gt + cge_lo
    thr = jnp.maximum((p_u - _HALF16) * 65536 + l_u, _FLOOR_KEY)

    def count_rows(fn, n_out):
        def cbody(c, accs):
            accs = [list(a) for a in accs]
            c0 = pl.multiple_of(c * tk, tk)
            kt = key_ref[pl.ds(c0, tk), :]
            for g in range(rows8):
                r0 = c0 + g * SUBLANES
                vals = fn(kt[g * SUBLANES:(g + 1) * SUBLANES], r0)
                for o in range(n_out):
                    accs[o][g % n_acc] = accs[o][g % n_acc] + vals[o]
            return tuple(tuple(a) for a in accs)
        z = jnp.zeros((SUBLANES, tq), I32)
        accs = lax.fori_loop(0, n_kt, cbody, tuple(tuple(z for _ in range(n_acc)) for _ in range(n_out)))
        return [jnp.sum(functools.reduce(lambda a, b: a + b, a), axis=0, keepdims=True) for a in accs]

    tie = jnp.logical_and(cge > topk, thr > _FLOOR_KEY)
    any_tie = jnp.max(jnp.where(tie, 1, 0)) > 0
    thr8 = jnp.broadcast_to(thr, (SUBLANES, tq))

    @pl.when(any_tie)
    def _():
        n_bits = int(np.log2(key_ref.shape[0]))
        sub = lax.broadcasted_iota(I32, (SUBLANES, tq), 0)

        def jbody(it, lo):
            cand = lo + lax.shift_left(jnp.int32(1), n_bits - 1 - it)
            pos = jnp.broadcast_to(cand - 1, (SUBLANES, tq))

            def f(kk, r0):
                eq = jnp.logical_and(kk == thr8, sub + r0 <= pos)
                return jnp.where(kk > thr8, 1, 0), jnp.where(eq, 1, 0)

            gt, eq = count_rows(f, 2)
            return jnp.where(gt + eq >= topk, lo, cand)

        jrow = jnp.broadcast_to(lax.fori_loop(0, n_bits, jbody, jnp.zeros((1, tq), I32)), (SUBLANES, tq))
        tie8 = jnp.broadcast_to(jnp.where(tie, 1, 0), (SUBLANES, tq)) > 0

        def fix(c, carry):
            for g in range(rows8):
                r0 = pl.multiple_of(c * tk + g * SUBLANES, SUBLANES)
                kk = key_ref[pl.ds(r0, SUBLANES), :]
                drop = jnp.logical_and(tie8, jnp.logical_and(kk == thr8, sub + r0 > jrow))
                key_ref[pl.ds(r0, SUBLANES), :] = jnp.where(drop, _INT_MIN, kk)
            return carry

        lax.fori_loop(0, n_kt, fix, 0)

    _pad_q(qt_ref, qpad_ref)
    _flash_init(m_ref, l_ref, acc_ref)

    def att_body(j, carry):
        c0 = pl.multiple_of(j * tk, tk)
        bias_ref[...] = jnp.where(key_ref[pl.ds(c0, tk), :] >= thr, 0.0, NEG_INF)
        _flash_tile_step(k_ref[pl.ds(c0, tk), :],
                         lambda h: vt_ref[h * HEAD_DIM:(h + 1) * HEAD_DIM, pl.ds(c0, tk)],
                         lambda h: bias_ref[...], qpad_ref, s_ref, mn_ref, m_ref, l_ref, acc_ref)
        return carry

    lax.fori_loop(0, n_kt, att_body, 0)
    _flash_finish(o_ref, l_ref, acc_ref)


def _flash_scratch(tk, tq):
    stat = pltpu.VMEM((N_HEADS, SUBLANES, tq), F32)
    return [pltpu.VMEM((N_HEADS, tk, tq), F32), stat, stat, stat, pltpu.VMEM((N_HEADS, HEAD_DIM, tq), F32)]


def _dsa(P3, TT, B, S):
    tq = tk = 256
    topk = min(IDX_TOPK_MAX, S // 4)
    assert S % tq == 0 and topk <= tk
    per = TILE // LANES

    def tt_q(t):
        return pl.BlockSpec((None, TILE, tq), lambda b, i: (b, t, i))

    return pl.pallas_call(
        functools.partial(_dsa_kernel, tq=tq, tk=tk, topk=topk),
        grid=(B, S // tq),
        in_specs=[tt_q(TT_QB),
                  pl.BlockSpec((None, S, TILE), lambda b, i: (b, 0, T_KB)),
                  pl.BlockSpec((None, TILE, S), lambda b, i: (b, TT_VB, 0)),
                  tt_q(TT_IQ),
                  pl.BlockSpec((None, tq, LANES), lambda b, i: (b, i, T_IKW * per)),
                  pl.BlockSpec((None, S, LANES), lambda b, i: (b, 0, T_IKW * per))],
        out_specs=pl.BlockSpec((None, tq, TILE), lambda b, i: (b, i, 0)),
        out_shape=jax.ShapeDtypeStruct((B, S, TILE), BF16),
        scratch_shapes=[pltpu.VMEM((S, tq), I32), pltpu.VMEM((S, tq), I16), pltpu.VMEM((S, tq), I16),
                        pltpu.VMEM((N_HEADS * LANES, tq), BF16),
                        pltpu.VMEM((tk, tq), F32)] + _flash_scratch(tk, tq),
        compiler_params=_params("arbitrary", "arbitrary"),
        name="dsa",
    )(TT, P3, TT, TT, P3, P3)


def _moba_kernel(qt_ref, k_ref, vt_ref, o_ref, kmh_ref, kml_ref, qpad_ref, selb_ref,
                 s_ref, mn_ref, m_ref, l_ref, acc_ref, *, tq, topb):
    qi = pl.program_id(1)
    S = k_ref.shape[0]
    gl = N_HEADS * MOBA_SLOTS

    @pl.when(qi == 0)
    def _():
        blk_row = lax.broadcasted_iota(I32, (MOBA_SLOTS, S), 0)
        blk_col = jnp.right_shift(lax.broadcasted_iota(I32, (MOBA_SLOTS, S), 1), _LOG2_MOBA_BLOCK)
        avg = jnp.where(blk_row == blk_col, 1.0 / MOBA_BLOCK, 0.0).astype(BF16)
        kmean = _dot(avg, k_ref[...])
        km = jnp.tile(kmean, (N_HEADS, 1))
        r_head = jnp.right_shift(lax.broadcasted_iota(I32, (gl, TILE), 0), _LOG2_MOBA_SLOTS)
        c_head = jnp.right_shift(lax.broadcasted_iota(I32, (gl, TILE), 1), _LOG2_HEAD_DIM)
        km = jnp.where(r_head == c_head, km, 0.0)
        hi = km.astype(BF16)
        kmh_ref[...] = hi
        kml_ref[...] = (km - hi.astype(F32)).astype(BF16)

    qt = qt_ref[...]
    gate = _dot(kmh_ref[...], qt) + _dot(kml_ref[...], qt)
    blk = lax.broadcasted_iota(I32, (gl, tq), 0) & (MOBA_SLOTS - 1)
    past = blk < qi
    gate = jnp.where(past, gate, NEG_INF)
    rank = jnp.zeros((gl, tq), I32)
    for d in range(1, MOBA_SLOTS):
        fwd = pltpu.roll(gate, gl - d, 0)
        back = pltpu.roll(gate, MOBA_SLOTS - d, 0)
        wraps = blk + d >= MOBA_SLOTS
        other = jnp.where(wraps, back, fwd)
        rank = rank + jnp.where(wraps, jnp.where(other >= gate, 1, 0), jnp.where(other > gate, 1, 0))
    sel = jnp.logical_and(past, rank < topb)
    selb_ref[...] = jnp.where(sel, 0.0, NEG_INF)

    _pad_q(qt_ref, qpad_ref)
    _flash_init(m_ref, l_ref, acc_ref)

    def block_step(n, bias_of_head):
        c0 = pl.multiple_of(n * MOBA_BLOCK, MOBA_BLOCK)
        _flash_tile_step(k_ref[pl.ds(c0, MOBA_BLOCK), :],
                         lambda h: vt_ref[h * HEAD_DIM:(h + 1) * HEAD_DIM, pl.ds(c0, MOBA_BLOCK)],
                         bias_of_head, qpad_ref, s_ref, mn_ref, m_ref, l_ref, acc_ref)

    def att_body(n, carry):
        block_step(n, lambda h: selb_ref[pl.ds(h * MOBA_SLOTS + n, 1), :])
        return carry

    lax.fori_loop(0, qi, att_body, 0)

    r = lax.broadcasted_iota(I32, (MOBA_BLOCK, tq), 0)
    c = lax.broadcasted_iota(I32, (MOBA_BLOCK, tq), 1)
    causal = jnp.where(r <= c, 0.0, NEG_INF)
    block_step(qi, lambda h: causal)
    _flash_finish(o_ref, l_ref, acc_ref)


def _moba(P3, TT, B, S):
    tq = MOBA_BLOCK
    nblk = S // MOBA_BLOCK
    assert S % MOBA_BLOCK == 0 and nblk <= MOBA_SLOTS
    topb = min(MOBA_TOPK, nblk - 1)
    gl = N_HEADS * MOBA_SLOTS
    return pl.pallas_call(
        functools.partial(_moba_kernel, tq=tq, topb=topb),
        grid=(B, S // tq),
        in_specs=[pl.BlockSpec((None, TILE, tq), lambda b, i: (b, TT_QC, i)),
                  pl.BlockSpec((None, S, TILE), lambda b, i: (b, 0, T_KC)),
                  pl.BlockSpec((None, TILE, S), lambda b, i: (b, TT_VC, 0))],
        out_specs=pl.BlockSpec((None, tq, TILE), lambda b, i: (b, i, 0)),
        out_shape=jax.ShapeDtypeStruct((B, S, TILE), BF16),
        scratch_shapes=[pltpu.VMEM((gl, TILE), BF16), pltpu.VMEM((gl, TILE), BF16),
                        pltpu.VMEM((N_HEADS * LANES, tq), BF16), pltpu.VMEM((gl, tq), F32)]
                       + _flash_scratch(MOBA_BLOCK, tq),
        compiler_params=_params("arbitrary", "arbitrary"),
        name="moba",
    )(TT, P3, TT)


def _post_kernel(x_ref, oa0, oa1, oa2, la0, la1, la2, ob_ref, oc_ref, z0, z1, z2, g0, g1, g2,
                 wbr_ref, wout_ref, expand_ref, out_ref):
    l0, l1, l2 = la0[...], la1[...], la2[...]
    mx = jnp.maximum(jnp.maximum(l0, l1), l2)
    e0, e1, e2 = jnp.exp2(l0 - mx), jnp.exp2(l1 - mx), jnp.exp2(l2 - mx)
    den = e0 + e1 + e2

    def spread(w):
        hi = w.astype(BF16)
        lo = (w - hi.astype(F32)).astype(BF16)
        return _dot(hi, expand_ref[...]) + _dot(lo, expand_ref[...])

    o_a = (spread(e0 / den) * oa0[...].astype(F32) + spread(e1 / den) * oa1[...].astype(F32)
           + spread(e2 / den) * oa2[...].astype(F32))
    branches = (o_a, ob_ref[...].astype(F32), oc_ref[...].astype(F32))
    merged = jnp.zeros(out_ref.shape, F32)
    for n, (o, z, g) in enumerate(zip(branches, (z0, z1, z2), (g0, g1, g2))):
        y = _dot((o * z[...].astype(F32)).astype(BF16), wbr_ref[n])
        merged = merged + g[...].astype(F32) * y
    out_ref[...] = x_ref[...] + _dot(merged.astype(BF16), wout_ref[...])


def _post(x2, oa, la, ob, oc, P2, wbr, wout, tm):
    T = x2.shape[0]
    row = lambda width, t: pl.BlockSpec((tm, width), lambda i: (i, t))
    per_g = D_MODEL // TILE
    assert T_G % per_g == 0
    head_of = np.arange(TILE) // HEAD_DIM
    expand = jnp.asarray((np.arange(LANES)[:, None] == head_of[None, :]).astype(np.float32), BF16)
    in_specs = ([row(D_MODEL, 0)] + [row(TILE, 0)] * 3 + [row(LANES, 0)] * 3 + [row(TILE, 0)] * 2
                + [row(TILE, T_Z + n) for n in range(N_BRANCH)]
                + [row(D_MODEL, T_G // per_g + n) for n in range(N_BRANCH)]
                + [pl.BlockSpec((N_BRANCH, BRANCH_WIDTH, D_MODEL), lambda i: (0, 0, 0)),
                   pl.BlockSpec((D_MODEL, D_MODEL), lambda i: (0, 0)),
                   pl.BlockSpec((LANES, TILE), lambda i: (0, 0))])
    return pl.pallas_call(
        _post_kernel,
        grid=(T // tm,),
        in_specs=in_specs,
        out_specs=row(D_MODEL, 0),
        out_shape=jax.ShapeDtypeStruct((T, D_MODEL), F32),
        compiler_params=_params("arbitrary"),
        name="post",
    )(x2, oa[0], oa[1], oa[2], la[0], la[1], la[2], ob, oc, P2, P2, P2, P2, P2, P2, wbr, wout, expand)


def _rearrange_w_in(w):
    bw = BRANCH_WIDTH
    a_q, a_k, a_v = w[:, 0:3 * bw], w[:, 3 * bw:6 * bw], w[:, 6 * bw:9 * bw]
    off = 9 * bw
    b_q, b_k, b_v = (w[:, off + i * bw:off + (i + 1) * bw] for i in range(3)); off += 3 * bw
    iq = w[:, off:off + IDX_HEADS * IDX_DIM]; off += IDX_HEADS * IDX_DIM
    ik = w[:, off:off + IDX_DIM]; off += IDX_DIM
    iw = w[:, off:off + IDX_HEADS]; off += IDX_HEADS
    c_q, c_k, c_v = (w[:, off + i * bw:off + (i + 1) * bw] for i in range(3)); off += 3 * bw
    z = w[:, off:off + 3 * bw]; off += 3 * bw
    g = w[:, off:off + 3 * D_MODEL]; off += 3 * D_MODEL
    assert off == w.shape[1]
    cols = []
    for grp in (1, 2, 0):
        sl = slice(grp * bw, (grp + 1) * bw)
        cols += [a_q[:, sl], a_k[:, sl], a_v[:, sl]]
    pad = jnp.zeros((w.shape[0], TILE - IDX_DIM - IDX_HEADS), w.dtype)
    cols += [b_k, ik, iw, pad, c_k, g, z]
    w_nat = jnp.concatenate(cols, axis=1).astype(BF16)
    assert w_nat.shape[1] == J_T * TILE
    w_t = jnp.concatenate([b_q, b_v, iq, c_q, c_v], axis=1).T.astype(BF16)
    assert w_t.shape[0] == N_T_TILES * TILE
    return w_nat, w_t


def _tile_tables(qk_g):
    kinds = np.zeros((N_TILES,), np.int32)
    ones = jnp.ones((TILE,), F32)
    gains = [ones] * J_T
    gains_t = [ones] * N_T_TILES
    scale = HEAD_DIM ** -0.5 * LOG2_E
    head = lambda v: jnp.tile(v, N_HEADS)

    def qk(tile, mixer):
        kinds[tile] = kinds[tile + 1] = KIND_NORM_ROPE
        gains[tile] = head(qk_g[mixer, 0]) * scale
        gains[tile + 1] = head(qk_g[mixer, 1])

    qk(0, 0)
    qk(N_FOLD_TILES // 2, 0)
    qk(J_NAT + T_A, 0)
    for tile, mixer in ((T_KB, 1), (T_KC, 2)):
        kinds[J_NAT + tile] = KIND_NORM_ROPE
        gains[J_NAT + tile] = head(qk_g[mixer, 1])
    kinds[J_NAT + T_IKW] = KIND_ROPE_LOW
    kinds[J_NAT + T_Z:J_NAT + T_Z + 3] = KIND_SILU
    kinds[J_NAT + T_G:J_NAT + T_G + 6] = KIND_SIGMOID
    for tile, mixer in ((TT_QB, 1), (TT_QC, 2)):
        kinds[J_T + tile] = KIND_NORM_ROPE
        gains_t[tile] = head(qk_g[mixer, 0]) * scale
    kinds[J_T + TT_IQ] = KIND_ROPE
    gains_t[TT_IQ] = jnp.full((TILE,), IDX_DIM ** -0.5, F32)
    gains_t = jnp.broadcast_to(jnp.stack(gains_t)[:, :, None], (N_T_TILES, TILE, LANES))
    half = N_FOLD_TILES // 2
    dest = np.array([DEST_F1] * half + [DEST_F2] * half + [DEST_NAT] * N_NAT_TILES + [DEST_T] * N_T_TILES)
    assert all((k, d) in _ROUTES for k, d in zip(kinds.tolist(), dest.tolist()))
    routes = _route_code(kinds, dest).astype(np.int32)
    return jnp.asarray(routes), jnp.stack(gains)[:, None, :], gains_t


def _rope_tables(positions):
    inv = ROPE_THETA ** (-jnp.arange(0, ROT_DIM, 2, dtype=F32) / ROT_DIM)
    ang = positions.astype(F32).reshape(-1)[:, None] * inv
    cos, sin = jnp.cos(ang), jnp.sin(ang)
    T = cos.shape[0]
    z8 = jnp.zeros((T, ROT_HALF), F32)
    rest1 = jnp.ones((T, HEAD_DIM - ROT_DIM), F32)
    rest0 = jnp.zeros((T, HEAD_DIM - ROT_DIM), F32)
    c = jnp.concatenate([cos, cos, rest1], axis=1)
    s1 = jnp.concatenate([-sin, z8, rest0], axis=1)
    s2 = jnp.concatenate([z8, sin, rest0], axis=1)
    two = lambda t: jnp.concatenate([t, t], axis=1)
    return (two(c), two(s1), two(s2)), (c.T, s1.T, s2.T)


def _block_diag_mean():
    h = np.arange(MXU_DIM) // HEAD_DIM
    return jnp.asarray((h[:, None] == h[None, :]).astype(np.float32) / HEAD_DIM, BF16)


def _layer(x2, B, S, tabs, tabs_t, bd, norm_g, w_in, qk_g, w_br, w_out, tm_in, tm_post):
    kinds, gains, gains_t = _tile_tables(qk_g)
    w_nat, w_t = _rearrange_w_in(w_in)
    P2, f1, f2, TT = _inproj(x2, norm_g[None, :], w_nat, w_t, kinds, gains, gains_t, tabs, tabs_t, bd,
                             tm_in, B, S)
    P3 = P2.reshape(B, S, NP)
    oa, la = [], []
    for g, qkv in enumerate((P3.reshape(B, 1, S, NP), f1, f2)):
        o, lse = _attn_a(qkv, g, DIL_PATTERNS[g][1], B, S)
        oa.append(o.reshape(B * S, TILE))
        la.append(lse.reshape(B * S, LANES))
    ob = _dsa(P3, TT, B, S).reshape(B * S, TILE)
    oc = _moba(P3, TT, B, S).reshape(B * S, TILE)
    return _post(x2, oa, la, ob, oc, P2, w_br.astype(BF16), w_out.astype(BF16), tm_post)


def _forward(x, positions, norm_g, w_in, qk_g, w_br, w_out, tm_in=1024, tm_post=512):
    B, S, D = x.shape
    tabs, tabs_t = _rope_tables(positions)
    bd = _block_diag_mean()
    x2 = x.reshape(B * S, D)
    for layer in range(norm_g.shape[0]):
        x2 = _layer(x2, B, S, tabs, tabs_t, bd, norm_g[layer], w_in[layer], qk_g[layer],
                    w_br[layer], w_out[layer], tm_in, tm_post)
    return x2.reshape(B, S, D)


def kernel(x, positions, norm_g, w_in, qk_g, w_br, w_out):
    return _forward(x, positions, norm_g, w_in, qk_g, w_br, w_out)
```

```python
import functools

import jax
import jax.numpy as jnp
import numpy as np
from jax import lax
from jax.experimental import pallas as pl
from jax.experimental.pallas import tpu as pltpu

F32 = jnp.float32
BF16 = jnp.bfloat16
I32 = jnp.int32
I16 = jnp.int16

D_MODEL = 1024
HEAD_DIM = 64
ROT_DIM = HEAD_DIM // 4
ROT_HALF = ROT_DIM // 2
ROPE_THETA = 500000.0
NORM_EPS = 1e-6
NEG_INF = -1e30
LOG2_E = 1.4426950408889634
MAX_SOFTMAX_SHIFT = 60.0
SHIFT_SLACK = 1.05
N_HEADS = 8
BRANCH_WIDTH = N_HEADS * HEAD_DIM
N_BRANCH = 3
DIL_PATTERNS = ((128, 1), (512, 4), (2048, 16))
BAND = 128
IDX_HEADS = 8
IDX_DIM = 64
IDX_TOPK_MAX = 256
MOBA_BLOCK = 256
MOBA_TOPK = 3
MOBA_SLOTS = 16

LANES = 128
SUBLANES = 8
MXU_DIM = 256
VMEM_LIMIT_BYTES = 56 * 1024 * 1024

TILE = BRANCH_WIDTH
N_FOLD_TILES = 6
T_A = 0
T_KB = 3
T_IKW = 4
T_KC = 5
T_G = 6
T_Z = 12
N_NAT_TILES = 15
TT_QB, TT_VB, TT_IQ, TT_QC, TT_VC = range(5)
N_T_TILES = 5
J_NAT = N_FOLD_TILES
J_T = N_FOLD_TILES + N_NAT_TILES
N_TILES = J_T + N_T_TILES
NP = N_NAT_TILES * TILE
A_STEP_TOKENS = 2048

KIND_PLAIN, KIND_NORM_ROPE, KIND_ROPE, KIND_ROPE_LOW, KIND_SILU, KIND_SIGMOID = range(6)
DEST_NAT, DEST_F1, DEST_F2, DEST_T = range(4)
_ROUTES = ((KIND_PLAIN, DEST_NAT), (KIND_PLAIN, DEST_F1), (KIND_PLAIN, DEST_F2),
           (KIND_NORM_ROPE, DEST_NAT), (KIND_NORM_ROPE, DEST_F1), (KIND_NORM_ROPE, DEST_F2),
           (KIND_ROPE_LOW, DEST_NAT), (KIND_SILU, DEST_NAT), (KIND_SIGMOID, DEST_NAT),
           (KIND_PLAIN, DEST_T), (KIND_NORM_ROPE, DEST_T), (KIND_ROPE, DEST_T))


def _route_code(kind, dest):
    return kind * 4 + dest

_FLOOR_KEY = int(np.array(-5e29, np.float32).view(np.int32)) ^ 0x7FFFFFFF
_INT_MIN = -2 ** 31
_HALF16 = 32768
_PACK16 = 16
_LOG2_MOBA_BLOCK = MOBA_BLOCK.bit_length() - 1
_LOG2_N_HEADS = N_HEADS.bit_length() - 1
_LOG2_HEAD_DIM = HEAD_DIM.bit_length() - 1


def _dot(a, b):
    return jnp.dot(a, b, preferred_element_type=F32)


def _dot_nt(a, b):
    return lax.dot_general(a, b, (((1,), (1,)), ((), ())), preferred_element_type=F32)


def _params(*sem):
    return pltpu.CompilerParams(dimension_semantics=sem, vmem_limit_bytes=VMEM_LIMIT_BYTES)


def _inproj_kernel(route_ref, x_ref, ng_ref, w_ref, wt_ref, gain_ref, gain_t_ref,
                   cos_ref, s1_ref, s2_ref, cos_t_ref, s1_t_ref, s2_t_ref, bd_ref,
                   o_ref, f1_ref, f2_ref, tt_ref, h_ref, ht_ref, stage_ref):
    j = pl.program_id(1)
    tm = x_ref.shape[0]
    n_ch = TILE // LANES

    @pl.when(j == 0)
    def _():
        x = x_ref[...]
        ms = jnp.mean(x * x, axis=-1, keepdims=True)
        h = x * lax.rsqrt(ms + NORM_EPS) * ng_ref[...]
        h_ref[...] = h.astype(BF16)
        ht_ref[...] = jnp.transpose(h).astype(BF16)

    def fold(val, ref, dil):
        for c in range(n_ch):
            stage_ref[c] = val[:, c * LANES:(c + 1) * LANES]
        for r in range(dil):
            parts = [stage_ref[c, pl.ds(r, tm // dil, stride=dil), :] for c in range(n_ch)]
            ref[0, r] = jnp.concatenate(parts, axis=1).astype(ref.dtype)

    def rope(v, low_only):
        c = jnp.tile(cos_ref[...], (1, n_ch))
        s1 = jnp.tile(s1_ref[...], (1, n_ch))
        s2 = jnp.tile(s2_ref[...], (1, n_ch))
        if low_only:
            low = lax.broadcasted_iota(I32, v.shape, 1) < HEAD_DIM
            c = jnp.where(low, c, 1.0)
            s1 = jnp.where(low, s1, 0.0)
            s2 = jnp.where(low, s2, 0.0)
        return v * c + pltpu.roll(v, TILE - ROT_HALF, 1) * s1 + pltpu.roll(v, ROT_HALF, 1) * s2

    def rope_t(v):
        c = jnp.tile(cos_t_ref[...], (N_HEADS, 1))
        s1 = jnp.tile(s1_t_ref[...], (N_HEADS, 1))
        s2 = jnp.tile(s2_t_ref[...], (N_HEADS, 1))
        return v * c + pltpu.roll(v, TILE - ROT_HALF, 0) * s1 + pltpu.roll(v, ROT_HALF, 0) * s2

    def epilogue(kind, y):
        if kind == KIND_PLAIN:
            return y
        if kind == KIND_NORM_ROPE:
            y2 = (y * y).astype(BF16)
            ms = jnp.concatenate([_dot(y2[:, c * MXU_DIM:(c + 1) * MXU_DIM], bd_ref[...])
                                  for c in range(TILE // MXU_DIM)], axis=1)
            return rope(y * lax.rsqrt(ms + NORM_EPS) * gain_ref[0], False)
        if kind == KIND_ROPE_LOW:
            return rope(y, True)
        if kind == KIND_SILU:
            return y / (1.0 + jnp.exp(-y))
        assert kind == KIND_SIGMOID
        return 1.0 / (1.0 + jnp.exp(-y))

    def epilogue_t(kind, yt):
        if kind == KIND_PLAIN:
            return yt
        gain = jnp.tile(gain_t_ref[0], (1, tm // LANES))
        if kind == KIND_NORM_ROPE:
            y2 = (yt * yt).astype(BF16)
            ms = jnp.concatenate([_dot(bd_ref[...], y2[c * MXU_DIM:(c + 1) * MXU_DIM, :])
                                  for c in range(TILE // MXU_DIM)], axis=0)
            return rope_t(yt * lax.rsqrt(ms + NORM_EPS) * gain)
        assert kind == KIND_ROPE
        return rope_t(yt * gain)

    route = route_ref[j]
    for kind, dest in _ROUTES:
        @pl.when(route == _route_code(kind, dest))
        def _(kind=kind, dest=dest):
            if dest == DEST_T:
                tt_ref[...] = epilogue_t(kind, _dot(wt_ref[...], ht_ref[...])).astype(tt_ref.dtype)
                return
            val = epilogue(kind, _dot(h_ref[...], w_ref[...]))
            if dest == DEST_NAT:
                o_ref[...] = val.astype(o_ref.dtype)
            elif dest == DEST_F1:
                fold(val, f1_ref, DIL_PATTERNS[1][1])
            else:
                fold(val, f2_ref, DIL_PATTERNS[2][1])


def _inproj(x2, ng, w_nat, w_t, kinds, gains, gains_t, tabs, tabs_t, bd, tm, B, S):
    T = x2.shape[0]
    grid = (T // tm, N_TILES)
    per_b = S // tm
    d1, d2 = DIL_PATTERNS[1][1], DIL_PATTERNS[2][1]
    half = N_FOLD_TILES // 2
    assert S % tm == 0 and tm % (16 * d2) == 0
    n_w = J_T

    def fold_spec(dil, first):
        return pl.BlockSpec(
            (1, dil, tm // dil, TILE),
            lambda i, j, k: (i // per_b, 0, i % per_b, jnp.clip(j - first, 0, half - 1)))

    row_tab = pl.BlockSpec((tm, LANES), lambda i, j, k: (i, 0))
    col_tab = pl.BlockSpec((HEAD_DIM, tm), lambda i, j, k: (0, i))
    t_idx = lambda j: jnp.maximum(j - J_T, 0)
    return pl.pallas_call(
        _inproj_kernel,
        grid_spec=pltpu.PrefetchScalarGridSpec(
            num_scalar_prefetch=1,
            grid=grid,
            in_specs=[
                pl.BlockSpec((tm, D_MODEL), lambda i, j, k: (i, 0)),
                pl.BlockSpec((1, D_MODEL), lambda i, j, k: (0, 0)),
                pl.BlockSpec((D_MODEL, TILE), lambda i, j, k: (0, jnp.minimum(j, n_w - 1))),
                pl.BlockSpec((TILE, D_MODEL), lambda i, j, k: (t_idx(j), 0)),
                pl.BlockSpec((1, 1, TILE), lambda i, j, k: (jnp.minimum(j, n_w - 1), 0, 0)),
                pl.BlockSpec((1, TILE, LANES), lambda i, j, k: (t_idx(j), 0, 0)),
                row_tab, row_tab, row_tab, col_tab, col_tab, col_tab,
                pl.BlockSpec((MXU_DIM, MXU_DIM), lambda i, j, k: (0, 0)),
            ],
            out_specs=[
                pl.BlockSpec((tm, TILE), lambda i, j, k: (i, jnp.clip(j - J_NAT, 0, N_NAT_TILES - 1))),
                fold_spec(d1, 0),
                fold_spec(d2, half),
                pl.BlockSpec((None, TILE, tm), lambda i, j, k: (i // per_b, t_idx(j), i % per_b)),
            ],
            scratch_shapes=[pltpu.VMEM((tm, D_MODEL), BF16), pltpu.VMEM((D_MODEL, tm), BF16),
                            pltpu.VMEM((TILE // LANES, tm, LANES), F32)],
        ),
        out_shape=[jax.ShapeDtypeStruct((T, NP), BF16),
                   jax.ShapeDtypeStruct((B, d1, S // d1, half * TILE), BF16),
                   jax.ShapeDtypeStruct((B, d2, S // d2, half * TILE), BF16),
                   jax.ShapeDtypeStruct((B, N_T_TILES * TILE, S), BF16)],
        compiler_params=_params("arbitrary", "arbitrary"),
        name="inproj",
    )(kinds, x2, ng, w_nat, w_t, gains, gains_t, *tabs, *tabs_t, bd)


def _attn_a_kernel(q_ref, kc_ref, kp_ref, vc_ref, vp_ref, o_ref, lse_ref, kcat, vtcat, ost, lst,
                   s_ref, *, tq, dil):
    m = pl.program_id(1)
    nsub = tq // BAND
    n_ch = TILE // LANES
    nk = 2 * BAND
    key = lax.broadcasted_iota(I32, (nk, BAND), 0)
    qry = lax.broadcasted_iota(I32, (nk, BAND), 1) + BAND
    in_band = jnp.logical_and(key <= qry, key >= qry - BAND)
    lane = lax.broadcasted_iota(I32, (BAND, LANES), 1)
    half_mask = (jnp.where(lane < HEAD_DIM, 1.0, 0.0).astype(BF16),
                 jnp.where(lane < HEAD_DIM, 0.0, 1.0).astype(BF16))
    ones = jnp.ones((_PACK16, nk), BF16)

    def residue(r, carry):
        kcat[0:BAND, :] = kp_ref[r]
        kcat[BAND:, :] = kc_ref[r]
        vtcat[:, 0:BAND] = jnp.transpose(vp_ref[r].astype(F32)).astype(BF16)
        for c in range(nsub):
            chunk = vc_ref[r, c * BAND:(c + 1) * BAND, :].astype(F32)
            vtcat[:, (c + 1) * BAND:(c + 2) * BAND] = jnp.transpose(chunk).astype(BF16)

        def body(u, carry):
            r0 = pl.multiple_of(u * BAND, BAND)
            q = q_ref[r, pl.ds(r0, BAND), :]
            k2 = kcat[pl.ds(r0, nk), :]
            valid = jnp.logical_and(in_band, key >= jnp.where(m * nsub + u > 0, 0, BAND))
            maxes, outs, lses = [], [], []
            for h in range(N_HEADS):
                pair = slice((h // 2) * LANES, (h // 2 + 1) * LANES)
                q_h = q[:, pair] * half_mask[h % 2]
                s = jnp.where(valid, _dot_nt(k2[:, pair], q_h), NEG_INF)
                s_ref[h] = s
                maxes.append(jnp.max(s, axis=0, keepdims=True))
            for h in range(N_HEADS):
                mx = maxes[h]
                p = jnp.exp2(s_ref[h] - mx)
                vt_h = vtcat[h * HEAD_DIM:(h + 1) * HEAD_DIM, pl.ds(r0, nk)]
                pv = _dot(jnp.concatenate([vt_h, ones], axis=0), p.astype(BF16))
                den = pv[HEAD_DIM:HEAD_DIM + 1]
                outs.append(pv[:HEAD_DIM] / den)
                lses.append(mx + jnp.log2(den))
            o = jnp.transpose(jnp.concatenate(outs, axis=0))
            lse8 = jnp.concatenate(lses, axis=0)
            lse = jnp.transpose(jnp.tile(lse8, (LANES // N_HEADS, 1)))
            rows = pl.ds(r + u * (BAND * dil), BAND, stride=dil) if dil > 1 else pl.ds(r0, BAND)
            for c in range(n_ch):
                ost[c, rows, :] = o[:, c * LANES:(c + 1) * LANES]
            lst[rows, :] = lse
            return carry

        return lax.fori_loop(0, nsub, body, carry)

    lax.fori_loop(0, dil, residue, 0)
    o_ref[0] = jnp.concatenate([ost[c] for c in range(n_ch)], axis=1).astype(o_ref.dtype)
    lse_ref[0] = lst[...]


def _attn_a(qkv, g, dil, B, S):
    ts = min(A_STEP_TOKENS, S)
    tq = ts // dil
    assert S % ts == 0 and tq % BAND == 0
    sub = tq // BAND

    def spec_cur(t):
        return pl.BlockSpec((None, dil, tq, TILE), lambda b, m: (b, 0, m, t))

    def spec_prev(t):
        return pl.BlockSpec((None, dil, BAND, TILE), lambda b, m: (b, 0, jnp.maximum(m * sub - 1, 0), t))

    n_ch = TILE // LANES
    return pl.pallas_call(
        functools.partial(_attn_a_kernel, tq=tq, dil=dil),
        grid=(B, S // ts),
        in_specs=[spec_cur(0), spec_cur(1), spec_prev(1), spec_cur(2), spec_prev(2)],
        out_specs=[pl.BlockSpec((1, ts, TILE), lambda b, m: (b, m, 0)),
                   pl.BlockSpec((1, ts, LANES), lambda b, m: (b, m, 0))],
        out_shape=[jax.ShapeDtypeStruct((B, S, TILE), BF16), jax.ShapeDtypeStruct((B, S, LANES), F32)],
        scratch_shapes=[pltpu.VMEM((tq + BAND, TILE), BF16), pltpu.VMEM((TILE, tq + BAND), BF16),
                        pltpu.VMEM((n_ch, ts, LANES), F32), pltpu.VMEM((ts, LANES), F32),
                        pltpu.VMEM((N_HEADS, 2 * BAND, BAND), F32)],
        compiler_params=_params("arbitrary", "arbitrary"),
        name=f"attn_a{g}",
    )(qkv, qkv, qkv, qkv, qkv)


def _pad_q(qt_ref, qpad_ref):
    qpad_ref[...] = jnp.zeros(qpad_ref.shape, qpad_ref.dtype)
    for h in range(N_HEADS):
        r0 = h * LANES + (h % 2) * HEAD_DIM
        qpad_ref[r0:r0 + HEAD_DIM, :] = qt_ref[h * HEAD_DIM:(h + 1) * HEAD_DIM, :]


def _flash_init(m_ref, l_ref, acc_ref):
    m_ref[...] = jnp.full(m_ref.shape, NEG_INF, F32)
    l_ref[...] = jnp.zeros(l_ref.shape, F32)
    acc_ref[...] = jnp.zeros(acc_ref.shape, F32)


def _flash_tile_step(score_of_head, vt_of_head, s_ref, mn_ref, m_ref, l_ref, acc_ref):
    tk = s_ref.shape[1]
    for h in range(N_HEADS):
        s = score_of_head(h)
        s_ref[h] = s
        mn_ref[h] = jnp.maximum(m_ref[h], jnp.max(s, axis=0, keepdims=True))
    ones = jnp.ones((_PACK16, tk), BF16)
    for h in range(N_HEADS):
        m_prev, m_next = m_ref[h], mn_ref[h]
        p = jnp.exp2(s_ref[h] - jnp.tile(m_next, (tk // SUBLANES, 1)))
        alpha = jnp.exp2(m_prev - m_next)
        pv = _dot(jnp.concatenate([vt_of_head(h), ones], axis=0), p.astype(BF16))
        l_ref[h] = alpha * l_ref[h] + pv[HEAD_DIM:HEAD_DIM + SUBLANES]
        acc_ref[h] = acc_ref[h] * jnp.tile(alpha, (HEAD_DIM // SUBLANES, 1)) + pv[:HEAD_DIM]
        m_ref[h] = m_next


def _flash_tile_step_shifted(score_of_head, vt_of_head, p_ref, l_ref, acc_ref):
    tk = p_ref.shape[1]
    for h in range(N_HEADS):
        p_ref[h] = jnp.exp2(score_of_head(h)).astype(BF16)
    ones = jnp.ones((_PACK16, tk), BF16)
    for h in range(N_HEADS):
        pv = _dot(jnp.concatenate([vt_of_head(h), ones], axis=0), p_ref[h])
        l_ref[h] = l_ref[h] + pv[HEAD_DIM:HEAD_DIM + SUBLANES]
        acc_ref[h] = acc_ref[h] + pv[:HEAD_DIM]


def _flash_finish(o_ref, l_ref, acc_ref):
    outs = [acc_ref[h] / jnp.tile(l_ref[h], (HEAD_DIM // SUBLANES, 1)) for h in range(N_HEADS)]
    o_ref[...] = jnp.transpose(jnp.concatenate(outs, axis=0)).astype(o_ref.dtype)


def _dsa_kernel(shift_ref, qt_ref, k_ref, vt_ref, iqt_ref, ikw_q_ref, ikw_ref, o_ref,
                key_ref, hi_ref, lo_ref, qpad_ref, bias_ref, p_ref, s_ref, mn_ref, m_ref, l_ref, acc_ref,
                *, tq, tk, topk):
    qi = pl.program_id(1)
    n_kt = qi + 1
    t0 = qi * tq
    n_acc = 4
    rows8 = tk // SUBLANES

    w8 = jnp.transpose(ikw_q_ref[...].astype(F32))[HEAD_DIM:HEAD_DIM + IDX_HEADS, :] * (IDX_HEADS ** -0.5)
    krow = lax.broadcasted_iota(I32, (tk, tq), 0)
    qcol = t0 + lax.broadcasted_iota(I32, (tk, tq), 1)

    def score_body(c, carry):
        c0 = pl.multiple_of(c * tk, tk)
        kx = ikw_ref[pl.ds(c0, tk), :][:, :IDX_DIM]
        sc = jnp.zeros((tk, tq), F32)
        for h in range(IDX_HEADS):
            lg = _dot(kx, iqt_ref[h * IDX_DIM:(h + 1) * IDX_DIM, :])
            sc = sc + w8[h:h + 1, :] * jnp.maximum(lg, 0.0)
        sc = jnp.where(sc == 0.0, 0.0, sc)
        sc = jnp.where(krow + c0 <= qcol, sc, NEG_INF)
        bits = pltpu.bitcast(sc, I32)
        key = jnp.where(bits < 0, bits ^ 0x7FFFFFFF, bits)
        key_ref[pl.ds(c0, tk), :] = key
        hi_ref[pl.ds(c0, tk), :] = jnp.right_shift(key, 16).astype(I16)
        lo_ref[pl.ds(c0, tk), :] = ((key & 0xFFFF) - _HALF16).astype(I16)
        return carry

    lax.fori_loop(0, n_kt, score_body, 0)

    rows16 = tk // _PACK16
    one16 = jnp.ones((_PACK16, tq), I16)
    zero16 = jnp.zeros((_PACK16, tq), I16)

    def count16(ref, pred):
        def cbody(c, accs):
            accs = list(accs)
            c0 = pl.multiple_of(c * tk, tk)
            t = ref[pl.ds(c0, tk), :]
            for g in range(rows16):
                hit = jnp.where(pred(t[g * _PACK16:(g + 1) * _PACK16]), one16, zero16)
                accs[g % n_acc] = accs[g % n_acc] + hit
            return tuple(accs)
        accs = lax.fori_loop(0, n_kt, cbody, tuple(zero16 for _ in range(n_acc)))
        return jnp.sum(functools.reduce(lambda a, b: a + b, accs).astype(I32), axis=0, keepdims=True)

    def as16(v):
        return jnp.broadcast_to(v, (_PACK16, tq)).astype(I16)

    def select16(ref, need, cge0):
        def bit_body(it, carry):
            ans, cge = carry
            cand_u = ans | lax.shift_left(jnp.int32(1), 15 - it)
            cand = as16(cand_u - _HALF16)
            cnt = count16(ref, lambda t: t >= cand)
            ok = cnt >= need
            return jnp.where(ok, cand_u, ans), jnp.where(ok, cnt, cge)
        return lax.fori_loop(0, 16, bit_body, (jnp.zeros((1, tq), I32), cge0))

    n_all = jnp.zeros((1, tq), I32) + n_kt * tk
    p_u, cge_hi = select16(hi_ref, topk, n_all)
    p16 = as16(p_u - _HALF16)
    c_gt = count16(hi_ref, lambda t: t > p16)

    def bucket_body(c, carry):
        c0 = pl.multiple_of(c * tk, tk)
        lo_ref[pl.ds(c0, tk), :] = jnp.where(hi_ref[pl.ds(c0, tk), :] == jnp.tile(p16, (rows16, 1)),
                                             lo_ref[pl.ds(c0, tk), :], jnp.int16(-_HALF16))
        return carry

    lax.fori_loop(0, n_kt, bucket_body, 0)
    l_u, cge_lo = select16(lo_ref, topk - c_gt, cge_hi - c_gt)
    cge = c_gt + cge_lo
    thr = jnp.maximum((p_u - _HALF16) * 65536 + l_u, _FLOOR_KEY)

    def count_rows(fn, n_out):
        def cbody(c, accs):
            accs = [list(a) for a in accs]
            c0 = pl.multiple_of(c * tk, tk)
            kt = key_ref[pl.ds(c0, tk), :]
            for g in range(rows8):
                r0 = c0 + g * SUBLANES
                vals = fn(kt[g * SUBLANES:(g + 1) * SUBLANES], r0)
                for o in range(n_out):
                    accs[o][g % n_acc] = accs[o][g % n_acc] + vals[o]
            return tuple(tuple(a) for a in accs)
        z = jnp.zeros((SUBLANES, tq), I32)
        accs = lax.fori_loop(0, n_kt, cbody, tuple(tuple(z for _ in range(n_acc)) for _ in range(n_out)))
        return [jnp.sum(functools.reduce(lambda a, b: a + b, a), axis=0, keepdims=True) for a in accs]

    tie = jnp.logical_and(cge > topk, thr > _FLOOR_KEY)
    any_tie = jnp.max(jnp.where(tie, 1, 0)) > 0
    thr8 = jnp.broadcast_to(thr, (SUBLANES, tq))

    @pl.when(any_tie)
    def _():
        n_bits = int(np.log2(key_ref.shape[0]))
        sub = lax.broadcasted_iota(I32, (SUBLANES, tq), 0)

        def jbody(it, lo):
            cand = lo + lax.shift_left(jnp.int32(1), n_bits - 1 - it)
            pos = jnp.broadcast_to(cand - 1, (SUBLANES, tq))

            def f(kk, r0):
                eq = jnp.logical_and(kk == thr8, sub + r0 <= pos)
                return jnp.where(kk > thr8, 1, 0), jnp.where(eq, 1, 0)

            gt, eq = count_rows(f, 2)
            return jnp.where(gt + eq >= topk, lo, cand)

        jrow = jnp.broadcast_to(lax.fori_loop(0, n_bits, jbody, jnp.zeros((1, tq), I32)), (SUBLANES, tq))
        tie8 = jnp.broadcast_to(jnp.where(tie, 1, 0), (SUBLANES, tq)) > 0

        def fix(c, carry):
            for g in range(rows8):
                r0 = pl.multiple_of(c * tk + g * SUBLANES, SUBLANES)
                kk = key_ref[pl.ds(r0, SUBLANES), :]
                drop = jnp.logical_and(tie8, jnp.logical_and(kk == thr8, sub + r0 > jrow))
                key_ref[pl.ds(r0, SUBLANES), :] = jnp.where(drop, _INT_MIN, kk)
            return carry

        lax.fori_loop(0, n_kt, fix, 0)

    _pad_q(qt_ref, qpad_ref)
    _flash_init(m_ref, l_ref, acc_ref)
    shift = shift_ref[0]

    def attend(shifted):
        def att_body(j, carry):
            c0 = pl.multiple_of(j * tk, tk)
            sel = key_ref[pl.ds(c0, tk), :] >= thr
            bias_ref[...] = jnp.where(sel, -shift if shifted else 0.0, NEG_INF)
            k = k_ref[pl.ds(c0, tk), :]

            def score(h):
                k_pair = k[:, (h // 2) * LANES:(h // 2 + 1) * LANES]
                return _dot(k_pair, qpad_ref[h * LANES:(h + 1) * LANES, :]) + bias_ref[...]

            vt_of_head = lambda h: vt_ref[h * HEAD_DIM:(h + 1) * HEAD_DIM, pl.ds(c0, tk)]
            if shifted:
                _flash_tile_step_shifted(score, vt_of_head, p_ref, l_ref, acc_ref)
            else:
                _flash_tile_step(score, vt_of_head, s_ref, mn_ref, m_ref, l_ref, acc_ref)
            return carry

        lax.fori_loop(0, n_kt, att_body, 0)

    @pl.when(shift <= MAX_SOFTMAX_SHIFT)
    def _():
        attend(True)

    @pl.when(shift > MAX_SOFTMAX_SHIFT)
    def _():
        attend(False)

    _flash_finish(o_ref, l_ref, acc_ref)


def _flash_scratch(tk, tq):
    stat = pltpu.VMEM((N_HEADS, SUBLANES, tq), F32)
    return [pltpu.VMEM((N_HEADS, tk, tq), BF16), pltpu.VMEM((N_HEADS, tk, tq), F32), stat, stat, stat,
            pltpu.VMEM((N_HEADS, HEAD_DIM, tq), F32)]


def _dsa(P3, TT, shift, B, S):
    tq = tk = 256
    topk = min(IDX_TOPK_MAX, S // 4)
    assert S % tq == 0 and topk <= tk
    per = TILE // LANES

    def tt_q(t):
        return pl.BlockSpec((None, TILE, tq), lambda b, i: (b, t, i))

    return pl.pallas_call(
        functools.partial(_dsa_kernel, tq=tq, tk=tk, topk=topk),
        grid=(B, S // tq),
        in_specs=[pl.BlockSpec(memory_space=pltpu.SMEM),
                  tt_q(TT_QB),
                  pl.BlockSpec((None, S, TILE), lambda b, i: (b, 0, T_KB)),
                  pl.BlockSpec((None, TILE, S), lambda b, i: (b, TT_VB, 0)),
                  tt_q(TT_IQ),
                  pl.BlockSpec((None, tq, LANES), lambda b, i: (b, i, T_IKW * per)),
                  pl.BlockSpec((None, S, LANES), lambda b, i: (b, 0, T_IKW * per))],
        out_specs=pl.BlockSpec((None, tq, TILE), lambda b, i: (b, i, 0)),
        out_shape=jax.ShapeDtypeStruct((B, S, TILE), BF16),
        scratch_shapes=[pltpu.VMEM((S, tq), I32), pltpu.VMEM((S, tq), I16), pltpu.VMEM((S, tq), I16),
                        pltpu.VMEM((N_HEADS * LANES, tq), BF16),
                        pltpu.VMEM((tk, tq), F32)] + _flash_scratch(tk, tq),
        compiler_params=_params("arbitrary", "arbitrary"),
        name="dsa",
    )(shift, TT, P3, TT, TT, P3, P3)


def _moba_kernel(shift_ref, qt_ref, k_ref, vt_ref, o_ref, kmh_ref, kml_ref, qaug_ref,
                 p_ref, s_ref, mn_ref, m_ref, l_ref, acc_ref, *, tq, topb):
    qi = pl.program_id(1)
    S = k_ref.shape[0]
    gl = N_HEADS * MOBA_SLOTS

    @pl.when(qi == 0)
    def _():
        blk_row = jnp.right_shift(lax.broadcasted_iota(I32, (gl, S), 0), _LOG2_N_HEADS)
        blk_col = jnp.right_shift(lax.broadcasted_iota(I32, (gl, S), 1), _LOG2_MOBA_BLOCK)
        avg = jnp.where(blk_row == blk_col, 1.0 / MOBA_BLOCK, 0.0).astype(BF16)
        km = _dot(avg, k_ref[...])
        r_head = lax.broadcasted_iota(I32, (gl, TILE), 0) & (N_HEADS - 1)
        c_head = jnp.right_shift(lax.broadcasted_iota(I32, (gl, TILE), 1), _LOG2_HEAD_DIM)
        km = jnp.where(r_head == c_head, km, 0.0)
        hi = km.astype(BF16)
        kmh_ref[...] = hi
        kml_ref[...] = (km - hi.astype(F32)).astype(BF16)

    shifted = shift_ref[0] <= MAX_SOFTMAX_SHIFT
    shift = jnp.where(shifted, shift_ref[0], 0.0)
    qt = qt_ref[...]
    gate = _dot(kmh_ref[...], qt) + _dot(kml_ref[...], qt)
    row = lax.broadcasted_iota(I32, (gl, tq), 0)
    gate = jnp.where(jnp.right_shift(row, _LOG2_N_HEADS) < qi, gate, NEG_INF)
    g = [gate[n * N_HEADS:(n + 1) * N_HEADS] for n in range(MOBA_SLOTS)]
    biases = []
    for n in range(MOBA_SLOTS):
        beaten = jnp.where(n < qi, 0, topb) + jnp.zeros((N_HEADS, tq), I32)
        for n2 in range(MOBA_SLOTS):
            if n2 != n:
                beaten = beaten + jnp.where((g[n2] >= g[n]) if n2 < n else (g[n2] > g[n]), 1, 0)
        biases.append(jnp.where(beaten < topb, -shift, NEG_INF))
    selb = jnp.concatenate(biases, axis=0)

    qaug_ref[...] = jnp.zeros(qaug_ref.shape, qaug_ref.dtype)
    for h in range(N_HEADS):
        r0 = (h % 2) * HEAD_DIM
        qaug_ref[h, r0:r0 + HEAD_DIM, :] = qt_ref[h * HEAD_DIM:(h + 1) * HEAD_DIM, :]
        qaug_ref[h, LANES:, :] = jnp.where((row & (N_HEADS - 1)) == h, selb, 0.0).astype(BF16)
    _flash_init(m_ref, l_ref, acc_ref)
    lane_blk = jnp.right_shift(lax.broadcasted_iota(I32, (MOBA_BLOCK, LANES), 1), _LOG2_N_HEADS)

    def vt_of(c0):
        return lambda h: vt_ref[h * HEAD_DIM:(h + 1) * HEAD_DIM, pl.ds(c0, MOBA_BLOCK)]

    def attend(use_shift):
        def step(score, c0):
            if use_shift:
                _flash_tile_step_shifted(score, vt_of(c0), p_ref, l_ref, acc_ref)
            else:
                _flash_tile_step(score, vt_of(c0), s_ref, mn_ref, m_ref, l_ref, acc_ref)

        def att_body(n, carry):
            c0 = pl.multiple_of(n * MOBA_BLOCK, MOBA_BLOCK)
            k = k_ref[pl.ds(c0, MOBA_BLOCK), :]
            onehot = jnp.where(lane_blk == n, 1.0, 0.0).astype(BF16)

            def score(h):
                k_aug = jnp.concatenate([k[:, (h // 2) * LANES:(h // 2 + 1) * LANES], onehot], axis=1)
                return _dot(k_aug, qaug_ref[h])

            step(score, c0)
            return carry

        lax.fori_loop(0, qi, att_body, 0)

        c0 = pl.multiple_of(qi * MOBA_BLOCK, MOBA_BLOCK)
        k = k_ref[pl.ds(c0, MOBA_BLOCK), :]
        causal = jnp.where(lax.broadcasted_iota(I32, (MOBA_BLOCK, tq), 0)
                           <= lax.broadcasted_iota(I32, (MOBA_BLOCK, tq), 1), -shift, NEG_INF)
        step(lambda h: _dot(k[:, (h // 2) * LANES:(h // 2 + 1) * LANES], qaug_ref[h, :LANES, :]) + causal, c0)

    @pl.when(shifted)
    def _():
        attend(True)

    @pl.when(jnp.logical_not(shifted))
    def _():
        attend(False)

    _flash_finish(o_ref, l_ref, acc_ref)


def _moba(P3, TT, shift, B, S):
    tq = MOBA_BLOCK
    nblk = S // MOBA_BLOCK
    assert S % MOBA_BLOCK == 0 and nblk <= MOBA_SLOTS
    topb = min(MOBA_TOPK, nblk - 1)
    gl = N_HEADS * MOBA_SLOTS
    return pl.pallas_call(
        functools.partial(_moba_kernel, tq=tq, topb=topb),
        grid=(B, S // tq),
        in_specs=[pl.BlockSpec(memory_space=pltpu.SMEM),
                  pl.BlockSpec((None, TILE, tq), lambda b, i: (b, TT_QC, i)),
                  pl.BlockSpec((None, S, TILE), lambda b, i: (b, 0, T_KC)),
                  pl.BlockSpec((None, TILE, S), lambda b, i: (b, TT_VC, 0))],
        out_specs=pl.BlockSpec((None, tq, TILE), lambda b, i: (b, i, 0)),
        out_shape=jax.ShapeDtypeStruct((B, S, TILE), BF16),
        scratch_shapes=[pltpu.VMEM((gl, TILE), BF16), pltpu.VMEM((gl, TILE), BF16),
                        pltpu.VMEM((N_HEADS, 2 * LANES, tq), BF16)]
                       + _flash_scratch(MOBA_BLOCK, tq),
        compiler_params=_params("arbitrary", "arbitrary"),
        name="moba",
    )(shift, TT, P3, TT)


def _post_kernel(x_ref, oa0, oa1, oa2, la0, la1, la2, ob_ref, oc_ref, z0, z1, z2, g0, g1, g2,
                 wbr_ref, wout_ref, expand_ref, out_ref):
    l0, l1, l2 = la0[...], la1[...], la2[...]
    mx = jnp.maximum(jnp.maximum(l0, l1), l2)
    e0, e1, e2 = jnp.exp2(l0 - mx), jnp.exp2(l1 - mx), jnp.exp2(l2 - mx)
    den = e0 + e1 + e2

    def spread(w):
        hi = w.astype(BF16)
        lo = (w - hi.astype(F32)).astype(BF16)
        return _dot(hi, expand_ref[...]) + _dot(lo, expand_ref[...])

    o_a = (spread(e0 / den) * oa0[...].astype(F32) + spread(e1 / den) * oa1[...].astype(F32)
           + spread(e2 / den) * oa2[...].astype(F32))
    branches = (o_a, ob_ref[...].astype(F32), oc_ref[...].astype(F32))
    merged = jnp.zeros(out_ref.shape, F32)
    for n, (o, z, g) in enumerate(zip(branches, (z0, z1, z2), (g0, g1, g2))):
        y = _dot((o * z[...].astype(F32)).astype(BF16), wbr_ref[n])
        merged = merged + g[...].astype(F32) * y
    out_ref[...] = x_ref[...] + _dot(merged.astype(BF16), wout_ref[...])


def _post(x2, oa, la, ob, oc, P2, wbr, wout, tm):
    T = x2.shape[0]
    row = lambda width, t: pl.BlockSpec((tm, width), lambda i: (i, t))
    per_g = D_MODEL // TILE
    assert T_G % per_g == 0
    head_of = np.arange(TILE) // HEAD_DIM
    expand = jnp.asarray((np.arange(LANES)[:, None] == head_of[None, :]).astype(np.float32), BF16)
    in_specs = ([row(D_MODEL, 0)] + [row(TILE, 0)] * 3 + [row(LANES, 0)] * 3 + [row(TILE, 0)] * 2
                + [row(TILE, T_Z + n) for n in range(N_BRANCH)]
                + [row(D_MODEL, T_G // per_g + n) for n in range(N_BRANCH)]
                + [pl.BlockSpec((N_BRANCH, BRANCH_WIDTH, D_MODEL), lambda i: (0, 0, 0)),
                   pl.BlockSpec((D_MODEL, D_MODEL), lambda i: (0, 0)),
                   pl.BlockSpec((LANES, TILE), lambda i: (0, 0))])
    return pl.pallas_call(
        _post_kernel,
        grid=(T // tm,),
        in_specs=in_specs,
        out_specs=row(D_MODEL, 0),
        out_shape=jax.ShapeDtypeStruct((T, D_MODEL), F32),
        compiler_params=_params("arbitrary"),
        name="post",
    )(x2, oa[0], oa[1], oa[2], la[0], la[1], la[2], ob, oc, P2, P2, P2, P2, P2, P2, wbr, wout, expand)


def _rearrange_w_in(w):
    bw = BRANCH_WIDTH
    a_q, a_k, a_v = w[:, 0:3 * bw], w[:, 3 * bw:6 * bw], w[:, 6 * bw:9 * bw]
    off = 9 * bw
    b_q, b_k, b_v = (w[:, off + i * bw:off + (i + 1) * bw] for i in range(3)); off += 3 * bw
    iq = w[:, off:off + IDX_HEADS * IDX_DIM]; off += IDX_HEADS * IDX_DIM
    ik = w[:, off:off + IDX_DIM]; off += IDX_DIM
    iw = w[:, off:off + IDX_HEADS]; off += IDX_HEADS
    c_q, c_k, c_v = (w[:, off + i * bw:off + (i + 1) * bw] for i in range(3)); off += 3 * bw
    z = w[:, off:off + 3 * bw]; off += 3 * bw
    g = w[:, off:off + 3 * D_MODEL]; off += 3 * D_MODEL
    assert off == w.shape[1]
    cols = []
    for grp in (1, 2, 0):
        sl = slice(grp * bw, (grp + 1) * bw)
        cols += [a_q[:, sl], a_k[:, sl], a_v[:, sl]]
    pad = jnp.zeros((w.shape[0], TILE - IDX_DIM - IDX_HEADS), w.dtype)
    cols += [b_k, ik, iw, pad, c_k, g, z]
    w_nat = jnp.concatenate(cols, axis=1).astype(BF16)
    assert w_nat.shape[1] == J_T * TILE
    w_t = jnp.concatenate([b_q, b_v, iq, c_q, c_v], axis=1).T.astype(BF16)
    assert w_t.shape[0] == N_T_TILES * TILE
    return w_nat, w_t


def _tile_tables(qk_g):
    kinds = np.zeros((N_TILES,), np.int32)
    ones = jnp.ones((TILE,), F32)
    gains = [ones] * J_T
    gains_t = [ones] * N_T_TILES
    scale = HEAD_DIM ** -0.5 * LOG2_E
    head = lambda v: jnp.tile(v, N_HEADS)

    def qk(tile, mixer):
        kinds[tile] = kinds[tile + 1] = KIND_NORM_ROPE
        gains[tile] = head(qk_g[mixer, 0]) * scale
        gains[tile + 1] = head(qk_g[mixer, 1])

    qk(0, 0)
    qk(N_FOLD_TILES // 2, 0)
    qk(J_NAT + T_A, 0)
    for tile, mixer in ((T_KB, 1), (T_KC, 2)):
        kinds[J_NAT + tile] = KIND_NORM_ROPE
        gains[J_NAT + tile] = head(qk_g[mixer, 1])
    kinds[J_NAT + T_IKW] = KIND_ROPE_LOW
    kinds[J_NAT + T_Z:J_NAT + T_Z + 3] = KIND_SILU
    kinds[J_NAT + T_G:J_NAT + T_G + 6] = KIND_SIGMOID
    for tile, mixer in ((TT_QB, 1), (TT_QC, 2)):
        kinds[J_T + tile] = KIND_NORM_ROPE
        gains_t[tile] = head(qk_g[mixer, 0]) * scale
    kinds[J_T + TT_IQ] = KIND_ROPE
    gains_t[TT_IQ] = jnp.full((TILE,), IDX_DIM ** -0.5, F32)
    gains_t = jnp.broadcast_to(jnp.stack(gains_t)[:, :, None], (N_T_TILES, TILE, LANES))
    half = N_FOLD_TILES // 2
    dest = np.array([DEST_F1] * half + [DEST_F2] * half + [DEST_NAT] * N_NAT_TILES + [DEST_T] * N_T_TILES)
    assert all((k, d) in _ROUTES for k, d in zip(kinds.tolist(), dest.tolist()))
    routes = _route_code(kinds, dest).astype(np.int32)
    return jnp.asarray(routes), jnp.stack(gains)[:, None, :], gains_t


def _softmax_shift(qk_gain):
    bound = HEAD_DIM * jnp.max(jnp.abs(qk_gain[0])) * jnp.max(jnp.abs(qk_gain[1]))
    return (SHIFT_SLACK * HEAD_DIM ** -0.5 * LOG2_E * bound).reshape(1).astype(BF16).astype(F32)


def _rope_tables(positions):
    inv = ROPE_THETA ** (-jnp.arange(0, ROT_DIM, 2, dtype=F32) / ROT_DIM)
    ang = positions.astype(F32).reshape(-1)[:, None] * inv
    cos, sin = jnp.cos(ang), jnp.sin(ang)
    T = cos.shape[0]
    z8 = jnp.zeros((T, ROT_HALF), F32)
    rest1 = jnp.ones((T, HEAD_DIM - ROT_DIM), F32)
    rest0 = jnp.zeros((T, HEAD_DIM - ROT_DIM), F32)
    c = jnp.concatenate([cos, cos, rest1], axis=1)
    s1 = jnp.concatenate([-sin, z8, rest0], axis=1)
    s2 = jnp.concatenate([z8, sin, rest0], axis=1)
    two = lambda t: jnp.concatenate([t, t], axis=1)
    return (two(c), two(s1), two(s2)), (c.T, s1.T, s2.T)


def _block_diag_mean():
    h = np.arange(MXU_DIM) // HEAD_DIM
    return jnp.asarray((h[:, None] == h[None, :]).astype(np.float32) / HEAD_DIM, BF16)


def _layer(x2, B, S, tabs, tabs_t, bd, norm_g, w_in, qk_g, w_br, w_out, tm_in, tm_post):
    kinds, gains, gains_t = _tile_tables(qk_g)
    w_nat, w_t = _rearrange_w_in(w_in)
    P2, f1, f2, TT = _inproj(x2, norm_g[None, :], w_nat, w_t, kinds, gains, gains_t, tabs, tabs_t, bd,
                             tm_in, B, S)
    P3 = P2.reshape(B, S, NP)
    oa, la = [], []
    for g, qkv in enumerate((P3.reshape(B, 1, S, NP), f1, f2)):
        o, lse = _attn_a(qkv, g, DIL_PATTERNS[g][1], B, S)
        oa.append(o.reshape(B * S, TILE))
        la.append(lse.reshape(B * S, LANES))
    ob = _dsa(P3, TT, _softmax_shift(qk_g[1]), B, S).reshape(B * S, TILE)
    oc = _moba(P3, TT, _softmax_shift(qk_g[2]), B, S).reshape(B * S, TILE)
    return _post(x2, oa, la, ob, oc, P2, w_br.astype(BF16), w_out.astype(BF16), tm_post)


def _forward(x, positions, norm_g, w_in, qk_g, w_br, w_out, tm_in=1024, tm_post=512):
    B, S, D = x.shape
    tabs, tabs_t = _rope_tables(positions)
    bd = _block_diag_mean()
    x2 = x.reshape(B * S, D)
    for layer in range(norm_g.shape[0]):
        x2 = _layer(x2, B, S, tabs, tabs_t, bd, norm_g[layer], w_in[layer], qk_g[layer],
                    w_br[layer], w_out[layer], tm_in, tm_post)
    return x2.reshape(B, S, D)


def kernel(x, positions, norm_g, w_in, qk_g, w_br, w_out):
    return _forward(x, positions, norm_g, w_in, qk_g, w_br, w_out)
```

```python
import functools

import jax
import jax.numpy as jnp
import numpy as np
from jax import lax
from jax.experimental import pallas as pl
from jax.experimental.pallas import tpu as pltpu

F32 = jnp.float32
BF16 = jnp.bfloat16
I32 = jnp.int32
I16 = jnp.int16

D_MODEL = 1024
HEAD_DIM = 64
ROT_DIM = HEAD_DIM // 4
ROT_HALF = ROT_DIM // 2
ROPE_THETA = 500000.0
NORM_EPS = 1e-6
NEG_INF = -1e30
LOG2_E = 1.4426950408889634
MAX_SOFTMAX_SHIFT = 60.0
SHIFT_SLACK = 1.05
N_HEADS = 8
BRANCH_WIDTH = N_HEADS * HEAD_DIM
N_BRANCH = 3
DIL_PATTERNS = ((128, 1), (512, 4), (2048, 16))
BAND = 128
IDX_HEADS = 8
IDX_DIM = 64
IDX_TOPK_MAX = 256
MOBA_BLOCK = 256
MOBA_TOPK = 3
MOBA_SLOTS = 16

LANES = 128
SUBLANES = 8
MXU_DIM = 256
VMEM_LIMIT_BYTES = 56 * 1024 * 1024

TILE = BRANCH_WIDTH
N_FOLD_TILES = 6
T_A = 0
T_KB = 3
T_IKW = 4
T_KC = 5
T_G = 6
T_Z = 12
N_NAT_TILES = 15
TT_QB, TT_VB, TT_IQ, TT_QC, TT_VC = range(5)
N_T_TILES = 5
J_NAT = N_FOLD_TILES
J_T = N_FOLD_TILES + N_NAT_TILES
N_TILES = J_T + N_T_TILES
IN_ROW_TILE = 1024
IN_CHUNK = 256
A_STEP_TOKENS = 2048
SPARSE_TILE = 256
POST_ROW_TILE = 512

KIND_PLAIN, KIND_NORM_ROPE, KIND_ROPE, KIND_ROPE_LOW, KIND_SILU, KIND_SIGMOID = range(6)
DEST_NAT, DEST_F1, DEST_F2, DEST_T = range(4)
_ROUTES = ((KIND_PLAIN, DEST_NAT), (KIND_PLAIN, DEST_F1), (KIND_PLAIN, DEST_F2),
           (KIND_NORM_ROPE, DEST_NAT), (KIND_NORM_ROPE, DEST_F1), (KIND_NORM_ROPE, DEST_F2),
           (KIND_ROPE_LOW, DEST_NAT), (KIND_SILU, DEST_NAT), (KIND_SIGMOID, DEST_NAT),
           (KIND_PLAIN, DEST_T), (KIND_NORM_ROPE, DEST_T), (KIND_ROPE, DEST_T))


def _route_code(kind, dest):
    return kind * 4 + dest

_FLOOR_KEY = int(np.array(-5e29, np.float32).view(np.int32)) ^ 0x7FFFFFFF
_INT_MIN = -2 ** 31
_HALF16 = 32768
_PACK16 = 16
_LOG2_MOBA_BLOCK = MOBA_BLOCK.bit_length() - 1
_LOG2_N_HEADS = N_HEADS.bit_length() - 1
_LOG2_HEAD_DIM = HEAD_DIM.bit_length() - 1


def _dot(a, b):
    return jnp.dot(a, b, preferred_element_type=F32)


def _dot_nt(a, b):
    return lax.dot_general(a, b, (((1,), (1,)), ((), ())), preferred_element_type=F32)


def _params(*sem):
    return pltpu.CompilerParams(dimension_semantics=sem, vmem_limit_bytes=VMEM_LIMIT_BYTES)


def _inproj_kernel(route_ref, x_ref, ng_ref, w_ref, wt_ref, gain_ref, gain_t_ref,
                   cos_ref, s1_ref, s2_ref, cos_t_ref, s1_t_ref, s2_t_ref, bd_ref,
                   o_ref, f1_ref, f2_ref, tt_ref, h_ref, ht_ref, stage_ref):
    j = pl.program_id(1)
    tm = x_ref.shape[0]
    n_ch = TILE // LANES

    @pl.when(j == 0)
    def _():
        x = x_ref[...]
        ms = jnp.mean(x * x, axis=-1, keepdims=True)
        h = x * lax.rsqrt(ms + NORM_EPS) * ng_ref[...]
        h_ref[...] = h.astype(BF16)
        ht_ref[...] = jnp.transpose(h).astype(BF16)

    def fold(val, ref, dil, tok):
        rows = (tok.stop - tok.start) // dil
        for c in range(n_ch):
            stage_ref[c, tok, :] = val[:, c * LANES:(c + 1) * LANES]
        for r in range(dil):
            parts = [stage_ref[c, pl.ds(tok.start + r, rows, stride=dil), :] for c in range(n_ch)]
            ref[r, tok.start // dil:tok.stop // dil, :] = jnp.concatenate(parts, axis=1).astype(ref.dtype)

    def rope(v, low_only, tok):
        c = jnp.tile(cos_ref[tok, :], (1, n_ch))
        s1 = jnp.tile(s1_ref[tok, :], (1, n_ch))
        s2 = jnp.tile(s2_ref[tok, :], (1, n_ch))
        if low_only:
            low = lax.broadcasted_iota(I32, v.shape, 1) < HEAD_DIM
            c = jnp.where(low, c, 1.0)
            s1 = jnp.where(low, s1, 0.0)
            s2 = jnp.where(low, s2, 0.0)
        return v * c + pltpu.roll(v, TILE - ROT_HALF, 1) * s1 + pltpu.roll(v, ROT_HALF, 1) * s2

    def rope_t(v, tok):
        c = jnp.tile(cos_t_ref[:, tok], (N_HEADS, 1))
        s1 = jnp.tile(s1_t_ref[:, tok], (N_HEADS, 1))
        s2 = jnp.tile(s2_t_ref[:, tok], (N_HEADS, 1))
        return v * c + pltpu.roll(v, TILE - ROT_HALF, 0) * s1 + pltpu.roll(v, ROT_HALF, 0) * s2

    def epilogue(kind, y, tok):
        if kind == KIND_PLAIN:
            return y
        if kind == KIND_NORM_ROPE:
            y2 = (y * y).astype(BF16)
            ms = jnp.concatenate([_dot(y2[:, c * MXU_DIM:(c + 1) * MXU_DIM], bd_ref[...])
                                  for c in range(TILE // MXU_DIM)], axis=1)
            return rope(y * lax.rsqrt(ms + NORM_EPS) * gain_ref[0], False, tok)
        if kind == KIND_ROPE_LOW:
            return rope(y, True, tok)
        if kind == KIND_SILU:
            return y / (1.0 + jnp.exp(-y))
        assert kind == KIND_SIGMOID
        return 1.0 / (1.0 + jnp.exp(-y))

    def epilogue_t(kind, yt, tok):
        if kind == KIND_PLAIN:
            return yt
        gain = jnp.tile(gain_t_ref[0], (1, yt.shape[1] // LANES))
        if kind == KIND_NORM_ROPE:
            y2 = (yt * yt).astype(BF16)
            ms = jnp.concatenate([_dot(bd_ref[...], y2[c * MXU_DIM:(c + 1) * MXU_DIM, :])
                                  for c in range(TILE // MXU_DIM)], axis=0)
            return rope_t(yt * lax.rsqrt(ms + NORM_EPS) * gain, tok)
        assert kind == KIND_ROPE
        return rope_t(yt * gain, tok)

    route = route_ref[j]
    for kind, dest in _ROUTES:
        @pl.when(route == _route_code(kind, dest))
        def _(kind=kind, dest=dest):
            if dest == DEST_T:
                tok = slice(0, tm)
                tt_ref[...] = epilogue_t(kind, _dot(wt_ref[...], ht_ref[...]), tok).astype(tt_ref.dtype)
                return
            chunk = max(IN_CHUNK, _PACK16 * DIL_PATTERNS[2][1]) if dest == DEST_F2 else IN_CHUNK
            for m in range(tm // chunk):
                tok = slice(m * chunk, (m + 1) * chunk)
                val = epilogue(kind, _dot(h_ref[tok, :], w_ref[...]), tok)
                if dest == DEST_NAT:
                    o_ref[tok, :] = val.astype(o_ref.dtype)
                elif dest == DEST_F1:
                    fold(val, f1_ref, DIL_PATTERNS[1][1], tok)
                else:
                    fold(val, f2_ref, DIL_PATTERNS[2][1], tok)


def _inproj(x2, ng, w_nat, w_t, routes, gains, gains_t, tabs, tabs_t, bd, tm, B, S):
    T = x2.shape[0]
    grid = (T // tm, N_TILES)
    per_b = S // tm
    d1, d2 = DIL_PATTERNS[1][1], DIL_PATTERNS[2][1]
    half = N_FOLD_TILES // 2
    assert S % tm == 0 and tm % (_PACK16 * d2) == 0 and tm % IN_CHUNK == 0 and IN_CHUNK % (_PACK16 * d1) == 0
    n_w = J_T

    def fold_spec(dil, first):
        return pl.BlockSpec(
            (None, None, dil, tm // dil, TILE),
            lambda i, j, k: (jnp.clip(j - first, 0, half - 1), i // per_b, 0, i % per_b, 0))

    row_tab = pl.BlockSpec((tm, LANES), lambda i, j, k: (i, 0))
    col_tab = pl.BlockSpec((HEAD_DIM, tm), lambda i, j, k: (0, i))
    t_idx = lambda j: jnp.maximum(j - J_T, 0)
    return pl.pallas_call(
        _inproj_kernel,
        grid_spec=pltpu.PrefetchScalarGridSpec(
            num_scalar_prefetch=1,
            grid=grid,
            in_specs=[
                pl.BlockSpec((tm, D_MODEL), lambda i, j, k: (i, 0)),
                pl.BlockSpec((1, D_MODEL), lambda i, j, k: (0, 0)),
                pl.BlockSpec((None, D_MODEL, TILE), lambda i, j, k: (jnp.minimum(j, n_w - 1), 0, 0)),
                pl.BlockSpec((TILE, D_MODEL), lambda i, j, k: (t_idx(j), 0)),
                pl.BlockSpec((1, 1, TILE), lambda i, j, k: (jnp.minimum(j, n_w - 1), 0, 0)),
                pl.BlockSpec((1, TILE, LANES), lambda i, j, k: (t_idx(j), 0, 0)),
                row_tab, row_tab, row_tab, col_tab, col_tab, col_tab,
                pl.BlockSpec((MXU_DIM, MXU_DIM), lambda i, j, k: (0, 0)),
            ],
            out_specs=[
                pl.BlockSpec((None, tm, TILE), lambda i, j, k: (jnp.clip(j - J_NAT, 0, N_NAT_TILES - 1), i, 0)),
                fold_spec(d1, 0),
                fold_spec(d2, half),
                pl.BlockSpec((None, TILE, tm), lambda i, j, k: (i // per_b, t_idx(j), i % per_b)),
            ],
            scratch_shapes=[pltpu.VMEM((tm, D_MODEL), BF16), pltpu.VMEM((D_MODEL, tm), BF16),
                            pltpu.VMEM((TILE // LANES, tm, LANES), F32)],
        ),
        out_shape=[jax.ShapeDtypeStruct((N_NAT_TILES, T, TILE), BF16),
                   jax.ShapeDtypeStruct((half, B, d1, S // d1, TILE), BF16),
                   jax.ShapeDtypeStruct((half, B, d2, S // d2, TILE), BF16),
                   jax.ShapeDtypeStruct((B, N_T_TILES * TILE, S), BF16)],
        compiler_params=_params("arbitrary", "arbitrary"),
        name="inproj",
    )(routes, x2, ng, w_nat, w_t, gains, gains_t, *tabs, *tabs_t, bd)


def _attn_a_kernel(shift_ref, q_ref, kc_ref, kp_ref, vc_ref, vp_ref, o_ref, lse_ref, kcat, vtcat, ost, lst,
                   s_ref, p_ref, *, tq, dil):
    m = pl.program_id(1)
    nsub = tq // BAND
    n_ch = TILE // LANES
    nk = 2 * BAND
    key = lax.broadcasted_iota(I32, (nk, BAND), 0)
    qry = lax.broadcasted_iota(I32, (nk, BAND), 1) + BAND
    in_band = jnp.logical_and(key <= qry, key >= qry - BAND)
    lane = lax.broadcasted_iota(I32, (BAND, LANES), 1)
    half_mask = (jnp.where(lane < HEAD_DIM, 1.0, 0.0).astype(BF16),
                 jnp.where(lane < HEAD_DIM, 0.0, 1.0).astype(BF16))
    ones = jnp.ones((_PACK16, nk), BF16)
    shifted = shift_ref[0] <= MAX_SOFTMAX_SHIFT
    shift = jnp.where(shifted, shift_ref[0], 0.0)

    def residue(r, carry):
        kcat[0:BAND, :] = kp_ref[r]
        kcat[BAND:, :] = kc_ref[r]
        vtcat[:, 0:BAND] = jnp.transpose(vp_ref[r].astype(F32)).astype(BF16)
        for c in range(nsub):
            chunk = vc_ref[r, c * BAND:(c + 1) * BAND, :].astype(F32)
            vtcat[:, (c + 1) * BAND:(c + 2) * BAND] = jnp.transpose(chunk).astype(BF16)

        def body(u, carry):
            r0 = pl.multiple_of(u * BAND, BAND)
            q = q_ref[r, pl.ds(r0, BAND), :]
            k2 = kcat[pl.ds(r0, nk), :]
            valid = jnp.logical_and(in_band, key >= jnp.where(m * nsub + u > 0, 0, BAND))
            bias = jnp.where(valid, -shift, NEG_INF)

            def score(h):
                pair = slice((h // 2) * LANES, (h // 2 + 1) * LANES)
                q_h = q[:, pair] * half_mask[h % 2]
                return _dot_nt(k2[:, pair], q_h) + bias

            def pv_of(h, p):
                vt_h = vtcat[h * HEAD_DIM:(h + 1) * HEAD_DIM, pl.ds(r0, nk)]
                pv = _dot(jnp.concatenate([vt_h, ones], axis=0), p)
                return pv[:HEAD_DIM], pv[HEAD_DIM:HEAD_DIM + 1]

            def block(use_shift):
                outs, lses = [], []
                if use_shift:
                    for h in range(N_HEADS):
                        p_ref[h] = jnp.exp2(score(h)).astype(BF16)
                    for h in range(N_HEADS):
                        acc, den = pv_of(h, p_ref[h])
                        outs.append(acc / den)
                        lses.append(shift + jnp.log2(den))
                else:
                    maxes = []
                    for h in range(N_HEADS):
                        s = score(h)
                        s_ref[h] = s
                        maxes.append(jnp.max(s, axis=0, keepdims=True))
                    for h in range(N_HEADS):
                        acc, den = pv_of(h, jnp.exp2(s_ref[h] - maxes[h]).astype(BF16))
                        outs.append(acc / den)
                        lses.append(maxes[h] + jnp.log2(den))
                o = jnp.transpose(jnp.concatenate(outs, axis=0))
                lse8 = jnp.concatenate(lses, axis=0)
                lse = jnp.transpose(jnp.tile(lse8, (LANES // N_HEADS, 1)))
                rows = pl.ds(r + u * (BAND * dil), BAND, stride=dil) if dil > 1 else pl.ds(r0, BAND)
                for c in range(n_ch):
                    ost[c, rows, :] = o[:, c * LANES:(c + 1) * LANES]
                lst[rows, :] = lse

            @pl.when(shifted)
            def _():
                block(True)

            @pl.when(jnp.logical_not(shifted))
            def _():
                block(False)

            return carry

        return lax.fori_loop(0, nsub, body, carry)

    lax.fori_loop(0, dil, residue, 0)
    o_ref[0] = jnp.concatenate([ost[c] for c in range(n_ch)], axis=1).astype(o_ref.dtype)
    lse_ref[0] = lst[...]


def _attn_a(qkv, shift, g, dil, B, S):
    ts = min(A_STEP_TOKENS, S)
    tq = ts // dil
    assert S % ts == 0 and tq % BAND == 0
    sub = tq // BAND

    def spec_cur(t):
        return pl.BlockSpec((None, None, dil, tq, TILE), lambda b, m: (t, b, 0, m, 0))

    def spec_prev(t):
        return pl.BlockSpec((None, None, dil, BAND, TILE),
                            lambda b, m: (t, b, 0, jnp.maximum(m * sub - 1, 0), 0))

    n_ch = TILE // LANES
    return pl.pallas_call(
        functools.partial(_attn_a_kernel, tq=tq, dil=dil),
        grid=(B, S // ts),
        in_specs=[pl.BlockSpec(memory_space=pltpu.SMEM),
                  spec_cur(0), spec_cur(1), spec_prev(1), spec_cur(2), spec_prev(2)],
        out_specs=[pl.BlockSpec((1, ts, TILE), lambda b, m: (b, m, 0)),
                   pl.BlockSpec((1, ts, LANES), lambda b, m: (b, m, 0))],
        out_shape=[jax.ShapeDtypeStruct((B, S, TILE), BF16), jax.ShapeDtypeStruct((B, S, LANES), F32)],
        scratch_shapes=[pltpu.VMEM((tq + BAND, TILE), BF16), pltpu.VMEM((TILE, tq + BAND), BF16),
                        pltpu.VMEM((n_ch, ts, LANES), F32), pltpu.VMEM((ts, LANES), F32),
                        pltpu.VMEM((N_HEADS, 2 * BAND, BAND), F32),
                        pltpu.VMEM((N_HEADS, 2 * BAND, BAND), BF16)],
        compiler_params=_params("arbitrary", "arbitrary"),
        name=f"attn_a{g}",
    )(shift, qkv, qkv, qkv, qkv, qkv)


def _pad_q(qt_ref, qpad_ref):
    qpad_ref[...] = jnp.zeros(qpad_ref.shape, qpad_ref.dtype)
    for h in range(N_HEADS):
        r0 = h * LANES + (h % 2) * HEAD_DIM
        qpad_ref[r0:r0 + HEAD_DIM, :] = qt_ref[h * HEAD_DIM:(h + 1) * HEAD_DIM, :]


def _flash_init(m_ref, l_ref, acc_ref):
    m_ref[...] = jnp.full(m_ref.shape, NEG_INF, F32)
    l_ref[...] = jnp.zeros(l_ref.shape, F32)
    acc_ref[...] = jnp.zeros(acc_ref.shape, F32)


def _flash_tile_step(score_of_head, vt_of_head, s_ref, mn_ref, m_ref, l_ref, acc_ref):
    tk = s_ref.shape[1]
    for h in range(N_HEADS):
        s = score_of_head(h)
        s_ref[h] = s
        mn_ref[h] = jnp.maximum(m_ref[h], jnp.max(s, axis=0, keepdims=True))
    ones = jnp.ones((_PACK16, tk), BF16)
    for h in range(N_HEADS):
        m_prev, m_next = m_ref[h], mn_ref[h]
        p = jnp.exp2(s_ref[h] - jnp.tile(m_next, (tk // SUBLANES, 1)))
        alpha = jnp.exp2(m_prev - m_next)
        pv = _dot(jnp.concatenate([vt_of_head(h), ones], axis=0), p.astype(BF16))
        l_ref[h] = alpha * l_ref[h] + pv[HEAD_DIM:HEAD_DIM + SUBLANES]
        acc_ref[h] = acc_ref[h] * jnp.tile(alpha, (HEAD_DIM // SUBLANES, 1)) + pv[:HEAD_DIM]
        m_ref[h] = m_next


def _flash_tile_step_shifted(score_of_head, vt_of_head, p_ref, l_ref, acc_ref):
    tk = p_ref.shape[1]
    for h in range(N_HEADS):
        p_ref[h] = jnp.exp2(score_of_head(h)).astype(BF16)
    ones = jnp.ones((_PACK16, tk), BF16)
    for h in range(N_HEADS):
        pv = _dot(jnp.concatenate([vt_of_head(h), ones], axis=0), p_ref[h])
        l_ref[h] = l_ref[h] + pv[HEAD_DIM:HEAD_DIM + SUBLANES]
        acc_ref[h] = acc_ref[h] + pv[:HEAD_DIM]


def _flash_finish(o_ref, l_ref, acc_ref):
    outs = [acc_ref[h] / jnp.tile(l_ref[h], (HEAD_DIM // SUBLANES, 1)) for h in range(N_HEADS)]
    o_ref[...] = jnp.transpose(jnp.concatenate(outs, axis=0)).astype(o_ref.dtype)


def _dsa_kernel(shift_ref, qt_ref, k_ref, vt_ref, iqt_ref, ikw_q_ref, ikw_ref, o_ref,
                key_ref, hi_ref, lo_ref, qpad_ref, bias_ref, p_ref, s_ref, mn_ref, m_ref, l_ref, acc_ref,
                *, tq, tk, topk):
    qi = pl.program_id(1)
    n_kt = qi + 1
    n_acc = 4
    rows8 = tk // SUBLANES

    w8 = jnp.transpose(ikw_q_ref[...].astype(F32))[HEAD_DIM:HEAD_DIM + IDX_HEADS, :] * (IDX_HEADS ** -0.5)
    krow = lax.broadcasted_iota(I32, (tk, tq), 0)
    qcol = lax.broadcasted_iota(I32, (tk, tq), 1)

    def score_tile(c, diagonal):
        c0 = pl.multiple_of(c * tk, tk)
        kx = ikw_ref[pl.ds(c0, tk), :][:, :IDX_DIM]
        sc = jnp.zeros((tk, tq), F32)
        for h in range(IDX_HEADS):
            lg = _dot(kx, iqt_ref[h * IDX_DIM:(h + 1) * IDX_DIM, :])
            sc = sc + w8[h:h + 1, :] * jnp.maximum(lg, 0.0)
        sc = jnp.where(sc == 0.0, 0.0, sc)
        if diagonal:
            sc = jnp.where(krow <= qcol, sc, NEG_INF)
        bits = pltpu.bitcast(sc, I32)
        key = jnp.where(bits < 0, bits ^ 0x7FFFFFFF, bits)
        key_ref[pl.ds(c0, tk), :] = key
        hi_ref[pl.ds(c0, tk), :] = jnp.right_shift(key, 16).astype(I16)
        lo_ref[pl.ds(c0, tk), :] = ((key & 0xFFFF) - _HALF16).astype(I16)

    def score_body(c, carry):
        score_tile(c, False)
        return carry

    lax.fori_loop(0, qi, score_body, 0)
    score_tile(qi, True)

    rows16 = tk // _PACK16
    one16 = jnp.ones((_PACK16, tq), I16)
    zero16 = jnp.zeros((_PACK16, tq), I16)

    def count16(ref, pred):
        def cbody(c, accs):
            accs = list(accs)
            c0 = pl.multiple_of(c * tk, tk)
            t = ref[pl.ds(c0, tk), :]
            for g in range(rows16):
                hit = jnp.where(pred(t[g * _PACK16:(g + 1) * _PACK16]), one16, zero16)
                accs[g % n_acc] = accs[g % n_acc] + hit
            return tuple(accs)
        accs = lax.fori_loop(0, n_kt, cbody, tuple(zero16 for _ in range(n_acc)))
        return jnp.sum(functools.reduce(lambda a, b: a + b, accs).astype(I32), axis=0, keepdims=True)

    def as16(v):
        return jnp.broadcast_to(v, (_PACK16, tq)).astype(I16)

    def select16(ref, need, cge0):
        def bit_body(it, carry):
            ans, cge = carry
            cand_u = ans | lax.shift_left(jnp.int32(1), 15 - it)
            cand = as16(cand_u - _HALF16)
            cnt = count16(ref, lambda t: t >= cand)
            ok = cnt >= need
            return jnp.where(ok, cand_u, ans), jnp.where(ok, cnt, cge)
        return lax.fori_loop(0, 16, bit_body, (jnp.zeros((1, tq), I32), cge0))

    n_all = jnp.zeros((1, tq), I32) + n_kt * tk
    p_u, cge_hi = select16(hi_ref, topk, n_all)
    p16 = as16(p_u - _HALF16)
    c_gt = count16(hi_ref, lambda t: t > p16)

    def bucket_body(c, carry):
        c0 = pl.multiple_of(c * tk, tk)
        lo_ref[pl.ds(c0, tk), :] = jnp.where(hi_ref[pl.ds(c0, tk), :] == jnp.tile(p16, (rows16, 1)),
                                             lo_ref[pl.ds(c0, tk), :], jnp.int16(-_HALF16))
        return carry

    lax.fori_loop(0, n_kt, bucket_body, 0)
    l_u, cge_lo = select16(lo_ref, topk - c_gt, cge_hi - c_gt)
    cge = c_gt + cge_lo
    thr = jnp.maximum((p_u - _HALF16) * 65536 + l_u, _FLOOR_KEY)

    def count_rows(fn, n_out):
        def cbody(c, accs):
            accs = [list(a) for a in accs]
            c0 = pl.multiple_of(c * tk, tk)
            kt = key_ref[pl.ds(c0, tk), :]
            for g in range(rows8):
                r0 = c0 + g * SUBLANES
                vals = fn(kt[g * SUBLANES:(g + 1) * SUBLANES], r0)
                for o in range(n_out):
                    accs[o][g % n_acc] = accs[o][g % n_acc] + vals[o]
            return tuple(tuple(a) for a in accs)
        z = jnp.zeros((SUBLANES, tq), I32)
        accs = lax.fori_loop(0, n_kt, cbody, tuple(tuple(z for _ in range(n_acc)) for _ in range(n_out)))
        return [jnp.sum(functools.reduce(lambda a, b: a + b, a), axis=0, keepdims=True) for a in accs]

    tie = jnp.logical_and(cge > topk, thr > _FLOOR_KEY)
    any_tie = jnp.max(jnp.where(tie, 1, 0)) > 0
    thr8 = jnp.broadcast_to(thr, (SUBLANES, tq))

    @pl.when(any_tie)
    def _():
        n_bits = int(np.log2(key_ref.shape[0]))
        sub = lax.broadcasted_iota(I32, (SUBLANES, tq), 0)

        def jbody(it, lo):
            cand = lo + lax.shift_left(jnp.int32(1), n_bits - 1 - it)
            pos = jnp.broadcast_to(cand - 1, (SUBLANES, tq))

            def f(kk, r0):
                eq = jnp.logical_and(kk == thr8, sub + r0 <= pos)
                return jnp.where(kk > thr8, 1, 0), jnp.where(eq, 1, 0)

            gt, eq = count_rows(f, 2)
            return jnp.where(gt + eq >= topk, lo, cand)

        jrow = jnp.broadcast_to(lax.fori_loop(0, n_bits, jbody, jnp.zeros((1, tq), I32)), (SUBLANES, tq))
        tie8 = jnp.broadcast_to(jnp.where(tie, 1, 0), (SUBLANES, tq)) > 0

        def fix(c, carry):
            for g in range(rows8):
                r0 = pl.multiple_of(c * tk + g * SUBLANES, SUBLANES)
                kk = key_ref[pl.ds(r0, SUBLANES), :]
                drop = jnp.logical_and(tie8, jnp.logical_and(kk == thr8, sub + r0 > jrow))
                key_ref[pl.ds(r0, SUBLANES), :] = jnp.where(drop, _INT_MIN, kk)
            return carry

        lax.fori_loop(0, n_kt, fix, 0)

    _pad_q(qt_ref, qpad_ref)
    _flash_init(m_ref, l_ref, acc_ref)
    shift = shift_ref[0]

    def attend(shifted):
        def att_body(j, carry):
            c0 = pl.multiple_of(j * tk, tk)
            sel = key_ref[pl.ds(c0, tk), :] >= thr
            bias_ref[...] = jnp.where(sel, -shift if shifted else 0.0, NEG_INF)
            k = k_ref[pl.ds(c0, tk), :]

            def score(h):
                k_pair = k[:, (h // 2) * LANES:(h // 2 + 1) * LANES]
                return _dot(k_pair, qpad_ref[h * LANES:(h + 1) * LANES, :]) + bias_ref[...]

            vt_of_head = lambda h: vt_ref[h * HEAD_DIM:(h + 1) * HEAD_DIM, pl.ds(c0, tk)]
            if shifted:
                _flash_tile_step_shifted(score, vt_of_head, p_ref, l_ref, acc_ref)
            else:
                _flash_tile_step(score, vt_of_head, s_ref, mn_ref, m_ref, l_ref, acc_ref)
            return carry

        lax.fori_loop(0, n_kt, att_body, 0)

    @pl.when(shift <= MAX_SOFTMAX_SHIFT)
    def _():
        attend(True)

    @pl.when(shift > MAX_SOFTMAX_SHIFT)
    def _():
        attend(False)

    _flash_finish(o_ref, l_ref, acc_ref)


def _flash_scratch(tk, tq):
    stat = pltpu.VMEM((N_HEADS, SUBLANES, tq), F32)
    return [pltpu.VMEM((N_HEADS, tk, tq), BF16), pltpu.VMEM((N_HEADS, tk, tq), F32), stat, stat, stat,
            pltpu.VMEM((N_HEADS, HEAD_DIM, tq), F32)]


def _dsa(P4, TT, shift, B, S):
    tq = tk = SPARSE_TILE
    topk = min(IDX_TOPK_MAX, S // 4)
    assert S % tq == 0 and topk <= tk
    def tt_q(t):
        return pl.BlockSpec((None, TILE, tq), lambda b, i: (b, t, i))

    return pl.pallas_call(
        functools.partial(_dsa_kernel, tq=tq, tk=tk, topk=topk),
        grid=(B, S // tq),
        in_specs=[pl.BlockSpec(memory_space=pltpu.SMEM),
                  tt_q(TT_QB),
                  pl.BlockSpec((None, None, S, TILE), lambda b, i: (T_KB, b, 0, 0)),
                  pl.BlockSpec((None, TILE, S), lambda b, i: (b, TT_VB, 0)),
                  tt_q(TT_IQ),
                  pl.BlockSpec((None, None, tq, LANES), lambda b, i: (T_IKW, b, i, 0)),
                  pl.BlockSpec((None, None, S, LANES), lambda b, i: (T_IKW, b, 0, 0))],
        out_specs=pl.BlockSpec((None, tq, TILE), lambda b, i: (b, i, 0)),
        out_shape=jax.ShapeDtypeStruct((B, S, TILE), BF16),
        scratch_shapes=[pltpu.VMEM((S, tq), I32), pltpu.VMEM((S, tq), I16), pltpu.VMEM((S, tq), I16),
                        pltpu.VMEM((N_HEADS * LANES, tq), BF16),
                        pltpu.VMEM((tk, tq), F32)] + _flash_scratch(tk, tq),
        compiler_params=_params("arbitrary", "arbitrary"),
        name="dsa",
    )(shift, TT, P4, TT, TT, P4, P4)


def _moba_kernel(shift_ref, qt_ref, k_ref, vt_ref, o_ref, kmh_ref, kml_ref, qaug_ref,
                 p_ref, s_ref, mn_ref, m_ref, l_ref, acc_ref, *, tq, topb):
    qi = pl.program_id(1)
    S = k_ref.shape[0]
    gl = N_HEADS * MOBA_SLOTS

    @pl.when(qi == 0)
    def _():
        blk_row = jnp.right_shift(lax.broadcasted_iota(I32, (gl, S), 0), _LOG2_N_HEADS)
        blk_col = jnp.right_shift(lax.broadcasted_iota(I32, (gl, S), 1), _LOG2_MOBA_BLOCK)
        avg = jnp.where(blk_row == blk_col, 1.0 / MOBA_BLOCK, 0.0).astype(BF16)
        km = _dot(avg, k_ref[...])
        r_head = lax.broadcasted_iota(I32, (gl, TILE), 0) & (N_HEADS - 1)
        c_head = jnp.right_shift(lax.broadcasted_iota(I32, (gl, TILE), 1), _LOG2_HEAD_DIM)
        km = jnp.where(r_head == c_head, km, 0.0)
        hi = km.astype(BF16)
        kmh_ref[...] = hi
        kml_ref[...] = (km - hi.astype(F32)).astype(BF16)

    shifted = shift_ref[0] <= MAX_SOFTMAX_SHIFT
    shift = jnp.where(shifted, shift_ref[0], 0.0)
    qt = qt_ref[...]
    gate = _dot(kmh_ref[...], qt) + _dot(kml_ref[...], qt)
    row = lax.broadcasted_iota(I32, (gl, tq), 0)
    gate = jnp.where(jnp.right_shift(row, _LOG2_N_HEADS) < qi, gate, NEG_INF)
    g = [gate[n * N_HEADS:(n + 1) * N_HEADS] for n in range(MOBA_SLOTS)]
    biases = []
    for n in range(MOBA_SLOTS):
        beaten = jnp.where(n < qi, 0, topb) + jnp.zeros((N_HEADS, tq), I32)
        for n2 in range(MOBA_SLOTS):
            if n2 != n:
                beaten = beaten + jnp.where((g[n2] >= g[n]) if n2 < n else (g[n2] > g[n]), 1, 0)
        biases.append(jnp.where(beaten < topb, -shift, NEG_INF))
    selb = jnp.concatenate(biases, axis=0)

    qaug_ref[...] = jnp.zeros(qaug_ref.shape, qaug_ref.dtype)
    for h in range(N_HEADS):
        r0 = (h % 2) * HEAD_DIM
        qaug_ref[h, r0:r0 + HEAD_DIM, :] = qt_ref[h * HEAD_DIM:(h + 1) * HEAD_DIM, :]
        qaug_ref[h, LANES:, :] = jnp.where((row & (N_HEADS - 1)) == h, selb, 0.0).astype(BF16)
    _flash_init(m_ref, l_ref, acc_ref)
    lane_blk = jnp.right_shift(lax.broadcasted_iota(I32, (MOBA_BLOCK, LANES), 1), _LOG2_N_HEADS)

    def vt_of(c0):
        return lambda h: vt_ref[h * HEAD_DIM:(h + 1) * HEAD_DIM, pl.ds(c0, MOBA_BLOCK)]

    def attend(use_shift):
        def step(score, c0):
            if use_shift:
                _flash_tile_step_shifted(score, vt_of(c0), p_ref, l_ref, acc_ref)
            else:
                _flash_tile_step(score, vt_of(c0), s_ref, mn_ref, m_ref, l_ref, acc_ref)

        def att_body(n, carry):
            c0 = pl.multiple_of(n * MOBA_BLOCK, MOBA_BLOCK)
            k = k_ref[pl.ds(c0, MOBA_BLOCK), :]
            onehot = jnp.where(lane_blk == n, 1.0, 0.0).astype(BF16)

            def score(h):
                k_aug = jnp.concatenate([k[:, (h // 2) * LANES:(h // 2 + 1) * LANES], onehot], axis=1)
                return _dot(k_aug, qaug_ref[h])

            step(score, c0)
            return carry

        lax.fori_loop(0, qi, att_body, 0)

        c0 = pl.multiple_of(qi * MOBA_BLOCK, MOBA_BLOCK)
        k = k_ref[pl.ds(c0, MOBA_BLOCK), :]
        causal = jnp.where(lax.broadcasted_iota(I32, (MOBA_BLOCK, tq), 0)
                           <= lax.broadcasted_iota(I32, (MOBA_BLOCK, tq), 1), -shift, NEG_INF)
        step(lambda h: _dot(k[:, (h // 2) * LANES:(h // 2 + 1) * LANES], qaug_ref[h, :LANES, :]) + causal, c0)

    @pl.when(shifted)
    def _():
        attend(True)

    @pl.when(jnp.logical_not(shifted))
    def _():
        attend(False)

    _flash_finish(o_ref, l_ref, acc_ref)


def _moba(P4, TT, shift, B, S):
    tq = MOBA_BLOCK
    nblk = S // MOBA_BLOCK
    assert S % MOBA_BLOCK == 0 and nblk <= MOBA_SLOTS
    topb = min(MOBA_TOPK, nblk - 1)
    gl = N_HEADS * MOBA_SLOTS
    return pl.pallas_call(
        functools.partial(_moba_kernel, tq=tq, topb=topb),
        grid=(B, S // tq),
        in_specs=[pl.BlockSpec(memory_space=pltpu.SMEM),
                  pl.BlockSpec((None, TILE, tq), lambda b, i: (b, TT_QC, i)),
                  pl.BlockSpec((None, None, S, TILE), lambda b, i: (T_KC, b, 0, 0)),
                  pl.BlockSpec((None, TILE, S), lambda b, i: (b, TT_VC, 0))],
        out_specs=pl.BlockSpec((None, tq, TILE), lambda b, i: (b, i, 0)),
        out_shape=jax.ShapeDtypeStruct((B, S, TILE), BF16),
        scratch_shapes=[pltpu.VMEM((gl, TILE), BF16), pltpu.VMEM((gl, TILE), BF16),
                        pltpu.VMEM((N_HEADS, 2 * LANES, tq), BF16)]
                       + _flash_scratch(MOBA_BLOCK, tq),
        compiler_params=_params("arbitrary", "arbitrary"),
        name="moba",
    )(shift, TT, P4, TT)


def _post_kernel(x_ref, oa0, oa1, oa2, la0, la1, la2, ob_ref, oc_ref, z0, z1, z2, g0, g1, g2,
                 wbr_ref, wout_ref, expand_ref, out_ref):
    l0, l1, l2 = la0[...], la1[...], la2[...]
    mx = jnp.maximum(jnp.maximum(l0, l1), l2)
    e0, e1, e2 = jnp.exp2(l0 - mx), jnp.exp2(l1 - mx), jnp.exp2(l2 - mx)
    den = e0 + e1 + e2

    def spread(w):
        hi = w.astype(BF16)
        lo = (w - hi.astype(F32)).astype(BF16)
        return _dot(hi, expand_ref[...]) + _dot(lo, expand_ref[...])

    o_a = (spread(e0 / den) * oa0[...].astype(F32) + spread(e1 / den) * oa1[...].astype(F32)
           + spread(e2 / den) * oa2[...].astype(F32))
    branches = (o_a, ob_ref[...].astype(F32), oc_ref[...].astype(F32))
    merged = jnp.zeros(out_ref.shape, F32)
    for n, (o, z, g) in enumerate(zip(branches, (z0, z1, z2), (g0, g1, g2))):
        y = _dot((o * z[0].astype(F32)).astype(BF16), wbr_ref[n])
        gate = jnp.concatenate([g[t] for t in range(g.shape[0])], axis=1)
        merged = merged + gate.astype(F32) * y
    out_ref[...] = x_ref[...] + _dot(merged.astype(BF16), wout_ref[...])


def _post(x2, oa, la, ob, oc, P, wbr, wout, tm):
    T = x2.shape[0]
    row = lambda width, t: pl.BlockSpec((tm, width), lambda i: (i, t))
    tiles = lambda count, first: pl.BlockSpec((count, tm, TILE), lambda i: (first // count, i, 0))
    per_g = D_MODEL // TILE
    assert T_G % per_g == 0
    head_of = np.arange(TILE) // HEAD_DIM
    expand = jnp.asarray((np.arange(LANES)[:, None] == head_of[None, :]).astype(np.float32), BF16)
    in_specs = ([row(D_MODEL, 0)] + [row(TILE, 0)] * 3 + [row(LANES, 0)] * 3 + [row(TILE, 0)] * 2
                + [tiles(1, T_Z + n) for n in range(N_BRANCH)]
                + [tiles(per_g, T_G + per_g * n) for n in range(N_BRANCH)]
                + [pl.BlockSpec((N_BRANCH, BRANCH_WIDTH, D_MODEL), lambda i: (0, 0, 0)),
                   pl.BlockSpec((D_MODEL, D_MODEL), lambda i: (0, 0)),
                   pl.BlockSpec((LANES, TILE), lambda i: (0, 0))])
    return pl.pallas_call(
        _post_kernel,
        grid=(T // tm,),
        in_specs=in_specs,
        out_specs=row(D_MODEL, 0),
        out_shape=jax.ShapeDtypeStruct((T, D_MODEL), F32),
        compiler_params=_params("arbitrary"),
        name="post",
    )(x2, oa[0], oa[1], oa[2], la[0], la[1], la[2], ob, oc, P, P, P, P, P, P, wbr, wout, expand)


def _rearrange_w_in(w):
    bw = BRANCH_WIDTH
    a_q, a_k, a_v = w[:, 0:3 * bw], w[:, 3 * bw:6 * bw], w[:, 6 * bw:9 * bw]
    off = 9 * bw
    b_q, b_k, b_v = (w[:, off + i * bw:off + (i + 1) * bw] for i in range(3)); off += 3 * bw
    iq = w[:, off:off + IDX_HEADS * IDX_DIM]; off += IDX_HEADS * IDX_DIM
    ik = w[:, off:off + IDX_DIM]; off += IDX_DIM
    iw = w[:, off:off + IDX_HEADS]; off += IDX_HEADS
    c_q, c_k, c_v = (w[:, off + i * bw:off + (i + 1) * bw] for i in range(3)); off += 3 * bw
    z = w[:, off:off + 3 * bw]; off += 3 * bw
    g = w[:, off:off + 3 * D_MODEL]; off += 3 * D_MODEL
    assert off == w.shape[1]
    cols = []
    for grp in (1, 2, 0):
        sl = slice(grp * bw, (grp + 1) * bw)
        cols += [a_q[:, sl], a_k[:, sl], a_v[:, sl]]
    pad = jnp.zeros((w.shape[0], TILE - IDX_DIM - IDX_HEADS), w.dtype)
    cols += [b_k, ik, iw, pad, c_k, g, z]
    w_nat = jnp.concatenate(cols, axis=1).astype(BF16)
    assert w_nat.shape[1] == J_T * TILE
    w_nat = w_nat.reshape(-1, J_T, TILE).transpose(1, 0, 2)
    w_t = jnp.concatenate([b_q, b_v, iq, c_q, c_v], axis=1).T.astype(BF16)
    assert w_t.shape[0] == N_T_TILES * TILE
    return w_nat, w_t


def _tile_tables(qk_g):
    kinds = np.zeros((N_TILES,), np.int32)
    ones = jnp.ones((TILE,), F32)
    gains = [ones] * J_T
    gains_t = [ones] * N_T_TILES
    scale = HEAD_DIM ** -0.5 * LOG2_E
    head = lambda v: jnp.tile(v, N_HEADS)

    def qk(tile, mixer):
        kinds[tile] = kinds[tile + 1] = KIND_NORM_ROPE
        gains[tile] = head(qk_g[mixer, 0]) * scale
        gains[tile + 1] = head(qk_g[mixer, 1])

    qk(0, 0)
    qk(N_FOLD_TILES // 2, 0)
    qk(J_NAT + T_A, 0)
    for tile, mixer in ((T_KB, 1), (T_KC, 2)):
        kinds[J_NAT + tile] = KIND_NORM_ROPE
        gains[J_NAT + tile] = head(qk_g[mixer, 1])
    kinds[J_NAT + T_IKW] = KIND_ROPE_LOW
    kinds[J_NAT + T_Z:J_NAT + T_Z + 3] = KIND_SILU
    kinds[J_NAT + T_G:J_NAT + T_G + 6] = KIND_SIGMOID
    for tile, mixer in ((TT_QB, 1), (TT_QC, 2)):
        kinds[J_T + tile] = KIND_NORM_ROPE
        gains_t[tile] = head(qk_g[mixer, 0]) * scale
    kinds[J_T + TT_IQ] = KIND_ROPE
    gains_t[TT_IQ] = jnp.full((TILE,), IDX_DIM ** -0.5, F32)
    gains_t = jnp.broadcast_to(jnp.stack(gains_t)[:, :, None], (N_T_TILES, TILE, LANES))
    half = N_FOLD_TILES // 2
    dest = np.array([DEST_F1] * half + [DEST_F2] * half + [DEST_NAT] * N_NAT_TILES + [DEST_T] * N_T_TILES)
    assert all((k, d) in _ROUTES for k, d in zip(kinds.tolist(), dest.tolist()))
    routes = _route_code(kinds, dest).astype(np.int32)
    return jnp.asarray(routes), jnp.stack(gains)[:, None, :], gains_t


def _softmax_shift(qk_gain):
    bound = HEAD_DIM * jnp.max(jnp.abs(qk_gain[0])) * jnp.max(jnp.abs(qk_gain[1]))
    return (SHIFT_SLACK * HEAD_DIM ** -0.5 * LOG2_E * bound).reshape(1).astype(BF16).astype(F32)


def _rope_tables(positions):
    inv = ROPE_THETA ** (-jnp.arange(0, ROT_DIM, 2, dtype=F32) / ROT_DIM)
    ang = positions.astype(F32).reshape(-1)[:, None] * inv
    cos, sin = jnp.cos(ang), jnp.sin(ang)
    T = cos.shape[0]
    z8 = jnp.zeros((T, ROT_HALF), F32)
    rest1 = jnp.ones((T, HEAD_DIM - ROT_DIM), F32)
    rest0 = jnp.zeros((T, HEAD_DIM - ROT_DIM), F32)
    c = jnp.concatenate([cos, cos, rest1], axis=1)
    s1 = jnp.concatenate([-sin, z8, rest0], axis=1)
    s2 = jnp.concatenate([z8, sin, rest0], axis=1)
    two = lambda t: jnp.concatenate([t, t], axis=1)
    return (two(c), two(s1), two(s2)), (c.T, s1.T, s2.T)


def _block_diag_mean():
    h = np.arange(MXU_DIM) // HEAD_DIM
    return jnp.asarray((h[:, None] == h[None, :]).astype(np.float32) / HEAD_DIM, BF16)


def _layer(x2, B, S, tabs, tabs_t, bd, norm_g, w_in, qk_g, w_br, w_out, tm_in, tm_post):
    routes, gains, gains_t = _tile_tables(qk_g)
    w_nat, w_t = _rearrange_w_in(w_in)
    P, f1, f2, TT = _inproj(x2, norm_g[None, :], w_nat, w_t, routes, gains, gains_t, tabs, tabs_t, bd,
                            tm_in, B, S)
    P4 = P.reshape(N_NAT_TILES, B, S, TILE)
    oa, la = [], []
    for g, qkv in enumerate((P4.reshape(N_NAT_TILES, B, 1, S, TILE), f1, f2)):
        o, lse = _attn_a(qkv, _softmax_shift(qk_g[0]), g, DIL_PATTERNS[g][1], B, S)
        oa.append(o.reshape(B * S, TILE))
        la.append(lse.reshape(B * S, LANES))
    ob = _dsa(P4, TT, _softmax_shift(qk_g[1]), B, S).reshape(B * S, TILE)
    oc = _moba(P4, TT, _softmax_shift(qk_g[2]), B, S).reshape(B * S, TILE)
    return _post(x2, oa, la, ob, oc, P, w_br.astype(BF16), w_out.astype(BF16), tm_post)


def _forward(x, positions, norm_g, w_in, qk_g, w_br, w_out, tm_in=IN_ROW_TILE, tm_post=POST_ROW_TILE):
    B, S, D = x.shape
    tabs, tabs_t = _rope_tables(positions)
    bd = _block_diag_mean()
    x2 = x.reshape(B * S, D)
    for layer in range(norm_g.shape[0]):
        x2 = _layer(x2, B, S, tabs, tabs_t, bd, norm_g[layer], w_in[layer], qk_g[layer],
                    w_br[layer], w_out[layer], tm_in, tm_post)
    return x2.reshape(B, S, D)


def kernel(x, positions, norm_g, w_in, qk_g, w_br, w_out):
    return _forward(x, positions, norm_g, w_in, qk_g, w_br, w_out)
```

```python
import functools

import jax
import jax.numpy as jnp
import numpy as np
from jax import lax
from jax.experimental import pallas as pl
from jax.experimental.pallas import tpu as pltpu

F32 = jnp.float32
BF16 = jnp.bfloat16
I32 = jnp.int32
I16 = jnp.int16

D_MODEL = 1024
HEAD_DIM = 64
ROT_DIM = HEAD_DIM // 4
ROT_HALF = ROT_DIM // 2
ROPE_THETA = 500000.0
NORM_EPS = 1e-6
NEG_INF = -1e30
LOG2_E = 1.4426950408889634
MAX_SOFTMAX_SHIFT = 60.0
SHIFT_SLACK = 1.05
N_HEADS = 8
BRANCH_WIDTH = N_HEADS * HEAD_DIM
N_BRANCH = 3
DIL_PATTERNS = ((128, 1), (512, 4), (2048, 16))
BAND = 128
IDX_HEADS = 8
IDX_DIM = 64
IDX_TOPK_MAX = 256
MOBA_BLOCK = 256
MOBA_TOPK = 3
MOBA_SLOTS = 16

LANES = 128
SUBLANES = 8
MXU_DIM = 256
VMEM_LIMIT_BYTES = 56 * 1024 * 1024

TILE = BRANCH_WIDTH
N_FOLD_TILES = 6
T_A = 0
T_KB = 3
T_IKW = 4
T_KC = 5
T_G = 6
T_Z = 12
N_NAT_TILES = 15
TT_QB, TT_VB, TT_IQ, TT_QC, TT_VC = range(5)
N_T_TILES = 5
J_NAT = N_FOLD_TILES
J_T = N_FOLD_TILES + N_NAT_TILES
N_TILES = J_T + N_T_TILES
IN_ROW_TILE = 1024
IN_CHUNK = 256
A_STEP_TOKENS = 2048
SPARSE_TILE = 256
POST_ROW_TILE = 512

KIND_PLAIN, KIND_NORM_ROPE, KIND_ROPE, KIND_ROPE_LOW, KIND_SILU, KIND_SIGMOID = range(6)
DEST_NAT, DEST_F1, DEST_F2, DEST_T = range(4)
_ROUTES = ((KIND_PLAIN, DEST_NAT), (KIND_PLAIN, DEST_F1), (KIND_PLAIN, DEST_F2),
           (KIND_NORM_ROPE, DEST_NAT), (KIND_NORM_ROPE, DEST_F1), (KIND_NORM_ROPE, DEST_F2),
           (KIND_ROPE_LOW, DEST_NAT), (KIND_SILU, DEST_NAT), (KIND_SIGMOID, DEST_NAT),
           (KIND_PLAIN, DEST_T), (KIND_NORM_ROPE, DEST_T), (KIND_ROPE, DEST_T))


def _route_code(kind, dest):
    return kind * 4 + dest

_FLOOR_KEY = int(np.array(-5e29, np.float32).view(np.int32)) ^ 0x7FFFFFFF
_INT_MIN = -2 ** 31
_HALF16 = 32768
_PACK16 = 16
_LOG2_MOBA_BLOCK = MOBA_BLOCK.bit_length() - 1
_LOG2_N_HEADS = N_HEADS.bit_length() - 1
_LOG2_HEAD_DIM = HEAD_DIM.bit_length() - 1


def _dot(a, b):
    return jnp.dot(a, b, preferred_element_type=F32)


def _dot_nt(a, b):
    return lax.dot_general(a, b, (((1,), (1,)), ((), ())), preferred_element_type=F32)


def _params(*sem):
    return pltpu.CompilerParams(dimension_semantics=sem, vmem_limit_bytes=VMEM_LIMIT_BYTES)


def _inproj_kernel(route_ref, x_ref, ng_ref, w_ref, wt_ref, gain_ref, gain_t_ref,
                   cos_ref, s1_ref, s2_ref, cos_t_ref, s1_t_ref, s2_t_ref, bd_ref,
                   o_ref, f1_ref, f2_ref, tt_ref, h_ref, ht_ref, stage_ref):
    j = pl.program_id(1)
    tm = x_ref.shape[0]
    n_ch = TILE // LANES

    @pl.when(j == 0)
    def _():
        x = x_ref[...]
        ms = jnp.mean(x * x, axis=-1, keepdims=True)
        h = x * lax.rsqrt(ms + NORM_EPS) * ng_ref[...]
        h_ref[...] = h.astype(BF16)
        ht_ref[...] = jnp.transpose(h).astype(BF16)

    def fold(val, ref, dil, tok):
        rows = (tok.stop - tok.start) // dil
        for c in range(n_ch):
            stage_ref[c, tok, :] = val[:, c * LANES:(c + 1) * LANES]
        for r in range(dil):
            parts = [stage_ref[c, pl.ds(tok.start + r, rows, stride=dil), :] for c in range(n_ch)]
            ref[r, tok.start // dil:tok.stop // dil, :] = jnp.concatenate(parts, axis=1).astype(ref.dtype)

    def rope(v, low_only, tok):
        c = jnp.tile(cos_ref[tok, :], (1, n_ch))
        s1 = jnp.tile(s1_ref[tok, :], (1, n_ch))
        s2 = jnp.tile(s2_ref[tok, :], (1, n_ch))
        if low_only:
            low = lax.broadcasted_iota(I32, v.shape, 1) < HEAD_DIM
            c = jnp.where(low, c, 1.0)
            s1 = jnp.where(low, s1, 0.0)
            s2 = jnp.where(low, s2, 0.0)
        return v * c + pltpu.roll(v, TILE - ROT_HALF, 1) * s1 + pltpu.roll(v, ROT_HALF, 1) * s2

    def rope_t(v, tok):
        c = jnp.tile(cos_t_ref[:, tok], (N_HEADS, 1))
        s1 = jnp.tile(s1_t_ref[:, tok], (N_HEADS, 1))
        s2 = jnp.tile(s2_t_ref[:, tok], (N_HEADS, 1))
        return v * c + pltpu.roll(v, TILE - ROT_HALF, 0) * s1 + pltpu.roll(v, ROT_HALF, 0) * s2

    def epilogue(kind, y, tok):
        if kind == KIND_PLAIN:
            return y
        if kind == KIND_NORM_ROPE:
            y2 = (y * y).astype(BF16)
            ms = jnp.concatenate([_dot(y2[:, c * MXU_DIM:(c + 1) * MXU_DIM], bd_ref[...])
                                  for c in range(TILE // MXU_DIM)], axis=1)
            return rope(y * lax.rsqrt(ms + NORM_EPS) * gain_ref[0], False, tok)
        if kind == KIND_ROPE_LOW:
            return rope(y, True, tok)
        if kind == KIND_SILU:
            return y / (1.0 + jnp.exp(-y))
        assert kind == KIND_SIGMOID
        return 1.0 / (1.0 + jnp.exp(-y))

    def epilogue_t(kind, yt, tok):
        if kind == KIND_PLAIN:
            return yt
        gain = jnp.tile(gain_t_ref[0], (1, yt.shape[1] // LANES))
        if kind == KIND_NORM_ROPE:
            y2 = (yt * yt).astype(BF16)
            ms = jnp.concatenate([_dot(bd_ref[...], y2[c * MXU_DIM:(c + 1) * MXU_DIM, :])
                                  for c in range(TILE // MXU_DIM)], axis=0)
            return rope_t(yt * lax.rsqrt(ms + NORM_EPS) * gain, tok)
        assert kind == KIND_ROPE
        return rope_t(yt * gain, tok)

    route = route_ref[j]
    for kind, dest in _ROUTES:
        @pl.when(route == _route_code(kind, dest))
        def _(kind=kind, dest=dest):
            if dest == DEST_T:
                tok = slice(0, tm)
                tt_ref[...] = epilogue_t(kind, _dot(wt_ref[...], ht_ref[...]), tok).astype(tt_ref.dtype)
                return
            chunk = max(IN_CHUNK, _PACK16 * DIL_PATTERNS[2][1]) if dest == DEST_F2 else IN_CHUNK
            for m in range(tm // chunk):
                tok = slice(m * chunk, (m + 1) * chunk)
                val = epilogue(kind, _dot(h_ref[tok, :], w_ref[...]), tok)
                if dest == DEST_NAT:
                    o_ref[tok, :] = val.astype(o_ref.dtype)
                elif dest == DEST_F1:
                    fold(val, f1_ref, DIL_PATTERNS[1][1], tok)
                else:
                    fold(val, f2_ref, DIL_PATTERNS[2][1], tok)


def _inproj(x2, ng, w_nat, w_t, routes, gains, gains_t, tabs, tabs_t, bd, tm, B, S):
    T = x2.shape[0]
    grid = (T // tm, N_TILES)
    per_b = S // tm
    d1, d2 = DIL_PATTERNS[1][1], DIL_PATTERNS[2][1]
    half = N_FOLD_TILES // 2
    assert S % tm == 0 and tm % (_PACK16 * d2) == 0 and tm % IN_CHUNK == 0 and IN_CHUNK % (_PACK16 * d1) == 0
    n_w = J_T

    def fold_spec(dil, first):
        return pl.BlockSpec(
            (None, None, dil, tm // dil, TILE),
            lambda i, j, k: (jnp.clip(j - first, 0, half - 1), i // per_b, 0, i % per_b, 0))

    row_tab = pl.BlockSpec((tm, LANES), lambda i, j, k: (i, 0))
    col_tab = pl.BlockSpec((HEAD_DIM, tm), lambda i, j, k: (0, i))
    t_idx = lambda j: jnp.maximum(j - J_T, 0)
    return pl.pallas_call(
        _inproj_kernel,
        grid_spec=pltpu.PrefetchScalarGridSpec(
            num_scalar_prefetch=1,
            grid=grid,
            in_specs=[
                pl.BlockSpec((tm, D_MODEL), lambda i, j, k: (i, 0)),
                pl.BlockSpec((1, D_MODEL), lambda i, j, k: (0, 0)),
                pl.BlockSpec((None, D_MODEL, TILE), lambda i, j, k: (jnp.minimum(j, n_w - 1), 0, 0)),
                pl.BlockSpec((TILE, D_MODEL), lambda i, j, k: (t_idx(j), 0)),
                pl.BlockSpec((1, 1, TILE), lambda i, j, k: (jnp.minimum(j, n_w - 1), 0, 0)),
                pl.BlockSpec((1, TILE, LANES), lambda i, j, k: (t_idx(j), 0, 0)),
                row_tab, row_tab, row_tab, col_tab, col_tab, col_tab,
                pl.BlockSpec((MXU_DIM, MXU_DIM), lambda i, j, k: (0, 0)),
            ],
            out_specs=[
                pl.BlockSpec((None, tm, TILE), lambda i, j, k: (jnp.clip(j - J_NAT, 0, N_NAT_TILES - 1), i, 0)),
                fold_spec(d1, 0),
                fold_spec(d2, half),
                pl.BlockSpec((None, TILE, tm), lambda i, j, k: (i // per_b, t_idx(j), i % per_b)),
            ],
            scratch_shapes=[pltpu.VMEM((tm, D_MODEL), BF16), pltpu.VMEM((D_MODEL, tm), BF16),
                            pltpu.VMEM((TILE // LANES, tm, LANES), F32)],
        ),
        out_shape=[jax.ShapeDtypeStruct((N_NAT_TILES, T, TILE), BF16),
                   jax.ShapeDtypeStruct((half, B, d1, S // d1, TILE), BF16),
                   jax.ShapeDtypeStruct((half, B, d2, S // d2, TILE), BF16),
                   jax.ShapeDtypeStruct((B, N_T_TILES * TILE, S), BF16)],
        compiler_params=_params("arbitrary", "arbitrary"),
        name="inproj",
    )(routes, x2, ng, w_nat, w_t, gains, gains_t, *tabs, *tabs_t, bd)


def _attn_a_kernel(shift_ref, q_ref, kc_ref, kp_ref, vc_ref, vp_ref, o_ref, lse_ref, kcat, vtcat, ost, lst,
                   s_ref, p_ref, *, tq, dil):
    m = pl.program_id(1)
    nsub = tq // BAND
    n_ch = TILE // LANES
    nk = 2 * BAND
    key = lax.broadcasted_iota(I32, (nk, BAND), 0)
    qry = lax.broadcasted_iota(I32, (nk, BAND), 1) + BAND
    in_band = jnp.logical_and(key <= qry, key >= qry - BAND)
    lane = lax.broadcasted_iota(I32, (BAND, LANES), 1)
    half_mask = (jnp.where(lane < HEAD_DIM, 1.0, 0.0).astype(BF16),
                 jnp.where(lane < HEAD_DIM, 0.0, 1.0).astype(BF16))
    ones = jnp.ones((_PACK16, nk), BF16)
    shifted = shift_ref[0] <= MAX_SOFTMAX_SHIFT
    shift = jnp.where(shifted, shift_ref[0], 0.0)

    def residue(r, carry):
        kcat[0:BAND, :] = kp_ref[r]
        kcat[BAND:, :] = kc_ref[r]
        vtcat[:, 0:BAND] = jnp.transpose(vp_ref[r].astype(F32)).astype(BF16)
        for c in range(nsub):
            chunk = vc_ref[r, c * BAND:(c + 1) * BAND, :].astype(F32)
            vtcat[:, (c + 1) * BAND:(c + 2) * BAND] = jnp.transpose(chunk).astype(BF16)

        def body(u, carry):
            r0 = pl.multiple_of(u * BAND, BAND)
            q = q_ref[r, pl.ds(r0, BAND), :]
            k2 = kcat[pl.ds(r0, nk), :]
            valid = jnp.logical_and(in_band, key >= jnp.where(m * nsub + u > 0, 0, BAND))
            bias = jnp.where(valid, -shift, NEG_INF)

            def score(h):
                pair = slice((h // 2) * LANES, (h // 2 + 1) * LANES)
                q_h = q[:, pair] * half_mask[h % 2]
                return _dot_nt(k2[:, pair], q_h) + bias

            def pv_of(h, p):
                vt_h = vtcat[h * HEAD_DIM:(h + 1) * HEAD_DIM, pl.ds(r0, nk)]
                pv = _dot(jnp.concatenate([vt_h, ones], axis=0), p)
                return pv[:HEAD_DIM], pv[HEAD_DIM:HEAD_DIM + 1]

            def block(use_shift):
                outs, lses = [], []
                if use_shift:
                    for h in range(N_HEADS):
                        p_ref[h] = jnp.exp2(score(h)).astype(BF16)
                    for h in range(N_HEADS):
                        acc, den = pv_of(h, p_ref[h])
                        outs.append(acc / den)
                        lses.append(shift + jnp.log2(den))
                else:
                    maxes = []
                    for h in range(N_HEADS):
                        s = score(h)
                        s_ref[h] = s
                        maxes.append(jnp.max(s, axis=0, keepdims=True))
                    for h in range(N_HEADS):
                        acc, den = pv_of(h, jnp.exp2(s_ref[h] - maxes[h]).astype(BF16))
                        outs.append(acc / den)
                        lses.append(maxes[h] + jnp.log2(den))
                o = jnp.transpose(jnp.concatenate(outs, axis=0))
                lse8 = jnp.concatenate(lses, axis=0)
                lse = jnp.transpose(jnp.tile(lse8, (LANES // N_HEADS, 1)))
                rows = pl.ds(r + u * (BAND * dil), BAND, stride=dil) if dil > 1 else pl.ds(r0, BAND)
                for c in range(n_ch):
                    ost[c, rows, :] = o[:, c * LANES:(c + 1) * LANES]
                lst[rows, :] = lse

            @pl.when(shifted)
            def _():
                block(True)

            @pl.when(jnp.logical_not(shifted))
            def _():
                block(False)

            return carry

        return lax.fori_loop(0, nsub, body, carry)

    lax.fori_loop(0, dil, residue, 0)
    o_ref[0] = jnp.concatenate([ost[c] for c in range(n_ch)], axis=1).astype(o_ref.dtype)
    lse_ref[0] = lst[...]


def _attn_a(qkv, shift, g, dil, B, S):
    ts = min(A_STEP_TOKENS, S)
    tq = ts // dil
    assert S % ts == 0 and tq % BAND == 0
    sub = tq // BAND

    def spec_cur(t):
        return pl.BlockSpec((None, None, dil, tq, TILE), lambda b, m: (t, b, 0, m, 0))

    def spec_prev(t):
        return pl.BlockSpec((None, None, dil, BAND, TILE),
                            lambda b, m: (t, b, 0, jnp.maximum(m * sub - 1, 0), 0))

    n_ch = TILE // LANES
    return pl.pallas_call(
        functools.partial(_attn_a_kernel, tq=tq, dil=dil),
        grid=(B, S // ts),
        in_specs=[pl.BlockSpec(memory_space=pltpu.SMEM),
                  spec_cur(0), spec_cur(1), spec_prev(1), spec_cur(2), spec_prev(2)],
        out_specs=[pl.BlockSpec((1, ts, TILE), lambda b, m: (b, m, 0)),
                   pl.BlockSpec((1, ts, LANES), lambda b, m: (b, m, 0))],
        out_shape=[jax.ShapeDtypeStruct((B, S, TILE), BF16), jax.ShapeDtypeStruct((B, S, LANES), F32)],
        scratch_shapes=[pltpu.VMEM((tq + BAND, TILE), BF16), pltpu.VMEM((TILE, tq + BAND), BF16),
                        pltpu.VMEM((n_ch, ts, LANES), F32), pltpu.VMEM((ts, LANES), F32),
                        pltpu.VMEM((N_HEADS, 2 * BAND, BAND), F32),
                        pltpu.VMEM((N_HEADS, 2 * BAND, BAND), BF16)],
        compiler_params=_params("arbitrary", "arbitrary"),
        name=f"attn_a{g}",
    )(shift, qkv, qkv, qkv, qkv, qkv)


def _pad_q(qt_ref, qpad_ref):
    qpad_ref[...] = jnp.zeros(qpad_ref.shape, qpad_ref.dtype)
    for h in range(N_HEADS):
        r0 = h * LANES + (h % 2) * HEAD_DIM
        qpad_ref[r0:r0 + HEAD_DIM, :] = qt_ref[h * HEAD_DIM:(h + 1) * HEAD_DIM, :]


def _flash_init(m_ref, l_ref, acc_ref):
    m_ref[...] = jnp.full(m_ref.shape, NEG_INF, F32)
    l_ref[...] = jnp.zeros(l_ref.shape, F32)
    acc_ref[...] = jnp.zeros(acc_ref.shape, F32)


def _flash_tile_step(score_of_head, vt_of_head, s_ref, mn_ref, m_ref, l_ref, acc_ref):
    tk = s_ref.shape[1]
    for h in range(N_HEADS):
        s = score_of_head(h)
        s_ref[h] = s
        mn_ref[h] = jnp.maximum(m_ref[h], jnp.max(s, axis=0, keepdims=True))
    ones = jnp.ones((_PACK16, tk), BF16)
    for h in range(N_HEADS):
        m_prev, m_next = m_ref[h], mn_ref[h]
        p = jnp.exp2(s_ref[h] - jnp.tile(m_next, (tk // SUBLANES, 1)))
        alpha = jnp.exp2(m_prev - m_next)
        pv = _dot(jnp.concatenate([vt_of_head(h), ones], axis=0), p.astype(BF16))
        l_ref[h] = alpha * l_ref[h] + pv[HEAD_DIM:HEAD_DIM + SUBLANES]
        acc_ref[h] = acc_ref[h] * jnp.tile(alpha, (HEAD_DIM // SUBLANES, 1)) + pv[:HEAD_DIM]
        m_ref[h] = m_next


def _flash_tile_step_shifted(score_of_head, vt_of_head, p_ref, l_ref, acc_ref):
    tk = p_ref.shape[1]
    for h in range(N_HEADS):
        p_ref[h] = jnp.exp2(score_of_head(h)).astype(BF16)
    ones = jnp.ones((_PACK16, tk), BF16)
    for h in range(N_HEADS):
        pv = _dot(jnp.concatenate([vt_of_head(h), ones], axis=0), p_ref[h])
        l_ref[h] = l_ref[h] + pv[HEAD_DIM:HEAD_DIM + SUBLANES]
        acc_ref[h] = acc_ref[h] + pv[:HEAD_DIM]


def _flash_finish(o_ref, l_ref, acc_ref):
    outs = [acc_ref[h] / jnp.tile(l_ref[h], (HEAD_DIM // SUBLANES, 1)) for h in range(N_HEADS)]
    o_ref[...] = jnp.transpose(jnp.concatenate(outs, axis=0)).astype(o_ref.dtype)


def _dsa_kernel(shift_ref, qt_ref, k_ref, vt_ref, iqt_ref, ikw_q_ref, ikw_ref, o_ref,
                key_ref, hi_ref, lo_ref, qpad_ref, bias_ref, p_ref, s_ref, mn_ref, m_ref, l_ref, acc_ref,
                *, tq, tk, topk):
    qi = pl.program_id(1)
    n_kt = qi + 1
    n_acc = 4
    rows8 = tk // SUBLANES

    w8 = jnp.transpose(ikw_q_ref[...].astype(F32))[HEAD_DIM:HEAD_DIM + IDX_HEADS, :] * (IDX_HEADS ** -0.5)
    krow = lax.broadcasted_iota(I32, (tk, tq), 0)
    qcol = lax.broadcasted_iota(I32, (tk, tq), 1)

    def score_tile(c, diagonal):
        c0 = pl.multiple_of(c * tk, tk)
        kx = ikw_ref[pl.ds(c0, tk), :][:, :IDX_DIM]
        sc = jnp.zeros((tk, tq), F32)
        for h in range(IDX_HEADS):
            lg = _dot(kx, iqt_ref[h * IDX_DIM:(h + 1) * IDX_DIM, :])
            sc = sc + w8[h:h + 1, :] * jnp.maximum(lg, 0.0)
        sc = jnp.where(sc == 0.0, 0.0, sc)
        if diagonal:
            sc = jnp.where(krow <= qcol, sc, NEG_INF)
        bits = pltpu.bitcast(sc, I32)
        key = jnp.where(bits < 0, bits ^ 0x7FFFFFFF, bits)
        key_ref[pl.ds(c0, tk), :] = key
        hi_ref[pl.ds(c0, tk), :] = jnp.right_shift(key, 16).astype(I16)
        lo_ref[pl.ds(c0, tk), :] = ((key & 0xFFFF) - _HALF16).astype(I16)

    def score_body(c, carry):
        score_tile(c, False)
        return carry

    lax.fori_loop(0, qi, score_body, 0)
    score_tile(qi, True)

    rows16 = tk // _PACK16
    one16 = jnp.ones((_PACK16, tq), I16)
    zero16 = jnp.zeros((_PACK16, tq), I16)

    def count16(ref, pred):
        def cbody(c, accs):
            accs = list(accs)
            c0 = pl.multiple_of(c * tk, tk)
            t = ref[pl.ds(c0, tk), :]
            for g in range(rows16):
                hit = jnp.where(pred(t[g * _PACK16:(g + 1) * _PACK16]), one16, zero16)
                accs[g % n_acc] = accs[g % n_acc] + hit
            return tuple(accs)
        accs = lax.fori_loop(0, n_kt, cbody, tuple(zero16 for _ in range(n_acc)))
        return jnp.sum(functools.reduce(lambda a, b: a + b, accs).astype(I32), axis=0, keepdims=True)

    def as16(v):
        return jnp.broadcast_to(v, (_PACK16, tq)).astype(I16)

    def select16(ref, need, cge0):
        def bit_body(it, carry):
            ans, cge = carry
            cand_u = ans | lax.shift_left(jnp.int32(1), 15 - it)
            cand = as16(cand_u - _HALF16)
            cnt = count16(ref, lambda t: t >= cand)
            ok = cnt >= need
            return jnp.where(ok, cand_u, ans), jnp.where(ok, cnt, cge)
        return lax.fori_loop(0, 16, bit_body, (jnp.zeros((1, tq), I32), cge0))

    n_all = jnp.zeros((1, tq), I32) + n_kt * tk
    p_u, cge_hi = select16(hi_ref, topk, n_all)
    p16 = as16(p_u - _HALF16)
    c_gt = count16(hi_ref, lambda t: t > p16)

    def bucket_body(c, carry):
        c0 = pl.multiple_of(c * tk, tk)
        lo_ref[pl.ds(c0, tk), :] = jnp.where(hi_ref[pl.ds(c0, tk), :] == jnp.tile(p16, (rows16, 1)),
                                             lo_ref[pl.ds(c0, tk), :], jnp.int16(-_HALF16))
        return carry

    lax.fori_loop(0, n_kt, bucket_body, 0)
    l_u, cge_lo = select16(lo_ref, topk - c_gt, cge_hi - c_gt)
    cge = c_gt + cge_lo
    thr = jnp.maximum((p_u - _HALF16) * 65536 + l_u, _FLOOR_KEY)

    def count_rows(fn, n_out):
        def cbody(c, accs):
            accs = [list(a) for a in accs]
            c0 = pl.multiple_of(c * tk, tk)
            kt = key_ref[pl.ds(c0, tk), :]
            for g in range(rows8):
                r0 = c0 + g * SUBLANES
                vals = fn(kt[g * SUBLANES:(g + 1) * SUBLANES], r0)
                for o in range(n_out):
                    accs[o][g % n_acc] = accs[o][g % n_acc] + vals[o]
            return tuple(tuple(a) for a in accs)
        z = jnp.zeros((SUBLANES, tq), I32)
        accs = lax.fori_loop(0, n_kt, cbody, tuple(tuple(z for _ in range(n_acc)) for _ in range(n_out)))
        return [jnp.sum(functools.reduce(lambda a, b: a + b, a), axis=0, keepdims=True) for a in accs]

    tie = jnp.logical_and(cge > topk, thr > _FLOOR_KEY)
    any_tie = jnp.max(jnp.where(tie, 1, 0)) > 0
    thr8 = jnp.broadcast_to(thr, (SUBLANES, tq))

    @pl.when(any_tie)
    def _():
        n_bits = int(np.log2(key_ref.shape[0]))
        sub = lax.broadcasted_iota(I32, (SUBLANES, tq), 0)

        def jbody(it, lo):
            cand = lo + lax.shift_left(jnp.int32(1), n_bits - 1 - it)
            pos = jnp.broadcast_to(cand - 1, (SUBLANES, tq))

            def f(kk, r0):
                eq = jnp.logical_and(kk == thr8, sub + r0 <= pos)
                return jnp.where(kk > thr8, 1, 0), jnp.where(eq, 1, 0)

            gt, eq = count_rows(f, 2)
            return jnp.where(gt + eq >= topk, lo, cand)

        jrow = jnp.broadcast_to(lax.fori_loop(0, n_bits, jbody, jnp.zeros((1, tq), I32)), (SUBLANES, tq))
        tie8 = jnp.broadcast_to(jnp.where(tie, 1, 0), (SUBLANES, tq)) > 0

        def fix(c, carry):
            for g in range(rows8):
                r0 = pl.multiple_of(c * tk + g * SUBLANES, SUBLANES)
                kk = key_ref[pl.ds(r0, SUBLANES), :]
                drop = jnp.logical_and(tie8, jnp.logical_and(kk == thr8, sub + r0 > jrow))
                key_ref[pl.ds(r0, SUBLANES), :] = jnp.where(drop, _INT_MIN, kk)
            return carry

        lax.fori_loop(0, n_kt, fix, 0)

    _pad_q(qt_ref, qpad_ref)
    _flash_init(m_ref, l_ref, acc_ref)
    shift = shift_ref[0]

    def attend(shifted):
        def att_body(j, carry):
            c0 = pl.multiple_of(j * tk, tk)
            sel = key_ref[pl.ds(c0, tk), :] >= thr
            bias_ref[...] = jnp.where(sel, -shift if shifted else 0.0, NEG_INF)
            k = k_ref[pl.ds(c0, tk), :]

            def score(h):
                k_pair = k[:, (h // 2) * LANES:(h // 2 + 1) * LANES]
                return _dot(k_pair, qpad_ref[h * LANES:(h + 1) * LANES, :]) + bias_ref[...]

            vt_of_head = lambda h: vt_ref[h * HEAD_DIM:(h + 1) * HEAD_DIM, pl.ds(c0, tk)]
            if shifted:
                _flash_tile_step_shifted(score, vt_of_head, p_ref, l_ref, acc_ref)
            else:
                _flash_tile_step(score, vt_of_head, s_ref, mn_ref, m_ref, l_ref, acc_ref)
            return carry

        lax.fori_loop(0, n_kt, att_body, 0)

    @pl.when(shift <= MAX_SOFTMAX_SHIFT)
    def _():
        attend(True)

    @pl.when(shift > MAX_SOFTMAX_SHIFT)
    def _():
        attend(False)

    _flash_finish(o_ref, l_ref, acc_ref)


def _flash_scratch(tk, tq):
    stat = pltpu.VMEM((N_HEADS, SUBLANES, tq), F32)
    return [pltpu.VMEM((N_HEADS, tk, tq), BF16), pltpu.VMEM((N_HEADS, tk, tq), F32), stat, stat, stat,
            pltpu.VMEM((N_HEADS, HEAD_DIM, tq), F32)]


def _dsa(P4, TT, shift, B, S):
    tq = tk = SPARSE_TILE
    topk = min(IDX_TOPK_MAX, S // 4)
    assert S % tq == 0 and topk <= tk
    def tt_q(t):
        return pl.BlockSpec((None, TILE, tq), lambda b, i: (b, t, i))

    return pl.pallas_call(
        functools.partial(_dsa_kernel, tq=tq, tk=tk, topk=topk),
        grid=(B, S // tq),
        in_specs=[pl.BlockSpec(memory_space=pltpu.SMEM),
                  tt_q(TT_QB),
                  pl.BlockSpec((None, None, S, TILE), lambda b, i: (T_KB, b, 0, 0)),
                  pl.BlockSpec((None, TILE, S), lambda b, i: (b, TT_VB, 0)),
                  tt_q(TT_IQ),
                  pl.BlockSpec((None, None, tq, LANES), lambda b, i: (T_IKW, b, i, 0)),
                  pl.BlockSpec((None, None, S, LANES), lambda b, i: (T_IKW, b, 0, 0))],
        out_specs=pl.BlockSpec((None, tq, TILE), lambda b, i: (b, i, 0)),
        out_shape=jax.ShapeDtypeStruct((B, S, TILE), BF16),
        scratch_shapes=[pltpu.VMEM((S, tq), I32), pltpu.VMEM((S, tq), I16), pltpu.VMEM((S, tq), I16),
                        pltpu.VMEM((N_HEADS * LANES, tq), BF16),
                        pltpu.VMEM((tk, tq), F32)] + _flash_scratch(tk, tq),
        compiler_params=_params("arbitrary", "arbitrary"),
        name="dsa",
    )(shift, TT, P4, TT, TT, P4, P4)


def _moba_kernel(shift_ref, qt_ref, k_ref, vt_ref, o_ref, kmh_ref, kml_ref, qaug_ref,
                 p_ref, s_ref, mn_ref, m_ref, l_ref, acc_ref, *, tq, topb):
    qi = pl.program_id(1)
    S = k_ref.shape[0]
    gl = N_HEADS * MOBA_SLOTS

    @pl.when(qi == 0)
    def _():
        blk_row = jnp.right_shift(lax.broadcasted_iota(I32, (gl, S), 0), _LOG2_N_HEADS)
        blk_col = jnp.right_shift(lax.broadcasted_iota(I32, (gl, S), 1), _LOG2_MOBA_BLOCK)
        avg = jnp.where(blk_row == blk_col, 1.0 / MOBA_BLOCK, 0.0).astype(BF16)
        km = _dot(avg, k_ref[...])
        r_head = lax.broadcasted_iota(I32, (gl, TILE), 0) & (N_HEADS - 1)
        c_head = jnp.right_shift(lax.broadcasted_iota(I32, (gl, TILE), 1), _LOG2_HEAD_DIM)
        km = jnp.where(r_head == c_head, km, 0.0)
        hi = km.astype(BF16)
        kmh_ref[...] = hi
        kml_ref[...] = (km - hi.astype(F32)).astype(BF16)

    shifted = shift_ref[0] <= MAX_SOFTMAX_SHIFT
    shift = jnp.where(shifted, shift_ref[0], 0.0)
    qt = qt_ref[...]
    gate = _dot(kmh_ref[...], qt) + _dot(kml_ref[...], qt)
    row = lax.broadcasted_iota(I32, (gl, tq), 0)
    gate = jnp.where(jnp.right_shift(row, _LOG2_N_HEADS) < qi, gate, NEG_INF)
    g = [gate[n * N_HEADS:(n + 1) * N_HEADS] for n in range(MOBA_SLOTS)]
    biases = []
    for n in range(MOBA_SLOTS):
        beaten = jnp.where(n < qi, 0, topb) + jnp.zeros((N_HEADS, tq), I32)
        for n2 in range(MOBA_SLOTS):
            if n2 != n:
                beaten = beaten + jnp.where((g[n2] >= g[n]) if n2 < n else (g[n2] > g[n]), 1, 0)
        biases.append(jnp.where(beaten < topb, -shift, NEG_INF))
    selb = jnp.concatenate(biases, axis=0)

    qaug_ref[...] = jnp.zeros(qaug_ref.shape, qaug_ref.dtype)
    for h in range(N_HEADS):
        r0 = (h % 2) * HEAD_DIM
        qaug_ref[h, r0:r0 + HEAD_DIM, :] = qt_ref[h * HEAD_DIM:(h + 1) * HEAD_DIM, :]
        qaug_ref[h, LANES:, :] = jnp.where((row & (N_HEADS - 1)) == h, selb, 0.0).astype(BF16)
    _flash_init(m_ref, l_ref, acc_ref)
    lane_blk = jnp.right_shift(lax.broadcasted_iota(I32, (MOBA_BLOCK, LANES), 1), _LOG2_N_HEADS)

    def vt_of_block(n):
        c0 = pl.multiple_of(n * MOBA_BLOCK, MOBA_BLOCK)
        return lambda h: vt_ref[h * HEAD_DIM:(h + 1) * HEAD_DIM, pl.ds(c0, MOBA_BLOCK)]

    def past_score(n):
        k = k_ref[pl.ds(pl.multiple_of(n * MOBA_BLOCK, MOBA_BLOCK), MOBA_BLOCK), :]
        onehot = jnp.where(lane_blk == n, 1.0, 0.0).astype(BF16)

        def score(h):
            k_aug = jnp.concatenate([k[:, (h // 2) * LANES:(h // 2 + 1) * LANES], onehot], axis=1)
            return _dot(k_aug, qaug_ref[h])

        return score

    def own_score():
        k = k_ref[pl.ds(pl.multiple_of(qi * MOBA_BLOCK, MOBA_BLOCK), MOBA_BLOCK), :]
        causal = jnp.where(lax.broadcasted_iota(I32, (MOBA_BLOCK, tq), 0)
                           <= lax.broadcasted_iota(I32, (MOBA_BLOCK, tq), 1), -shift, NEG_INF)
        return lambda h: _dot(k[:, (h // 2) * LANES:(h // 2 + 1) * LANES], qaug_ref[h, :LANES, :]) + causal

    @pl.when(shifted)
    def _():
        def att_body(n, carry):
            _flash_tile_step_shifted(past_score(n), vt_of_block(n), p_ref, l_ref, acc_ref)
            return carry

        lax.fori_loop(0, qi, att_body, 0)
        _flash_tile_step_shifted(own_score(), vt_of_block(qi), p_ref, l_ref, acc_ref)

    @pl.when(jnp.logical_not(shifted))
    def _():
        def att_body(n, carry):
            _flash_tile_step(past_score(n), vt_of_block(n), s_ref, mn_ref, m_ref, l_ref, acc_ref)
            return carry

        lax.fori_loop(0, qi, att_body, 0)
        _flash_tile_step(own_score(), vt_of_block(qi), s_ref, mn_ref, m_ref, l_ref, acc_ref)

    _flash_finish(o_ref, l_ref, acc_ref)


def _moba(P4, TT, shift, B, S):
    tq = MOBA_BLOCK
    nblk = S // MOBA_BLOCK
    assert S % MOBA_BLOCK == 0 and nblk <= MOBA_SLOTS
    topb = min(MOBA_TOPK, nblk - 1)
    gl = N_HEADS * MOBA_SLOTS
    return pl.pallas_call(
        functools.partial(_moba_kernel, tq=tq, topb=topb),
        grid=(B, S // tq),
        in_specs=[pl.BlockSpec(memory_space=pltpu.SMEM),
                  pl.BlockSpec((None, TILE, tq), lambda b, i: (b, TT_QC, i)),
                  pl.BlockSpec((None, None, S, TILE), lambda b, i: (T_KC, b, 0, 0)),
                  pl.BlockSpec((None, TILE, S), lambda b, i: (b, TT_VC, 0))],
        out_specs=pl.BlockSpec((None, tq, TILE), lambda b, i: (b, i, 0)),
        out_shape=jax.ShapeDtypeStruct((B, S, TILE), BF16),
        scratch_shapes=[pltpu.VMEM((gl, TILE), BF16), pltpu.VMEM((gl, TILE), BF16),
                        pltpu.VMEM((N_HEADS, 2 * LANES, tq), BF16)]
                       + _flash_scratch(MOBA_BLOCK, tq),
        compiler_params=_params("arbitrary", "arbitrary"),
        name="moba",
    )(shift, TT, P4, TT)


def _post_kernel(x_ref, oa0, oa1, oa2, la0, la1, la2, ob_ref, oc_ref, z0, z1, z2, g0, g1, g2,
                 wbr_ref, wout_ref, expand_ref, out_ref):
    l0, l1, l2 = la0[...], la1[...], la2[...]
    mx = jnp.maximum(jnp.maximum(l0, l1), l2)
    e0, e1, e2 = jnp.exp2(l0 - mx), jnp.exp2(l1 - mx), jnp.exp2(l2 - mx)
    den = e0 + e1 + e2

    def spread(w):
        hi = w.astype(BF16)
        lo = (w - hi.astype(F32)).astype(BF16)
        return _dot(hi, expand_ref[...]) + _dot(lo, expand_ref[...])

    o_a = (spread(e0 / den) * oa0[...].astype(F32) + spread(e1 / den) * oa1[...].astype(F32)
           + spread(e2 / den) * oa2[...].astype(F32))
    branches = (o_a, ob_ref[...].astype(F32), oc_ref[...].astype(F32))
    merged = jnp.zeros(out_ref.shape, F32)
    for n, (o, z, g) in enumerate(zip(branches, (z0, z1, z2), (g0, g1, g2))):
        y = _dot((o * z[0].astype(F32)).astype(BF16), wbr_ref[n])
        gate = jnp.concatenate([g[t] for t in range(g.shape[0])], axis=1)
        merged = merged + gate.astype(F32) * y
    out_ref[...] = x_ref[...] + _dot(merged.astype(BF16), wout_ref[...])


def _post(x2, oa, la, ob, oc, P, wbr, wout, tm):
    T = x2.shape[0]
    row = lambda width, t: pl.BlockSpec((tm, width), lambda i: (i, t))
    tiles = lambda count, first: pl.BlockSpec((count, tm, TILE), lambda i: (first // count, i, 0))
    per_g = D_MODEL // TILE
    assert T_G % per_g == 0
    head_of = np.arange(TILE) // HEAD_DIM
    expand = jnp.asarray((np.arange(LANES)[:, None] == head_of[None, :]).astype(np.float32), BF16)
    in_specs = ([row(D_MODEL, 0)] + [row(TILE, 0)] * 3 + [row(LANES, 0)] * 3 + [row(TILE, 0)] * 2
                + [tiles(1, T_Z + n) for n in range(N_BRANCH)]
                + [tiles(per_g, T_G + per_g * n) for n in range(N_BRANCH)]
                + [pl.BlockSpec((N_BRANCH, BRANCH_WIDTH, D_MODEL), lambda i: (0, 0, 0)),
                   pl.BlockSpec((D_MODEL, D_MODEL), lambda i: (0, 0)),
                   pl.BlockSpec((LANES, TILE), lambda i: (0, 0))])
    return pl.pallas_call(
        _post_kernel,
        grid=(T // tm,),
        in_specs=in_specs,
        out_specs=row(D_MODEL, 0),
        out_shape=jax.ShapeDtypeStruct((T, D_MODEL), F32),
        compiler_params=_params("arbitrary"),
        name="post",
    )(x2, oa[0], oa[1], oa[2], la[0], la[1], la[2], ob, oc, P, P, P, P, P, P, wbr, wout, expand)


def _rearrange_w_in(w):
    bw = BRANCH_WIDTH
    a_q, a_k, a_v = w[:, 0:3 * bw], w[:, 3 * bw:6 * bw], w[:, 6 * bw:9 * bw]
    off = 9 * bw
    b_q, b_k, b_v = (w[:, off + i * bw:off + (i + 1) * bw] for i in range(3)); off += 3 * bw
    iq = w[:, off:off + IDX_HEADS * IDX_DIM]; off += IDX_HEADS * IDX_DIM
    ik = w[:, off:off + IDX_DIM]; off += IDX_DIM
    iw = w[:, off:off + IDX_HEADS]; off += IDX_HEADS
    c_q, c_k, c_v = (w[:, off + i * bw:off + (i + 1) * bw] for i in range(3)); off += 3 * bw
    z = w[:, off:off + 3 * bw]; off += 3 * bw
    g = w[:, off:off + 3 * D_MODEL]; off += 3 * D_MODEL
    assert off == w.shape[1]
    cols = []
    for grp in (1, 2, 0):
        sl = slice(grp * bw, (grp + 1) * bw)
        cols += [a_q[:, sl], a_k[:, sl], a_v[:, sl]]
    pad = jnp.zeros((w.shape[0], TILE - IDX_DIM - IDX_HEADS), w.dtype)
    cols += [b_k, ik, iw, pad, c_k, g, z]
    w_nat = jnp.concatenate(cols, axis=1).astype(BF16)
    assert w_nat.shape[1] == J_T * TILE
    w_nat = w_nat.reshape(-1, J_T, TILE).transpose(1, 0, 2)
    w_t = jnp.concatenate([b_q, b_v, iq, c_q, c_v], axis=1).T.astype(BF16)
    assert w_t.shape[0] == N_T_TILES * TILE
    return w_nat, w_t


def _tile_tables(qk_g):
    kinds = np.zeros((N_TILES,), np.int32)
    scale = np.array([HEAD_DIM ** -0.5 * LOG2_E, 1.0], np.float32)
    rows = jnp.tile(qk_g.astype(F32) * scale[None, :, None], (1, 1, N_HEADS)).reshape(2 * N_BRANCH, TILE)
    rows = jnp.concatenate([rows, jnp.full((1, TILE), IDX_DIM ** -0.5, F32), jnp.ones((1, TILE), F32)])
    row_iq, row_one = 2 * N_BRANCH, 2 * N_BRANCH + 1
    which = np.full((N_TILES,), row_one, np.int32)

    def qk(tile, mixer):
        kinds[tile] = kinds[tile + 1] = KIND_NORM_ROPE
        which[tile], which[tile + 1] = 2 * mixer, 2 * mixer + 1

    qk(0, 0)
    qk(N_FOLD_TILES // 2, 0)
    qk(J_NAT + T_A, 0)
    for tile, mixer in ((T_KB, 1), (T_KC, 2)):
        kinds[J_NAT + tile] = KIND_NORM_ROPE
        which[J_NAT + tile] = 2 * mixer + 1
    kinds[J_NAT + T_IKW] = KIND_ROPE_LOW
    kinds[J_NAT + T_Z:J_NAT + T_Z + 3] = KIND_SILU
    kinds[J_NAT + T_G:J_NAT + T_G + 6] = KIND_SIGMOID
    for tile, mixer in ((TT_QB, 1), (TT_QC, 2)):
        kinds[J_T + tile] = KIND_NORM_ROPE
        which[J_T + tile] = 2 * mixer
    kinds[J_T + TT_IQ] = KIND_ROPE
    which[J_T + TT_IQ] = row_iq
    gains = rows[which[:J_T]][:, None, :]
    gains_t = jnp.broadcast_to(rows[which[J_T:]][:, :, None], (N_T_TILES, TILE, LANES))
    half = N_FOLD_TILES // 2
    dest = np.array([DEST_F1] * half + [DEST_F2] * half + [DEST_NAT] * N_NAT_TILES + [DEST_T] * N_T_TILES)
    assert all((k, d) in _ROUTES for k, d in zip(kinds.tolist(), dest.tolist()))
    routes = _route_code(kinds, dest).astype(np.int32)
    return jnp.asarray(routes), gains, gains_t


def _softmax_shift(qk_gain):
    bound = HEAD_DIM * jnp.max(jnp.abs(qk_gain[0])) * jnp.max(jnp.abs(qk_gain[1]))
    return (SHIFT_SLACK * HEAD_DIM ** -0.5 * LOG2_E * bound).reshape(1).astype(BF16).astype(F32)


def _rope_tables(positions):
    inv = ROPE_THETA ** (-jnp.arange(0, ROT_DIM, 2, dtype=F32) / ROT_DIM)
    ang = positions.astype(F32).reshape(-1)[:, None] * inv
    cos, sin = jnp.cos(ang), jnp.sin(ang)
    T = cos.shape[0]
    z8 = jnp.zeros((T, ROT_HALF), F32)
    rest1 = jnp.ones((T, HEAD_DIM - ROT_DIM), F32)
    rest0 = jnp.zeros((T, HEAD_DIM - ROT_DIM), F32)
    c = jnp.concatenate([cos, cos, rest1], axis=1)
    s1 = jnp.concatenate([-sin, z8, rest0], axis=1)
    s2 = jnp.concatenate([z8, sin, rest0], axis=1)
    two = lambda t: jnp.concatenate([t, t], axis=1)
    return (two(c), two(s1), two(s2)), (c.T, s1.T, s2.T)


def _block_diag_mean():
    h = np.arange(MXU_DIM) // HEAD_DIM
    return jnp.asarray((h[:, None] == h[None, :]).astype(np.float32) / HEAD_DIM, BF16)


def _layer(x2, B, S, tabs, tabs_t, bd, norm_g, w_in, qk_g, w_br, w_out, tm_in, tm_post):
    routes, gains, gains_t = _tile_tables(qk_g)
    w_nat, w_t = _rearrange_w_in(w_in)
    P, f1, f2, TT = _inproj(x2, norm_g[None, :], w_nat, w_t, routes, gains, gains_t, tabs, tabs_t, bd,
                            tm_in, B, S)
    P4 = P.reshape(N_NAT_TILES, B, S, TILE)
    oa, la = [], []
    for g, qkv in enumerate((P4.reshape(N_NAT_TILES, B, 1, S, TILE), f1, f2)):
        o, lse = _attn_a(qkv, _softmax_shift(qk_g[0]), g, DIL_PATTERNS[g][1], B, S)
        oa.append(o.reshape(B * S, TILE))
        la.append(lse.reshape(B * S, LANES))
    ob = _dsa(P4, TT, _softmax_shift(qk_g[1]), B, S).reshape(B * S, TILE)
    oc = _moba(P4, TT, _softmax_shift(qk_g[2]), B, S).reshape(B * S, TILE)
    return _post(x2, oa, la, ob, oc, P, w_br.astype(BF16), w_out.astype(BF16), tm_post)


def _forward(x, positions, norm_g, w_in, qk_g, w_br, w_out, tm_in=IN_ROW_TILE, tm_post=POST_ROW_TILE):
    B, S, D = x.shape
    tabs, tabs_t = _rope_tables(positions)
    bd = _block_diag_mean()
    x2 = x.reshape(B * S, D)
    for layer in range(norm_g.shape[0]):
        x2 = _layer(x2, B, S, tabs, tabs_t, bd, norm_g[layer], w_in[layer], qk_g[layer],
                    w_br[layer], w_out[layer], tm_in, tm_post)
    return x2.reshape(B, S, D)


def kernel(x, positions, norm_g, w_in, qk_g, w_br, w_out):
    return _forward(x, positions, norm_g, w_in, qk_g, w_br, w_out)
```

```python
import functools

import jax
import jax.numpy as jnp
import numpy as np
from jax import lax
from jax.experimental import pallas as pl
from jax.experimental.pallas import tpu as pltpu

F32 = jnp.float32
BF16 = jnp.bfloat16
I32 = jnp.int32
I16 = jnp.int16

D_MODEL = 1024
HEAD_DIM = 64
ROT_DIM = HEAD_DIM // 4
ROT_HALF = ROT_DIM // 2
ROPE_THETA = 500000.0
NORM_EPS = 1e-6
NEG_INF = -1e30
LOG2_E = 1.4426950408889634
MAX_SOFTMAX_SHIFT = 60.0
SHIFT_SLACK = 1.05
N_HEADS = 8
BRANCH_WIDTH = N_HEADS * HEAD_DIM
N_BRANCH = 3
DIL_PATTERNS = ((128, 1), (512, 4), (2048, 16))
BAND = 128
IDX_HEADS = 8
IDX_DIM = 64
IDX_TOPK_MAX = 256
MOBA_BLOCK = 256
MOBA_TOPK = 3
MOBA_SLOTS = 16

LANES = 128
SUBLANES = 8
MXU_DIM = 256
VMEM_LIMIT_BYTES = 56 * 1024 * 1024

TILE = BRANCH_WIDTH
N_FOLD_TILES = 6
T_A = 0
T_KB = 3
T_IKW = 4
T_KC = 5
T_G = 6
T_Z = 12
N_NAT_TILES = 15
TT_QB, TT_VB, TT_IQ, TT_QC, TT_VC = range(5)
N_T_TILES = 5
J_NAT = N_FOLD_TILES
J_T = N_FOLD_TILES + N_NAT_TILES
N_TILES = J_T + N_T_TILES
IN_ROW_TILE = 2048
IN_CHUNK = 256
A_STEP_TOKENS = 2048
SPARSE_TILE = 256
POST_ROW_TILE = 512

KIND_PLAIN, KIND_NORM_ROPE, KIND_ROPE, KIND_ROPE_LOW, KIND_SILU, KIND_SIGMOID = range(6)
DEST_NAT, DEST_F1, DEST_F2, DEST_T = range(4)
_ROUTES = ((KIND_PLAIN, DEST_NAT), (KIND_PLAIN, DEST_F1), (KIND_PLAIN, DEST_F2),
           (KIND_NORM_ROPE, DEST_NAT), (KIND_NORM_ROPE, DEST_F1), (KIND_NORM_ROPE, DEST_F2),
           (KIND_ROPE_LOW, DEST_NAT), (KIND_SILU, DEST_NAT), (KIND_SIGMOID, DEST_NAT),
           (KIND_PLAIN, DEST_T), (KIND_NORM_ROPE, DEST_T), (KIND_ROPE, DEST_T))


def _route_code(kind, dest):
    return kind * 4 + dest

_FLOOR_KEY = int(np.array(-5e29, np.float32).view(np.int32)) ^ 0x7FFFFFFF
_INT_MIN = -2 ** 31
_HALF16 = 32768
_PACK16 = 16
_LOG2_MOBA_BLOCK = MOBA_BLOCK.bit_length() - 1
_LOG2_N_HEADS = N_HEADS.bit_length() - 1
_LOG2_HEAD_DIM = HEAD_DIM.bit_length() - 1


def _dot(a, b):
    return jnp.dot(a, b, preferred_element_type=F32)


def _dot_nt(a, b):
    return lax.dot_general(a, b, (((1,), (1,)), ((), ())), preferred_element_type=F32)


def _params(*sem):
    return pltpu.CompilerParams(dimension_semantics=sem, vmem_limit_bytes=VMEM_LIMIT_BYTES)


def _inproj_kernel(route_ref, x_ref, ng_ref, w_ref, wt_ref, gain_ref, gain_t_ref,
                   cos_ref, s1_ref, s2_ref, cos_t_ref, s1_t_ref, s2_t_ref, bd_ref,
                   o_ref, f1_ref, f2_ref, tt_ref, h_ref, ht_ref, stage_ref):
    j = pl.program_id(1)
    tm = x_ref.shape[0]
    n_ch = TILE // LANES

    @pl.when(j == 0)
    def _():
        x = x_ref[...]
        ms = jnp.mean(x * x, axis=-1, keepdims=True)
        h = x * lax.rsqrt(ms + NORM_EPS) * ng_ref[...]
        h_ref[...] = h.astype(BF16)
        ht_ref[...] = jnp.transpose(h).astype(BF16)

    def fold(val, ref, dil, tok):
        rows = (tok.stop - tok.start) // dil
        for c in range(n_ch):
            stage_ref[c, tok, :] = val[:, c * LANES:(c + 1) * LANES]
        for r in range(dil):
            parts = [stage_ref[c, pl.ds(tok.start + r, rows, stride=dil), :] for c in range(n_ch)]
            ref[r, tok.start // dil:tok.stop // dil, :] = jnp.concatenate(parts, axis=1).astype(ref.dtype)

    def rope(v, low_only, tok):
        c = jnp.tile(cos_ref[tok, :], (1, n_ch))
        s1 = jnp.tile(s1_ref[tok, :], (1, n_ch))
        s2 = jnp.tile(s2_ref[tok, :], (1, n_ch))
        if low_only:
            low = lax.broadcasted_iota(I32, v.shape, 1) < HEAD_DIM
            c = jnp.where(low, c, 1.0)
            s1 = jnp.where(low, s1, 0.0)
            s2 = jnp.where(low, s2, 0.0)
        return v * c + pltpu.roll(v, TILE - ROT_HALF, 1) * s1 + pltpu.roll(v, ROT_HALF, 1) * s2

    def rope_t(v, tok):
        c = jnp.tile(cos_t_ref[:, tok], (N_HEADS, 1))
        s1 = jnp.tile(s1_t_ref[:, tok], (N_HEADS, 1))
        s2 = jnp.tile(s2_t_ref[:, tok], (N_HEADS, 1))
        return v * c + pltpu.roll(v, TILE - ROT_HALF, 0) * s1 + pltpu.roll(v, ROT_HALF, 0) * s2

    def epilogue(kind, y, tok):
        if kind == KIND_PLAIN:
            return y
        if kind == KIND_NORM_ROPE:
            y2 = (y * y).astype(BF16)
            ms = jnp.concatenate([_dot(y2[:, c * MXU_DIM:(c + 1) * MXU_DIM], bd_ref[...])
                                  for c in range(TILE // MXU_DIM)], axis=1)
            return rope(y * lax.rsqrt(ms + NORM_EPS) * gain_ref[0], False, tok)
        if kind == KIND_ROPE_LOW:
            return rope(y, True, tok)
        if kind == KIND_SILU:
            return y / (1.0 + jnp.exp(-y))
        assert kind == KIND_SIGMOID
        return 1.0 / (1.0 + jnp.exp(-y))

    def epilogue_t(kind, yt, tok):
        if kind == KIND_PLAIN:
            return yt
        gain = jnp.tile(gain_t_ref[0], (1, yt.shape[1] // LANES))
        if kind == KIND_NORM_ROPE:
            y2 = (yt * yt).astype(BF16)
            ms = jnp.concatenate([_dot(bd_ref[...], y2[c * MXU_DIM:(c + 1) * MXU_DIM, :])
                                  for c in range(TILE // MXU_DIM)], axis=0)
            return rope_t(yt * lax.rsqrt(ms + NORM_EPS) * gain, tok)
        assert kind == KIND_ROPE
        return rope_t(yt * gain, tok)

    route = route_ref[j]
    for kind, dest in _ROUTES:
        @pl.when(route == _route_code(kind, dest))
        def _(kind=kind, dest=dest):
            if dest == DEST_T:
                tok = slice(0, tm)
                tt_ref[...] = epilogue_t(kind, _dot(wt_ref[...], ht_ref[...]), tok).astype(tt_ref.dtype)
                return
            chunk = max(IN_CHUNK, _PACK16 * DIL_PATTERNS[2][1]) if dest == DEST_F2 else IN_CHUNK
            for m in range(tm // chunk):
                tok = slice(m * chunk, (m + 1) * chunk)
                val = epilogue(kind, _dot(h_ref[tok, :], w_ref[...]), tok)
                if dest == DEST_NAT:
                    o_ref[tok, :] = val.astype(o_ref.dtype)
                elif dest == DEST_F1:
                    fold(val, f1_ref, DIL_PATTERNS[1][1], tok)
                else:
                    fold(val, f2_ref, DIL_PATTERNS[2][1], tok)


def _inproj(x2, ng, w_nat, w_t, routes, gains, gains_t, tabs, tabs_t, bd, tm, B, S):
    T = x2.shape[0]
    grid = (T // tm, N_TILES)
    per_b = S // tm
    d1, d2 = DIL_PATTERNS[1][1], DIL_PATTERNS[2][1]
    half = N_FOLD_TILES // 2
    assert S % tm == 0 and tm % (_PACK16 * d2) == 0 and tm % IN_CHUNK == 0 and IN_CHUNK % (_PACK16 * d1) == 0
    n_w = J_T

    def fold_spec(dil, first):
        return pl.BlockSpec(
            (None, None, dil, tm // dil, TILE),
            lambda i, j, k: (jnp.clip(j - first, 0, half - 1), i // per_b, 0, i % per_b, 0))

    once_per_row_tile = pl.Buffered(1)
    row_tab = pl.BlockSpec((tm, LANES), lambda i, j, k: (i, 0), pipeline_mode=once_per_row_tile)
    col_tab = pl.BlockSpec((HEAD_DIM, tm), lambda i, j, k: (0, i), pipeline_mode=once_per_row_tile)
    t_idx = lambda j: jnp.maximum(j - J_T, 0)
    return pl.pallas_call(
        _inproj_kernel,
        grid_spec=pltpu.PrefetchScalarGridSpec(
            num_scalar_prefetch=1,
            grid=grid,
            in_specs=[
                pl.BlockSpec((tm, D_MODEL), lambda i, j, k: (i, 0), pipeline_mode=once_per_row_tile),
                pl.BlockSpec((1, D_MODEL), lambda i, j, k: (0, 0)),
                pl.BlockSpec((None, D_MODEL, TILE), lambda i, j, k: (jnp.minimum(j, n_w - 1), 0, 0)),
                pl.BlockSpec((TILE, D_MODEL), lambda i, j, k: (t_idx(j), 0)),
                pl.BlockSpec((1, 1, TILE), lambda i, j, k: (jnp.minimum(j, n_w - 1), 0, 0)),
                pl.BlockSpec((1, TILE, LANES), lambda i, j, k: (t_idx(j), 0, 0)),
                row_tab, row_tab, row_tab, col_tab, col_tab, col_tab,
                pl.BlockSpec((MXU_DIM, MXU_DIM), lambda i, j, k: (0, 0)),
            ],
            out_specs=[
                pl.BlockSpec((None, tm, TILE), lambda i, j, k: (jnp.clip(j - J_NAT, 0, N_NAT_TILES - 1), i, 0)),
                fold_spec(d1, 0),
                fold_spec(d2, half),
                pl.BlockSpec((None, TILE, tm), lambda i, j, k: (i // per_b, t_idx(j), i % per_b)),
            ],
            scratch_shapes=[pltpu.VMEM((tm, D_MODEL), BF16), pltpu.VMEM((D_MODEL, tm), BF16),
                            pltpu.VMEM((TILE // LANES, tm, LANES), F32)],
        ),
        out_shape=[jax.ShapeDtypeStruct((N_NAT_TILES, T, TILE), BF16),
                   jax.ShapeDtypeStruct((half, B, d1, S // d1, TILE), BF16),
                   jax.ShapeDtypeStruct((half, B, d2, S // d2, TILE), BF16),
                   jax.ShapeDtypeStruct((B, N_T_TILES * TILE, S), BF16)],
        compiler_params=_params("arbitrary", "arbitrary"),
        name="inproj",
    )(routes, x2, ng, w_nat, w_t, gains, gains_t, *tabs, *tabs_t, bd)


def _attn_a_kernel(shift_ref, q_ref, kc_ref, kp_ref, vc_ref, vp_ref, o_ref, lse_ref, kcat, vtcat, ost, lst,
                   s_ref, p_ref, *, tq, dil):
    m = pl.program_id(1)
    nsub = tq // BAND
    n_ch = TILE // LANES
    nk = 2 * BAND
    key = lax.broadcasted_iota(I32, (nk, BAND), 0)
    qry = lax.broadcasted_iota(I32, (nk, BAND), 1) + BAND
    in_band = jnp.logical_and(key <= qry, key >= qry - BAND)
    lane = lax.broadcasted_iota(I32, (BAND, LANES), 1)
    half_mask = (jnp.where(lane < HEAD_DIM, 1.0, 0.0).astype(BF16),
                 jnp.where(lane < HEAD_DIM, 0.0, 1.0).astype(BF16))
    ones = jnp.ones((_PACK16, nk), BF16)
    shifted = shift_ref[0] <= MAX_SOFTMAX_SHIFT
    shift = jnp.where(shifted, shift_ref[0], 0.0)

    def residue(r, carry):
        kcat[0:BAND, :] = kp_ref[r]
        kcat[BAND:, :] = kc_ref[r]
        vtcat[:, 0:BAND] = jnp.transpose(vp_ref[r].astype(F32)).astype(BF16)
        for c in range(nsub):
            chunk = vc_ref[r, c * BAND:(c + 1) * BAND, :].astype(F32)
            vtcat[:, (c + 1) * BAND:(c + 2) * BAND] = jnp.transpose(chunk).astype(BF16)

        def body(u, carry):
            r0 = pl.multiple_of(u * BAND, BAND)
            q = q_ref[r, pl.ds(r0, BAND), :]
            k2 = kcat[pl.ds(r0, nk), :]
            valid = jnp.logical_and(in_band, key >= jnp.where(m * nsub + u > 0, 0, BAND))
            bias = jnp.where(valid, -shift, NEG_INF)

            def score(h):
                pair = slice((h // 2) * LANES, (h // 2 + 1) * LANES)
                q_h = q[:, pair] * half_mask[h % 2]
                return _dot_nt(k2[:, pair], q_h) + bias

            def pv_of(h, p):
                vt_h = vtcat[h * HEAD_DIM:(h + 1) * HEAD_DIM, pl.ds(r0, nk)]
                pv = _dot(jnp.concatenate([vt_h, ones], axis=0), p)
                return pv[:HEAD_DIM], pv[HEAD_DIM:HEAD_DIM + 1]

            def block(use_shift):
                outs, lses = [], []
                if use_shift:
                    for h in range(N_HEADS):
                        p_ref[h] = jnp.exp2(score(h)).astype(BF16)
                    for h in range(N_HEADS):
                        acc, den = pv_of(h, p_ref[h])
                        outs.append(acc / den)
                        lses.append(shift + jnp.log2(den))
                else:
                    maxes = []
                    for h in range(N_HEADS):
                        s = score(h)
                        s_ref[h] = s
                        maxes.append(jnp.max(s, axis=0, keepdims=True))
                    for h in range(N_HEADS):
                        acc, den = pv_of(h, jnp.exp2(s_ref[h] - maxes[h]).astype(BF16))
                        outs.append(acc / den)
                        lses.append(maxes[h] + jnp.log2(den))
                o = jnp.transpose(jnp.concatenate(outs, axis=0))
                lse8 = jnp.concatenate(lses, axis=0)
                lse = jnp.transpose(jnp.tile(lse8, (LANES // N_HEADS, 1)))
                rows = pl.ds(r + u * (BAND * dil), BAND, stride=dil) if dil > 1 else pl.ds(r0, BAND)
                for c in range(n_ch):
                    ost[c, rows, :] = o[:, c * LANES:(c + 1) * LANES]
                lst[rows, :] = lse

            @pl.when(shifted)
            def _():
                block(True)

            @pl.when(jnp.logical_not(shifted))
            def _():
                block(False)

            return carry

        return lax.fori_loop(0, nsub, body, carry)

    lax.fori_loop(0, dil, residue, 0)
    o_ref[0] = jnp.concatenate([ost[c] for c in range(n_ch)], axis=1).astype(o_ref.dtype)
    lse_ref[0] = lst[...]


def _attn_a(qkv, shift, g, dil, B, S):
    ts = min(A_STEP_TOKENS, S)
    tq = ts // dil
    assert S % ts == 0 and tq % BAND == 0
    sub = tq // BAND

    def spec_cur(t):
        return pl.BlockSpec((None, None, dil, tq, TILE), lambda b, m: (t, b, 0, m, 0))

    def spec_prev(t):
        return pl.BlockSpec((None, None, dil, BAND, TILE),
                            lambda b, m: (t, b, 0, jnp.maximum(m * sub - 1, 0), 0))

    n_ch = TILE // LANES
    return pl.pallas_call(
        functools.partial(_attn_a_kernel, tq=tq, dil=dil),
        grid=(B, S // ts),
        in_specs=[pl.BlockSpec(memory_space=pltpu.SMEM),
                  spec_cur(0), spec_cur(1), spec_prev(1), spec_cur(2), spec_prev(2)],
        out_specs=[pl.BlockSpec((1, ts, TILE), lambda b, m: (b, m, 0)),
                   pl.BlockSpec((1, ts, LANES), lambda b, m: (b, m, 0))],
        out_shape=[jax.ShapeDtypeStruct((B, S, TILE), BF16), jax.ShapeDtypeStruct((B, S, LANES), F32)],
        scratch_shapes=[pltpu.VMEM((tq + BAND, TILE), BF16), pltpu.VMEM((TILE, tq + BAND), BF16),
                        pltpu.VMEM((n_ch, ts, LANES), F32), pltpu.VMEM((ts, LANES), F32),
                        pltpu.VMEM((N_HEADS, 2 * BAND, BAND), F32),
                        pltpu.VMEM((N_HEADS, 2 * BAND, BAND), BF16)],
        compiler_params=_params("arbitrary", "arbitrary"),
        name=f"attn_a{g}",
    )(shift, qkv, qkv, qkv, qkv, qkv)


def _pad_q(qt_ref, qpad_ref):
    qpad_ref[...] = jnp.zeros(qpad_ref.shape, qpad_ref.dtype)
    for h in range(N_HEADS):
        r0 = h * LANES + (h % 2) * HEAD_DIM
        qpad_ref[r0:r0 + HEAD_DIM, :] = qt_ref[h * HEAD_DIM:(h + 1) * HEAD_DIM, :]


def _flash_init(m_ref, l_ref, acc_ref):
    m_ref[...] = jnp.full(m_ref.shape, NEG_INF, F32)
    l_ref[...] = jnp.zeros(l_ref.shape, F32)
    acc_ref[...] = jnp.zeros(acc_ref.shape, F32)


def _flash_tile_step(score_of_head, vt_of_head, s_ref, mn_ref, m_ref, l_ref, acc_ref):
    tk = s_ref.shape[1]
    for h in range(N_HEADS):
        s = score_of_head(h)
        s_ref[h] = s
        mn_ref[h] = jnp.maximum(m_ref[h], jnp.max(s, axis=0, keepdims=True))
    ones = jnp.ones((_PACK16, tk), BF16)
    for h in range(N_HEADS):
        m_prev, m_next = m_ref[h], mn_ref[h]
        p = jnp.exp2(s_ref[h] - jnp.tile(m_next, (tk // SUBLANES, 1)))
        alpha = jnp.exp2(m_prev - m_next)
        pv = _dot(jnp.concatenate([vt_of_head(h), ones], axis=0), p.astype(BF16))
        l_ref[h] = alpha * l_ref[h] + pv[HEAD_DIM:HEAD_DIM + SUBLANES]
        acc_ref[h] = acc_ref[h] * jnp.tile(alpha, (HEAD_DIM // SUBLANES, 1)) + pv[:HEAD_DIM]
        m_ref[h] = m_next


def _flash_tile_step_shifted(score_of_head, vt_of_head, p_ref, l_ref, acc_ref):
    tk = p_ref.shape[1]
    for h in range(N_HEADS):
        p_ref[h] = jnp.exp2(score_of_head(h)).astype(BF16)
    ones = jnp.ones((_PACK16, tk), BF16)
    for h in range(N_HEADS):
        pv = _dot(jnp.concatenate([vt_of_head(h), ones], axis=0), p_ref[h])
        l_ref[h] = l_ref[h] + pv[HEAD_DIM:HEAD_DIM + SUBLANES]
        acc_ref[h] = acc_ref[h] + pv[:HEAD_DIM]


def _flash_finish(o_ref, l_ref, acc_ref):
    outs = [acc_ref[h] / jnp.tile(l_ref[h], (HEAD_DIM // SUBLANES, 1)) for h in range(N_HEADS)]
    o_ref[...] = jnp.transpose(jnp.concatenate(outs, axis=0)).astype(o_ref.dtype)


def _dsa_kernel(shift_ref, qt_ref, k_ref, vt_ref, iqt_ref, ikw_q_ref, ikw_ref, o_ref,
                key_ref, hi_ref, lo_ref, qpad_ref, bias_ref, p_ref, s_ref, mn_ref, m_ref, l_ref, acc_ref,
                *, tq, tk, topk):
    qi = pl.program_id(1)
    n_kt = qi + 1
    n_acc = 4
    rows8 = tk // SUBLANES

    w8 = jnp.transpose(ikw_q_ref[...].astype(F32))[HEAD_DIM:HEAD_DIM + IDX_HEADS, :] * (IDX_HEADS ** -0.5)
    krow = lax.broadcasted_iota(I32, (tk, tq), 0)
    qcol = lax.broadcasted_iota(I32, (tk, tq), 1)

    def score_tile(c, diagonal):
        c0 = pl.multiple_of(c * tk, tk)
        kx = ikw_ref[pl.ds(c0, tk), :][:, :IDX_DIM]
        sc = jnp.zeros((tk, tq), F32)
        for h in range(IDX_HEADS):
            lg = _dot(kx, iqt_ref[h * IDX_DIM:(h + 1) * IDX_DIM, :])
            sc = sc + w8[h:h + 1, :] * jnp.maximum(lg, 0.0)
        sc = jnp.where(sc == 0.0, 0.0, sc)
        if diagonal:
            sc = jnp.where(krow <= qcol, sc, NEG_INF)
        bits = pltpu.bitcast(sc, I32)
        key = jnp.where(bits < 0, bits ^ 0x7FFFFFFF, bits)
        key_ref[pl.ds(c0, tk), :] = key
        hi_ref[pl.ds(c0, tk), :] = jnp.right_shift(key, 16).astype(I16)
        lo_ref[pl.ds(c0, tk), :] = ((key & 0xFFFF) - _HALF16).astype(I16)

    def score_body(c, carry):
        score_tile(c, False)
        return carry

    lax.fori_loop(0, qi, score_body, 0)
    score_tile(qi, True)

    rows16 = tk // _PACK16
    one16 = jnp.ones((_PACK16, tq), I16)
    zero16 = jnp.zeros((_PACK16, tq), I16)

    def count16(ref, pred):
        def cbody(c, accs):
            accs = list(accs)
            c0 = pl.multiple_of(c * tk, tk)
            t = ref[pl.ds(c0, tk), :]
            for g in range(rows16):
                hit = jnp.where(pred(t[g * _PACK16:(g + 1) * _PACK16]), one16, zero16)
                accs[g % n_acc] = accs[g % n_acc] + hit
            return tuple(accs)
        accs = lax.fori_loop(0, n_kt, cbody, tuple(zero16 for _ in range(n_acc)))
        return jnp.sum(functools.reduce(lambda a, b: a + b, accs).astype(I32), axis=0, keepdims=True)

    def as16(v):
        return jnp.broadcast_to(v, (_PACK16, tq)).astype(I16)

    def select16(ref, need, cge0):
        def bit_body(it, carry):
            ans, cge = carry
            cand_u = ans | lax.shift_left(jnp.int32(1), 15 - it)
            cand = as16(cand_u - _HALF16)
            cnt = count16(ref, lambda t: t >= cand)
            ok = cnt >= need
            return jnp.where(ok, cand_u, ans), jnp.where(ok, cnt, cge)
        return lax.fori_loop(0, 16, bit_body, (jnp.zeros((1, tq), I32), cge0))

    n_all = jnp.zeros((1, tq), I32) + n_kt * tk
    p_u, cge_hi = select16(hi_ref, topk, n_all)
    p16 = as16(p_u - _HALF16)
    c_gt = count16(hi_ref, lambda t: t > p16)

    def bucket_body(c, carry):
        c0 = pl.multiple_of(c * tk, tk)
        lo_ref[pl.ds(c0, tk), :] = jnp.where(hi_ref[pl.ds(c0, tk), :] == jnp.tile(p16, (rows16, 1)),
                                             lo_ref[pl.ds(c0, tk), :], jnp.int16(-_HALF16))
        return carry

    lax.fori_loop(0, n_kt, bucket_body, 0)
    l_u, cge_lo = select16(lo_ref, topk - c_gt, cge_hi - c_gt)
    cge = c_gt + cge_lo
    thr = jnp.maximum((p_u - _HALF16) * 65536 + l_u, _FLOOR_KEY)

    def count_rows(fn, n_out):
        def cbody(c, accs):
            accs = [list(a) for a in accs]
            c0 = pl.multiple_of(c * tk, tk)
            kt = key_ref[pl.ds(c0, tk), :]
            for g in range(rows8):
                r0 = c0 + g * SUBLANES
                vals = fn(kt[g * SUBLANES:(g + 1) * SUBLANES], r0)
                for o in range(n_out):
                    accs[o][g % n_acc] = accs[o][g % n_acc] + vals[o]
            return tuple(tuple(a) for a in accs)
        z = jnp.zeros((SUBLANES, tq), I32)
        accs = lax.fori_loop(0, n_kt, cbody, tuple(tuple(z for _ in range(n_acc)) for _ in range(n_out)))
        return [jnp.sum(functools.reduce(lambda a, b: a + b, a), axis=0, keepdims=True) for a in accs]

    tie = jnp.logical_and(cge > topk, thr > _FLOOR_KEY)
    any_tie = jnp.max(jnp.where(tie, 1, 0)) > 0
    thr8 = jnp.broadcast_to(thr, (SUBLANES, tq))

    @pl.when(any_tie)
    def _():
        n_bits = int(np.log2(key_ref.shape[0]))
        sub = lax.broadcasted_iota(I32, (SUBLANES, tq), 0)

        def jbody(it, lo):
            cand = lo + lax.shift_left(jnp.int32(1), n_bits - 1 - it)
            pos = jnp.broadcast_to(cand - 1, (SUBLANES, tq))

            def f(kk, r0):
                eq = jnp.logical_and(kk == thr8, sub + r0 <= pos)
                return jnp.where(kk > thr8, 1, 0), jnp.where(eq, 1, 0)

            gt, eq = count_rows(f, 2)
            return jnp.where(gt + eq >= topk, lo, cand)

        jrow = jnp.broadcast_to(lax.fori_loop(0, n_bits, jbody, jnp.zeros((1, tq), I32)), (SUBLANES, tq))
        tie8 = jnp.broadcast_to(jnp.where(tie, 1, 0), (SUBLANES, tq)) > 0

        def fix(c, carry):
            for g in range(rows8):
                r0 = pl.multiple_of(c * tk + g * SUBLANES, SUBLANES)
                kk = key_ref[pl.ds(r0, SUBLANES), :]
                drop = jnp.logical_and(tie8, jnp.logical_and(kk == thr8, sub + r0 > jrow))
                key_ref[pl.ds(r0, SUBLANES), :] = jnp.where(drop, _INT_MIN, kk)
            return carry

        lax.fori_loop(0, n_kt, fix, 0)

    _pad_q(qt_ref, qpad_ref)
    _flash_init(m_ref, l_ref, acc_ref)
    shift = shift_ref[0]

    def attend(shifted):
        def att_body(j, carry):
            c0 = pl.multiple_of(j * tk, tk)
            sel = key_ref[pl.ds(c0, tk), :] >= thr
            bias_ref[...] = jnp.where(sel, -shift if shifted else 0.0, NEG_INF)
            k = k_ref[pl.ds(c0, tk), :]

            def score(h):
                k_pair = k[:, (h // 2) * LANES:(h // 2 + 1) * LANES]
                return _dot(k_pair, qpad_ref[h * LANES:(h + 1) * LANES, :]) + bias_ref[...]

            vt_of_head = lambda h: vt_ref[h * HEAD_DIM:(h + 1) * HEAD_DIM, pl.ds(c0, tk)]
            if shifted:
                _flash_tile_step_shifted(score, vt_of_head, p_ref, l_ref, acc_ref)
            else:
                _flash_tile_step(score, vt_of_head, s_ref, mn_ref, m_ref, l_ref, acc_ref)
            return carry

        lax.fori_loop(0, n_kt, att_body, 0)

    @pl.when(shift <= MAX_SOFTMAX_SHIFT)
    def _():
        attend(True)

    @pl.when(shift > MAX_SOFTMAX_SHIFT)
    def _():
        attend(False)

    _flash_finish(o_ref, l_ref, acc_ref)


def _flash_scratch(tk, tq):
    stat = pltpu.VMEM((N_HEADS, SUBLANES, tq), F32)
    return [pltpu.VMEM((N_HEADS, tk, tq), BF16), pltpu.VMEM((N_HEADS, tk, tq), F32), stat, stat, stat,
            pltpu.VMEM((N_HEADS, HEAD_DIM, tq), F32)]


def _dsa(P4, TT, shift, B, S):
    tq = tk = SPARSE_TILE
    topk = min(IDX_TOPK_MAX, S // 4)
    assert S % tq == 0 and topk <= tk
    def tt_q(t):
        return pl.BlockSpec((None, TILE, tq), lambda b, i: (b, t, i))

    return pl.pallas_call(
        functools.partial(_dsa_kernel, tq=tq, tk=tk, topk=topk),
        grid=(B, S // tq),
        in_specs=[pl.BlockSpec(memory_space=pltpu.SMEM),
                  tt_q(TT_QB),
                  pl.BlockSpec((None, None, S, TILE), lambda b, i: (T_KB, b, 0, 0)),
                  pl.BlockSpec((None, TILE, S), lambda b, i: (b, TT_VB, 0)),
                  tt_q(TT_IQ),
                  pl.BlockSpec((None, None, tq, LANES), lambda b, i: (T_IKW, b, i, 0)),
                  pl.BlockSpec((None, None, S, LANES), lambda b, i: (T_IKW, b, 0, 0))],
        out_specs=pl.BlockSpec((None, tq, TILE), lambda b, i: (b, i, 0)),
        out_shape=jax.ShapeDtypeStruct((B, S, TILE), BF16),
        scratch_shapes=[pltpu.VMEM((S, tq), I32), pltpu.VMEM((S, tq), I16), pltpu.VMEM((S, tq), I16),
                        pltpu.VMEM((N_HEADS * LANES, tq), BF16),
                        pltpu.VMEM((tk, tq), F32)] + _flash_scratch(tk, tq),
        compiler_params=_params("arbitrary", "arbitrary"),
        name="dsa",
    )(shift, TT, P4, TT, TT, P4, P4)


def _moba_kernel(shift_ref, qt_ref, k_ref, vt_ref, o_ref, kmh_ref, kml_ref, qaug_ref,
                 p_ref, s_ref, mn_ref, m_ref, l_ref, acc_ref, *, tq, topb):
    qi = pl.program_id(1)
    S = k_ref.shape[0]
    gl = N_HEADS * MOBA_SLOTS

    @pl.when(qi == 0)
    def _():
        blk_row = jnp.right_shift(lax.broadcasted_iota(I32, (gl, S), 0), _LOG2_N_HEADS)
        blk_col = jnp.right_shift(lax.broadcasted_iota(I32, (gl, S), 1), _LOG2_MOBA_BLOCK)
        avg = jnp.where(blk_row == blk_col, 1.0 / MOBA_BLOCK, 0.0).astype(BF16)
        km = _dot(avg, k_ref[...])
        r_head = lax.broadcasted_iota(I32, (gl, TILE), 0) & (N_HEADS - 1)
        c_head = jnp.right_shift(lax.broadcasted_iota(I32, (gl, TILE), 1), _LOG2_HEAD_DIM)
        km = jnp.where(r_head == c_head, km, 0.0)
        hi = km.astype(BF16)
        kmh_ref[...] = hi
        kml_ref[...] = (km - hi.astype(F32)).astype(BF16)

    shifted = shift_ref[0] <= MAX_SOFTMAX_SHIFT
    shift = jnp.where(shifted, shift_ref[0], 0.0)
    qt = qt_ref[...]
    gate = _dot(kmh_ref[...], qt) + _dot(kml_ref[...], qt)
    row = lax.broadcasted_iota(I32, (gl, tq), 0)
    gate = jnp.where(jnp.right_shift(row, _LOG2_N_HEADS) < qi, gate, NEG_INF)
    g = [gate[n * N_HEADS:(n + 1) * N_HEADS] for n in range(MOBA_SLOTS)]
    biases = []
    for n in range(MOBA_SLOTS):
        beaten = jnp.where(n < qi, 0, topb) + jnp.zeros((N_HEADS, tq), I32)
        for n2 in range(MOBA_SLOTS):
            if n2 != n:
                beaten = beaten + jnp.where((g[n2] >= g[n]) if n2 < n else (g[n2] > g[n]), 1, 0)
        biases.append(jnp.where(beaten < topb, -shift, NEG_INF))
    selb = jnp.concatenate(biases, axis=0)

    qaug_ref[...] = jnp.zeros(qaug_ref.shape, qaug_ref.dtype)
    for h in range(N_HEADS):
        r0 = (h % 2) * HEAD_DIM
        qaug_ref[h, r0:r0 + HEAD_DIM, :] = qt_ref[h * HEAD_DIM:(h + 1) * HEAD_DIM, :]
        qaug_ref[h, LANES:, :] = jnp.where((row & (N_HEADS - 1)) == h, selb, 0.0).astype(BF16)
    _flash_init(m_ref, l_ref, acc_ref)
    lane_blk = jnp.right_shift(lax.broadcasted_iota(I32, (MOBA_BLOCK, LANES), 1), _LOG2_N_HEADS)

    def vt_of_block(n):
        c0 = pl.multiple_of(n * MOBA_BLOCK, MOBA_BLOCK)
        return lambda h: vt_ref[h * HEAD_DIM:(h + 1) * HEAD_DIM, pl.ds(c0, MOBA_BLOCK)]

    def past_score(n):
        k = k_ref[pl.ds(pl.multiple_of(n * MOBA_BLOCK, MOBA_BLOCK), MOBA_BLOCK), :]
        onehot = jnp.where(lane_blk == n, 1.0, 0.0).astype(BF16)

        def score(h):
            k_aug = jnp.concatenate([k[:, (h // 2) * LANES:(h // 2 + 1) * LANES], onehot], axis=1)
            return _dot(k_aug, qaug_ref[h])

        return score

    def own_score():
        k = k_ref[pl.ds(pl.multiple_of(qi * MOBA_BLOCK, MOBA_BLOCK), MOBA_BLOCK), :]
        causal = jnp.where(lax.broadcasted_iota(I32, (MOBA_BLOCK, tq), 0)
                           <= lax.broadcasted_iota(I32, (MOBA_BLOCK, tq), 1), -shift, NEG_INF)
        return lambda h: _dot(k[:, (h // 2) * LANES:(h // 2 + 1) * LANES], qaug_ref[h, :LANES, :]) + causal

    @pl.when(shifted)
    def _():
        def att_body(n, carry):
            _flash_tile_step_shifted(past_score(n), vt_of_block(n), p_ref, l_ref, acc_ref)
            return carry

        lax.fori_loop(0, qi, att_body, 0)
        _flash_tile_step_shifted(own_score(), vt_of_block(qi), p_ref, l_ref, acc_ref)

    @pl.when(jnp.logical_not(shifted))
    def _():
        def att_body(n, carry):
            _flash_tile_step(past_score(n), vt_of_block(n), s_ref, mn_ref, m_ref, l_ref, acc_ref)
            return carry

        lax.fori_loop(0, qi, att_body, 0)
        _flash_tile_step(own_score(), vt_of_block(qi), s_ref, mn_ref, m_ref, l_ref, acc_ref)

    _flash_finish(o_ref, l_ref, acc_ref)


def _moba(P4, TT, shift, B, S):
    tq = MOBA_BLOCK
    nblk = S // MOBA_BLOCK
    assert S % MOBA_BLOCK == 0 and nblk <= MOBA_SLOTS
    topb = min(MOBA_TOPK, nblk - 1)
    gl = N_HEADS * MOBA_SLOTS
    return pl.pallas_call(
        functools.partial(_moba_kernel, tq=tq, topb=topb),
        grid=(B, S // tq),
        in_specs=[pl.BlockSpec(memory_space=pltpu.SMEM),
                  pl.BlockSpec((None, TILE, tq), lambda b, i: (b, TT_QC, i)),
                  pl.BlockSpec((None, None, S, TILE), lambda b, i: (T_KC, b, 0, 0)),
                  pl.BlockSpec((None, TILE, S), lambda b, i: (b, TT_VC, 0))],
        out_specs=pl.BlockSpec((None, tq, TILE), lambda b, i: (b, i, 0)),
        out_shape=jax.ShapeDtypeStruct((B, S, TILE), BF16),
        scratch_shapes=[pltpu.VMEM((gl, TILE), BF16), pltpu.VMEM((gl, TILE), BF16),
                        pltpu.VMEM((N_HEADS, 2 * LANES, tq), BF16)]
                       + _flash_scratch(MOBA_BLOCK, tq),
        compiler_params=_params("arbitrary", "arbitrary"),
        name="moba",
    )(shift, TT, P4, TT)


def _post_kernel(x_ref, oa0, oa1, oa2, la0, la1, la2, ob_ref, oc_ref, z0, z1, z2, g0, g1, g2,
                 wbr_ref, wout_ref, expand_ref, out_ref):
    l0, l1, l2 = la0[...], la1[...], la2[...]
    mx = jnp.maximum(jnp.maximum(l0, l1), l2)
    e0, e1, e2 = jnp.exp2(l0 - mx), jnp.exp2(l1 - mx), jnp.exp2(l2 - mx)
    den = e0 + e1 + e2

    def spread(w):
        hi = w.astype(BF16)
        lo = (w - hi.astype(F32)).astype(BF16)
        return _dot(hi, expand_ref[...]) + _dot(lo, expand_ref[...])

    o_a = (spread(e0 / den) * oa0[...].astype(F32) + spread(e1 / den) * oa1[...].astype(F32)
           + spread(e2 / den) * oa2[...].astype(F32))
    branches = (o_a, ob_ref[...].astype(F32), oc_ref[...].astype(F32))
    merged = jnp.zeros(out_ref.shape, F32)
    for n, (o, z, g) in enumerate(zip(branches, (z0, z1, z2), (g0, g1, g2))):
        y = _dot((o * z[0].astype(F32)).astype(BF16), wbr_ref[n])
        gate = jnp.concatenate([g[t] for t in range(g.shape[0])], axis=1)
        merged = merged + gate.astype(F32) * y
    out_ref[...] = x_ref[...] + _dot(merged.astype(BF16), wout_ref[...])


def _post(x2, oa, la, ob, oc, P, wbr, wout, tm):
    T = x2.shape[0]
    row = lambda width, t: pl.BlockSpec((tm, width), lambda i: (i, t))
    tiles = lambda count, first: pl.BlockSpec((count, tm, TILE), lambda i: (first // count, i, 0))
    per_g = D_MODEL // TILE
    assert T_G % per_g == 0
    head_of = np.arange(TILE) // HEAD_DIM
    expand = jnp.asarray((np.arange(LANES)[:, None] == head_of[None, :]).astype(np.float32), BF16)
    in_specs = ([row(D_MODEL, 0)] + [row(TILE, 0)] * 3 + [row(LANES, 0)] * 3 + [row(TILE, 0)] * 2
                + [tiles(1, T_Z + n) for n in range(N_BRANCH)]
                + [tiles(per_g, T_G + per_g * n) for n in range(N_BRANCH)]
                + [pl.BlockSpec((N_BRANCH, BRANCH_WIDTH, D_MODEL), lambda i: (0, 0, 0)),
                   pl.BlockSpec((D_MODEL, D_MODEL), lambda i: (0, 0)),
                   pl.BlockSpec((LANES, TILE), lambda i: (0, 0))])
    return pl.pallas_call(
        _post_kernel,
        grid=(T // tm,),
        in_specs=in_specs,
        out_specs=row(D_MODEL, 0),
        out_shape=jax.ShapeDtypeStruct((T, D_MODEL), F32),
        compiler_params=_params("arbitrary"),
        name="post",
    )(x2, oa[0], oa[1], oa[2], la[0], la[1], la[2], ob, oc, P, P, P, P, P, P, wbr, wout, expand)


def _rearrange_w_in(w):
    bw = BRANCH_WIDTH
    a_q, a_k, a_v = w[:, 0:3 * bw], w[:, 3 * bw:6 * bw], w[:, 6 * bw:9 * bw]
    off = 9 * bw
    b_q, b_k, b_v = (w[:, off + i * bw:off + (i + 1) * bw] for i in range(3)); off += 3 * bw
    iq = w[:, off:off + IDX_HEADS * IDX_DIM]; off += IDX_HEADS * IDX_DIM
    ik = w[:, off:off + IDX_DIM]; off += IDX_DIM
    iw = w[:, off:off + IDX_HEADS]; off += IDX_HEADS
    c_q, c_k, c_v = (w[:, off + i * bw:off + (i + 1) * bw] for i in range(3)); off += 3 * bw
    z = w[:, off:off + 3 * bw]; off += 3 * bw
    g = w[:, off:off + 3 * D_MODEL]; off += 3 * D_MODEL
    assert off == w.shape[1]
    cols = []
    for grp in (1, 2, 0):
        sl = slice(grp * bw, (grp + 1) * bw)
        cols += [a_q[:, sl], a_k[:, sl], a_v[:, sl]]
    pad = jnp.zeros((w.shape[0], TILE - IDX_DIM - IDX_HEADS), w.dtype)
    cols += [b_k, ik, iw, pad, c_k, g, z]
    w_nat = jnp.concatenate(cols, axis=1).astype(BF16)
    assert w_nat.shape[1] == J_T * TILE
    w_nat = w_nat.reshape(-1, J_T, TILE).transpose(1, 0, 2)
    w_t = jnp.concatenate([b_q, b_v, iq, c_q, c_v], axis=1).T.astype(BF16)
    assert w_t.shape[0] == N_T_TILES * TILE
    return w_nat, w_t


def _tile_tables(qk_g):
    kinds = np.zeros((N_TILES,), np.int32)
    scale = np.array([HEAD_DIM ** -0.5 * LOG2_E, 1.0], np.float32)
    rows = jnp.tile(qk_g.astype(F32) * scale[None, :, None], (1, 1, N_HEADS)).reshape(2 * N_BRANCH, TILE)
    rows = jnp.concatenate([rows, jnp.full((1, TILE), IDX_DIM ** -0.5, F32), jnp.ones((1, TILE), F32)])
    row_iq, row_one = 2 * N_BRANCH, 2 * N_BRANCH + 1
    which = np.full((N_TILES,), row_one, np.int32)

    def qk(tile, mixer):
        kinds[tile] = kinds[tile + 1] = KIND_NORM_ROPE
        which[tile], which[tile + 1] = 2 * mixer, 2 * mixer + 1

    qk(0, 0)
    qk(N_FOLD_TILES // 2, 0)
    qk(J_NAT + T_A, 0)
    for tile, mixer in ((T_KB, 1), (T_KC, 2)):
        kinds[J_NAT + tile] = KIND_NORM_ROPE
        which[J_NAT + tile] = 2 * mixer + 1
    kinds[J_NAT + T_IKW] = KIND_ROPE_LOW
    kinds[J_NAT + T_Z:J_NAT + T_Z + 3] = KIND_SILU
    kinds[J_NAT + T_G:J_NAT + T_G + 6] = KIND_SIGMOID
    for tile, mixer in ((TT_QB, 1), (TT_QC, 2)):
        kinds[J_T + tile] = KIND_NORM_ROPE
        which[J_T + tile] = 2 * mixer
    kinds[J_T + TT_IQ] = KIND_ROPE
    which[J_T + TT_IQ] = row_iq
    gains = rows[which[:J_T]][:, None, :]
    gains_t = jnp.broadcast_to(rows[which[J_T:]][:, :, None], (N_T_TILES, TILE, LANES))
    half = N_FOLD_TILES // 2
    dest = np.array([DEST_F1] * half + [DEST_F2] * half + [DEST_NAT] * N_NAT_TILES + [DEST_T] * N_T_TILES)
    assert all((k, d) in _ROUTES for k, d in zip(kinds.tolist(), dest.tolist()))
    routes = _route_code(kinds, dest).astype(np.int32)
    return jnp.asarray(routes), gains, gains_t


def _softmax_shift(qk_gain):
    bound = HEAD_DIM * jnp.max(jnp.abs(qk_gain[0])) * jnp.max(jnp.abs(qk_gain[1]))
    return (SHIFT_SLACK * HEAD_DIM ** -0.5 * LOG2_E * bound).reshape(1).astype(BF16).astype(F32)


def _rope_tables(positions):
    inv = ROPE_THETA ** (-jnp.arange(0, ROT_DIM, 2, dtype=F32) / ROT_DIM)
    ang = positions.astype(F32).reshape(-1)[:, None] * inv
    cos, sin = jnp.cos(ang), jnp.sin(ang)
    T = cos.shape[0]
    z8 = jnp.zeros((T, ROT_HALF), F32)
    rest1 = jnp.ones((T, HEAD_DIM - ROT_DIM), F32)
    rest0 = jnp.zeros((T, HEAD_DIM - ROT_DIM), F32)
    c = jnp.concatenate([cos, cos, rest1], axis=1)
    s1 = jnp.concatenate([-sin, z8, rest0], axis=1)
    s2 = jnp.concatenate([z8, sin, rest0], axis=1)
    two = lambda t: jnp.concatenate([t, t], axis=1)
    return (two(c), two(s1), two(s2)), (c.T, s1.T, s2.T)


def _block_diag_mean():
    h = np.arange(MXU_DIM) // HEAD_DIM
    return jnp.asarray((h[:, None] == h[None, :]).astype(np.float32) / HEAD_DIM, BF16)


def _layer(x2, B, S, tabs, tabs_t, bd, norm_g, w_in, qk_g, w_br, w_out, tm_in, tm_post):
    routes, gains, gains_t = _tile_tables(qk_g)
    w_nat, w_t = _rearrange_w_in(w_in)
    P, f1, f2, TT = _inproj(x2, norm_g[None, :], w_nat, w_t, routes, gains, gains_t, tabs, tabs_t, bd,
                            tm_in, B, S)
    P4 = P.reshape(N_NAT_TILES, B, S, TILE)
    oa, la = [], []
    for g, qkv in enumerate((P4.reshape(N_NAT_TILES, B, 1, S, TILE), f1, f2)):
        o, lse = _attn_a(qkv, _softmax_shift(qk_g[0]), g, DIL_PATTERNS[g][1], B, S)
        oa.append(o.reshape(B * S, TILE))
        la.append(lse.reshape(B * S, LANES))
    ob = _dsa(P4, TT, _softmax_shift(qk_g[1]), B, S).reshape(B * S, TILE)
    oc = _moba(P4, TT, _softmax_shift(qk_g[2]), B, S).reshape(B * S, TILE)
    return _post(x2, oa, la, ob, oc, P, w_br.astype(BF16), w_out.astype(BF16), tm_post)


def _forward(x, positions, norm_g, w_in, qk_g, w_br, w_out, tm_in=IN_ROW_TILE, tm_post=POST_ROW_TILE):
    B, S, D = x.shape
    tabs, tabs_t = _rope_tables(positions)
    bd = _block_diag_mean()
    x2 = x.reshape(B * S, D)
    for layer in range(norm_g.shape[0]):
        x2 = _layer(x2, B, S, tabs, tabs_t, bd, norm_g[layer], w_in[layer], qk_g[layer],
                    w_br[layer], w_out[layer], tm_in, tm_post)
    return x2.reshape(B, S, D)


def kernel(x, positions, norm_g, w_in, qk_g, w_br, w_out):
    return _forward(x, positions, norm_g, w_in, qk_g, w_br, w_out)
```

```python
import functools

import jax
import jax.numpy as jnp
import numpy as np
from jax import lax
from jax.experimental import pallas as pl
from jax.experimental.pallas import tpu as pltpu

F32 = jnp.float32
BF16 = jnp.bfloat16
I32 = jnp.int32
I16 = jnp.int16

D_MODEL = 1024
HEAD_DIM = 64
ROT_DIM = HEAD_DIM // 4
ROT_HALF = ROT_DIM // 2
ROPE_THETA = 500000.0
NORM_EPS = 1e-6
NEG_INF = -1e30
LOG2_E = 1.4426950408889634
MAX_SOFTMAX_SHIFT = 60.0
SHIFT_SLACK = 1.05
N_HEADS = 8
BRANCH_WIDTH = N_HEADS * HEAD_DIM
N_BRANCH = 3
DIL_PATTERNS = ((128, 1), (512, 4), (2048, 16))
BAND = 128
IDX_HEADS = 8
IDX_DIM = 64
IDX_TOPK_MAX = 256
MOBA_BLOCK = 256
MOBA_TOPK = 3
MOBA_SLOTS = 16

LANES = 128
SUBLANES = 8
MXU_DIM = 256
VMEM_LIMIT_BYTES = 56 * 1024 * 1024

TILE = BRANCH_WIDTH
N_FOLD_TILES = 6
T_A = 0
T_KB = 3
T_IKW = 4
T_KC = 5
T_G = 6
T_Z = 12
N_NAT_TILES = 15
TT_QB, TT_VB, TT_IQ, TT_QC, TT_VC = range(5)
N_T_TILES = 5
J_NAT = N_FOLD_TILES
J_T = N_FOLD_TILES + N_NAT_TILES
N_TILES = J_T + N_T_TILES
IN_ROW_TILE = 2048
IN_CHUNK = 256
A_STEP_TOKENS = 2048
SPARSE_TILE = 256
POST_ROW_TILE = 512

KIND_PLAIN, KIND_NORM_ROPE, KIND_ROPE, KIND_ROPE_LOW, KIND_SILU, KIND_SIGMOID = range(6)
DEST_NAT, DEST_F1, DEST_F2, DEST_T = range(4)
_ROUTES = ((KIND_PLAIN, DEST_NAT), (KIND_PLAIN, DEST_F1), (KIND_PLAIN, DEST_F2),
           (KIND_NORM_ROPE, DEST_NAT), (KIND_NORM_ROPE, DEST_F1), (KIND_NORM_ROPE, DEST_F2),
           (KIND_ROPE_LOW, DEST_NAT), (KIND_SILU, DEST_NAT), (KIND_SIGMOID, DEST_NAT),
           (KIND_PLAIN, DEST_T), (KIND_NORM_ROPE, DEST_T), (KIND_ROPE, DEST_T))


def _route_code(kind, dest):
    return kind * 4 + dest

_FLOOR_KEY = int(np.array(-5e29, np.float32).view(np.int32)) ^ 0x7FFFFFFF
_INT_MIN = -2 ** 31
_HALF16 = 32768
_PACK16 = 16
_LOG2_MOBA_BLOCK = MOBA_BLOCK.bit_length() - 1
_LOG2_N_HEADS = N_HEADS.bit_length() - 1
_LOG2_HEAD_DIM = HEAD_DIM.bit_length() - 1


def _dot(a, b):
    return jnp.dot(a, b, preferred_element_type=F32)


def _dot_nt(a, b):
    return lax.dot_general(a, b, (((1,), (1,)), ((), ())), preferred_element_type=F32)


def _params(*sem):
    return pltpu.CompilerParams(dimension_semantics=sem, vmem_limit_bytes=VMEM_LIMIT_BYTES)


def _inproj_kernel(route_ref, x_ref, ng_ref, w_ref, wt_ref, gain_ref, gain_t_ref,
                   cos_ref, s1_ref, s2_ref, cos_t_ref, s1_t_ref, s2_t_ref, bd_ref,
                   o_ref, f1_ref, f2_ref, tt_ref, h_ref, ht_ref, stage_ref):
    j = pl.program_id(1)
    tm = x_ref.shape[0]
    n_ch = TILE // LANES

    @pl.when(j == 0)
    def _():
        x = x_ref[...]
        ms = jnp.mean(x * x, axis=-1, keepdims=True)
        h = x * lax.rsqrt(ms + NORM_EPS) * ng_ref[...]
        h_ref[...] = h.astype(BF16)
        ht_ref[...] = jnp.transpose(h).astype(BF16)

    def fold(val, ref, dil, tok):
        rows = (tok.stop - tok.start) // dil
        for c in range(n_ch):
            stage_ref[c, tok, :] = val[:, c * LANES:(c + 1) * LANES]
        for r in range(dil):
            parts = [stage_ref[c, pl.ds(tok.start + r, rows, stride=dil), :] for c in range(n_ch)]
            ref[r, tok.start // dil:tok.stop // dil, :] = jnp.concatenate(parts, axis=1).astype(ref.dtype)

    def rope(v, low_only, tok):
        c = jnp.tile(cos_ref[tok, :], (1, n_ch))
        s1 = jnp.tile(s1_ref[tok, :], (1, n_ch))
        s2 = jnp.tile(s2_ref[tok, :], (1, n_ch))
        if low_only:
            low = lax.broadcasted_iota(I32, v.shape, 1) < HEAD_DIM
            c = jnp.where(low, c, 1.0)
            s1 = jnp.where(low, s1, 0.0)
            s2 = jnp.where(low, s2, 0.0)
        return v * c + pltpu.roll(v, TILE - ROT_HALF, 1) * s1 + pltpu.roll(v, ROT_HALF, 1) * s2

    def rope_t(v, tok):
        c = jnp.tile(cos_t_ref[:, tok], (N_HEADS, 1))
        s1 = jnp.tile(s1_t_ref[:, tok], (N_HEADS, 1))
        s2 = jnp.tile(s2_t_ref[:, tok], (N_HEADS, 1))
        return v * c + pltpu.roll(v, TILE - ROT_HALF, 0) * s1 + pltpu.roll(v, ROT_HALF, 0) * s2

    def epilogue(kind, y, tok):
        if kind == KIND_PLAIN:
            return y
        if kind == KIND_NORM_ROPE:
            y2 = (y * y).astype(BF16)
            ms = jnp.concatenate([_dot(y2[:, c * MXU_DIM:(c + 1) * MXU_DIM], bd_ref[...])
                                  for c in range(TILE // MXU_DIM)], axis=1)
            return rope(y * lax.rsqrt(ms + NORM_EPS) * gain_ref[0], False, tok)
        if kind == KIND_ROPE_LOW:
            return rope(y, True, tok)
        if kind == KIND_SILU:
            return y / (1.0 + jnp.exp(-y))
        assert kind == KIND_SIGMOID
        return 1.0 / (1.0 + jnp.exp(-y))

    def epilogue_t(kind, yt, tok):
        if kind == KIND_PLAIN:
            return yt
        gain = jnp.tile(gain_t_ref[0], (1, yt.shape[1] // LANES))
        if kind == KIND_NORM_ROPE:
            y2 = (yt * yt).astype(BF16)
            ms = jnp.concatenate([_dot(bd_ref[...], y2[c * MXU_DIM:(c + 1) * MXU_DIM, :])
                                  for c in range(TILE // MXU_DIM)], axis=0)
            return rope_t(yt * lax.rsqrt(ms + NORM_EPS) * gain, tok)
        assert kind == KIND_ROPE
        return rope_t(yt * gain, tok)

    route = route_ref[j]
    for kind, dest in _ROUTES:
        @pl.when(route == _route_code(kind, dest))
        def _(kind=kind, dest=dest):
            if dest == DEST_T:
                tok = slice(0, tm)
                tt_ref[...] = epilogue_t(kind, _dot(wt_ref[...], ht_ref[...]), tok).astype(tt_ref.dtype)
                return
            chunk = max(IN_CHUNK, _PACK16 * DIL_PATTERNS[2][1]) if dest == DEST_F2 else IN_CHUNK
            for m in range(tm // chunk):
                tok = slice(m * chunk, (m + 1) * chunk)
                val = epilogue(kind, _dot(h_ref[tok, :], w_ref[...]), tok)
                if dest == DEST_NAT:
                    o_ref[tok, :] = val.astype(o_ref.dtype)
                elif dest == DEST_F1:
                    fold(val, f1_ref, DIL_PATTERNS[1][1], tok)
                else:
                    fold(val, f2_ref, DIL_PATTERNS[2][1], tok)


def _inproj(x2, ng, w_nat, w_t, routes, gains, gains_t, tabs, tabs_t, bd, tm, B, S):
    T = x2.shape[0]
    grid = (T // tm, N_TILES)
    per_b = S // tm
    d1, d2 = DIL_PATTERNS[1][1], DIL_PATTERNS[2][1]
    half = N_FOLD_TILES // 2
    assert S % tm == 0 and tm % (_PACK16 * d2) == 0 and tm % IN_CHUNK == 0 and IN_CHUNK % (_PACK16 * d1) == 0
    n_w = J_T

    def fold_spec(dil, first):
        return pl.BlockSpec(
            (None, None, dil, tm // dil, TILE),
            lambda i, j, k: (jnp.clip(j - first, 0, half - 1), i // per_b, 0, i % per_b, 0))

    once_per_row_tile = pl.Buffered(1)
    row_tab = pl.BlockSpec((tm, LANES), lambda i, j, k: (i, 0), pipeline_mode=once_per_row_tile)
    col_tab = pl.BlockSpec((HEAD_DIM, tm), lambda i, j, k: (0, i), pipeline_mode=once_per_row_tile)
    t_idx = lambda j: jnp.maximum(j - J_T, 0)
    return pl.pallas_call(
        _inproj_kernel,
        grid_spec=pltpu.PrefetchScalarGridSpec(
            num_scalar_prefetch=1,
            grid=grid,
            in_specs=[
                pl.BlockSpec((tm, D_MODEL), lambda i, j, k: (i, 0), pipeline_mode=once_per_row_tile),
                pl.BlockSpec((1, D_MODEL), lambda i, j, k: (0, 0)),
                pl.BlockSpec((None, D_MODEL, TILE), lambda i, j, k: (jnp.minimum(j, n_w - 1), 0, 0)),
                pl.BlockSpec((TILE, D_MODEL), lambda i, j, k: (t_idx(j), 0)),
                pl.BlockSpec((1, 1, TILE), lambda i, j, k: (jnp.minimum(j, n_w - 1), 0, 0)),
                pl.BlockSpec((1, TILE, LANES), lambda i, j, k: (t_idx(j), 0, 0)),
                row_tab, row_tab, row_tab, col_tab, col_tab, col_tab,
                pl.BlockSpec((MXU_DIM, MXU_DIM), lambda i, j, k: (0, 0)),
            ],
            out_specs=[
                pl.BlockSpec((None, tm, TILE), lambda i, j, k: (jnp.clip(j - J_NAT, 0, N_NAT_TILES - 1), i, 0)),
                fold_spec(d1, 0),
                fold_spec(d2, half),
                pl.BlockSpec((None, TILE, tm), lambda i, j, k: (i // per_b, t_idx(j), i % per_b)),
            ],
            scratch_shapes=[pltpu.VMEM((tm, D_MODEL), BF16), pltpu.VMEM((D_MODEL, tm), BF16),
                            pltpu.VMEM((TILE // LANES, tm, LANES), F32)],
        ),
        out_shape=[jax.ShapeDtypeStruct((N_NAT_TILES, T, TILE), BF16),
                   jax.ShapeDtypeStruct((half, B, d1, S // d1, TILE), BF16),
                   jax.ShapeDtypeStruct((half, B, d2, S // d2, TILE), BF16),
                   jax.ShapeDtypeStruct((B, N_T_TILES * TILE, S), BF16)],
        compiler_params=_params("arbitrary", "arbitrary"),
        name="inproj",
    )(routes, x2, ng, w_nat, w_t, gains, gains_t, *tabs, *tabs_t, bd)


def _attn_a_kernel(shift_ref, q_ref, kc_ref, kp_ref, vc_ref, vp_ref, o_ref, lse_ref, kcat, vtcat, ost, lst,
                   s_ref, p_ref, *, tq, dil):
    m = pl.program_id(1)
    nsub = tq // BAND
    n_ch = TILE // LANES
    nk = 2 * BAND
    key = lax.broadcasted_iota(I32, (nk, BAND), 0)
    qry = lax.broadcasted_iota(I32, (nk, BAND), 1) + BAND
    in_band = jnp.logical_and(key <= qry, key >= qry - BAND)
    lane = lax.broadcasted_iota(I32, (BAND, LANES), 1)
    half_mask = (jnp.where(lane < HEAD_DIM, 1.0, 0.0).astype(BF16),
                 jnp.where(lane < HEAD_DIM, 0.0, 1.0).astype(BF16))
    ones = jnp.ones((_PACK16, nk), BF16)
    shifted = shift_ref[0] <= MAX_SOFTMAX_SHIFT
    shift = jnp.where(shifted, shift_ref[0], 0.0)

    def load_residue(r, slot):
        kcat[slot, 0:BAND, :] = kp_ref[r]
        kcat[slot, BAND:, :] = kc_ref[r]
        vtcat[slot, :, 0:BAND] = jnp.transpose(vp_ref[r].astype(F32)).astype(BF16)
        for c in range(nsub):
            chunk = vc_ref[r, c * BAND:(c + 1) * BAND, :].astype(F32)
            vtcat[slot, :, (c + 1) * BAND:(c + 2) * BAND] = jnp.transpose(chunk).astype(BF16)

    def block(r, u, use_shift, slot, par):
        r0 = pl.multiple_of(u * BAND, BAND)
        q = q_ref[r, pl.ds(r0, BAND), :]
        k2 = kcat[slot, pl.ds(r0, nk), :]
        valid = jnp.logical_and(in_band, key >= jnp.where(m * nsub + u > 0, 0, BAND))
        bias = jnp.where(valid, -shift, NEG_INF)

        def score(h):
            pair = slice((h // 2) * LANES, (h // 2 + 1) * LANES)
            q_h = q[:, pair] * half_mask[h % 2]
            return _dot_nt(k2[:, pair], q_h) + bias

        def pv_of(h, p):
            vt_h = vtcat[slot, h * HEAD_DIM:(h + 1) * HEAD_DIM, pl.ds(r0, nk)]
            pv = _dot(jnp.concatenate([vt_h, ones], axis=0), p)
            return pv[:HEAD_DIM], pv[HEAD_DIM:HEAD_DIM + 1]

        outs, lses = [], []
        if use_shift:
            for h in range(N_HEADS):
                p_ref[par, h] = jnp.exp2(score(h)).astype(BF16)
            for h in range(N_HEADS):
                acc, den = pv_of(h, p_ref[par, h])
                outs.append(acc / den)
                lses.append(shift + jnp.log2(den))
        else:
            maxes = []
            for h in range(N_HEADS):
                s = score(h)
                s_ref[par, h] = s
                maxes.append(jnp.max(s, axis=0, keepdims=True))
            for h in range(N_HEADS):
                acc, den = pv_of(h, jnp.exp2(s_ref[par, h] - maxes[h]).astype(BF16))
                outs.append(acc / den)
                lses.append(maxes[h] + jnp.log2(den))
        o = jnp.transpose(jnp.concatenate(outs, axis=0))
        lse8 = jnp.concatenate(lses, axis=0)
        lse = jnp.transpose(jnp.tile(lse8, (LANES // N_HEADS, 1)))
        rows = pl.ds(r + u * (BAND * dil), BAND, stride=dil) if dil > 1 else pl.ds(r0, BAND)
        for c in range(n_ch):
            ost[c, rows, :] = o[:, c * LANES:(c + 1) * LANES]
        lst[rows, :] = lse

    def both_paths(blocks):
        @pl.when(shifted)
        def _():
            for r, u, slot, par in blocks:
                block(r, u, True, slot, par)

        @pl.when(jnp.logical_not(shifted))
        def _():
            for r, u, slot, par in blocks:
                block(r, u, False, slot, par)

    if nsub % 2 == 0:
        def residue(r, carry):
            load_residue(r, 0)

            def body(i, carry):
                both_paths([(r, 2 * i, 0, 0), (r, 2 * i + 1, 0, 1)])
                return carry

            return lax.fori_loop(0, nsub // 2, body, carry)

        lax.fori_loop(0, dil, residue, 0)
    else:
        assert nsub == 1 and dil % 2 == 0

        def residue_pair(i, carry):
            load_residue(2 * i, 0)
            load_residue(2 * i + 1, 1)
            both_paths([(2 * i, 0, 0, 0), (2 * i + 1, 0, 1, 1)])
            return carry

        lax.fori_loop(0, dil // 2, residue_pair, 0)
    o_ref[0] = jnp.concatenate([ost[c] for c in range(n_ch)], axis=1).astype(o_ref.dtype)
    lse_ref[0] = lst[...]


def _attn_a(qkv, shift, g, dil, B, S):
    ts = min(A_STEP_TOKENS, S)
    tq = ts // dil
    assert S % ts == 0 and tq % BAND == 0
    sub = tq // BAND

    def spec_cur(t):
        return pl.BlockSpec((None, None, dil, tq, TILE), lambda b, m: (t, b, 0, m, 0))

    def spec_prev(t):
        return pl.BlockSpec((None, None, dil, BAND, TILE),
                            lambda b, m: (t, b, 0, jnp.maximum(m * sub - 1, 0), 0))

    n_ch = TILE // LANES
    return pl.pallas_call(
        functools.partial(_attn_a_kernel, tq=tq, dil=dil),
        grid=(B, S // ts),
        in_specs=[pl.BlockSpec(memory_space=pltpu.SMEM),
                  spec_cur(0), spec_cur(1), spec_prev(1), spec_cur(2), spec_prev(2)],
        out_specs=[pl.BlockSpec((1, ts, TILE), lambda b, m: (b, m, 0)),
                   pl.BlockSpec((1, ts, LANES), lambda b, m: (b, m, 0))],
        out_shape=[jax.ShapeDtypeStruct((B, S, TILE), BF16), jax.ShapeDtypeStruct((B, S, LANES), F32)],
        scratch_shapes=[pltpu.VMEM((2, tq + BAND, TILE), BF16), pltpu.VMEM((2, TILE, tq + BAND), BF16),
                        pltpu.VMEM((n_ch, ts, LANES), F32), pltpu.VMEM((ts, LANES), F32),
                        pltpu.VMEM((2, N_HEADS, 2 * BAND, BAND), F32),
                        pltpu.VMEM((2, N_HEADS, 2 * BAND, BAND), BF16)],
        compiler_params=_params("arbitrary", "arbitrary"),
        name=f"attn_a{g}",
    )(shift, qkv, qkv, qkv, qkv, qkv)


def _pad_q(qt_ref, qpad_ref):
    qpad_ref[...] = jnp.zeros(qpad_ref.shape, qpad_ref.dtype)
    for h in range(N_HEADS):
        r0 = h * LANES + (h % 2) * HEAD_DIM
        qpad_ref[r0:r0 + HEAD_DIM, :] = qt_ref[h * HEAD_DIM:(h + 1) * HEAD_DIM, :]


def _flash_init(m_ref, l_ref, acc_ref):
    m_ref[...] = jnp.full(m_ref.shape, NEG_INF, F32)
    l_ref[...] = jnp.zeros(l_ref.shape, F32)
    acc_ref[...] = jnp.zeros(acc_ref.shape, F32)


def _flash_tile_step(score_of_head, vt_of_head, s_ref, mn_ref, m_ref, l_ref, acc_ref):
    tk = s_ref.shape[1]
    for h in range(N_HEADS):
        s = score_of_head(h)
        s_ref[h] = s
        mn_ref[h] = jnp.maximum(m_ref[h], jnp.max(s, axis=0, keepdims=True))
    ones = jnp.ones((_PACK16, tk), BF16)
    for h in range(N_HEADS):
        m_prev, m_next = m_ref[h], mn_ref[h]
        p = jnp.exp2(s_ref[h] - jnp.tile(m_next, (tk // SUBLANES, 1)))
        alpha = jnp.exp2(m_prev - m_next)
        pv = _dot(jnp.concatenate([vt_of_head(h), ones], axis=0), p.astype(BF16))
        l_ref[h] = alpha * l_ref[h] + pv[HEAD_DIM:HEAD_DIM + SUBLANES]
        acc_ref[h] = acc_ref[h] * jnp.tile(alpha, (HEAD_DIM // SUBLANES, 1)) + pv[:HEAD_DIM]
        m_ref[h] = m_next


def _flash_tile_step_shifted(score_of_head, vt_of_head, p_ref, l_ref, acc_ref):
    tk = p_ref.shape[1]
    for h in range(N_HEADS):
        p_ref[h] = jnp.exp2(score_of_head(h)).astype(BF16)
    ones = jnp.ones((_PACK16, tk), BF16)
    for h in range(N_HEADS):
        pv = _dot(jnp.concatenate([vt_of_head(h), ones], axis=0), p_ref[h])
        l_ref[h] = l_ref[h] + pv[HEAD_DIM:HEAD_DIM + SUBLANES]
        acc_ref[h] = acc_ref[h] + pv[:HEAD_DIM]


def _flash_finish(o_ref, l_ref, acc_ref):
    outs = [acc_ref[h] / jnp.tile(l_ref[h], (HEAD_DIM // SUBLANES, 1)) for h in range(N_HEADS)]
    o_ref[...] = jnp.transpose(jnp.concatenate(outs, axis=0)).astype(o_ref.dtype)


def _dsa_kernel(shift_ref, qt_ref, k_ref, vt_ref, iqt_ref, ikw_q_ref, ikw_ref, o_ref,
                key_ref, hi_ref, lo_ref, qpad_ref, bias_ref, p_ref, s_ref, mn_ref, m_ref, l_ref, acc_ref,
                *, tq, tk, topk):
    qi = pl.program_id(1)
    n_kt = qi + 1
    n_acc = 4
    rows8 = tk // SUBLANES

    w8 = jnp.transpose(ikw_q_ref[...].astype(F32))[HEAD_DIM:HEAD_DIM + IDX_HEADS, :] * (IDX_HEADS ** -0.5)
    krow = lax.broadcasted_iota(I32, (tk, tq), 0)
    qcol = lax.broadcasted_iota(I32, (tk, tq), 1)

    def score_tile(c, diagonal):
        c0 = pl.multiple_of(c * tk, tk)
        kx = ikw_ref[pl.ds(c0, tk), :][:, :IDX_DIM]
        sc = jnp.zeros((tk, tq), F32)
        for h in range(IDX_HEADS):
            lg = _dot(kx, iqt_ref[h * IDX_DIM:(h + 1) * IDX_DIM, :])
            sc = sc + w8[h:h + 1, :] * jnp.maximum(lg, 0.0)
        sc = jnp.where(sc == 0.0, 0.0, sc)
        if diagonal:
            sc = jnp.where(krow <= qcol, sc, NEG_INF)
        bits = pltpu.bitcast(sc, I32)
        key = jnp.where(bits < 0, bits ^ 0x7FFFFFFF, bits)
        key_ref[pl.ds(c0, tk), :] = key
        hi_ref[pl.ds(c0, tk), :] = jnp.right_shift(key, 16).astype(I16)
        lo_ref[pl.ds(c0, tk), :] = ((key & 0xFFFF) - _HALF16).astype(I16)

    def score_body(c, carry):
        score_tile(c, False)
        return carry

    lax.fori_loop(0, qi, score_body, 0)
    score_tile(qi, True)

    rows16 = tk // _PACK16
    one16 = jnp.ones((_PACK16, tq), I16)
    zero16 = jnp.zeros((_PACK16, tq), I16)

    def count16(ref, pred):
        def cbody(c, accs):
            accs = list(accs)
            c0 = pl.multiple_of(c * tk, tk)
            t = ref[pl.ds(c0, tk), :]
            for g in range(rows16):
                hit = jnp.where(pred(t[g * _PACK16:(g + 1) * _PACK16]), one16, zero16)
                accs[g % n_acc] = accs[g % n_acc] + hit
            return tuple(accs)
        accs = lax.fori_loop(0, n_kt, cbody, tuple(zero16 for _ in range(n_acc)))
        return jnp.sum(functools.reduce(lambda a, b: a + b, accs).astype(I32), axis=0, keepdims=True)

    def as16(v):
        return jnp.broadcast_to(v, (_PACK16, tq)).astype(I16)

    def select16(ref, need, cge0):
        def bit_body(it, carry):
            ans, cge = carry
            cand_u = ans | lax.shift_left(jnp.int32(1), 15 - it)
            cand = as16(cand_u - _HALF16)
            cnt = count16(ref, lambda t: t >= cand)
            ok = cnt >= need
            return jnp.where(ok, cand_u, ans), jnp.where(ok, cnt, cge)
        return lax.fori_loop(0, 16, bit_body, (jnp.zeros((1, tq), I32), cge0))

    n_all = jnp.zeros((1, tq), I32) + n_kt * tk
    p_u, cge_hi = select16(hi_ref, topk, n_all)
    p16 = as16(p_u - _HALF16)
    c_gt = count16(hi_ref, lambda t: t > p16)

    def bucket_body(c, carry):
        c0 = pl.multiple_of(c * tk, tk)
        lo_ref[pl.ds(c0, tk), :] = jnp.where(hi_ref[pl.ds(c0, tk), :] == jnp.tile(p16, (rows16, 1)),
                                             lo_ref[pl.ds(c0, tk), :], jnp.int16(-_HALF16))
        return carry

    lax.fori_loop(0, n_kt, bucket_body, 0)
    l_u, cge_lo = select16(lo_ref, topk - c_gt, cge_hi - c_gt)
    cge = c_gt + cge_lo
    thr = jnp.maximum((p_u - _HALF16) * 65536 + l_u, _FLOOR_KEY)

    def count_rows(fn, n_out):
        def cbody(c, accs):
            accs = [list(a) for a in accs]
            c0 = pl.multiple_of(c * tk, tk)
            kt = key_ref[pl.ds(c0, tk), :]
            for g in range(rows8):
                r0 = c0 + g * SUBLANES
                vals = fn(kt[g * SUBLANES:(g + 1) * SUBLANES], r0)
                for o in range(n_out):
                    accs[o][g % n_acc] = accs[o][g % n_acc] + vals[o]
            return tuple(tuple(a) for a in accs)
        z = jnp.zeros((SUBLANES, tq), I32)
        accs = lax.fori_loop(0, n_kt, cbody, tuple(tuple(z for _ in range(n_acc)) for _ in range(n_out)))
        return [jnp.sum(functools.reduce(lambda a, b: a + b, a), axis=0, keepdims=True) for a in accs]

    tie = jnp.logical_and(cge > topk, thr > _FLOOR_KEY)
    any_tie = jnp.max(jnp.where(tie, 1, 0)) > 0
    thr8 = jnp.broadcast_to(thr, (SUBLANES, tq))

    @pl.when(any_tie)
    def _():
        n_bits = int(np.log2(key_ref.shape[0]))
        sub = lax.broadcasted_iota(I32, (SUBLANES, tq), 0)

        def jbody(it, lo):
            cand = lo + lax.shift_left(jnp.int32(1), n_bits - 1 - it)
            pos = jnp.broadcast_to(cand - 1, (SUBLANES, tq))

            def f(kk, r0):
                eq = jnp.logical_and(kk == thr8, sub + r0 <= pos)
                return jnp.where(kk > thr8, 1, 0), jnp.where(eq, 1, 0)

            gt, eq = count_rows(f, 2)
            return jnp.where(gt + eq >= topk, lo, cand)

        jrow = jnp.broadcast_to(lax.fori_loop(0, n_bits, jbody, jnp.zeros((1, tq), I32)), (SUBLANES, tq))
        tie8 = jnp.broadcast_to(jnp.where(tie, 1, 0), (SUBLANES, tq)) > 0

        def fix(c, carry):
            for g in range(rows8):
                r0 = pl.multiple_of(c * tk + g * SUBLANES, SUBLANES)
                kk = key_ref[pl.ds(r0, SUBLANES), :]
                drop = jnp.logical_and(tie8, jnp.logical_and(kk == thr8, sub + r0 > jrow))
                key_ref[pl.ds(r0, SUBLANES), :] = jnp.where(drop, _INT_MIN, kk)
            return carry

        lax.fori_loop(0, n_kt, fix, 0)

    _pad_q(qt_ref, qpad_ref)
    _flash_init(m_ref, l_ref, acc_ref)
    shift = shift_ref[0]

    def attend(shifted):
        def att_body(j, carry):
            c0 = pl.multiple_of(j * tk, tk)
            sel = key_ref[pl.ds(c0, tk), :] >= thr
            bias_ref[...] = jnp.where(sel, -shift if shifted else 0.0, NEG_INF)
            k = k_ref[pl.ds(c0, tk), :]

            def score(h):
                k_pair = k[:, (h // 2) * LANES:(h // 2 + 1) * LANES]
                return _dot(k_pair, qpad_ref[h * LANES:(h + 1) * LANES, :]) + bias_ref[...]

            vt_of_head = lambda h: vt_ref[h * HEAD_DIM:(h + 1) * HEAD_DIM, pl.ds(c0, tk)]
            if shifted:
                _flash_tile_step_shifted(score, vt_of_head, p_ref, l_ref, acc_ref)
            else:
                _flash_tile_step(score, vt_of_head, s_ref, mn_ref, m_ref, l_ref, acc_ref)
            return carry

        lax.fori_loop(0, n_kt, att_body, 0)

    @pl.when(shift <= MAX_SOFTMAX_SHIFT)
    def _():
        attend(True)

    @pl.when(shift > MAX_SOFTMAX_SHIFT)
    def _():
        attend(False)

    _flash_finish(o_ref, l_ref, acc_ref)


def _flash_scratch(tk, tq):
    stat = pltpu.VMEM((N_HEADS, SUBLANES, tq), F32)
    return [pltpu.VMEM((N_HEADS, tk, tq), BF16), pltpu.VMEM((N_HEADS, tk, tq), F32), stat, stat, stat,
            pltpu.VMEM((N_HEADS, HEAD_DIM, tq), F32)]


def _dsa(P4, TT, shift, B, S):
    tq = tk = SPARSE_TILE
    topk = min(IDX_TOPK_MAX, S // 4)
    assert S % tq == 0 and topk <= tk
    def tt_q(t):
        return pl.BlockSpec((None, TILE, tq), lambda b, i: (b, t, i))

    return pl.pallas_call(
        functools.partial(_dsa_kernel, tq=tq, tk=tk, topk=topk),
        grid=(B, S // tq),
        in_specs=[pl.BlockSpec(memory_space=pltpu.SMEM),
                  tt_q(TT_QB),
                  pl.BlockSpec((None, None, S, TILE), lambda b, i: (T_KB, b, 0, 0)),
                  pl.BlockSpec((None, TILE, S), lambda b, i: (b, TT_VB, 0)),
                  tt_q(TT_IQ),
                  pl.BlockSpec((None, None, tq, LANES), lambda b, i: (T_IKW, b, i, 0)),
                  pl.BlockSpec((None, None, S, LANES), lambda b, i: (T_IKW, b, 0, 0))],
        out_specs=pl.BlockSpec((None, tq, TILE), lambda b, i: (b, i, 0)),
        out_shape=jax.ShapeDtypeStruct((B, S, TILE), BF16),
        scratch_shapes=[pltpu.VMEM((S, tq), I32), pltpu.VMEM((S, tq), I16), pltpu.VMEM((S, tq), I16),
                        pltpu.VMEM((N_HEADS * LANES, tq), BF16),
                        pltpu.VMEM((tk, tq), F32)] + _flash_scratch(tk, tq),
        compiler_params=_params("arbitrary", "arbitrary"),
        name="dsa",
    )(shift, TT, P4, TT, TT, P4, P4)


def _moba_kernel(shift_ref, qt_ref, k_ref, vt_ref, o_ref, kmh_ref, kml_ref, qaug_ref,
                 p_ref, s_ref, mn_ref, m_ref, l_ref, acc_ref, *, tq, topb):
    qi = pl.program_id(1)
    S = k_ref.shape[0]
    gl = N_HEADS * MOBA_SLOTS

    @pl.when(qi == 0)
    def _():
        blk_row = jnp.right_shift(lax.broadcasted_iota(I32, (gl, S), 0), _LOG2_N_HEADS)
        blk_col = jnp.right_shift(lax.broadcasted_iota(I32, (gl, S), 1), _LOG2_MOBA_BLOCK)
        avg = jnp.where(blk_row == blk_col, 1.0 / MOBA_BLOCK, 0.0).astype(BF16)
        km = _dot(avg, k_ref[...])
        r_head = lax.broadcasted_iota(I32, (gl, TILE), 0) & (N_HEADS - 1)
        c_head = jnp.right_shift(lax.broadcasted_iota(I32, (gl, TILE), 1), _LOG2_HEAD_DIM)
        km = jnp.where(r_head == c_head, km, 0.0)
        hi = km.astype(BF16)
        kmh_ref[...] = hi
        kml_ref[...] = (km - hi.astype(F32)).astype(BF16)

    shifted = shift_ref[0] <= MAX_SOFTMAX_SHIFT
    shift = jnp.where(shifted, shift_ref[0], 0.0)
    qt = qt_ref[...]
    gate = _dot(kmh_ref[...], qt) + _dot(kml_ref[...], qt)
    row = lax.broadcasted_iota(I32, (gl, tq), 0)
    gate = jnp.where(jnp.right_shift(row, _LOG2_N_HEADS) < qi, gate, NEG_INF)
    g = [gate[n * N_HEADS:(n + 1) * N_HEADS] for n in range(MOBA_SLOTS)]
    biases = []
    for n in range(MOBA_SLOTS):
        beaten = jnp.where(n < qi, 0, topb) + jnp.zeros((N_HEADS, tq), I32)
        for n2 in range(MOBA_SLOTS):
            if n2 != n:
                beaten = beaten + jnp.where((g[n2] >= g[n]) if n2 < n else (g[n2] > g[n]), 1, 0)
        biases.append(jnp.where(beaten < topb, -shift, NEG_INF))
    selb = jnp.concatenate(biases, axis=0)

    qaug_ref[...] = jnp.zeros(qaug_ref.shape, qaug_ref.dtype)
    for h in range(N_HEADS):
        r0 = (h % 2) * HEAD_DIM
        qaug_ref[h, r0:r0 + HEAD_DIM, :] = qt_ref[h * HEAD_DIM:(h + 1) * HEAD_DIM, :]
        qaug_ref[h, LANES:, :] = jnp.where((row & (N_HEADS - 1)) == h, selb, 0.0).astype(BF16)
    _flash_init(m_ref, l_ref, acc_ref)
    lane_blk = jnp.right_shift(lax.broadcasted_iota(I32, (MOBA_BLOCK, LANES), 1), _LOG2_N_HEADS)

    def vt_of_block(n):
        c0 = pl.multiple_of(n * MOBA_BLOCK, MOBA_BLOCK)
        return lambda h: vt_ref[h * HEAD_DIM:(h + 1) * HEAD_DIM, pl.ds(c0, MOBA_BLOCK)]

    def past_score(n):
        k = k_ref[pl.ds(pl.multiple_of(n * MOBA_BLOCK, MOBA_BLOCK), MOBA_BLOCK), :]
        onehot = jnp.where(lane_blk == n, 1.0, 0.0).astype(BF16)

        def score(h):
            k_aug = jnp.concatenate([k[:, (h // 2) * LANES:(h // 2 + 1) * LANES], onehot], axis=1)
            return _dot(k_aug, qaug_ref[h])

        return score

    def own_score():
        k = k_ref[pl.ds(pl.multiple_of(qi * MOBA_BLOCK, MOBA_BLOCK), MOBA_BLOCK), :]
        causal = jnp.where(lax.broadcasted_iota(I32, (MOBA_BLOCK, tq), 0)
                           <= lax.broadcasted_iota(I32, (MOBA_BLOCK, tq), 1), -shift, NEG_INF)
        return lambda h: _dot(k[:, (h // 2) * LANES:(h // 2 + 1) * LANES], qaug_ref[h, :LANES, :]) + causal

    @pl.when(shifted)
    def _():
        def att_body(n, carry):
            _flash_tile_step_shifted(past_score(n), vt_of_block(n), p_ref, l_ref, acc_ref)
            return carry

        lax.fori_loop(0, qi, att_body, 0)
        _flash_tile_step_shifted(own_score(), vt_of_block(qi), p_ref, l_ref, acc_ref)

    @pl.when(jnp.logical_not(shifted))
    def _():
        def att_body(n, carry):
            _flash_tile_step(past_score(n), vt_of_block(n), s_ref, mn_ref, m_ref, l_ref, acc_ref)
            return carry

        lax.fori_loop(0, qi, att_body, 0)
        _flash_tile_step(own_score(), vt_of_block(qi), s_ref, mn_ref, m_ref, l_ref, acc_ref)

    _flash_finish(o_ref, l_ref, acc_ref)


def _moba(P4, TT, shift, B, S):
    tq = MOBA_BLOCK
    nblk = S // MOBA_BLOCK
    assert S % MOBA_BLOCK == 0 and nblk <= MOBA_SLOTS
    topb = min(MOBA_TOPK, nblk - 1)
    gl = N_HEADS * MOBA_SLOTS
    return pl.pallas_call(
        functools.partial(_moba_kernel, tq=tq, topb=topb),
        grid=(B, S // tq),
        in_specs=[pl.BlockSpec(memory_space=pltpu.SMEM),
                  pl.BlockSpec((None, TILE, tq), lambda b, i: (b, TT_QC, i)),
                  pl.BlockSpec((None, None, S, TILE), lambda b, i: (T_KC, b, 0, 0)),
                  pl.BlockSpec((None, TILE, S), lambda b, i: (b, TT_VC, 0))],
        out_specs=pl.BlockSpec((None, tq, TILE), lambda b, i: (b, i, 0)),
        out_shape=jax.ShapeDtypeStruct((B, S, TILE), BF16),
        scratch_shapes=[pltpu.VMEM((gl, TILE), BF16), pltpu.VMEM((gl, TILE), BF16),
                        pltpu.VMEM((N_HEADS, 2 * LANES, tq), BF16)]
                       + _flash_scratch(MOBA_BLOCK, tq),
        compiler_params=_params("arbitrary", "arbitrary"),
        name="moba",
    )(shift, TT, P4, TT)


def _post_kernel(x_ref, oa0, oa1, oa2, la0, la1, la2, ob_ref, oc_ref, z0, z1, z2, g0, g1, g2,
                 wbr_ref, wout_ref, expand_ref, out_ref):
    l0, l1, l2 = la0[...], la1[...], la2[...]
    mx = jnp.maximum(jnp.maximum(l0, l1), l2)
    e0, e1, e2 = jnp.exp2(l0 - mx), jnp.exp2(l1 - mx), jnp.exp2(l2 - mx)
    den = e0 + e1 + e2

    def spread(w):
        hi = w.astype(BF16)
        lo = (w - hi.astype(F32)).astype(BF16)
        return _dot(hi, expand_ref[...]) + _dot(lo, expand_ref[...])

    o_a = (spread(e0 / den) * oa0[...].astype(F32) + spread(e1 / den) * oa1[...].astype(F32)
           + spread(e2 / den) * oa2[...].astype(F32))
    branches = (o_a, ob_ref[...].astype(F32), oc_ref[...].astype(F32))
    merged = jnp.zeros(out_ref.shape, F32)
    for n, (o, z, g) in enumerate(zip(branches, (z0, z1, z2), (g0, g1, g2))):
        y = _dot((o * z[0].astype(F32)).astype(BF16), wbr_ref[n])
        gate = jnp.concatenate([g[t] for t in range(g.shape[0])], axis=1)
        merged = merged + gate.astype(F32) * y
    out_ref[...] = x_ref[...] + _dot(merged.astype(BF16), wout_ref[...])


def _post(x2, oa, la, ob, oc, P, wbr, wout, tm):
    T = x2.shape[0]
    row = lambda width, t: pl.BlockSpec((tm, width), lambda i: (i, t))
    tiles = lambda count, first: pl.BlockSpec((count, tm, TILE), lambda i: (first // count, i, 0))
    per_g = D_MODEL // TILE
    assert T_G % per_g == 0
    head_of = np.arange(TILE) // HEAD_DIM
    expand = jnp.asarray((np.arange(LANES)[:, None] == head_of[None, :]).astype(np.float32), BF16)
    in_specs = ([row(D_MODEL, 0)] + [row(TILE, 0)] * 3 + [row(LANES, 0)] * 3 + [row(TILE, 0)] * 2
                + [tiles(1, T_Z + n) for n in range(N_BRANCH)]
                + [tiles(per_g, T_G + per_g * n) for n in range(N_BRANCH)]
                + [pl.BlockSpec((N_BRANCH, BRANCH_WIDTH, D_MODEL), lambda i: (0, 0, 0)),
                   pl.BlockSpec((D_MODEL, D_MODEL), lambda i: (0, 0)),
                   pl.BlockSpec((LANES, TILE), lambda i: (0, 0))])
    return pl.pallas_call(
        _post_kernel,
        grid=(T // tm,),
        in_specs=in_specs,
        out_specs=row(D_MODEL, 0),
        out_shape=jax.ShapeDtypeStruct((T, D_MODEL), F32),
        compiler_params=_params("arbitrary"),
        name="post",
    )(x2, oa[0], oa[1], oa[2], la[0], la[1], la[2], ob, oc, P, P, P, P, P, P, wbr, wout, expand)


def _rearrange_w_in(w):
    bw = BRANCH_WIDTH
    a_q, a_k, a_v = w[:, 0:3 * bw], w[:, 3 * bw:6 * bw], w[:, 6 * bw:9 * bw]
    off = 9 * bw
    b_q, b_k, b_v = (w[:, off + i * bw:off + (i + 1) * bw] for i in range(3)); off += 3 * bw
    iq = w[:, off:off + IDX_HEADS * IDX_DIM]; off += IDX_HEADS * IDX_DIM
    ik = w[:, off:off + IDX_DIM]; off += IDX_DIM
    iw = w[:, off:off + IDX_HEADS]; off += IDX_HEADS
    c_q, c_k, c_v = (w[:, off + i * bw:off + (i + 1) * bw] for i in range(3)); off += 3 * bw
    z = w[:, off:off + 3 * bw]; off += 3 * bw
    g = w[:, off:off + 3 * D_MODEL]; off += 3 * D_MODEL
    assert off == w.shape[1]
    cols = []
    for grp in (1, 2, 0):
        sl = slice(grp * bw, (grp + 1) * bw)
        cols += [a_q[:, sl], a_k[:, sl], a_v[:, sl]]
    pad = jnp.zeros((w.shape[0], TILE - IDX_DIM - IDX_HEADS), w.dtype)
    cols += [b_k, ik, iw, pad, c_k, g, z]
    w_nat = jnp.concatenate(cols, axis=1).astype(BF16)
    assert w_nat.shape[1] == J_T * TILE
    w_nat = w_nat.reshape(-1, J_T, TILE).transpose(1, 0, 2)
    w_t = jnp.concatenate([b_q, b_v, iq, c_q, c_v], axis=1).T.astype(BF16)
    assert w_t.shape[0] == N_T_TILES * TILE
    return w_nat, w_t


def _tile_tables(qk_g):
    kinds = np.zeros((N_TILES,), np.int32)
    scale = np.array([HEAD_DIM ** -0.5 * LOG2_E, 1.0], np.float32)
    rows = jnp.tile(qk_g.astype(F32) * scale[None, :, None], (1, 1, N_HEADS)).reshape(2 * N_BRANCH, TILE)
    rows = jnp.concatenate([rows, jnp.full((1, TILE), IDX_DIM ** -0.5, F32), jnp.ones((1, TILE), F32)])
    row_iq, row_one = 2 * N_BRANCH, 2 * N_BRANCH + 1
    which = np.full((N_TILES,), row_one, np.int32)

    def qk(tile, mixer):
        kinds[tile] = kinds[tile + 1] = KIND_NORM_ROPE
        which[tile], which[tile + 1] = 2 * mixer, 2 * mixer + 1

    qk(0, 0)
    qk(N_FOLD_TILES // 2, 0)
    qk(J_NAT + T_A, 0)
    for tile, mixer in ((T_KB, 1), (T_KC, 2)):
        kinds[J_NAT + tile] = KIND_NORM_ROPE
        which[J_NAT + tile] = 2 * mixer + 1
    kinds[J_NAT + T_IKW] = KIND_ROPE_LOW
    kinds[J_NAT + T_Z:J_NAT + T_Z + 3] = KIND_SILU
    kinds[J_NAT + T_G:J_NAT + T_G + 6] = KIND_SIGMOID
    for tile, mixer in ((TT_QB, 1), (TT_QC, 2)):
        kinds[J_T + tile] = KIND_NORM_ROPE
        which[J_T + tile] = 2 * mixer
    kinds[J_T + TT_IQ] = KIND_ROPE
    which[J_T + TT_IQ] = row_iq
    gains = rows[which[:J_T]][:, None, :]
    gains_t = jnp.broadcast_to(rows[which[J_T:]][:, :, None], (N_T_TILES, TILE, LANES))
    half = N_FOLD_TILES // 2
    dest = np.array([DEST_F1] * half + [DEST_F2] * half + [DEST_NAT] * N_NAT_TILES + [DEST_T] * N_T_TILES)
    assert all((k, d) in _ROUTES for k, d in zip(kinds.tolist(), dest.tolist()))
    routes = _route_code(kinds, dest).astype(np.int32)
    return jnp.asarray(routes), gains, gains_t


def _softmax_shift(qk_gain):
    bound = HEAD_DIM * jnp.max(jnp.abs(qk_gain[0])) * jnp.max(jnp.abs(qk_gain[1]))
    return (SHIFT_SLACK * HEAD_DIM ** -0.5 * LOG2_E * bound).reshape(1).astype(BF16).astype(F32)


def _rope_tables(positions):
    inv = ROPE_THETA ** (-jnp.arange(0, ROT_DIM, 2, dtype=F32) / ROT_DIM)
    ang = positions.astype(F32).reshape(-1)[:, None] * inv
    cos, sin = jnp.cos(ang), jnp.sin(ang)
    T = cos.shape[0]
    z8 = jnp.zeros((T, ROT_HALF), F32)
    rest1 = jnp.ones((T, HEAD_DIM - ROT_DIM), F32)
    rest0 = jnp.zeros((T, HEAD_DIM - ROT_DIM), F32)
    c = jnp.concatenate([cos, cos, rest1], axis=1)
    s1 = jnp.concatenate([-sin, z8, rest0], axis=1)
    s2 = jnp.concatenate([z8, sin, rest0], axis=1)
    two = lambda t: jnp.concatenate([t, t], axis=1)
    return (two(c), two(s1), two(s2)), (c.T, s1.T, s2.T)


def _block_diag_mean():
    h = np.arange(MXU_DIM) // HEAD_DIM
    return jnp.asarray((h[:, None] == h[None, :]).astype(np.float32) / HEAD_DIM, BF16)


def _layer(x2, B, S, tabs, tabs_t, bd, norm_g, w_in, qk_g, w_br, w_out, tm_in, tm_post):
    routes, gains, gains_t = _tile_tables(qk_g)
    w_nat, w_t = _rearrange_w_in(w_in)
    P, f1, f2, TT = _inproj(x2, norm_g[None, :], w_nat, w_t, routes, gains, gains_t, tabs, tabs_t, bd,
                            tm_in, B, S)
    P4 = P.reshape(N_NAT_TILES, B, S, TILE)
    oa, la = [], []
    for g, qkv in enumerate((P4.reshape(N_NAT_TILES, B, 1, S, TILE), f1, f2)):
        o, lse = _attn_a(qkv, _softmax_shift(qk_g[0]), g, DIL_PATTERNS[g][1], B, S)
        oa.append(o.reshape(B * S, TILE))
        la.append(lse.reshape(B * S, LANES))
    ob = _dsa(P4, TT, _softmax_shift(qk_g[1]), B, S).reshape(B * S, TILE)
    oc = _moba(P4, TT, _softmax_shift(qk_g[2]), B, S).reshape(B * S, TILE)
    return _post(x2, oa, la, ob, oc, P, w_br.astype(BF16), w_out.astype(BF16), tm_post)


def _forward(x, positions, norm_g, w_in, qk_g, w_br, w_out, tm_in=IN_ROW_TILE, tm_post=POST_ROW_TILE):
    B, S, D = x.shape
    tabs, tabs_t = _rope_tables(positions)
    bd = _block_diag_mean()
    x2 = x.reshape(B * S, D)
    for layer in range(norm_g.shape[0]):
        x2 = _layer(x2, B, S, tabs, tabs_t, bd, norm_g[layer], w_in[layer], qk_g[layer],
                    w_br[layer], w_out[layer], tm_in, tm_post)
    return x2.reshape(B, S, D)


def kernel(x, positions, norm_g, w_in, qk_g, w_br, w_out):
    return _forward(x, positions, norm_g, w_in, qk_g, w_br, w_out)
```

```python
import functools

import jax
import jax.numpy as jnp
import numpy as np
from jax import lax
from jax.experimental import pallas as pl
from jax.experimental.pallas import tpu as pltpu

F32 = jnp.float32
BF16 = jnp.bfloat16
I32 = jnp.int32
I16 = jnp.int16

D_MODEL = 1024
HEAD_DIM = 64
ROT_DIM = HEAD_DIM // 4
ROT_HALF = ROT_DIM // 2
ROPE_THETA = 500000.0
NORM_EPS = 1e-6
NEG_INF = -1e30
LOG2_E = 1.4426950408889634
MAX_SOFTMAX_SHIFT = 60.0
SHIFT_SLACK = 1.05
N_HEADS = 8
BRANCH_WIDTH = N_HEADS * HEAD_DIM
N_BRANCH = 3
DIL_PATTERNS = ((128, 1), (512, 4), (2048, 16))
BAND = 128
IDX_HEADS = 8
IDX_DIM = 64
IDX_TOPK_MAX = 256
MOBA_BLOCK = 256
MOBA_TOPK = 3
MOBA_SLOTS = 16

LANES = 128
SUBLANES = 8
MXU_DIM = 256
VMEM_LIMIT_BYTES = 56 * 1024 * 1024

TILE = BRANCH_WIDTH
N_FOLD_TILES = 6
T_A = 0
T_KB = 3
T_IKW = 4
T_KC = 5
T_G = 6
T_Z = 12
N_NAT_TILES = 15
TT_QB, TT_VB, TT_IQ, TT_QC, TT_VC = range(5)
N_T_TILES = 5
J_NAT = N_FOLD_TILES
J_T = N_FOLD_TILES + N_NAT_TILES
N_TILES = J_T + N_T_TILES
IN_ROW_TILE = 2048
IN_CHUNK = 256
A_STEP_TOKENS = 2048
SPARSE_TILE = 256
POST_ROW_TILE = 512

KIND_PLAIN, KIND_NORM_ROPE, KIND_ROPE, KIND_ROPE_LOW, KIND_SILU, KIND_SIGMOID = range(6)
DEST_NAT, DEST_F1, DEST_F2, DEST_T = range(4)
_ROUTES = ((KIND_PLAIN, DEST_NAT), (KIND_PLAIN, DEST_F1), (KIND_PLAIN, DEST_F2),
           (KIND_NORM_ROPE, DEST_NAT), (KIND_NORM_ROPE, DEST_F1), (KIND_NORM_ROPE, DEST_F2),
           (KIND_ROPE_LOW, DEST_NAT), (KIND_SILU, DEST_NAT), (KIND_SIGMOID, DEST_NAT),
           (KIND_PLAIN, DEST_T), (KIND_NORM_ROPE, DEST_T), (KIND_ROPE, DEST_T))


def _route_code(kind, dest):
    return kind * 4 + dest

_FLOOR_KEY = int(np.array(-5e29, np.float32).view(np.int32)) ^ 0x7FFFFFFF
_INT_MIN = -2 ** 31
_HALF16 = 32768
_PACK16 = 16
_LOG2_MOBA_BLOCK = MOBA_BLOCK.bit_length() - 1
_LOG2_N_HEADS = N_HEADS.bit_length() - 1
_LOG2_HEAD_DIM = HEAD_DIM.bit_length() - 1


def _dot(a, b):
    return jnp.dot(a, b, preferred_element_type=F32)


def _dot_nt(a, b):
    return lax.dot_general(a, b, (((1,), (1,)), ((), ())), preferred_element_type=F32)


def _params(*sem):
    return pltpu.CompilerParams(dimension_semantics=sem, vmem_limit_bytes=VMEM_LIMIT_BYTES)


def _inproj_kernel(route_ref, x_ref, ng_ref, w_ref, wt_ref, gain_ref, gain_t_ref,
                   cos_ref, s1_ref, s2_ref, cos_t_ref, s1_t_ref, s2_t_ref, bd_ref,
                   o_ref, f1_ref, f2_ref, tt_ref, h_ref, ht_ref, stage_ref):
    j = pl.program_id(1)
    tm = x_ref.shape[0]
    n_ch = TILE // LANES

    @pl.when(j == 0)
    def _():
        x = x_ref[...]
        ms = jnp.mean(x * x, axis=-1, keepdims=True)
        h = x * lax.rsqrt(ms + NORM_EPS) * ng_ref[...]
        h_ref[...] = h.astype(BF16)
        ht_ref[...] = jnp.transpose(h).astype(BF16)

    def fold(val, ref, dil, tok):
        rows = (tok.stop - tok.start) // dil
        for c in range(n_ch):
            stage_ref[c, tok, :] = val[:, c * LANES:(c + 1) * LANES]
        for r in range(dil):
            parts = [stage_ref[c, pl.ds(tok.start + r, rows, stride=dil), :] for c in range(n_ch)]
            ref[r, tok.start // dil:tok.stop // dil, :] = jnp.concatenate(parts, axis=1).astype(ref.dtype)

    def rope(v, low_only, tok):
        c = jnp.tile(cos_ref[tok, :], (1, n_ch))
        s1 = jnp.tile(s1_ref[tok, :], (1, n_ch))
        s2 = jnp.tile(s2_ref[tok, :], (1, n_ch))
        if low_only:
            low = lax.broadcasted_iota(I32, v.shape, 1) < HEAD_DIM
            c = jnp.where(low, c, 1.0)
            s1 = jnp.where(low, s1, 0.0)
            s2 = jnp.where(low, s2, 0.0)
        return v * c + pltpu.roll(v, TILE - ROT_HALF, 1) * s1 + pltpu.roll(v, ROT_HALF, 1) * s2

    def rope_t(v, tok):
        c = jnp.tile(cos_t_ref[:, tok], (N_HEADS, 1))
        s1 = jnp.tile(s1_t_ref[:, tok], (N_HEADS, 1))
        s2 = jnp.tile(s2_t_ref[:, tok], (N_HEADS, 1))
        return v * c + pltpu.roll(v, TILE - ROT_HALF, 0) * s1 + pltpu.roll(v, ROT_HALF, 0) * s2

    def epilogue(kind, y, tok):
        if kind == KIND_PLAIN:
            return y
        if kind == KIND_NORM_ROPE:
            y2 = (y * y).astype(BF16)
            ms = jnp.concatenate([_dot(y2[:, c * MXU_DIM:(c + 1) * MXU_DIM], bd_ref[...])
                                  for c in range(TILE // MXU_DIM)], axis=1)
            return rope(y * lax.rsqrt(ms + NORM_EPS) * gain_ref[0], False, tok)
        if kind == KIND_ROPE_LOW:
            return rope(y, True, tok)
        if kind == KIND_SILU:
            return y / (1.0 + jnp.exp(-y))
        assert kind == KIND_SIGMOID
        return 1.0 / (1.0 + jnp.exp(-y))

    def epilogue_t(kind, yt, tok):
        if kind == KIND_PLAIN:
            return yt
        gain = jnp.tile(gain_t_ref[0], (1, yt.shape[1] // LANES))
        if kind == KIND_NORM_ROPE:
            y2 = (yt * yt).astype(BF16)
            ms = jnp.concatenate([_dot(bd_ref[...], y2[c * MXU_DIM:(c + 1) * MXU_DIM, :])
                                  for c in range(TILE // MXU_DIM)], axis=0)
            return rope_t(yt * lax.rsqrt(ms + NORM_EPS) * gain, tok)
        assert kind == KIND_ROPE
        return rope_t(yt * gain, tok)

    route = route_ref[j]
    for kind, dest in _ROUTES:
        @pl.when(route == _route_code(kind, dest))
        def _(kind=kind, dest=dest):
            if dest == DEST_T:
                tok = slice(0, tm)
                tt_ref[...] = epilogue_t(kind, _dot(wt_ref[...], ht_ref[...]), tok).astype(tt_ref.dtype)
                return
            chunk = max(IN_CHUNK, _PACK16 * DIL_PATTERNS[2][1]) if dest == DEST_F2 else IN_CHUNK
            for m in range(tm // chunk):
                tok = slice(m * chunk, (m + 1) * chunk)
                val = epilogue(kind, _dot(h_ref[tok, :], w_ref[...]), tok)
                if dest == DEST_NAT:
                    o_ref[tok, :] = val.astype(o_ref.dtype)
                elif dest == DEST_F1:
                    fold(val, f1_ref, DIL_PATTERNS[1][1], tok)
                else:
                    fold(val, f2_ref, DIL_PATTERNS[2][1], tok)


def _inproj(x2, ng, w_nat, w_t, routes, gains, gains_t, tabs, tabs_t, bd, tm, B, S):
    T = x2.shape[0]
    grid = (T // tm, N_TILES)
    per_b = S // tm
    d1, d2 = DIL_PATTERNS[1][1], DIL_PATTERNS[2][1]
    half = N_FOLD_TILES // 2
    assert S % tm == 0 and tm % (_PACK16 * d2) == 0 and tm % IN_CHUNK == 0 and IN_CHUNK % (_PACK16 * d1) == 0
    n_w = J_T

    def fold_spec(dil, first):
        return pl.BlockSpec(
            (None, None, dil, tm // dil, TILE),
            lambda i, j, k: (jnp.clip(j - first, 0, half - 1), i // per_b, 0, i % per_b, 0))

    once_per_row_tile = pl.Buffered(1)
    row_tab = pl.BlockSpec((tm, LANES), lambda i, j, k: (i, 0), pipeline_mode=once_per_row_tile)
    col_tab = pl.BlockSpec((HEAD_DIM, tm), lambda i, j, k: (0, i), pipeline_mode=once_per_row_tile)
    t_idx = lambda j: jnp.maximum(j - J_T, 0)
    return pl.pallas_call(
        _inproj_kernel,
        grid_spec=pltpu.PrefetchScalarGridSpec(
            num_scalar_prefetch=1,
            grid=grid,
            in_specs=[
                pl.BlockSpec((tm, D_MODEL), lambda i, j, k: (i, 0), pipeline_mode=once_per_row_tile),
                pl.BlockSpec((1, D_MODEL), lambda i, j, k: (0, 0)),
                pl.BlockSpec((None, D_MODEL, TILE), lambda i, j, k: (jnp.minimum(j, n_w - 1), 0, 0)),
                pl.BlockSpec((TILE, D_MODEL), lambda i, j, k: (t_idx(j), 0)),
                pl.BlockSpec((1, 1, TILE), lambda i, j, k: (jnp.minimum(j, n_w - 1), 0, 0)),
                pl.BlockSpec((1, TILE, LANES), lambda i, j, k: (t_idx(j), 0, 0)),
                row_tab, row_tab, row_tab, col_tab, col_tab, col_tab,
                pl.BlockSpec((MXU_DIM, MXU_DIM), lambda i, j, k: (0, 0)),
            ],
            out_specs=[
                pl.BlockSpec((None, tm, TILE), lambda i, j, k: (jnp.clip(j - J_NAT, 0, N_NAT_TILES - 1), i, 0)),
                fold_spec(d1, 0),
                fold_spec(d2, half),
                pl.BlockSpec((None, TILE, tm), lambda i, j, k: (i // per_b, t_idx(j), i % per_b)),
            ],
            scratch_shapes=[pltpu.VMEM((tm, D_MODEL), BF16), pltpu.VMEM((D_MODEL, tm), BF16),
                            pltpu.VMEM((TILE // LANES, tm, LANES), F32)],
        ),
        out_shape=[jax.ShapeDtypeStruct((N_NAT_TILES, T, TILE), BF16),
                   jax.ShapeDtypeStruct((half, B, d1, S // d1, TILE), BF16),
                   jax.ShapeDtypeStruct((half, B, d2, S // d2, TILE), BF16),
                   jax.ShapeDtypeStruct((B, N_T_TILES * TILE, S), BF16)],
        compiler_params=_params("arbitrary", "arbitrary"),
        name="inproj",
    )(routes, x2, ng, w_nat, w_t, gains, gains_t, *tabs, *tabs_t, bd)


def _attn_a_kernel(shift_ref, q_ref, kc_ref, kp_ref, vc_ref, vp_ref, o_ref, lse_ref, kcat, vtcat, ost, lst,
                   s_ref, p_ref, *, tq, dil):
    m = pl.program_id(1)
    nsub = tq // BAND
    n_ch = TILE // LANES
    nk = 2 * BAND
    key = lax.broadcasted_iota(I32, (nk, BAND), 0)
    qry = lax.broadcasted_iota(I32, (nk, BAND), 1) + BAND
    in_band = jnp.logical_and(key <= qry, key >= qry - BAND)
    lane = lax.broadcasted_iota(I32, (BAND, LANES), 1)
    half_mask = (jnp.where(lane < HEAD_DIM, 1.0, 0.0).astype(BF16),
                 jnp.where(lane < HEAD_DIM, 0.0, 1.0).astype(BF16))
    ones = jnp.ones((_PACK16, nk), BF16)
    shifted = shift_ref[0] <= MAX_SOFTMAX_SHIFT
    shift = jnp.where(shifted, shift_ref[0], 0.0)

    def load_residue(r, slot):
        kcat[slot, 0:BAND, :] = kp_ref[r]
        kcat[slot, BAND:, :] = kc_ref[r]
        vtcat[slot, :, 0:BAND] = jnp.transpose(vp_ref[r].astype(F32)).astype(BF16)
        for c in range(nsub):
            chunk = vc_ref[r, c * BAND:(c + 1) * BAND, :].astype(F32)
            vtcat[slot, :, (c + 1) * BAND:(c + 2) * BAND] = jnp.transpose(chunk).astype(BF16)

    def block(r, u, use_shift, slot, par):
        r0 = pl.multiple_of(u * BAND, BAND)
        q = q_ref[r, pl.ds(r0, BAND), :]
        k2 = kcat[slot, pl.ds(r0, nk), :]
        valid = jnp.logical_and(in_band, key >= jnp.where(m * nsub + u > 0, 0, BAND))
        bias = jnp.where(valid, -shift, NEG_INF)

        def score(h):
            pair = slice((h // 2) * LANES, (h // 2 + 1) * LANES)
            q_h = q[:, pair] * half_mask[h % 2]
            return _dot_nt(k2[:, pair], q_h) + bias

        def pv_of(h, p):
            vt_h = vtcat[slot, h * HEAD_DIM:(h + 1) * HEAD_DIM, pl.ds(r0, nk)]
            pv = _dot(jnp.concatenate([vt_h, ones], axis=0), p)
            return pv[:HEAD_DIM], pv[HEAD_DIM:HEAD_DIM + 1]

        outs, lses = [], []
        if use_shift:
            for h in range(N_HEADS):
                p_ref[par, h] = jnp.exp2(score(h)).astype(BF16)
            for h in range(N_HEADS):
                acc, den = pv_of(h, p_ref[par, h])
                outs.append(acc / den)
                lses.append(shift + jnp.log2(den))
        else:
            maxes = []
            for h in range(N_HEADS):
                s = score(h)
                s_ref[par, h] = s
                maxes.append(jnp.max(s, axis=0, keepdims=True))
            for h in range(N_HEADS):
                acc, den = pv_of(h, jnp.exp2(s_ref[par, h] - maxes[h]).astype(BF16))
                outs.append(acc / den)
                lses.append(maxes[h] + jnp.log2(den))
        o = jnp.transpose(jnp.concatenate(outs, axis=0))
        lse8 = jnp.concatenate(lses, axis=0)
        lse = jnp.transpose(jnp.tile(lse8, (LANES // N_HEADS, 1)))
        rows = pl.ds(r + u * (BAND * dil), BAND, stride=dil) if dil > 1 else pl.ds(r0, BAND)
        for c in range(n_ch):
            ost[c, rows, :] = o[:, c * LANES:(c + 1) * LANES]
        lst[rows, :] = lse

    def both_paths(blocks):
        @pl.when(shifted)
        def _():
            for r, u, slot, par in blocks:
                block(r, u, True, slot, par)

        @pl.when(jnp.logical_not(shifted))
        def _():
            for r, u, slot, par in blocks:
                block(r, u, False, slot, par)

    if nsub % 2 == 0:
        def residue(r, carry):
            load_residue(r, 0)

            def body(i, carry):
                both_paths([(r, 2 * i, 0, 0), (r, 2 * i + 1, 0, 1)])
                return carry

            return lax.fori_loop(0, nsub // 2, body, carry)

        lax.fori_loop(0, dil, residue, 0)
    else:
        assert nsub == 1 and dil % 2 == 0

        def residue_pair(i, carry):
            load_residue(2 * i, 0)
            load_residue(2 * i + 1, 1)
            both_paths([(2 * i, 0, 0, 0), (2 * i + 1, 0, 1, 1)])
            return carry

        lax.fori_loop(0, dil // 2, residue_pair, 0)
    o_ref[0] = jnp.concatenate([ost[c] for c in range(n_ch)], axis=1).astype(o_ref.dtype)
    lse_ref[0] = lst[...]


def _attn_a(qkv, shift, g, dil, B, S):
    ts = min(A_STEP_TOKENS, S)
    tq = ts // dil
    assert S % ts == 0 and tq % BAND == 0
    sub = tq // BAND

    def spec_cur(t):
        return pl.BlockSpec((None, None, dil, tq, TILE), lambda b, m: (t, b, 0, m, 0))

    def spec_prev(t):
        return pl.BlockSpec((None, None, dil, BAND, TILE),
                            lambda b, m: (t, b, 0, jnp.maximum(m * sub - 1, 0), 0))

    n_ch = TILE // LANES
    return pl.pallas_call(
        functools.partial(_attn_a_kernel, tq=tq, dil=dil),
        grid=(B, S // ts),
        in_specs=[pl.BlockSpec(memory_space=pltpu.SMEM),
                  spec_cur(0), spec_cur(1), spec_prev(1), spec_cur(2), spec_prev(2)],
        out_specs=[pl.BlockSpec((1, ts, TILE), lambda b, m: (b, m, 0)),
                   pl.BlockSpec((1, ts, LANES), lambda b, m: (b, m, 0))],
        out_shape=[jax.ShapeDtypeStruct((B, S, TILE), BF16), jax.ShapeDtypeStruct((B, S, LANES), F32)],
        scratch_shapes=[pltpu.VMEM((2, tq + BAND, TILE), BF16), pltpu.VMEM((2, TILE, tq + BAND), BF16),
                        pltpu.VMEM((n_ch, ts, LANES), F32), pltpu.VMEM((ts, LANES), F32),
                        pltpu.VMEM((2, N_HEADS, 2 * BAND, BAND), F32),
                        pltpu.VMEM((2, N_HEADS, 2 * BAND, BAND), BF16)],
        compiler_params=_params("arbitrary", "arbitrary"),
        name=f"attn_a{g}",
    )(shift, qkv, qkv, qkv, qkv, qkv)


def _pad_q(qt_ref, qpad_ref):
    qpad_ref[...] = jnp.zeros(qpad_ref.shape, qpad_ref.dtype)
    for h in range(N_HEADS):
        r0 = h * LANES + (h % 2) * HEAD_DIM
        qpad_ref[r0:r0 + HEAD_DIM, :] = qt_ref[h * HEAD_DIM:(h + 1) * HEAD_DIM, :]


def _flash_init(m_ref, l_ref, acc_ref):
    m_ref[...] = jnp.full(m_ref.shape, NEG_INF, F32)
    l_ref[...] = jnp.zeros(l_ref.shape, F32)
    acc_ref[...] = jnp.zeros(acc_ref.shape, F32)


def _flash_tile_step(score_of_head, vt_of_head, s_ref, mn_ref, m_ref, l_ref, acc_ref):
    tk = s_ref.shape[1]
    for h in range(N_HEADS):
        s = score_of_head(h)
        s_ref[h] = s
        mn_ref[h] = jnp.maximum(m_ref[h], jnp.max(s, axis=0, keepdims=True))
    ones = jnp.ones((_PACK16, tk), BF16)
    for h in range(N_HEADS):
        m_prev, m_next = m_ref[h], mn_ref[h]
        p = jnp.exp2(s_ref[h] - jnp.tile(m_next, (tk // SUBLANES, 1)))
        alpha = jnp.exp2(m_prev - m_next)
        pv = _dot(jnp.concatenate([vt_of_head(h), ones], axis=0), p.astype(BF16))
        l_ref[h] = alpha * l_ref[h] + pv[HEAD_DIM:HEAD_DIM + SUBLANES]
        acc_ref[h] = acc_ref[h] * jnp.tile(alpha, (HEAD_DIM // SUBLANES, 1)) + pv[:HEAD_DIM]
        m_ref[h] = m_next


def _flash_tiles_shifted(tiles, p_ref, l_ref, acc_ref):
    for t, (score_of_head, _) in enumerate(tiles):
        for h in range(N_HEADS):
            p_ref[t, h] = jnp.exp2(score_of_head(h)).astype(BF16)
    ones = jnp.ones((_PACK16, p_ref.shape[2]), BF16)
    for t, (_, vt_of_head) in enumerate(tiles):
        for h in range(N_HEADS):
            pv = _dot(jnp.concatenate([vt_of_head(h), ones], axis=0), p_ref[t, h])
            l_ref[h] = l_ref[h] + pv[HEAD_DIM:HEAD_DIM + SUBLANES]
            acc_ref[h] = acc_ref[h] + pv[:HEAD_DIM]


def _flash_finish(o_ref, l_ref, acc_ref):
    outs = [acc_ref[h] / jnp.tile(l_ref[h], (HEAD_DIM // SUBLANES, 1)) for h in range(N_HEADS)]
    o_ref[...] = jnp.transpose(jnp.concatenate(outs, axis=0)).astype(o_ref.dtype)


def _dsa_kernel(shift_ref, qt_ref, k_ref, vt_ref, iqt_ref, ikw_q_ref, ikw_ref, o_ref,
                key_ref, hi_ref, lo_ref, qpad_ref, bias_ref, p_ref, s_ref, mn_ref, m_ref, l_ref, acc_ref,
                *, tq, tk, topk):
    qi = pl.program_id(1)
    n_kt = qi + 1
    n_acc = 4
    rows8 = tk // SUBLANES

    w8 = jnp.transpose(ikw_q_ref[...].astype(F32))[HEAD_DIM:HEAD_DIM + IDX_HEADS, :] * (IDX_HEADS ** -0.5)
    krow = lax.broadcasted_iota(I32, (tk, tq), 0)
    qcol = lax.broadcasted_iota(I32, (tk, tq), 1)

    def score_tile(c, diagonal):
        c0 = pl.multiple_of(c * tk, tk)
        kx = ikw_ref[pl.ds(c0, tk), :][:, :IDX_DIM]
        sc = jnp.zeros((tk, tq), F32)
        for h in range(IDX_HEADS):
            lg = _dot(kx, iqt_ref[h * IDX_DIM:(h + 1) * IDX_DIM, :])
            sc = sc + w8[h:h + 1, :] * jnp.maximum(lg, 0.0)
        sc = jnp.where(sc == 0.0, 0.0, sc)
        if diagonal:
            sc = jnp.where(krow <= qcol, sc, NEG_INF)
        bits = pltpu.bitcast(sc, I32)
        key = jnp.where(bits < 0, bits ^ 0x7FFFFFFF, bits)
        key_ref[pl.ds(c0, tk), :] = key
        hi_ref[pl.ds(c0, tk), :] = jnp.right_shift(key, 16).astype(I16)
        lo_ref[pl.ds(c0, tk), :] = ((key & 0xFFFF) - _HALF16).astype(I16)

    def score_body(i, carry):
        score_tile(2 * i, False)
        score_tile(2 * i + 1, False)
        return carry

    lax.fori_loop(0, lax.shift_right_logical(qi, 1), score_body, 0)

    @pl.when((qi & 1) == 1)
    def _():
        score_tile(qi - 1, False)
        score_tile(qi, True)

    @pl.when((qi & 1) == 0)
    def _():
        score_tile(qi, True)

    rows16 = tk // _PACK16
    one16 = jnp.ones((_PACK16, tq), I16)
    zero16 = jnp.zeros((_PACK16, tq), I16)

    def count16(ref, pred):
        def cbody(c, accs):
            accs = list(accs)
            c0 = pl.multiple_of(c * tk, tk)
            t = ref[pl.ds(c0, tk), :]
            for g in range(rows16):
                hit = jnp.where(pred(t[g * _PACK16:(g + 1) * _PACK16]), one16, zero16)
                accs[g % n_acc] = accs[g % n_acc] + hit
            return tuple(accs)
        accs = lax.fori_loop(0, n_kt, cbody, tuple(zero16 for _ in range(n_acc)))
        return jnp.sum(functools.reduce(lambda a, b: a + b, accs).astype(I32), axis=0, keepdims=True)

    def as16(v):
        return jnp.broadcast_to(v, (_PACK16, tq)).astype(I16)

    def select16(ref, need, cge0):
        def bit_body(it, carry):
            ans, cge = carry
            cand_u = ans | lax.shift_left(jnp.int32(1), 15 - it)
            cand = as16(cand_u - _HALF16)
            cnt = count16(ref, lambda t: t >= cand)
            ok = cnt >= need
            return jnp.where(ok, cand_u, ans), jnp.where(ok, cnt, cge)
        return lax.fori_loop(0, 16, bit_body, (jnp.zeros((1, tq), I32), cge0))

    n_all = jnp.zeros((1, tq), I32) + n_kt * tk
    p_u, cge_hi = select16(hi_ref, topk, n_all)
    p16 = as16(p_u - _HALF16)
    c_gt = count16(hi_ref, lambda t: t > p16)

    def bucket_body(c, carry):
        c0 = pl.multiple_of(c * tk, tk)
        lo_ref[pl.ds(c0, tk), :] = jnp.where(hi_ref[pl.ds(c0, tk), :] == jnp.tile(p16, (rows16, 1)),
                                             lo_ref[pl.ds(c0, tk), :], jnp.int16(-_HALF16))
        return carry

    lax.fori_loop(0, n_kt, bucket_body, 0)
    l_u, cge_lo = select16(lo_ref, topk - c_gt, cge_hi - c_gt)
    cge = c_gt + cge_lo
    thr = jnp.maximum((p_u - _HALF16) * 65536 + l_u, _FLOOR_KEY)

    def count_rows(fn, n_out):
        def cbody(c, accs):
            accs = [list(a) for a in accs]
            c0 = pl.multiple_of(c * tk, tk)
            kt = key_ref[pl.ds(c0, tk), :]
            for g in range(rows8):
                r0 = c0 + g * SUBLANES
                vals = fn(kt[g * SUBLANES:(g + 1) * SUBLANES], r0)
                for o in range(n_out):
                    accs[o][g % n_acc] = accs[o][g % n_acc] + vals[o]
            return tuple(tuple(a) for a in accs)
        z = jnp.zeros((SUBLANES, tq), I32)
        accs = lax.fori_loop(0, n_kt, cbody, tuple(tuple(z for _ in range(n_acc)) for _ in range(n_out)))
        return [jnp.sum(functools.reduce(lambda a, b: a + b, a), axis=0, keepdims=True) for a in accs]

    tie = jnp.logical_and(cge > topk, thr > _FLOOR_KEY)
    any_tie = jnp.max(jnp.where(tie, 1, 0)) > 0
    thr8 = jnp.broadcast_to(thr, (SUBLANES, tq))

    @pl.when(any_tie)
    def _():
        n_bits = int(np.log2(key_ref.shape[0]))
        sub = lax.broadcasted_iota(I32, (SUBLANES, tq), 0)

        def jbody(it, lo):
            cand = lo + lax.shift_left(jnp.int32(1), n_bits - 1 - it)
            pos = jnp.broadcast_to(cand - 1, (SUBLANES, tq))

            def f(kk, r0):
                eq = jnp.logical_and(kk == thr8, sub + r0 <= pos)
                return jnp.where(kk > thr8, 1, 0), jnp.where(eq, 1, 0)

            gt, eq = count_rows(f, 2)
            return jnp.where(gt + eq >= topk, lo, cand)

        jrow = jnp.broadcast_to(lax.fori_loop(0, n_bits, jbody, jnp.zeros((1, tq), I32)), (SUBLANES, tq))
        tie8 = jnp.broadcast_to(jnp.where(tie, 1, 0), (SUBLANES, tq)) > 0

        def fix(c, carry):
            for g in range(rows8):
                r0 = pl.multiple_of(c * tk + g * SUBLANES, SUBLANES)
                kk = key_ref[pl.ds(r0, SUBLANES), :]
                drop = jnp.logical_and(tie8, jnp.logical_and(kk == thr8, sub + r0 > jrow))
                key_ref[pl.ds(r0, SUBLANES), :] = jnp.where(drop, _INT_MIN, kk)
            return carry

        lax.fori_loop(0, n_kt, fix, 0)

    _pad_q(qt_ref, qpad_ref)
    _flash_init(m_ref, l_ref, acc_ref)
    shift = shift_ref[0]

    def tile(j, t, shifted):
        c0 = pl.multiple_of(j * tk, tk)
        sel = key_ref[pl.ds(c0, tk), :] >= thr
        bias_ref[t] = jnp.where(sel, -shift if shifted else 0.0, NEG_INF)
        k = k_ref[pl.ds(c0, tk), :]

        def score(h):
            k_pair = k[:, (h // 2) * LANES:(h // 2 + 1) * LANES]
            return _dot(k_pair, qpad_ref[h * LANES:(h + 1) * LANES, :]) + bias_ref[t]

        return score, lambda h: vt_ref[h * HEAD_DIM:(h + 1) * HEAD_DIM, pl.ds(c0, tk)]

    @pl.when(shift <= MAX_SOFTMAX_SHIFT)
    def _():
        def pair_body(i, carry):
            _flash_tiles_shifted([tile(2 * i, 0, True), tile(2 * i + 1, 1, True)], p_ref, l_ref, acc_ref)
            return carry

        lax.fori_loop(0, lax.shift_right_logical(n_kt, 1), pair_body, 0)

        @pl.when((n_kt & 1) == 1)
        def _():
            _flash_tiles_shifted([tile(qi, 0, True)], p_ref, l_ref, acc_ref)

    @pl.when(shift > MAX_SOFTMAX_SHIFT)
    def _():
        def att_body(j, carry):
            _flash_tile_step(*tile(j, 0, False), s_ref, mn_ref, m_ref, l_ref, acc_ref)
            return carry

        lax.fori_loop(0, n_kt, att_body, 0)

    _flash_finish(o_ref, l_ref, acc_ref)


def _flash_scratch(tk, tq):
    stat = pltpu.VMEM((N_HEADS, SUBLANES, tq), F32)
    return [pltpu.VMEM((2, N_HEADS, tk, tq), BF16), pltpu.VMEM((N_HEADS, tk, tq), F32), stat, stat, stat,
            pltpu.VMEM((N_HEADS, HEAD_DIM, tq), F32)]


def _dsa(P4, TT, shift, B, S):
    tq = tk = SPARSE_TILE
    topk = min(IDX_TOPK_MAX, S // 4)
    assert S % tq == 0 and topk <= tk
    def tt_q(t):
        return pl.BlockSpec((None, TILE, tq), lambda b, i: (b, t, i))

    return pl.pallas_call(
        functools.partial(_dsa_kernel, tq=tq, tk=tk, topk=topk),
        grid=(B, S // tq),
        in_specs=[pl.BlockSpec(memory_space=pltpu.SMEM),
                  tt_q(TT_QB),
                  pl.BlockSpec((None, None, S, TILE), lambda b, i: (T_KB, b, 0, 0)),
                  pl.BlockSpec((None, TILE, S), lambda b, i: (b, TT_VB, 0)),
                  tt_q(TT_IQ),
                  pl.BlockSpec((None, None, tq, LANES), lambda b, i: (T_IKW, b, i, 0)),
                  pl.BlockSpec((None, None, S, LANES), lambda b, i: (T_IKW, b, 0, 0))],
        out_specs=pl.BlockSpec((None, tq, TILE), lambda b, i: (b, i, 0)),
        out_shape=jax.ShapeDtypeStruct((B, S, TILE), BF16),
        scratch_shapes=[pltpu.VMEM((S, tq), I32), pltpu.VMEM((S, tq), I16), pltpu.VMEM((S, tq), I16),
                        pltpu.VMEM((N_HEADS * LANES, tq), BF16),
                        pltpu.VMEM((2, tk, tq), F32)] + _flash_scratch(tk, tq),
        compiler_params=_params("arbitrary", "arbitrary"),
        name="dsa",
    )(shift, TT, P4, TT, TT, P4, P4)


def _moba_kernel(shift_ref, qt_ref, k_ref, vt_ref, o_ref, kmh_ref, kml_ref, qaug_ref,
                 p_ref, s_ref, mn_ref, m_ref, l_ref, acc_ref, *, tq, topb):
    qi = pl.program_id(1)
    S = k_ref.shape[0]
    gl = N_HEADS * MOBA_SLOTS

    @pl.when(qi == 0)
    def _():
        blk_row = jnp.right_shift(lax.broadcasted_iota(I32, (gl, S), 0), _LOG2_N_HEADS)
        blk_col = jnp.right_shift(lax.broadcasted_iota(I32, (gl, S), 1), _LOG2_MOBA_BLOCK)
        avg = jnp.where(blk_row == blk_col, 1.0 / MOBA_BLOCK, 0.0).astype(BF16)
        km = _dot(avg, k_ref[...])
        r_head = lax.broadcasted_iota(I32, (gl, TILE), 0) & (N_HEADS - 1)
        c_head = jnp.right_shift(lax.broadcasted_iota(I32, (gl, TILE), 1), _LOG2_HEAD_DIM)
        km = jnp.where(r_head == c_head, km, 0.0)
        hi = km.astype(BF16)
        kmh_ref[...] = hi
        kml_ref[...] = (km - hi.astype(F32)).astype(BF16)

    shifted = shift_ref[0] <= MAX_SOFTMAX_SHIFT
    shift = jnp.where(shifted, shift_ref[0], 0.0)
    qt = qt_ref[...]
    gate = _dot(kmh_ref[...], qt) + _dot(kml_ref[...], qt)
    row = lax.broadcasted_iota(I32, (gl, tq), 0)
    gate = jnp.where(jnp.right_shift(row, _LOG2_N_HEADS) < qi, gate, NEG_INF)
    g = [gate[n * N_HEADS:(n + 1) * N_HEADS] for n in range(MOBA_SLOTS)]
    biases = []
    for n in range(MOBA_SLOTS):
        beaten = jnp.where(n < qi, 0, topb) + jnp.zeros((N_HEADS, tq), I32)
        for n2 in range(MOBA_SLOTS):
            if n2 != n:
                beaten = beaten + jnp.where((g[n2] >= g[n]) if n2 < n else (g[n2] > g[n]), 1, 0)
        biases.append(jnp.where(beaten < topb, -shift, NEG_INF))
    selb = jnp.concatenate(biases, axis=0)

    qaug_ref[...] = jnp.zeros(qaug_ref.shape, qaug_ref.dtype)
    for h in range(N_HEADS):
        r0 = (h % 2) * HEAD_DIM
        qaug_ref[h, r0:r0 + HEAD_DIM, :] = qt_ref[h * HEAD_DIM:(h + 1) * HEAD_DIM, :]
        qaug_ref[h, LANES:, :] = jnp.where((row & (N_HEADS - 1)) == h, selb, 0.0).astype(BF16)
    _flash_init(m_ref, l_ref, acc_ref)
    lane_blk = jnp.right_shift(lax.broadcasted_iota(I32, (MOBA_BLOCK, LANES), 1), _LOG2_N_HEADS)

    def vt_of_block(n):
        c0 = pl.multiple_of(n * MOBA_BLOCK, MOBA_BLOCK)
        return lambda h: vt_ref[h * HEAD_DIM:(h + 1) * HEAD_DIM, pl.ds(c0, MOBA_BLOCK)]

    def past_score(n):
        k = k_ref[pl.ds(pl.multiple_of(n * MOBA_BLOCK, MOBA_BLOCK), MOBA_BLOCK), :]
        onehot = jnp.where(lane_blk == n, 1.0, 0.0).astype(BF16)

        def score(h):
            k_aug = jnp.concatenate([k[:, (h // 2) * LANES:(h // 2 + 1) * LANES], onehot], axis=1)
            return _dot(k_aug, qaug_ref[h])

        return score

    def own_score():
        k = k_ref[pl.ds(pl.multiple_of(qi * MOBA_BLOCK, MOBA_BLOCK), MOBA_BLOCK), :]
        causal = jnp.where(lax.broadcasted_iota(I32, (MOBA_BLOCK, tq), 0)
                           <= lax.broadcasted_iota(I32, (MOBA_BLOCK, tq), 1), -shift, NEG_INF)
        return lambda h: _dot(k[:, (h // 2) * LANES:(h // 2 + 1) * LANES], qaug_ref[h, :LANES, :]) + causal

    @pl.when(shifted)
    def _():
        past = lambda n: (past_score(n), vt_of_block(n))
        own = lambda: (own_score(), vt_of_block(qi))

        def pair_body(i, carry):
            _flash_tiles_shifted([past(2 * i), past(2 * i + 1)], p_ref, l_ref, acc_ref)
            return carry

        lax.fori_loop(0, lax.shift_right_logical(qi, 1), pair_body, 0)

        @pl.when((qi & 1) == 1)
        def _():
            _flash_tiles_shifted([past(qi - 1), own()], p_ref, l_ref, acc_ref)

        @pl.when((qi & 1) == 0)
        def _():
            _flash_tiles_shifted([own()], p_ref, l_ref, acc_ref)

    @pl.when(jnp.logical_not(shifted))
    def _():
        def att_body(n, carry):
            _flash_tile_step(past_score(n), vt_of_block(n), s_ref, mn_ref, m_ref, l_ref, acc_ref)
            return carry

        lax.fori_loop(0, qi, att_body, 0)
        _flash_tile_step(own_score(), vt_of_block(qi), s_ref, mn_ref, m_ref, l_ref, acc_ref)

    _flash_finish(o_ref, l_ref, acc_ref)


def _moba(P4, TT, shift, B, S):
    tq = MOBA_BLOCK
    nblk = S // MOBA_BLOCK
    assert S % MOBA_BLOCK == 0 and nblk <= MOBA_SLOTS
    topb = min(MOBA_TOPK, nblk - 1)
    gl = N_HEADS * MOBA_SLOTS
    return pl.pallas_call(
        functools.partial(_moba_kernel, tq=tq, topb=topb),
        grid=(B, S // tq),
        in_specs=[pl.BlockSpec(memory_space=pltpu.SMEM),
                  pl.BlockSpec((None, TILE, tq), lambda b, i: (b, TT_QC, i)),
                  pl.BlockSpec((None, None, S, TILE), lambda b, i: (T_KC, b, 0, 0)),
                  pl.BlockSpec((None, TILE, S), lambda b, i: (b, TT_VC, 0))],
        out_specs=pl.BlockSpec((None, tq, TILE), lambda b, i: (b, i, 0)),
        out_shape=jax.ShapeDtypeStruct((B, S, TILE), BF16),
        scratch_shapes=[pltpu.VMEM((gl, TILE), BF16), pltpu.VMEM((gl, TILE), BF16),
                        pltpu.VMEM((N_HEADS, 2 * LANES, tq), BF16)]
                       + _flash_scratch(MOBA_BLOCK, tq),
        compiler_params=_params("arbitrary", "arbitrary"),
        name="moba",
    )(shift, TT, P4, TT)


def _post_kernel(x_ref, oa0, oa1, oa2, la0, la1, la2, ob_ref, oc_ref, z0, z1, z2, g0, g1, g2,
                 wbr_ref, wout_ref, expand_ref, out_ref):
    l0, l1, l2 = la0[...], la1[...], la2[...]
    mx = jnp.maximum(jnp.maximum(l0, l1), l2)
    e0, e1, e2 = jnp.exp2(l0 - mx), jnp.exp2(l1 - mx), jnp.exp2(l2 - mx)
    den = e0 + e1 + e2

    def spread(w):
        hi = w.astype(BF16)
        lo = (w - hi.astype(F32)).astype(BF16)
        return _dot(hi, expand_ref[...]) + _dot(lo, expand_ref[...])

    o_a = (spread(e0 / den) * oa0[...].astype(F32) + spread(e1 / den) * oa1[...].astype(F32)
           + spread(e2 / den) * oa2[...].astype(F32))
    branches = (o_a, ob_ref[...].astype(F32), oc_ref[...].astype(F32))
    merged = jnp.zeros(out_ref.shape, F32)
    for n, (o, z, g) in enumerate(zip(branches, (z0, z1, z2), (g0, g1, g2))):
        y = _dot((o * z[0].astype(F32)).astype(BF16), wbr_ref[n])
        gate = jnp.concatenate([g[t] for t in range(g.shape[0])], axis=1)
        merged = merged + gate.astype(F32) * y
    out_ref[...] = x_ref[...] + _dot(merged.astype(BF16), wout_ref[...])


def _post(x2, oa, la, ob, oc, P, wbr, wout, tm):
    T = x2.shape[0]
    row = lambda width, t: pl.BlockSpec((tm, width), lambda i: (i, t))
    tiles = lambda count, first: pl.BlockSpec((count, tm, TILE), lambda i: (first // count, i, 0))
    per_g = D_MODEL // TILE
    assert T_G % per_g == 0
    head_of = np.arange(TILE) // HEAD_DIM
    expand = jnp.asarray((np.arange(LANES)[:, None] == head_of[None, :]).astype(np.float32), BF16)
    in_specs = ([row(D_MODEL, 0)] + [row(TILE, 0)] * 3 + [row(LANES, 0)] * 3 + [row(TILE, 0)] * 2
                + [tiles(1, T_Z + n) for n in range(N_BRANCH)]
                + [tiles(per_g, T_G + per_g * n) for n in range(N_BRANCH)]
                + [pl.BlockSpec((N_BRANCH, BRANCH_WIDTH, D_MODEL), lambda i: (0, 0, 0)),
                   pl.BlockSpec((D_MODEL, D_MODEL), lambda i: (0, 0)),
                   pl.BlockSpec((LANES, TILE), lambda i: (0, 0))])
    return pl.pallas_call(
        _post_kernel,
        grid=(T // tm,),
        in_specs=in_specs,
        out_specs=row(D_MODEL, 0),
        out_shape=jax.ShapeDtypeStruct((T, D_MODEL), F32),
        compiler_params=_params("arbitrary"),
        name="post",
    )(x2, oa[0], oa[1], oa[2], la[0], la[1], la[2], ob, oc, P, P, P, P, P, P, wbr, wout, expand)


def _rearrange_w_in(w):
    bw = BRANCH_WIDTH
    a_q, a_k, a_v = w[:, 0:3 * bw], w[:, 3 * bw:6 * bw], w[:, 6 * bw:9 * bw]
    off = 9 * bw
    b_q, b_k, b_v = (w[:, off + i * bw:off + (i + 1) * bw] for i in range(3)); off += 3 * bw
    iq = w[:, off:off + IDX_HEADS * IDX_DIM]; off += IDX_HEADS * IDX_DIM
    ik = w[:, off:off + IDX_DIM]; off += IDX_DIM
    iw = w[:, off:off + IDX_HEADS]; off += IDX_HEADS
    c_q, c_k, c_v = (w[:, off + i * bw:off + (i + 1) * bw] for i in range(3)); off += 3 * bw
    z = w[:, off:off + 3 * bw]; off += 3 * bw
    g = w[:, off:off + 3 * D_MODEL]; off += 3 * D_MODEL
    assert off == w.shape[1]
    cols = []
    for grp in (1, 2, 0):
        sl = slice(grp * bw, (grp + 1) * bw)
        cols += [a_q[:, sl], a_k[:, sl], a_v[:, sl]]
    pad = jnp.zeros((w.shape[0], TILE - IDX_DIM - IDX_HEADS), w.dtype)
    cols += [b_k, ik, iw, pad, c_k, g, z]
    w_nat = jnp.concatenate(cols, axis=1).astype(BF16)
    assert w_nat.shape[1] == J_T * TILE
    w_nat = w_nat.reshape(-1, J_T, TILE).transpose(1, 0, 2)
    w_t = jnp.concatenate([b_q, b_v, iq, c_q, c_v], axis=1).T.astype(BF16)
    assert w_t.shape[0] == N_T_TILES * TILE
    return w_nat, w_t


def _tile_tables(qk_g):
    kinds = np.zeros((N_TILES,), np.int32)
    scale = np.array([HEAD_DIM ** -0.5 * LOG2_E, 1.0], np.float32)
    rows = jnp.tile(qk_g.astype(F32) * scale[None, :, None], (1, 1, N_HEADS)).reshape(2 * N_BRANCH, TILE)
    rows = jnp.concatenate([rows, jnp.full((1, TILE), IDX_DIM ** -0.5, F32), jnp.ones((1, TILE), F32)])
    row_iq, row_one = 2 * N_BRANCH, 2 * N_BRANCH + 1
    which = np.full((N_TILES,), row_one, np.int32)

    def qk(tile, mixer):
        kinds[tile] = kinds[tile + 1] = KIND_NORM_ROPE
        which[tile], which[tile + 1] = 2 * mixer, 2 * mixer + 1

    qk(0, 0)
    qk(N_FOLD_TILES // 2, 0)
    qk(J_NAT + T_A, 0)
    for tile, mixer in ((T_KB, 1), (T_KC, 2)):
        kinds[J_NAT + tile] = KIND_NORM_ROPE
        which[J_NAT + tile] = 2 * mixer + 1
    kinds[J_NAT + T_IKW] = KIND_ROPE_LOW
    kinds[J_NAT + T_Z:J_NAT + T_Z + 3] = KIND_SILU
    kinds[J_NAT + T_G:J_NAT + T_G + 6] = KIND_SIGMOID
    for tile, mixer in ((TT_QB, 1), (TT_QC, 2)):
        kinds[J_T + tile] = KIND_NORM_ROPE
        which[J_T + tile] = 2 * mixer
    kinds[J_T + TT_IQ] = KIND_ROPE
    which[J_T + TT_IQ] = row_iq
    gains = rows[which[:J_T]][:, None, :]
    gains_t = jnp.broadcast_to(rows[which[J_T:]][:, :, None], (N_T_TILES, TILE, LANES))
    half = N_FOLD_TILES // 2
    dest = np.array([DEST_F1] * half + [DEST_F2] * half + [DEST_NAT] * N_NAT_TILES + [DEST_T] * N_T_TILES)
    assert all((k, d) in _ROUTES for k, d in zip(kinds.tolist(), dest.tolist()))
    routes = _route_code(kinds, dest).astype(np.int32)
    return jnp.asarray(routes), gains, gains_t


def _softmax_shift(qk_gain):
    bound = HEAD_DIM * jnp.max(jnp.abs(qk_gain[0])) * jnp.max(jnp.abs(qk_gain[1]))
    return (SHIFT_SLACK * HEAD_DIM ** -0.5 * LOG2_E * bound).reshape(1).astype(BF16).astype(F32)


def _rope_tables(positions):
    inv = ROPE_THETA ** (-jnp.arange(0, ROT_DIM, 2, dtype=F32) / ROT_DIM)
    ang = positions.astype(F32).reshape(-1)[:, None] * inv
    cos, sin = jnp.cos(ang), jnp.sin(ang)
    T = cos.shape[0]
    z8 = jnp.zeros((T, ROT_HALF), F32)
    rest1 = jnp.ones((T, HEAD_DIM - ROT_DIM), F32)
    rest0 = jnp.zeros((T, HEAD_DIM - ROT_DIM), F32)
    c = jnp.concatenate([cos, cos, rest1], axis=1)
    s1 = jnp.concatenate([-sin, z8, rest0], axis=1)
    s2 = jnp.concatenate([z8, sin, rest0], axis=1)
    two = lambda t: jnp.concatenate([t, t], axis=1)
    return (two(c), two(s1), two(s2)), (c.T, s1.T, s2.T)


def _block_diag_mean():
    h = np.arange(MXU_DIM) // HEAD_DIM
    return jnp.asarray((h[:, None] == h[None, :]).astype(np.float32) / HEAD_DIM, BF16)


def _layer(x2, B, S, tabs, tabs_t, bd, norm_g, w_in, qk_g, w_br, w_out, tm_in, tm_post):
    routes, gains, gains_t = _tile_tables(qk_g)
    w_nat, w_t = _rearrange_w_in(w_in)
    P, f1, f2, TT = _inproj(x2, norm_g[None, :], w_nat, w_t, routes, gains, gains_t, tabs, tabs_t, bd,
                            tm_in, B, S)
    P4 = P.reshape(N_NAT_TILES, B, S, TILE)
    oa, la = [], []
    for g, qkv in enumerate((P4.reshape(N_NAT_TILES, B, 1, S, TILE), f1, f2)):
        o, lse = _attn_a(qkv, _softmax_shift(qk_g[0]), g, DIL_PATTERNS[g][1], B, S)
        oa.append(o.reshape(B * S, TILE))
        la.append(lse.reshape(B * S, LANES))
    ob = _dsa(P4, TT, _softmax_shift(qk_g[1]), B, S).reshape(B * S, TILE)
    oc = _moba(P4, TT, _softmax_shift(qk_g[2]), B, S).reshape(B * S, TILE)
    return _post(x2, oa, la, ob, oc, P, w_br.astype(BF16), w_out.astype(BF16), tm_post)


def _forward(x, positions, norm_g, w_in, qk_g, w_br, w_out, tm_in=IN_ROW_TILE, tm_post=POST_ROW_TILE):
    B, S, D = x.shape
    tabs, tabs_t = _rope_tables(positions)
    bd = _block_diag_mean()
    x2 = x.reshape(B * S, D)
    for layer in range(norm_g.shape[0]):
        x2 = _layer(x2, B, S, tabs, tabs_t, bd, norm_g[layer], w_in[layer], qk_g[layer],
                    w_br[layer], w_out[layer], tm_in, tm_post)
    return x2.reshape(B, S, D)


def kernel(x, positions, norm_g, w_in, qk_g, w_br, w_out):
    return _forward(x, positions, norm_g, w_in, qk_g, w_br, w_out)
```

```python
import functools

import jax
import jax.numpy as jnp
import numpy as np
from jax import lax
from jax.experimental import pallas as pl
from jax.experimental.pallas import tpu as pltpu

F32 = jnp.float32
BF16 = jnp.bfloat16
I32 = jnp.int32
I16 = jnp.int16

D_MODEL = 1024
HEAD_DIM = 64
ROT_DIM = HEAD_DIM // 4
ROT_HALF = ROT_DIM // 2
ROPE_THETA = 500000.0
NORM_EPS = 1e-6
NEG_INF = -1e30
LOG2_E = 1.4426950408889634
MAX_SOFTMAX_SHIFT = 60.0
SHIFT_SLACK = 1.05
N_HEADS = 8
BRANCH_WIDTH = N_HEADS * HEAD_DIM
N_BRANCH = 3
DIL_PATTERNS = ((128, 1), (512, 4), (2048, 16))
BAND = 128
IDX_HEADS = 8
IDX_DIM = 64
IDX_TOPK_MAX = 256
MOBA_BLOCK = 256
MOBA_TOPK = 3
MOBA_SLOTS = 16

LANES = 128
SUBLANES = 8
MXU_DIM = 256
VMEM_LIMIT_BYTES = 56 * 1024 * 1024

TILE = BRANCH_WIDTH
N_FOLD_TILES = 6
T_A = 0
T_KB = 3
T_IKW = 4
T_KC = 5
T_G = 6
T_Z = 12
N_NAT_TILES = 15
TT_QB, TT_VB, TT_IQ, TT_QC, TT_VC = range(5)
N_T_TILES = 5
J_NAT = N_FOLD_TILES
J_T = N_FOLD_TILES + N_NAT_TILES
N_TILES = J_T + N_T_TILES
IN_ROW_TILE = 2048
IN_CHUNK = 256
A_STEP_TOKENS = 2048
GROUP = 4
SPARSE_TILE = 256
POST_ROW_TILE = 512

KIND_PLAIN, KIND_NORM_ROPE, KIND_ROPE, KIND_ROPE_LOW, KIND_SILU, KIND_SIGMOID = range(6)
DEST_NAT, DEST_F1, DEST_F2, DEST_T = range(4)
_ROUTES = ((KIND_PLAIN, DEST_NAT), (KIND_PLAIN, DEST_F1), (KIND_PLAIN, DEST_F2),
           (KIND_NORM_ROPE, DEST_NAT), (KIND_NORM_ROPE, DEST_F1), (KIND_NORM_ROPE, DEST_F2),
           (KIND_ROPE_LOW, DEST_NAT), (KIND_SILU, DEST_NAT), (KIND_SIGMOID, DEST_NAT),
           (KIND_PLAIN, DEST_T), (KIND_NORM_ROPE, DEST_T), (KIND_ROPE, DEST_T))


def _route_code(kind, dest):
    return kind * 4 + dest

_FLOOR_KEY = int(np.array(-5e29, np.float32).view(np.int32)) ^ 0x7FFFFFFF
_INT_MIN = -2 ** 31
_HALF16 = 32768
_PACK16 = 16
_LOG2_MOBA_BLOCK = MOBA_BLOCK.bit_length() - 1
_LOG2_N_HEADS = N_HEADS.bit_length() - 1
_LOG2_HEAD_DIM = HEAD_DIM.bit_length() - 1


def _dot(a, b):
    return jnp.dot(a, b, preferred_element_type=F32)


def _dot_nt(a, b):
    return lax.dot_general(a, b, (((1,), (1,)), ((), ())), preferred_element_type=F32)


def _params(*sem):
    return pltpu.CompilerParams(dimension_semantics=sem, vmem_limit_bytes=VMEM_LIMIT_BYTES)


def _inproj_kernel(route_ref, x_ref, ng_ref, w_ref, wt_ref, gain_ref, gain_t_ref,
                   cos_ref, s1_ref, s2_ref, cos_t_ref, s1_t_ref, s2_t_ref, bd_ref,
                   o_ref, f1_ref, f2_ref, tt_ref, h_ref, ht_ref, stage_ref):
    j = pl.program_id(1)
    tm = x_ref.shape[0]
    n_ch = TILE // LANES

    @pl.when(j == 0)
    def _():
        x = x_ref[...]
        ms = jnp.mean(x * x, axis=-1, keepdims=True)
        h = x * lax.rsqrt(ms + NORM_EPS) * ng_ref[...]
        h_ref[...] = h.astype(BF16)
        ht_ref[...] = jnp.transpose(h).astype(BF16)

    def fold(val, ref, dil, tok):
        rows = (tok.stop - tok.start) // dil
        for c in range(n_ch):
            stage_ref[c, tok, :] = val[:, c * LANES:(c + 1) * LANES]
        for r in range(dil):
            parts = [stage_ref[c, pl.ds(tok.start + r, rows, stride=dil), :] for c in range(n_ch)]
            ref[r, tok.start // dil:tok.stop // dil, :] = jnp.concatenate(parts, axis=1).astype(ref.dtype)

    def rope(v, low_only, tok):
        c = jnp.tile(cos_ref[tok, :], (1, n_ch))
        s1 = jnp.tile(s1_ref[tok, :], (1, n_ch))
        s2 = jnp.tile(s2_ref[tok, :], (1, n_ch))
        if low_only:
            low = lax.broadcasted_iota(I32, v.shape, 1) < HEAD_DIM
            c = jnp.where(low, c, 1.0)
            s1 = jnp.where(low, s1, 0.0)
            s2 = jnp.where(low, s2, 0.0)
        return v * c + pltpu.roll(v, TILE - ROT_HALF, 1) * s1 + pltpu.roll(v, ROT_HALF, 1) * s2

    def rope_t(v, tok):
        c = jnp.tile(cos_t_ref[:, tok], (N_HEADS, 1))
        s1 = jnp.tile(s1_t_ref[:, tok], (N_HEADS, 1))
        s2 = jnp.tile(s2_t_ref[:, tok], (N_HEADS, 1))
        return v * c + pltpu.roll(v, TILE - ROT_HALF, 0) * s1 + pltpu.roll(v, ROT_HALF, 0) * s2

    def epilogue(kind, y, tok):
        if kind == KIND_PLAIN:
            return y
        if kind == KIND_NORM_ROPE:
            y2 = (y * y).astype(BF16)
            ms = jnp.concatenate([_dot(y2[:, c * MXU_DIM:(c + 1) * MXU_DIM], bd_ref[...])
                                  for c in range(TILE // MXU_DIM)], axis=1)
            return rope(y * lax.rsqrt(ms + NORM_EPS) * gain_ref[0], False, tok)
        if kind == KIND_ROPE_LOW:
            return rope(y, True, tok)
        if kind == KIND_SILU:
            return y / (1.0 + jnp.exp(-y))
        assert kind == KIND_SIGMOID
        return 1.0 / (1.0 + jnp.exp(-y))

    def epilogue_t(kind, yt, tok):
        if kind == KIND_PLAIN:
            return yt
        gain = jnp.tile(gain_t_ref[0], (1, yt.shape[1] // LANES))
        if kind == KIND_NORM_ROPE:
            y2 = (yt * yt).astype(BF16)
            ms = jnp.concatenate([_dot(bd_ref[...], y2[c * MXU_DIM:(c + 1) * MXU_DIM, :])
                                  for c in range(TILE // MXU_DIM)], axis=0)
            return rope_t(yt * lax.rsqrt(ms + NORM_EPS) * gain, tok)
        assert kind == KIND_ROPE
        return rope_t(yt * gain, tok)

    route = route_ref[j]
    for kind, dest in _ROUTES:
        @pl.when(route == _route_code(kind, dest))
        def _(kind=kind, dest=dest):
            if dest == DEST_T:
                tok = slice(0, tm)
                tt_ref[...] = epilogue_t(kind, _dot(wt_ref[...], ht_ref[...]), tok).astype(tt_ref.dtype)
                return
            chunk = max(IN_CHUNK, _PACK16 * DIL_PATTERNS[2][1]) if dest == DEST_F2 else IN_CHUNK
            for m in range(tm // chunk):
                tok = slice(m * chunk, (m + 1) * chunk)
                val = epilogue(kind, _dot(h_ref[tok, :], w_ref[...]), tok)
                if dest == DEST_NAT:
                    o_ref[tok, :] = val.astype(o_ref.dtype)
                elif dest == DEST_F1:
                    fold(val, f1_ref, DIL_PATTERNS[1][1], tok)
                else:
                    fold(val, f2_ref, DIL_PATTERNS[2][1], tok)


def _inproj(x2, ng, w_nat, w_t, routes, gains, gains_t, tabs, tabs_t, bd, tm, B, S):
    T = x2.shape[0]
    grid = (T // tm, N_TILES)
    per_b = S // tm
    d1, d2 = DIL_PATTERNS[1][1], DIL_PATTERNS[2][1]
    half = N_FOLD_TILES // 2
    assert S % tm == 0 and tm % (_PACK16 * d2) == 0 and tm % IN_CHUNK == 0 and IN_CHUNK % (_PACK16 * d1) == 0
    n_w = J_T

    def fold_spec(dil, first):
        return pl.BlockSpec(
            (None, None, dil, tm // dil, TILE),
            lambda i, j, k: (jnp.clip(j - first, 0, half - 1), i // per_b, 0, i % per_b, 0))

    once_per_row_tile = pl.Buffered(1)
    row_tab = pl.BlockSpec((tm, LANES), lambda i, j, k: (i, 0), pipeline_mode=once_per_row_tile)
    col_tab = pl.BlockSpec((HEAD_DIM, tm), lambda i, j, k: (0, i), pipeline_mode=once_per_row_tile)
    t_idx = lambda j: jnp.maximum(j - J_T, 0)
    return pl.pallas_call(
        _inproj_kernel,
        grid_spec=pltpu.PrefetchScalarGridSpec(
            num_scalar_prefetch=1,
            grid=grid,
            in_specs=[
                pl.BlockSpec((tm, D_MODEL), lambda i, j, k: (i, 0), pipeline_mode=once_per_row_tile),
                pl.BlockSpec((1, D_MODEL), lambda i, j, k: (0, 0)),
                pl.BlockSpec((None, D_MODEL, TILE), lambda i, j, k: (jnp.minimum(j, n_w - 1), 0, 0)),
                pl.BlockSpec((TILE, D_MODEL), lambda i, j, k: (t_idx(j), 0)),
                pl.BlockSpec((1, 1, TILE), lambda i, j, k: (jnp.minimum(j, n_w - 1), 0, 0)),
                pl.BlockSpec((1, TILE, LANES), lambda i, j, k: (t_idx(j), 0, 0)),
                row_tab, row_tab, row_tab, col_tab, col_tab, col_tab,
                pl.BlockSpec((MXU_DIM, MXU_DIM), lambda i, j, k: (0, 0)),
            ],
            out_specs=[
                pl.BlockSpec((None, tm, TILE), lambda i, j, k: (jnp.clip(j - J_NAT, 0, N_NAT_TILES - 1), i, 0)),
                fold_spec(d1, 0),
                fold_spec(d2, half),
                pl.BlockSpec((None, TILE, tm), lambda i, j, k: (i // per_b, t_idx(j), i % per_b)),
            ],
            scratch_shapes=[pltpu.VMEM((tm, D_MODEL), BF16), pltpu.VMEM((D_MODEL, tm), BF16),
                            pltpu.VMEM((TILE // LANES, tm, LANES), F32)],
        ),
        out_shape=[jax.ShapeDtypeStruct((N_NAT_TILES, T, TILE), BF16),
                   jax.ShapeDtypeStruct((half, B, d1, S // d1, TILE), BF16),
                   jax.ShapeDtypeStruct((half, B, d2, S // d2, TILE), BF16),
                   jax.ShapeDtypeStruct((B, N_T_TILES * TILE, S), BF16)],
        compiler_params=_params("arbitrary", "arbitrary"),
        name="inproj",
    )(routes, x2, ng, w_nat, w_t, gains, gains_t, *tabs, *tabs_t, bd)


def _attn_a_kernel(shift_ref, q_ref, kc_ref, kp_ref, vc_ref, vp_ref, o_ref, lse_ref, kcat, vtcat, ost, lst,
                   s_ref, p_ref, *, tq, dil):
    m = pl.program_id(1)
    nsub = tq // BAND
    n_ch = TILE // LANES
    nk = 2 * BAND
    key = lax.broadcasted_iota(I32, (nk, BAND), 0)
    qry = lax.broadcasted_iota(I32, (nk, BAND), 1) + BAND
    in_band = jnp.logical_and(key <= qry, key >= qry - BAND)
    lane = lax.broadcasted_iota(I32, (BAND, LANES), 1)
    half_mask = (jnp.where(lane < HEAD_DIM, 1.0, 0.0).astype(BF16),
                 jnp.where(lane < HEAD_DIM, 0.0, 1.0).astype(BF16))
    ones = jnp.ones((_PACK16, nk), BF16)
    shifted = shift_ref[0] <= MAX_SOFTMAX_SHIFT
    shift = jnp.where(shifted, shift_ref[0], 0.0)

    def load_residue(r, slot):
        kcat[slot, 0:BAND, :] = kp_ref[r]
        kcat[slot, BAND:, :] = kc_ref[r]
        vtcat[slot, :, 0:BAND] = jnp.transpose(vp_ref[r].astype(F32)).astype(BF16)
        for c in range(nsub):
            chunk = vc_ref[r, c * BAND:(c + 1) * BAND, :].astype(F32)
            vtcat[slot, :, (c + 1) * BAND:(c + 2) * BAND] = jnp.transpose(chunk).astype(BF16)

    def block(r, u, use_shift, slot, par):
        r0 = pl.multiple_of(u * BAND, BAND)
        q = q_ref[r, pl.ds(r0, BAND), :]
        k2 = kcat[slot, pl.ds(r0, nk), :]
        valid = jnp.logical_and(in_band, key >= jnp.where(m * nsub + u > 0, 0, BAND))
        bias = jnp.where(valid, -shift, NEG_INF)

        def score(h):
            pair = slice((h // 2) * LANES, (h // 2 + 1) * LANES)
            q_h = q[:, pair] * half_mask[h % 2]
            return _dot_nt(k2[:, pair], q_h) + bias

        def pv_of(h, p):
            vt_h = vtcat[slot, h * HEAD_DIM:(h + 1) * HEAD_DIM, pl.ds(r0, nk)]
            pv = _dot(jnp.concatenate([vt_h, ones], axis=0), p)
            return pv[:HEAD_DIM], pv[HEAD_DIM:HEAD_DIM + 1]

        outs, lses = [], []
        if use_shift:
            for h in range(N_HEADS):
                p_ref[par, h] = jnp.exp2(score(h)).astype(BF16)
            for h in range(N_HEADS):
                acc, den = pv_of(h, p_ref[par, h])
                outs.append(acc / den)
                lses.append(shift + jnp.log2(den))
        else:
            maxes = []
            for h in range(N_HEADS):
                s = score(h)
                s_ref[par, h] = s
                maxes.append(jnp.max(s, axis=0, keepdims=True))
            for h in range(N_HEADS):
                acc, den = pv_of(h, jnp.exp2(s_ref[par, h] - maxes[h]).astype(BF16))
                outs.append(acc / den)
                lses.append(maxes[h] + jnp.log2(den))
        o = jnp.transpose(jnp.concatenate(outs, axis=0))
        lse8 = jnp.concatenate(lses, axis=0)
        lse = jnp.transpose(jnp.tile(lse8, (LANES // N_HEADS, 1)))
        rows = pl.ds(r + u * (BAND * dil), BAND, stride=dil) if dil > 1 else pl.ds(r0, BAND)
        for c in range(n_ch):
            ost[c, rows, :] = o[:, c * LANES:(c + 1) * LANES]
        lst[rows, :] = lse

    def both_paths(blocks):
        @pl.when(shifted)
        def _():
            for r, u, slot, par in blocks:
                block(r, u, True, slot, par)

        @pl.when(jnp.logical_not(shifted))
        def _():
            for r, u, slot, par in blocks:
                block(r, u, False, slot, par)

    if nsub % 2 == 0:
        def residue(r, carry):
            load_residue(r, 0)

            def body(i, carry):
                both_paths([(r, 2 * i, 0, 0), (r, 2 * i + 1, 0, 1)])
                return carry

            return lax.fori_loop(0, nsub // 2, body, carry)

        lax.fori_loop(0, dil, residue, 0)
    else:
        assert nsub == 1 and dil % 2 == 0

        def residue_pair(i, carry):
            load_residue(2 * i, 0)
            load_residue(2 * i + 1, 1)
            both_paths([(2 * i, 0, 0, 0), (2 * i + 1, 0, 1, 1)])
            return carry

        lax.fori_loop(0, dil // 2, residue_pair, 0)
    o_ref[0] = jnp.concatenate([ost[c] for c in range(n_ch)], axis=1).astype(o_ref.dtype)
    lse_ref[0] = lst[...]


def _attn_a(qkv, shift, g, dil, B, S):
    ts = min(A_STEP_TOKENS, S)
    tq = ts // dil
    assert S % ts == 0 and tq % BAND == 0
    sub = tq // BAND

    def spec_cur(t):
        return pl.BlockSpec((None, None, dil, tq, TILE), lambda b, m: (t, b, 0, m, 0))

    def spec_prev(t):
        return pl.BlockSpec((None, None, dil, BAND, TILE),
                            lambda b, m: (t, b, 0, jnp.maximum(m * sub - 1, 0), 0))

    n_ch = TILE // LANES
    return pl.pallas_call(
        functools.partial(_attn_a_kernel, tq=tq, dil=dil),
        grid=(B, S // ts),
        in_specs=[pl.BlockSpec(memory_space=pltpu.SMEM),
                  spec_cur(0), spec_cur(1), spec_prev(1), spec_cur(2), spec_prev(2)],
        out_specs=[pl.BlockSpec((1, ts, TILE), lambda b, m: (b, m, 0)),
                   pl.BlockSpec((1, ts, LANES), lambda b, m: (b, m, 0))],
        out_shape=[jax.ShapeDtypeStruct((B, S, TILE), BF16), jax.ShapeDtypeStruct((B, S, LANES), F32)],
        scratch_shapes=[pltpu.VMEM((2, tq + BAND, TILE), BF16), pltpu.VMEM((2, TILE, tq + BAND), BF16),
                        pltpu.VMEM((n_ch, ts, LANES), F32), pltpu.VMEM((ts, LANES), F32),
                        pltpu.VMEM((2, N_HEADS, 2 * BAND, BAND), F32),
                        pltpu.VMEM((2, N_HEADS, 2 * BAND, BAND), BF16)],
        compiler_params=_params("arbitrary", "arbitrary"),
        name=f"attn_a{g}",
    )(shift, qkv, qkv, qkv, qkv, qkv)


def _pad_q(qt_ref, qpad_ref):
    qpad_ref[...] = jnp.zeros(qpad_ref.shape, qpad_ref.dtype)
    for h in range(N_HEADS):
        r0 = h * LANES + (h % 2) * HEAD_DIM
        qpad_ref[r0:r0 + HEAD_DIM, :] = qt_ref[h * HEAD_DIM:(h + 1) * HEAD_DIM, :]


def _flash_init(m_ref, l_ref, acc_ref):
    m_ref[...] = jnp.full(m_ref.shape, NEG_INF, F32)
    l_ref[...] = jnp.zeros(l_ref.shape, F32)
    acc_ref[...] = jnp.zeros(acc_ref.shape, F32)


def _flash_tile_step(score_of_head, vt_of_head, s_ref, mn_ref, m_ref, l_ref, acc_ref):
    tk = s_ref.shape[1]
    for h in range(N_HEADS):
        s = score_of_head(h)
        s_ref[h] = s
        mn_ref[h] = jnp.maximum(m_ref[h], jnp.max(s, axis=0, keepdims=True))
    ones = jnp.ones((_PACK16, tk), BF16)
    for h in range(N_HEADS):
        m_prev, m_next = m_ref[h], mn_ref[h]
        p = jnp.exp2(s_ref[h] - jnp.tile(m_next, (tk // SUBLANES, 1)))
        alpha = jnp.exp2(m_prev - m_next)
        pv = _dot(jnp.concatenate([vt_of_head(h), ones], axis=0), p.astype(BF16))
        l_ref[h] = alpha * l_ref[h] + pv[HEAD_DIM:HEAD_DIM + SUBLANES]
        acc_ref[h] = acc_ref[h] * jnp.tile(alpha, (HEAD_DIM // SUBLANES, 1)) + pv[:HEAD_DIM]
        m_ref[h] = m_next


def _flash_tiles_shifted(tiles, p_ref, l_ref, acc_ref):
    for t, (score_of_head, _) in enumerate(tiles):
        for h in range(N_HEADS):
            p_ref[t, h] = jnp.exp2(score_of_head(h)).astype(BF16)
    ones = jnp.ones((_PACK16, p_ref.shape[2]), BF16)
    for t, (_, vt_of_head) in enumerate(tiles):
        for h in range(N_HEADS):
            pv = _dot(jnp.concatenate([vt_of_head(h), ones], axis=0), p_ref[t, h])
            l_ref[h] = l_ref[h] + pv[HEAD_DIM:HEAD_DIM + SUBLANES]
            acc_ref[h] = acc_ref[h] + pv[:HEAD_DIM]


def _grouped_loop(n, group, smallest=1):
    def body(i, carry):
        group(i * GROUP, GROUP)
        return carry

    lax.fori_loop(0, lax.shift_right_logical(n, GROUP.bit_length() - 1), body, 0)
    size = GROUP // 2
    while size >= smallest:
        @pl.when((n & size) != 0)
        def _(size=size):
            group((n >> size.bit_length()) << size.bit_length(), size)
        size //= 2


def _flash_finish(o_ref, l_ref, acc_ref):
    outs = [acc_ref[h] / jnp.tile(l_ref[h], (HEAD_DIM // SUBLANES, 1)) for h in range(N_HEADS)]
    o_ref[...] = jnp.transpose(jnp.concatenate(outs, axis=0)).astype(o_ref.dtype)


def _dsa_kernel(shift_ref, qt_ref, k_ref, vt_ref, iqt_ref, ikw_q_ref, ikw_ref, o_ref,
                key_ref, hi_ref, lo_ref, qpad_ref, bias_ref, p_ref, s_ref, mn_ref, m_ref, l_ref, acc_ref,
                *, tq, tk, topk):
    qi = pl.program_id(1)
    n_kt = qi + 1
    n_acc = 4
    rows8 = tk // SUBLANES

    w8 = jnp.transpose(ikw_q_ref[...].astype(F32))[HEAD_DIM:HEAD_DIM + IDX_HEADS, :] * (IDX_HEADS ** -0.5)
    krow = lax.broadcasted_iota(I32, (tk, tq), 0)
    qcol = lax.broadcasted_iota(I32, (tk, tq), 1)

    def score_tile(c, diagonal):
        c0 = pl.multiple_of(c * tk, tk)
        kx = ikw_ref[pl.ds(c0, tk), :][:, :IDX_DIM]
        sc = jnp.zeros((tk, tq), F32)
        for h in range(IDX_HEADS):
            lg = _dot(kx, iqt_ref[h * IDX_DIM:(h + 1) * IDX_DIM, :])
            sc = sc + w8[h:h + 1, :] * jnp.maximum(lg, 0.0)
        sc = jnp.where(sc == 0.0, 0.0, sc)
        if diagonal:
            sc = jnp.where(krow <= qcol, sc, NEG_INF)
        bits = pltpu.bitcast(sc, I32)
        key = jnp.where(bits < 0, bits ^ 0x7FFFFFFF, bits)
        key_ref[pl.ds(c0, tk), :] = key
        hi_ref[pl.ds(c0, tk), :] = jnp.right_shift(key, 16).astype(I16)
        lo_ref[pl.ds(c0, tk), :] = ((key & 0xFFFF) - _HALF16).astype(I16)

    _grouped_loop(qi, lambda c0, count: [score_tile(c0 + t, False) for t in range(count)], smallest=2)

    @pl.when((qi & 1) == 1)
    def _():
        score_tile(qi - 1, False)
        score_tile(qi, True)

    @pl.when((qi & 1) == 0)
    def _():
        score_tile(qi, True)

    rows16 = tk // _PACK16
    one16 = jnp.ones((_PACK16, tq), I16)
    zero16 = jnp.zeros((_PACK16, tq), I16)

    def count16(ref, pred):
        def cbody(c, accs):
            accs = list(accs)
            c0 = pl.multiple_of(c * tk, tk)
            t = ref[pl.ds(c0, tk), :]
            for g in range(rows16):
                hit = jnp.where(pred(t[g * _PACK16:(g + 1) * _PACK16]), one16, zero16)
                accs[g % n_acc] = accs[g % n_acc] + hit
            return tuple(accs)
        accs = lax.fori_loop(0, n_kt, cbody, tuple(zero16 for _ in range(n_acc)))
        return jnp.sum(functools.reduce(lambda a, b: a + b, accs).astype(I32), axis=0, keepdims=True)

    def as16(v):
        return jnp.broadcast_to(v, (_PACK16, tq)).astype(I16)

    def select16(ref, need, cge0):
        def bit_body(it, carry):
            ans, cge = carry
            cand_u = ans | lax.shift_left(jnp.int32(1), 15 - it)
            cand = as16(cand_u - _HALF16)
            cnt = count16(ref, lambda t: t >= cand)
            ok = cnt >= need
            return jnp.where(ok, cand_u, ans), jnp.where(ok, cnt, cge)
        return lax.fori_loop(0, 16, bit_body, (jnp.zeros((1, tq), I32), cge0))

    n_all = jnp.zeros((1, tq), I32) + n_kt * tk
    p_u, cge_hi = select16(hi_ref, topk, n_all)
    p16 = as16(p_u - _HALF16)
    c_gt = count16(hi_ref, lambda t: t > p16)

    def bucket_body(c, carry):
        c0 = pl.multiple_of(c * tk, tk)
        lo_ref[pl.ds(c0, tk), :] = jnp.where(hi_ref[pl.ds(c0, tk), :] == jnp.tile(p16, (rows16, 1)),
                                             lo_ref[pl.ds(c0, tk), :], jnp.int16(-_HALF16))
        return carry

    lax.fori_loop(0, n_kt, bucket_body, 0)
    l_u, cge_lo = select16(lo_ref, topk - c_gt, cge_hi - c_gt)
    cge = c_gt + cge_lo
    thr = jnp.maximum((p_u - _HALF16) * 65536 + l_u, _FLOOR_KEY)

    def count_rows(fn, n_out):
        def cbody(c, accs):
            accs = [list(a) for a in accs]
            c0 = pl.multiple_of(c * tk, tk)
            kt = key_ref[pl.ds(c0, tk), :]
            for g in range(rows8):
                r0 = c0 + g * SUBLANES
                vals = fn(kt[g * SUBLANES:(g + 1) * SUBLANES], r0)
                for o in range(n_out):
                    accs[o][g % n_acc] = accs[o][g % n_acc] + vals[o]
            return tuple(tuple(a) for a in accs)
        z = jnp.zeros((SUBLANES, tq), I32)
        accs = lax.fori_loop(0, n_kt, cbody, tuple(tuple(z for _ in range(n_acc)) for _ in range(n_out)))
        return [jnp.sum(functools.reduce(lambda a, b: a + b, a), axis=0, keepdims=True) for a in accs]

    tie = jnp.logical_and(cge > topk, thr > _FLOOR_KEY)
    any_tie = jnp.max(jnp.where(tie, 1, 0)) > 0
    thr8 = jnp.broadcast_to(thr, (SUBLANES, tq))

    @pl.when(any_tie)
    def _():
        n_bits = int(np.log2(key_ref.shape[0]))
        sub = lax.broadcasted_iota(I32, (SUBLANES, tq), 0)

        def jbody(it, lo):
            cand = lo + lax.shift_left(jnp.int32(1), n_bits - 1 - it)
            pos = jnp.broadcast_to(cand - 1, (SUBLANES, tq))

            def f(kk, r0):
                eq = jnp.logical_and(kk == thr8, sub + r0 <= pos)
                return jnp.where(kk > thr8, 1, 0), jnp.where(eq, 1, 0)

            gt, eq = count_rows(f, 2)
            return jnp.where(gt + eq >= topk, lo, cand)

        jrow = jnp.broadcast_to(lax.fori_loop(0, n_bits, jbody, jnp.zeros((1, tq), I32)), (SUBLANES, tq))
        tie8 = jnp.broadcast_to(jnp.where(tie, 1, 0), (SUBLANES, tq)) > 0

        def fix(c, carry):
            for g in range(rows8):
                r0 = pl.multiple_of(c * tk + g * SUBLANES, SUBLANES)
                kk = key_ref[pl.ds(r0, SUBLANES), :]
                drop = jnp.logical_and(tie8, jnp.logical_and(kk == thr8, sub + r0 > jrow))
                key_ref[pl.ds(r0, SUBLANES), :] = jnp.where(drop, _INT_MIN, kk)
            return carry

        lax.fori_loop(0, n_kt, fix, 0)

    _pad_q(qt_ref, qpad_ref)
    _flash_init(m_ref, l_ref, acc_ref)
    shift = shift_ref[0]

    def tile(j, t, shifted):
        c0 = pl.multiple_of(j * tk, tk)
        sel = key_ref[pl.ds(c0, tk), :] >= thr
        bias_ref[t] = jnp.where(sel, -shift if shifted else 0.0, NEG_INF)
        k = k_ref[pl.ds(c0, tk), :]

        def score(h):
            k_pair = k[:, (h // 2) * LANES:(h // 2 + 1) * LANES]
            return _dot(k_pair, qpad_ref[h * LANES:(h + 1) * LANES, :]) + bias_ref[t]

        return score, lambda h: vt_ref[h * HEAD_DIM:(h + 1) * HEAD_DIM, pl.ds(c0, tk)]

    @pl.when(shift <= MAX_SOFTMAX_SHIFT)
    def _():
        def group(j0, count):
            _flash_tiles_shifted([tile(j0 + t, t, True) for t in range(count)], p_ref, l_ref, acc_ref)

        _grouped_loop(n_kt, group)

    @pl.when(shift > MAX_SOFTMAX_SHIFT)
    def _():
        def att_body(j, carry):
            _flash_tile_step(*tile(j, 0, False), s_ref, mn_ref, m_ref, l_ref, acc_ref)
            return carry

        lax.fori_loop(0, n_kt, att_body, 0)

    _flash_finish(o_ref, l_ref, acc_ref)


def _flash_scratch(tk, tq):
    stat = pltpu.VMEM((N_HEADS, SUBLANES, tq), F32)
    return [pltpu.VMEM((GROUP, N_HEADS, tk, tq), BF16), pltpu.VMEM((N_HEADS, tk, tq), F32), stat, stat, stat,
            pltpu.VMEM((N_HEADS, HEAD_DIM, tq), F32)]


def _dsa(P4, TT, shift, B, S):
    tq = tk = SPARSE_TILE
    topk = min(IDX_TOPK_MAX, S // 4)
    assert S % tq == 0 and topk <= tk
    def tt_q(t):
        return pl.BlockSpec((None, TILE, tq), lambda b, i: (b, t, i))

    return pl.pallas_call(
        functools.partial(_dsa_kernel, tq=tq, tk=tk, topk=topk),
        grid=(B, S // tq),
        in_specs=[pl.BlockSpec(memory_space=pltpu.SMEM),
                  tt_q(TT_QB),
                  pl.BlockSpec((None, None, S, TILE), lambda b, i: (T_KB, b, 0, 0)),
                  pl.BlockSpec((None, TILE, S), lambda b, i: (b, TT_VB, 0)),
                  tt_q(TT_IQ),
                  pl.BlockSpec((None, None, tq, LANES), lambda b, i: (T_IKW, b, i, 0)),
                  pl.BlockSpec((None, None, S, LANES), lambda b, i: (T_IKW, b, 0, 0))],
        out_specs=pl.BlockSpec((None, tq, TILE), lambda b, i: (b, i, 0)),
        out_shape=jax.ShapeDtypeStruct((B, S, TILE), BF16),
        scratch_shapes=[pltpu.VMEM((S, tq), I32), pltpu.VMEM((S, tq), I16), pltpu.VMEM((S, tq), I16),
                        pltpu.VMEM((N_HEADS * LANES, tq), BF16),
                        pltpu.VMEM((GROUP, tk, tq), F32)] + _flash_scratch(tk, tq),
        compiler_params=_params("arbitrary", "arbitrary"),
        name="dsa",
    )(shift, TT, P4, TT, TT, P4, P4)


def _moba_kernel(shift_ref, qt_ref, k_ref, vt_ref, o_ref, kmh_ref, kml_ref, qaug_ref,
                 p_ref, s_ref, mn_ref, m_ref, l_ref, acc_ref, *, tq, topb):
    qi = pl.program_id(1)
    S = k_ref.shape[0]
    gl = N_HEADS * MOBA_SLOTS

    @pl.when(qi == 0)
    def _():
        blk_row = jnp.right_shift(lax.broadcasted_iota(I32, (gl, S), 0), _LOG2_N_HEADS)
        blk_col = jnp.right_shift(lax.broadcasted_iota(I32, (gl, S), 1), _LOG2_MOBA_BLOCK)
        avg = jnp.where(blk_row == blk_col, 1.0 / MOBA_BLOCK, 0.0).astype(BF16)
        km = _dot(avg, k_ref[...])
        r_head = lax.broadcasted_iota(I32, (gl, TILE), 0) & (N_HEADS - 1)
        c_head = jnp.right_shift(lax.broadcasted_iota(I32, (gl, TILE), 1), _LOG2_HEAD_DIM)
        km = jnp.where(r_head == c_head, km, 0.0)
        hi = km.astype(BF16)
        kmh_ref[...] = hi
        kml_ref[...] = (km - hi.astype(F32)).astype(BF16)

    shifted = shift_ref[0] <= MAX_SOFTMAX_SHIFT
    shift = jnp.where(shifted, shift_ref[0], 0.0)
    qt = qt_ref[...]
    gate = _dot(kmh_ref[...], qt) + _dot(kml_ref[...], qt)
    row = lax.broadcasted_iota(I32, (gl, tq), 0)
    gate = jnp.where(jnp.right_shift(row, _LOG2_N_HEADS) < qi, gate, NEG_INF)
    g = [gate[n * N_HEADS:(n + 1) * N_HEADS] for n in range(MOBA_SLOTS)]
    biases = []
    for n in range(MOBA_SLOTS):
        beaten = jnp.where(n < qi, 0, topb) + jnp.zeros((N_HEADS, tq), I32)
        for n2 in range(MOBA_SLOTS):
            if n2 != n:
                beaten = beaten + jnp.where((g[n2] >= g[n]) if n2 < n else (g[n2] > g[n]), 1, 0)
        biases.append(jnp.where(beaten < topb, -shift, NEG_INF))
    selb = jnp.concatenate(biases, axis=0)

    qaug_ref[...] = jnp.zeros(qaug_ref.shape, qaug_ref.dtype)
    for h in range(N_HEADS):
        r0 = (h % 2) * HEAD_DIM
        qaug_ref[h, r0:r0 + HEAD_DIM, :] = qt_ref[h * HEAD_DIM:(h + 1) * HEAD_DIM, :]
        qaug_ref[h, LANES:, :] = jnp.where((row & (N_HEADS - 1)) == h, selb, 0.0).astype(BF16)
    _flash_init(m_ref, l_ref, acc_ref)
    lane_blk = jnp.right_shift(lax.broadcasted_iota(I32, (MOBA_BLOCK, LANES), 1), _LOG2_N_HEADS)

    def vt_of_block(n):
        c0 = pl.multiple_of(n * MOBA_BLOCK, MOBA_BLOCK)
        return lambda h: vt_ref[h * HEAD_DIM:(h + 1) * HEAD_DIM, pl.ds(c0, MOBA_BLOCK)]

    def past_score(n):
        k = k_ref[pl.ds(pl.multiple_of(n * MOBA_BLOCK, MOBA_BLOCK), MOBA_BLOCK), :]
        onehot = jnp.where(lane_blk == n, 1.0, 0.0).astype(BF16)

        def score(h):
            k_aug = jnp.concatenate([k[:, (h // 2) * LANES:(h // 2 + 1) * LANES], onehot], axis=1)
            return _dot(k_aug, qaug_ref[h])

        return score

    def own_score():
        k = k_ref[pl.ds(pl.multiple_of(qi * MOBA_BLOCK, MOBA_BLOCK), MOBA_BLOCK), :]
        causal = jnp.where(lax.broadcasted_iota(I32, (MOBA_BLOCK, tq), 0)
                           <= lax.broadcasted_iota(I32, (MOBA_BLOCK, tq), 1), -shift, NEG_INF)
        return lambda h: _dot(k[:, (h // 2) * LANES:(h // 2 + 1) * LANES], qaug_ref[h, :LANES, :]) + causal

    @pl.when(shifted)
    def _():
        past = lambda n: (past_score(n), vt_of_block(n))
        own = lambda: (own_score(), vt_of_block(qi))

        def group(n0, count):
            _flash_tiles_shifted([past(n0 + t) for t in range(count)], p_ref, l_ref, acc_ref)

        _grouped_loop(qi, group, smallest=2)

        @pl.when((qi & 1) == 1)
        def _():
            _flash_tiles_shifted([past(qi - 1), own()], p_ref, l_ref, acc_ref)

        @pl.when((qi & 1) == 0)
        def _():
            _flash_tiles_shifted([own()], p_ref, l_ref, acc_ref)

    @pl.when(jnp.logical_not(shifted))
    def _():
        def att_body(n, carry):
            _flash_tile_step(past_score(n), vt_of_block(n), s_ref, mn_ref, m_ref, l_ref, acc_ref)
            return carry

        lax.fori_loop(0, qi, att_body, 0)
        _flash_tile_step(own_score(), vt_of_block(qi), s_ref, mn_ref, m_ref, l_ref, acc_ref)

    _flash_finish(o_ref, l_ref, acc_ref)


def _moba(P4, TT, shift, B, S):
    tq = MOBA_BLOCK
    nblk = S // MOBA_BLOCK
    assert S % MOBA_BLOCK == 0 and nblk <= MOBA_SLOTS
    topb = min(MOBA_TOPK, nblk - 1)
    gl = N_HEADS * MOBA_SLOTS
    return pl.pallas_call(
        functools.partial(_moba_kernel, tq=tq, topb=topb),
        grid=(B, S // tq),
        in_specs=[pl.BlockSpec(memory_space=pltpu.SMEM),
                  pl.BlockSpec((None, TILE, tq), lambda b, i: (b, TT_QC, i)),
                  pl.BlockSpec((None, None, S, TILE), lambda b, i: (T_KC, b, 0, 0)),
                  pl.BlockSpec((None, TILE, S), lambda b, i: (b, TT_VC, 0))],
        out_specs=pl.BlockSpec((None, tq, TILE), lambda b, i: (b, i, 0)),
        out_shape=jax.ShapeDtypeStruct((B, S, TILE), BF16),
        scratch_shapes=[pltpu.VMEM((gl, TILE), BF16), pltpu.VMEM((gl, TILE), BF16),
                        pltpu.VMEM((N_HEADS, 2 * LANES, tq), BF16)]
                       + _flash_scratch(MOBA_BLOCK, tq),
        compiler_params=_params("arbitrary", "arbitrary"),
        name="moba",
    )(shift, TT, P4, TT)


def _post_kernel(x_ref, oa0, oa1, oa2, la0, la1, la2, ob_ref, oc_ref, z0, z1, z2, g0, g1, g2,
                 wbr_ref, wout_ref, expand_ref, out_ref):
    l0, l1, l2 = la0[...], la1[...], la2[...]
    mx = jnp.maximum(jnp.maximum(l0, l1), l2)
    e0, e1, e2 = jnp.exp2(l0 - mx), jnp.exp2(l1 - mx), jnp.exp2(l2 - mx)
    den = e0 + e1 + e2

    def spread(w):
        hi = w.astype(BF16)
        lo = (w - hi.astype(F32)).astype(BF16)
        return _dot(hi, expand_ref[...]) + _dot(lo, expand_ref[...])

    o_a = (spread(e0 / den) * oa0[...].astype(F32) + spread(e1 / den) * oa1[...].astype(F32)
           + spread(e2 / den) * oa2[...].astype(F32))
    branches = (o_a, ob_ref[...].astype(F32), oc_ref[...].astype(F32))
    merged = jnp.zeros(out_ref.shape, F32)
    for n, (o, z, g) in enumerate(zip(branches, (z0, z1, z2), (g0, g1, g2))):
        y = _dot((o * z[0].astype(F32)).astype(BF16), wbr_ref[n])
        gate = jnp.concatenate([g[t] for t in range(g.shape[0])], axis=1)
        merged = merged + gate.astype(F32) * y
    out_ref[...] = x_ref[...] + _dot(merged.astype(BF16), wout_ref[...])


def _post(x2, oa, la, ob, oc, P, wbr, wout, tm):
    T = x2.shape[0]
    row = lambda width, t: pl.BlockSpec((tm, width), lambda i: (i, t))
    tiles = lambda count, first: pl.BlockSpec((count, tm, TILE), lambda i: (first // count, i, 0))
    per_g = D_MODEL // TILE
    assert T_G % per_g == 0
    head_of = np.arange(TILE) // HEAD_DIM
    expand = jnp.asarray((np.arange(LANES)[:, None] == head_of[None, :]).astype(np.float32), BF16)
    in_specs = ([row(D_MODEL, 0)] + [row(TILE, 0)] * 3 + [row(LANES, 0)] * 3 + [row(TILE, 0)] * 2
                + [tiles(1, T_Z + n) for n in range(N_BRANCH)]
                + [tiles(per_g, T_G + per_g * n) for n in range(N_BRANCH)]
                + [pl.BlockSpec((N_BRANCH, BRANCH_WIDTH, D_MODEL), lambda i: (0, 0, 0)),
                   pl.BlockSpec((D_MODEL, D_MODEL), lambda i: (0, 0)),
                   pl.BlockSpec((LANES, TILE), lambda i: (0, 0))])
    return pl.pallas_call(
        _post_kernel,
        grid=(T // tm,),
        in_specs=in_specs,
        out_specs=row(D_MODEL, 0),
        out_shape=jax.ShapeDtypeStruct((T, D_MODEL), F32),
        compiler_params=_params("arbitrary"),
        name="post",
    )(x2, oa[0], oa[1], oa[2], la[0], la[1], la[2], ob, oc, P, P, P, P, P, P, wbr, wout, expand)


def _rearrange_w_in(w):
    bw = BRANCH_WIDTH
    a_q, a_k, a_v = w[:, 0:3 * bw], w[:, 3 * bw:6 * bw], w[:, 6 * bw:9 * bw]
    off = 9 * bw
    b_q, b_k, b_v = (w[:, off + i * bw:off + (i + 1) * bw] for i in range(3)); off += 3 * bw
    iq = w[:, off:off + IDX_HEADS * IDX_DIM]; off += IDX_HEADS * IDX_DIM
    ik = w[:, off:off + IDX_DIM]; off += IDX_DIM
    iw = w[:, off:off + IDX_HEADS]; off += IDX_HEADS
    c_q, c_k, c_v = (w[:, off + i * bw:off + (i + 1) * bw] for i in range(3)); off += 3 * bw
    z = w[:, off:off + 3 * bw]; off += 3 * bw
    g = w[:, off:off + 3 * D_MODEL]; off += 3 * D_MODEL
    assert off == w.shape[1]
    cols = []
    for grp in (1, 2, 0):
        sl = slice(grp * bw, (grp + 1) * bw)
        cols += [a_q[:, sl], a_k[:, sl], a_v[:, sl]]
    pad = jnp.zeros((w.shape[0], TILE - IDX_DIM - IDX_HEADS), w.dtype)
    cols += [b_k, ik, iw, pad, c_k, g, z]
    w_nat = jnp.concatenate(cols, axis=1).astype(BF16)
    assert w_nat.shape[1] == J_T * TILE
    w_nat = w_nat.reshape(-1, J_T, TILE).transpose(1, 0, 2)
    w_t = jnp.concatenate([b_q, b_v, iq, c_q, c_v], axis=1).T.astype(BF16)
    assert w_t.shape[0] == N_T_TILES * TILE
    return w_nat, w_t


def _tile_tables(qk_g):
    kinds = np.zeros((N_TILES,), np.int32)
    scale = np.array([HEAD_DIM ** -0.5 * LOG2_E, 1.0], np.float32)
    rows = jnp.tile(qk_g.astype(F32) * scale[None, :, None], (1, 1, N_HEADS)).reshape(2 * N_BRANCH, TILE)
    rows = jnp.concatenate([rows, jnp.full((1, TILE), IDX_DIM ** -0.5, F32), jnp.ones((1, TILE), F32)])
    row_iq, row_one = 2 * N_BRANCH, 2 * N_BRANCH + 1
    which = np.full((N_TILES,), row_one, np.int32)

    def qk(tile, mixer):
        kinds[tile] = kinds[tile + 1] = KIND_NORM_ROPE
        which[tile], which[tile + 1] = 2 * mixer, 2 * mixer + 1

    qk(0, 0)
    qk(N_FOLD_TILES // 2, 0)
    qk(J_NAT + T_A, 0)
    for tile, mixer in ((T_KB, 1), (T_KC, 2)):
        kinds[J_NAT + tile] = KIND_NORM_ROPE
        which[J_NAT + tile] = 2 * mixer + 1
    kinds[J_NAT + T_IKW] = KIND_ROPE_LOW
    kinds[J_NAT + T_Z:J_NAT + T_Z + 3] = KIND_SILU
    kinds[J_NAT + T_G:J_NAT + T_G + 6] = KIND_SIGMOID
    for tile, mixer in ((TT_QB, 1), (TT_QC, 2)):
        kinds[J_T + tile] = KIND_NORM_ROPE
        which[J_T + tile] = 2 * mixer
    kinds[J_T + TT_IQ] = KIND_ROPE
    which[J_T + TT_IQ] = row_iq
    gains = rows[which[:J_T]][:, None, :]
    gains_t = jnp.broadcast_to(rows[which[J_T:]][:, :, None], (N_T_TILES, TILE, LANES))
    half = N_FOLD_TILES // 2
    dest = np.array([DEST_F1] * half + [DEST_F2] * half + [DEST_NAT] * N_NAT_TILES + [DEST_T] * N_T_TILES)
    assert all((k, d) in _ROUTES for k, d in zip(kinds.tolist(), dest.tolist()))
    routes = _route_code(kinds, dest).astype(np.int32)
    return jnp.asarray(routes), gains, gains_t


def _softmax_shift(qk_gain):
    bound = HEAD_DIM * jnp.max(jnp.abs(qk_gain[0])) * jnp.max(jnp.abs(qk_gain[1]))
    return (SHIFT_SLACK * HEAD_DIM ** -0.5 * LOG2_E * bound).reshape(1).astype(BF16).astype(F32)


def _rope_tables(positions):
    inv = ROPE_THETA ** (-jnp.arange(0, ROT_DIM, 2, dtype=F32) / ROT_DIM)
    ang = positions.astype(F32).reshape(-1)[:, None] * inv
    cos, sin = jnp.cos(ang), jnp.sin(ang)
    T = cos.shape[0]
    z8 = jnp.zeros((T, ROT_HALF), F32)
    rest1 = jnp.ones((T, HEAD_DIM - ROT_DIM), F32)
    rest0 = jnp.zeros((T, HEAD_DIM - ROT_DIM), F32)
    c = jnp.concatenate([cos, cos, rest1], axis=1)
    s1 = jnp.concatenate([-sin, z8, rest0], axis=1)
    s2 = jnp.concatenate([z8, sin, rest0], axis=1)
    two = lambda t: jnp.concatenate([t, t], axis=1)
    return (two(c), two(s1), two(s2)), (c.T, s1.T, s2.T)


def _block_diag_mean():
    h = np.arange(MXU_DIM) // HEAD_DIM
    return jnp.asarray((h[:, None] == h[None, :]).astype(np.float32) / HEAD_DIM, BF16)


def _layer(x2, B, S, tabs, tabs_t, bd, norm_g, w_in, qk_g, w_br, w_out, tm_in, tm_post):
    routes, gains, gains_t = _tile_tables(qk_g)
    w_nat, w_t = _rearrange_w_in(w_in)
    P, f1, f2, TT = _inproj(x2, norm_g[None, :], w_nat, w_t, routes, gains, gains_t, tabs, tabs_t, bd,
                            tm_in, B, S)
    P4 = P.reshape(N_NAT_TILES, B, S, TILE)
    oa, la = [], []
    for g, qkv in enumerate((P4.reshape(N_NAT_TILES, B, 1, S, TILE), f1, f2)):
        o, lse = _attn_a(qkv, _softmax_shift(qk_g[0]), g, DIL_PATTERNS[g][1], B, S)
        oa.append(o.reshape(B * S, TILE))
        la.append(lse.reshape(B * S, LANES))
    ob = _dsa(P4, TT, _softmax_shift(qk_g[1]), B, S).reshape(B * S, TILE)
    oc = _moba(P4, TT, _softmax_shift(qk_g[2]), B, S).reshape(B * S, TILE)
    return _post(x2, oa, la, ob, oc, P, w_br.astype(BF16), w_out.astype(BF16), tm_post)


def _forward(x, positions, norm_g, w_in, qk_g, w_br, w_out, tm_in=IN_ROW_TILE, tm_post=POST_ROW_TILE):
    B, S, D = x.shape
    tabs, tabs_t = _rope_tables(positions)
    bd = _block_diag_mean()
    x2 = x.reshape(B * S, D)
    for layer in range(norm_g.shape[0]):
        x2 = _layer(x2, B, S, tabs, tabs_t, bd, norm_g[layer], w_in[layer], qk_g[layer],
                    w_br[layer], w_out[layer], tm_in, tm_post)
    return x2.reshape(B, S, D)


def kernel(x, positions, norm_g, w_in, qk_g, w_br, w_out):
    return _forward(x, positions, norm_g, w_in, qk_g, w_br, w_out)
```

```python
import functools

import jax
import jax.numpy as jnp
import numpy as np
from jax import lax
from jax.experimental import pallas as pl
from jax.experimental.pallas import tpu as pltpu

F32 = jnp.float32
BF16 = jnp.bfloat16
I32 = jnp.int32
I16 = jnp.int16

D_MODEL = 1024
HEAD_DIM = 64
ROT_DIM = HEAD_DIM // 4
ROT_HALF = ROT_DIM // 2
ROPE_THETA = 500000.0
NORM_EPS = 1e-6
NEG_INF = -1e30
LOG2_E = 1.4426950408889634
MAX_SOFTMAX_SHIFT = 60.0
SHIFT_SLACK = 1.05
N_HEADS = 8
BRANCH_WIDTH = N_HEADS * HEAD_DIM
N_BRANCH = 3
DIL_PATTERNS = ((128, 1), (512, 4), (2048, 16))
BAND = 128
IDX_HEADS = 8
IDX_DIM = 64
IDX_TOPK_MAX = 256
MOBA_BLOCK = 256
MOBA_TOPK = 3
MOBA_SLOTS = 16

LANES = 128
SUBLANES = 8
MXU_DIM = 256
VMEM_LIMIT_BYTES = 56 * 1024 * 1024

TILE = BRANCH_WIDTH
N_FOLD_TILES = 6
T_A = 0
T_KB = 3
T_IKW = 4
T_KC = 5
T_G = 6
T_Z = 12
N_NAT_TILES = 15
TT_QB, TT_VB, TT_IQ, TT_QC, TT_VC = range(5)
N_T_TILES = 5
J_NAT = N_FOLD_TILES
J_T = N_FOLD_TILES + N_NAT_TILES
N_TILES = J_T + N_T_TILES
IN_ROW_TILE = 2048
IN_CHUNK = 256
A_STEP_TOKENS = 2048
GROUP = 4
SPARSE_TILE = 256
POST_ROW_TILE = 512

KIND_PLAIN, KIND_NORM_ROPE, KIND_ROPE, KIND_ROPE_LOW, KIND_SILU, KIND_SIGMOID = range(6)
DEST_NAT, DEST_F1, DEST_F2, DEST_T = range(4)
_ROUTES = ((KIND_PLAIN, DEST_NAT), (KIND_PLAIN, DEST_F1), (KIND_PLAIN, DEST_F2),
           (KIND_NORM_ROPE, DEST_NAT), (KIND_NORM_ROPE, DEST_F1), (KIND_NORM_ROPE, DEST_F2),
           (KIND_ROPE_LOW, DEST_NAT), (KIND_SILU, DEST_NAT), (KIND_SIGMOID, DEST_NAT),
           (KIND_PLAIN, DEST_T), (KIND_NORM_ROPE, DEST_T), (KIND_ROPE, DEST_T))


def _route_code(kind, dest):
    return kind * 4 + dest

_FLOOR_KEY = int(np.array(-5e29, np.float32).view(np.int32)) ^ 0x7FFFFFFF
_INT_MIN = -2 ** 31
_HALF16 = 32768
_PACK16 = 16
_LOG2_MOBA_BLOCK = MOBA_BLOCK.bit_length() - 1
_LOG2_N_HEADS = N_HEADS.bit_length() - 1
_LOG2_HEAD_DIM = HEAD_DIM.bit_length() - 1


def _dot(a, b):
    return jnp.dot(a, b, preferred_element_type=F32)


def _dot_nt(a, b):
    return lax.dot_general(a, b, (((1,), (1,)), ((), ())), preferred_element_type=F32)


def _params(*sem):
    return pltpu.CompilerParams(dimension_semantics=sem, vmem_limit_bytes=VMEM_LIMIT_BYTES)


def _inproj_kernel(route_ref, x_ref, ng_ref, w_ref, wt_ref, gain_ref, gain_t_ref,
                   cos_ref, s1_ref, s2_ref, cos_t_ref, s1_t_ref, s2_t_ref, bd_ref,
                   o_ref, f1_ref, f2_ref, tt_ref, h_ref, ht_ref, stage_ref):
    j = pl.program_id(1)
    tm = x_ref.shape[0]
    n_ch = TILE // LANES

    @pl.when(j == 0)
    def _():
        x = x_ref[...]
        ms = jnp.mean(x * x, axis=-1, keepdims=True)
        h = x * lax.rsqrt(ms + NORM_EPS) * ng_ref[...]
        h_ref[...] = h.astype(BF16)
        ht_ref[...] = jnp.transpose(h).astype(BF16)

    def fold(val, ref, dil, tok):
        rows = (tok.stop - tok.start) // dil
        for c in range(n_ch):
            stage_ref[c, tok, :] = val[:, c * LANES:(c + 1) * LANES]
        for r in range(dil):
            parts = [stage_ref[c, pl.ds(tok.start + r, rows, stride=dil), :] for c in range(n_ch)]
            ref[r, tok.start // dil:tok.stop // dil, :] = jnp.concatenate(parts, axis=1).astype(ref.dtype)

    def rope(v, low_only, tok):
        c = jnp.tile(cos_ref[tok, :], (1, n_ch))
        s1 = jnp.tile(s1_ref[tok, :], (1, n_ch))
        s2 = jnp.tile(s2_ref[tok, :], (1, n_ch))
        if low_only:
            low = lax.broadcasted_iota(I32, v.shape, 1) < HEAD_DIM
            c = jnp.where(low, c, 1.0)
            s1 = jnp.where(low, s1, 0.0)
            s2 = jnp.where(low, s2, 0.0)
        return v * c + pltpu.roll(v, TILE - ROT_HALF, 1) * s1 + pltpu.roll(v, ROT_HALF, 1) * s2

    def rope_t(v, tok):
        c = jnp.tile(cos_t_ref[:, tok], (N_HEADS, 1))
        s1 = jnp.tile(s1_t_ref[:, tok], (N_HEADS, 1))
        s2 = jnp.tile(s2_t_ref[:, tok], (N_HEADS, 1))
        return v * c + pltpu.roll(v, TILE - ROT_HALF, 0) * s1 + pltpu.roll(v, ROT_HALF, 0) * s2

    def epilogue(kind, y, tok):
        if kind == KIND_PLAIN:
            return y
        if kind == KIND_NORM_ROPE:
            y2 = (y * y).astype(BF16)
            ms = jnp.concatenate([_dot(y2[:, c * MXU_DIM:(c + 1) * MXU_DIM], bd_ref[...])
                                  for c in range(TILE // MXU_DIM)], axis=1)
            return rope(y * lax.rsqrt(ms + NORM_EPS) * gain_ref[0], False, tok)
        if kind == KIND_ROPE_LOW:
            return rope(y, True, tok)
        if kind == KIND_SILU:
            return y / (1.0 + jnp.exp(-y))
        assert kind == KIND_SIGMOID
        return 1.0 / (1.0 + jnp.exp(-y))

    def epilogue_t(kind, yt, tok):
        if kind == KIND_PLAIN:
            return yt
        gain = jnp.tile(gain_t_ref[0], (1, yt.shape[1] // LANES))
        if kind == KIND_NORM_ROPE:
            y2 = (yt * yt).astype(BF16)
            ms = jnp.concatenate([_dot(bd_ref[...], y2[c * MXU_DIM:(c + 1) * MXU_DIM, :])
                                  for c in range(TILE // MXU_DIM)], axis=0)
            return rope_t(yt * lax.rsqrt(ms + NORM_EPS) * gain, tok)
        assert kind == KIND_ROPE
        return rope_t(yt * gain, tok)

    route = route_ref[j]
    for kind, dest in _ROUTES:
        @pl.when(route == _route_code(kind, dest))
        def _(kind=kind, dest=dest):
            if dest == DEST_T:
                tok = slice(0, tm)
                tt_ref[...] = epilogue_t(kind, _dot(wt_ref[...], ht_ref[...]), tok).astype(tt_ref.dtype)
                return
            chunk = max(IN_CHUNK, _PACK16 * DIL_PATTERNS[2][1]) if dest == DEST_F2 else IN_CHUNK
            for m in range(tm // chunk):
                tok = slice(m * chunk, (m + 1) * chunk)
                val = epilogue(kind, _dot(h_ref[tok, :], w_ref[...]), tok)
                if dest == DEST_NAT:
                    o_ref[tok, :] = val.astype(o_ref.dtype)
                elif dest == DEST_F1:
                    fold(val, f1_ref, DIL_PATTERNS[1][1], tok)
                else:
                    fold(val, f2_ref, DIL_PATTERNS[2][1], tok)


def _inproj(x2, ng, w_nat, w_t, routes, gains, gains_t, tabs, tabs_t, bd, tm, B, S):
    T = x2.shape[0]
    grid = (T // tm, N_TILES)
    per_b = S // tm
    d1, d2 = DIL_PATTERNS[1][1], DIL_PATTERNS[2][1]
    half = N_FOLD_TILES // 2
    assert S % tm == 0 and tm % (_PACK16 * d2) == 0 and tm % IN_CHUNK == 0 and IN_CHUNK % (_PACK16 * d1) == 0
    n_w = J_T

    def fold_spec(dil, first):
        return pl.BlockSpec(
            (None, None, dil, tm // dil, TILE),
            lambda i, j, k: (jnp.clip(j - first, 0, half - 1), i // per_b, 0, i % per_b, 0))

    once_per_row_tile = pl.Buffered(1)
    row_tab = pl.BlockSpec((tm, LANES), lambda i, j, k: (i, 0), pipeline_mode=once_per_row_tile)
    col_tab = pl.BlockSpec((HEAD_DIM, tm), lambda i, j, k: (0, i), pipeline_mode=once_per_row_tile)
    t_idx = lambda j: jnp.maximum(j - J_T, 0)
    return pl.pallas_call(
        _inproj_kernel,
        grid_spec=pltpu.PrefetchScalarGridSpec(
            num_scalar_prefetch=1,
            grid=grid,
            in_specs=[
                pl.BlockSpec((tm, D_MODEL), lambda i, j, k: (i, 0), pipeline_mode=once_per_row_tile),
                pl.BlockSpec((1, D_MODEL), lambda i, j, k: (0, 0)),
                pl.BlockSpec((None, D_MODEL, TILE), lambda i, j, k: (jnp.minimum(j, n_w - 1), 0, 0)),
                pl.BlockSpec((TILE, D_MODEL), lambda i, j, k: (t_idx(j), 0)),
                pl.BlockSpec((1, 1, TILE), lambda i, j, k: (jnp.minimum(j, n_w - 1), 0, 0)),
                pl.BlockSpec((1, TILE, LANES), lambda i, j, k: (t_idx(j), 0, 0)),
                row_tab, row_tab, row_tab, col_tab, col_tab, col_tab,
                pl.BlockSpec((MXU_DIM, MXU_DIM), lambda i, j, k: (0, 0)),
            ],
            out_specs=[
                pl.BlockSpec((None, tm, TILE), lambda i, j, k: (jnp.clip(j - J_NAT, 0, N_NAT_TILES - 1), i, 0)),
                fold_spec(d1, 0),
                fold_spec(d2, half),
                pl.BlockSpec((None, TILE, tm), lambda i, j, k: (i // per_b, t_idx(j), i % per_b)),
            ],
            scratch_shapes=[pltpu.VMEM((tm, D_MODEL), BF16), pltpu.VMEM((D_MODEL, tm), BF16),
                            pltpu.VMEM((TILE // LANES, tm, LANES), F32)],
        ),
        out_shape=[jax.ShapeDtypeStruct((N_NAT_TILES, T, TILE), BF16),
                   jax.ShapeDtypeStruct((half, B, d1, S // d1, TILE), BF16),
                   jax.ShapeDtypeStruct((half, B, d2, S // d2, TILE), BF16),
                   jax.ShapeDtypeStruct((B, N_T_TILES * TILE, S), BF16)],
        compiler_params=_params("arbitrary", "arbitrary"),
        name="inproj",
    )(routes, x2, ng, w_nat, w_t, gains, gains_t, *tabs, *tabs_t, bd)


def _attn_a_kernel(shift_ref, q_ref, kc_ref, kp_ref, vc_ref, vp_ref, o_ref, lse_ref, kcat, vtcat, ost, lst,
                   s_ref, p_ref, *, tq, dil):
    m = pl.program_id(1)
    nsub = tq // BAND
    n_ch = TILE // LANES
    nk = 2 * BAND
    key = lax.broadcasted_iota(I32, (nk, BAND), 0)
    qry = lax.broadcasted_iota(I32, (nk, BAND), 1) + BAND
    in_band = jnp.logical_and(key <= qry, key >= qry - BAND)
    lane = lax.broadcasted_iota(I32, (BAND, LANES), 1)
    half_mask = (jnp.where(lane < HEAD_DIM, 1.0, 0.0).astype(BF16),
                 jnp.where(lane < HEAD_DIM, 0.0, 1.0).astype(BF16))
    ones = jnp.ones((_PACK16, nk), BF16)
    shifted = shift_ref[0] <= MAX_SOFTMAX_SHIFT
    shift = jnp.where(shifted, shift_ref[0], 0.0)

    def load_residue(r, slot):
        kcat[slot, 0:BAND, :] = kp_ref[r]
        kcat[slot, BAND:, :] = kc_ref[r]
        vtcat[slot, :, 0:BAND] = jnp.transpose(vp_ref[r].astype(F32)).astype(BF16)
        for c in range(nsub):
            chunk = vc_ref[r, c * BAND:(c + 1) * BAND, :].astype(F32)
            vtcat[slot, :, (c + 1) * BAND:(c + 2) * BAND] = jnp.transpose(chunk).astype(BF16)

    def block(r, u, use_shift, slot, par):
        r0 = pl.multiple_of(u * BAND, BAND)
        q = q_ref[r, pl.ds(r0, BAND), :]
        k2 = kcat[slot, pl.ds(r0, nk), :]
        valid = jnp.logical_and(in_band, key >= jnp.where(m * nsub + u > 0, 0, BAND))
        bias = jnp.where(valid, -shift, NEG_INF)

        def score(h):
            pair = slice((h // 2) * LANES, (h // 2 + 1) * LANES)
            q_h = q[:, pair] * half_mask[h % 2]
            return _dot_nt(k2[:, pair], q_h) + bias

        def pv_of(h, p):
            vt_h = vtcat[slot, h * HEAD_DIM:(h + 1) * HEAD_DIM, pl.ds(r0, nk)]
            pv = _dot(jnp.concatenate([vt_h, ones], axis=0), p)
            return pv[:HEAD_DIM], pv[HEAD_DIM:HEAD_DIM + 1]

        outs, lses = [], []
        if use_shift:
            for h in range(N_HEADS):
                p_ref[par, h] = jnp.exp2(score(h)).astype(BF16)
            for h in range(N_HEADS):
                acc, den = pv_of(h, p_ref[par, h])
                outs.append(acc / den)
                lses.append(shift + jnp.log2(den))
        else:
            maxes = []
            for h in range(N_HEADS):
                s = score(h)
                s_ref[par, h] = s
                maxes.append(jnp.max(s, axis=0, keepdims=True))
            for h in range(N_HEADS):
                acc, den = pv_of(h, jnp.exp2(s_ref[par, h] - maxes[h]).astype(BF16))
                outs.append(acc / den)
                lses.append(maxes[h] + jnp.log2(den))
        o = jnp.transpose(jnp.concatenate(outs, axis=0))
        lse8 = jnp.concatenate(lses, axis=0)
        lse = jnp.transpose(jnp.tile(lse8, (LANES // N_HEADS, 1)))
        rows = pl.ds(r + u * (BAND * dil), BAND, stride=dil) if dil > 1 else pl.ds(r0, BAND)
        for c in range(n_ch):
            ost[c, rows, :] = o[:, c * LANES:(c + 1) * LANES]
        lst[rows, :] = lse

    def both_paths(blocks):
        @pl.when(shifted)
        def _():
            for r, u, slot, par in blocks:
                block(r, u, True, slot, par)

        @pl.when(jnp.logical_not(shifted))
        def _():
            for r, u, slot, par in blocks:
                block(r, u, False, slot, par)

    if nsub % GROUP == 0:
        def residue(r, carry):
            load_residue(r, 0)

            def body(i, carry):
                both_paths([(r, GROUP * i + t, 0, t) for t in range(GROUP)])
                return carry

            return lax.fori_loop(0, nsub // GROUP, body, carry)

        lax.fori_loop(0, dil, residue, 0)
    else:
        assert nsub == 1 and dil % GROUP == 0

        def residue_group(i, carry):
            for t in range(GROUP):
                load_residue(GROUP * i + t, t)
            both_paths([(GROUP * i + t, 0, t, t) for t in range(GROUP)])
            return carry

        lax.fori_loop(0, dil // GROUP, residue_group, 0)
    o_ref[0] = jnp.concatenate([ost[c] for c in range(n_ch)], axis=1).astype(o_ref.dtype)
    lse_ref[0] = lst[...]


def _attn_a(qkv, shift, g, dil, B, S):
    ts = min(A_STEP_TOKENS, S)
    tq = ts // dil
    assert S % ts == 0 and tq % BAND == 0
    sub = tq // BAND

    def spec_cur(t):
        return pl.BlockSpec((None, None, dil, tq, TILE), lambda b, m: (t, b, 0, m, 0))

    def spec_prev(t):
        return pl.BlockSpec((None, None, dil, BAND, TILE),
                            lambda b, m: (t, b, 0, jnp.maximum(m * sub - 1, 0), 0))

    n_ch = TILE // LANES
    kv_slots = 1 if sub % GROUP == 0 else GROUP
    return pl.pallas_call(
        functools.partial(_attn_a_kernel, tq=tq, dil=dil),
        grid=(B, S // ts),
        in_specs=[pl.BlockSpec(memory_space=pltpu.SMEM),
                  spec_cur(0), spec_cur(1), spec_prev(1), spec_cur(2), spec_prev(2)],
        out_specs=[pl.BlockSpec((1, ts, TILE), lambda b, m: (b, m, 0)),
                   pl.BlockSpec((1, ts, LANES), lambda b, m: (b, m, 0))],
        out_shape=[jax.ShapeDtypeStruct((B, S, TILE), BF16), jax.ShapeDtypeStruct((B, S, LANES), F32)],
        scratch_shapes=[pltpu.VMEM((kv_slots, tq + BAND, TILE), BF16), pltpu.VMEM((kv_slots, TILE, tq + BAND), BF16),
                        pltpu.VMEM((n_ch, ts, LANES), F32), pltpu.VMEM((ts, LANES), F32),
                        pltpu.VMEM((GROUP, N_HEADS, 2 * BAND, BAND), F32),
                        pltpu.VMEM((GROUP, N_HEADS, 2 * BAND, BAND), BF16)],
        compiler_params=_params("arbitrary", "arbitrary"),
        name=f"attn_a{g}",
    )(shift, qkv, qkv, qkv, qkv, qkv)


def _pad_q(qt_ref, qpad_ref):
    qpad_ref[...] = jnp.zeros(qpad_ref.shape, qpad_ref.dtype)
    for h in range(N_HEADS):
        r0 = h * LANES + (h % 2) * HEAD_DIM
        qpad_ref[r0:r0 + HEAD_DIM, :] = qt_ref[h * HEAD_DIM:(h + 1) * HEAD_DIM, :]


def _flash_init(m_ref, l_ref, acc_ref):
    m_ref[...] = jnp.full(m_ref.shape, NEG_INF, F32)
    l_ref[...] = jnp.zeros(l_ref.shape, F32)
    acc_ref[...] = jnp.zeros(acc_ref.shape, F32)


def _flash_tile_step(score_of_head, vt_of_head, s_ref, mn_ref, m_ref, l_ref, acc_ref):
    tk = s_ref.shape[1]
    for h in range(N_HEADS):
        s = score_of_head(h)
        s_ref[h] = s
        mn_ref[h] = jnp.maximum(m_ref[h], jnp.max(s, axis=0, keepdims=True))
    ones = jnp.ones((_PACK16, tk), BF16)
    for h in range(N_HEADS):
        m_prev, m_next = m_ref[h], mn_ref[h]
        p = jnp.exp2(s_ref[h] - jnp.tile(m_next, (tk // SUBLANES, 1)))
        alpha = jnp.exp2(m_prev - m_next)
        pv = _dot(jnp.concatenate([vt_of_head(h), ones], axis=0), p.astype(BF16))
        l_ref[h] = alpha * l_ref[h] + pv[HEAD_DIM:HEAD_DIM + SUBLANES]
        acc_ref[h] = acc_ref[h] * jnp.tile(alpha, (HEAD_DIM // SUBLANES, 1)) + pv[:HEAD_DIM]
        m_ref[h] = m_next


def _flash_tiles_shifted(tiles, p_ref, l_ref, acc_ref):
    for t, (score_of_head, _) in enumerate(tiles):
        for h in range(N_HEADS):
            p_ref[t, h] = jnp.exp2(score_of_head(h)).astype(BF16)
    ones = jnp.ones((_PACK16, p_ref.shape[2]), BF16)
    for t, (_, vt_of_head) in enumerate(tiles):
        for h in range(N_HEADS):
            pv = _dot(jnp.concatenate([vt_of_head(h), ones], axis=0), p_ref[t, h])
            l_ref[h] = l_ref[h] + pv[HEAD_DIM:HEAD_DIM + SUBLANES]
            acc_ref[h] = acc_ref[h] + pv[:HEAD_DIM]


def _grouped_loop(n, group, smallest=1):
    def body(i, carry):
        group(i * GROUP, GROUP)
        return carry

    lax.fori_loop(0, lax.shift_right_logical(n, GROUP.bit_length() - 1), body, 0)
    size = GROUP // 2
    while size >= smallest:
        @pl.when((n & size) != 0)
        def _(size=size):
            group((n >> size.bit_length()) << size.bit_length(), size)
        size //= 2


def _flash_finish(o_ref, l_ref, acc_ref):
    outs = [acc_ref[h] / jnp.tile(l_ref[h], (HEAD_DIM // SUBLANES, 1)) for h in range(N_HEADS)]
    o_ref[...] = jnp.transpose(jnp.concatenate(outs, axis=0)).astype(o_ref.dtype)


def _dsa_kernel(shift_ref, qt_ref, k_ref, vt_ref, iqt_ref, ikw_q_ref, ikw_ref, o_ref,
                key_ref, hi_ref, lo_ref, qpad_ref, bias_ref, p_ref, s_ref, mn_ref, m_ref, l_ref, acc_ref,
                *, tq, tk, topk):
    qi = pl.program_id(1)
    n_kt = qi + 1
    n_acc = 4
    rows8 = tk // SUBLANES

    w8 = jnp.transpose(ikw_q_ref[...].astype(F32))[HEAD_DIM:HEAD_DIM + IDX_HEADS, :] * (IDX_HEADS ** -0.5)
    krow = lax.broadcasted_iota(I32, (tk, tq), 0)
    qcol = lax.broadcasted_iota(I32, (tk, tq), 1)

    def score_tile(c, diagonal):
        c0 = pl.multiple_of(c * tk, tk)
        kx = ikw_ref[pl.ds(c0, tk), :][:, :IDX_DIM]
        sc = jnp.zeros((tk, tq), F32)
        for h in range(IDX_HEADS):
            lg = _dot(kx, iqt_ref[h * IDX_DIM:(h + 1) * IDX_DIM, :])
            sc = sc + w8[h:h + 1, :] * jnp.maximum(lg, 0.0)
        sc = jnp.where(sc == 0.0, 0.0, sc)
        if diagonal:
            sc = jnp.where(krow <= qcol, sc, NEG_INF)
        bits = pltpu.bitcast(sc, I32)
        key = jnp.where(bits < 0, bits ^ 0x7FFFFFFF, bits)
        key_ref[pl.ds(c0, tk), :] = key
        hi_ref[pl.ds(c0, tk), :] = jnp.right_shift(key, 16).astype(I16)
        lo_ref[pl.ds(c0, tk), :] = ((key & 0xFFFF) - _HALF16).astype(I16)

    _grouped_loop(qi, lambda c0, count: [score_tile(c0 + t, False) for t in range(count)], smallest=2)

    @pl.when((qi & 1) == 1)
    def _():
        score_tile(qi - 1, False)
        score_tile(qi, True)

    @pl.when((qi & 1) == 0)
    def _():
        score_tile(qi, True)

    rows16 = tk // _PACK16
    one16 = jnp.ones((_PACK16, tq), I16)
    zero16 = jnp.zeros((_PACK16, tq), I16)

    def count16(ref, pred):
        def cbody(c, accs):
            accs = list(accs)
            c0 = pl.multiple_of(c * tk, tk)
            t = ref[pl.ds(c0, tk), :]
            for g in range(rows16):
                hit = jnp.where(pred(t[g * _PACK16:(g + 1) * _PACK16]), one16, zero16)
                accs[g % n_acc] = accs[g % n_acc] + hit
            return tuple(accs)
        accs = lax.fori_loop(0, n_kt, cbody, tuple(zero16 for _ in range(n_acc)))
        return jnp.sum(functools.reduce(lambda a, b: a + b, accs).astype(I32), axis=0, keepdims=True)

    def as16(v):
        return jnp.broadcast_to(v, (_PACK16, tq)).astype(I16)

    def select16(ref, need, cge0):
        def bit_body(it, carry):
            ans, cge = carry
            cand_u = ans | lax.shift_left(jnp.int32(1), 15 - it)
            cand = as16(cand_u - _HALF16)
            cnt = count16(ref, lambda t: t >= cand)
            ok = cnt >= need
            return jnp.where(ok, cand_u, ans), jnp.where(ok, cnt, cge)
        return lax.fori_loop(0, 16, bit_body, (jnp.zeros((1, tq), I32), cge0))

    n_all = jnp.zeros((1, tq), I32) + n_kt * tk
    p_u, cge_hi = select16(hi_ref, topk, n_all)
    p16 = as16(p_u - _HALF16)
    c_gt = count16(hi_ref, lambda t: t > p16)

    def bucket_body(c, carry):
        c0 = pl.multiple_of(c * tk, tk)
        lo_ref[pl.ds(c0, tk), :] = jnp.where(hi_ref[pl.ds(c0, tk), :] == jnp.tile(p16, (rows16, 1)),
                                             lo_ref[pl.ds(c0, tk), :], jnp.int16(-_HALF16))
        return carry

    lax.fori_loop(0, n_kt, bucket_body, 0)
    l_u, cge_lo = select16(lo_ref, topk - c_gt, cge_hi - c_gt)
    cge = c_gt + cge_lo
    thr = jnp.maximum((p_u - _HALF16) * 65536 + l_u, _FLOOR_KEY)

    def count_rows(fn, n_out):
        def cbody(c, accs):
            accs = [list(a) for a in accs]
            c0 = pl.multiple_of(c * tk, tk)
            kt = key_ref[pl.ds(c0, tk), :]
            for g in range(rows8):
                r0 = c0 + g * SUBLANES
                vals = fn(kt[g * SUBLANES:(g + 1) * SUBLANES], r0)
                for o in range(n_out):
                    accs[o][g % n_acc] = accs[o][g % n_acc] + vals[o]
            return tuple(tuple(a) for a in accs)
        z = jnp.zeros((SUBLANES, tq), I32)
        accs = lax.fori_loop(0, n_kt, cbody, tuple(tuple(z for _ in range(n_acc)) for _ in range(n_out)))
        return [jnp.sum(functools.reduce(lambda a, b: a + b, a), axis=0, keepdims=True) for a in accs]

    tie = jnp.logical_and(cge > topk, thr > _FLOOR_KEY)
    any_tie = jnp.max(jnp.where(tie, 1, 0)) > 0
    thr8 = jnp.broadcast_to(thr, (SUBLANES, tq))

    @pl.when(any_tie)
    def _():
        n_bits = int(np.log2(key_ref.shape[0]))
        sub = lax.broadcasted_iota(I32, (SUBLANES, tq), 0)

        def jbody(it, lo):
            cand = lo + lax.shift_left(jnp.int32(1), n_bits - 1 - it)
            pos = jnp.broadcast_to(cand - 1, (SUBLANES, tq))

            def f(kk, r0):
                eq = jnp.logical_and(kk == thr8, sub + r0 <= pos)
                return jnp.where(kk > thr8, 1, 0), jnp.where(eq, 1, 0)

            gt, eq = count_rows(f, 2)
            return jnp.where(gt + eq >= topk, lo, cand)

        jrow = jnp.broadcast_to(lax.fori_loop(0, n_bits, jbody, jnp.zeros((1, tq), I32)), (SUBLANES, tq))
        tie8 = jnp.broadcast_to(jnp.where(tie, 1, 0), (SUBLANES, tq)) > 0

        def fix(c, carry):
            for g in range(rows8):
                r0 = pl.multiple_of(c * tk + g * SUBLANES, SUBLANES)
                kk = key_ref[pl.ds(r0, SUBLANES), :]
                drop = jnp.logical_and(tie8, jnp.logical_and(kk == thr8, sub + r0 > jrow))
                key_ref[pl.ds(r0, SUBLANES), :] = jnp.where(drop, _INT_MIN, kk)
            return carry

        lax.fori_loop(0, n_kt, fix, 0)

    _pad_q(qt_ref, qpad_ref)
    _flash_init(m_ref, l_ref, acc_ref)
    shift = shift_ref[0]

    def tile(j, t, shifted):
        c0 = pl.multiple_of(j * tk, tk)
        sel = key_ref[pl.ds(c0, tk), :] >= thr
        bias_ref[t] = jnp.where(sel, -shift if shifted else 0.0, NEG_INF)
        k = k_ref[pl.ds(c0, tk), :]

        def score(h):
            k_pair = k[:, (h // 2) * LANES:(h // 2 + 1) * LANES]
            return _dot(k_pair, qpad_ref[h * LANES:(h + 1) * LANES, :]) + bias_ref[t]

        return score, lambda h: vt_ref[h * HEAD_DIM:(h + 1) * HEAD_DIM, pl.ds(c0, tk)]

    @pl.when(shift <= MAX_SOFTMAX_SHIFT)
    def _():
        def group(j0, count):
            _flash_tiles_shifted([tile(j0 + t, t, True) for t in range(count)], p_ref, l_ref, acc_ref)

        _grouped_loop(n_kt, group)

    @pl.when(shift > MAX_SOFTMAX_SHIFT)
    def _():
        def att_body(j, carry):
            _flash_tile_step(*tile(j, 0, False), s_ref, mn_ref, m_ref, l_ref, acc_ref)
            return carry

        lax.fori_loop(0, n_kt, att_body, 0)

    _flash_finish(o_ref, l_ref, acc_ref)


def _flash_scratch(tk, tq):
    stat = pltpu.VMEM((N_HEADS, SUBLANES, tq), F32)
    return [pltpu.VMEM((GROUP, N_HEADS, tk, tq), BF16), pltpu.VMEM((N_HEADS, tk, tq), F32), stat, stat, stat,
            pltpu.VMEM((N_HEADS, HEAD_DIM, tq), F32)]


def _dsa(P4, TT, shift, B, S):
    tq = tk = SPARSE_TILE
    topk = min(IDX_TOPK_MAX, S // 4)
    assert S % tq == 0 and topk <= tk
    def tt_q(t):
        return pl.BlockSpec((None, TILE, tq), lambda b, i: (b, t, i))

    return pl.pallas_call(
        functools.partial(_dsa_kernel, tq=tq, tk=tk, topk=topk),
        grid=(B, S // tq),
        in_specs=[pl.BlockSpec(memory_space=pltpu.SMEM),
                  tt_q(TT_QB),
                  pl.BlockSpec((None, None, S, TILE), lambda b, i: (T_KB, b, 0, 0)),
                  pl.BlockSpec((None, TILE, S), lambda b, i: (b, TT_VB, 0)),
                  tt_q(TT_IQ),
                  pl.BlockSpec((None, None, tq, LANES), lambda b, i: (T_IKW, b, i, 0)),
                  pl.BlockSpec((None, None, S, LANES), lambda b, i: (T_IKW, b, 0, 0))],
        out_specs=pl.BlockSpec((None, tq, TILE), lambda b, i: (b, i, 0)),
        out_shape=jax.ShapeDtypeStruct((B, S, TILE), BF16),
        scratch_shapes=[pltpu.VMEM((S, tq), I32), pltpu.VMEM((S, tq), I16), pltpu.VMEM((S, tq), I16),
                        pltpu.VMEM((N_HEADS * LANES, tq), BF16),
                        pltpu.VMEM((GROUP, tk, tq), F32)] + _flash_scratch(tk, tq),
        compiler_params=_params("arbitrary", "arbitrary"),
        name="dsa",
    )(shift, TT, P4, TT, TT, P4, P4)


def _moba_kernel(shift_ref, qt_ref, k_ref, vt_ref, o_ref, kmh_ref, kml_ref, qaug_ref,
                 p_ref, s_ref, mn_ref, m_ref, l_ref, acc_ref, *, tq, topb):
    qi = pl.program_id(1)
    S = k_ref.shape[0]
    gl = N_HEADS * MOBA_SLOTS

    @pl.when(qi == 0)
    def _():
        blk_row = jnp.right_shift(lax.broadcasted_iota(I32, (gl, S), 0), _LOG2_N_HEADS)
        blk_col = jnp.right_shift(lax.broadcasted_iota(I32, (gl, S), 1), _LOG2_MOBA_BLOCK)
        avg = jnp.where(blk_row == blk_col, 1.0 / MOBA_BLOCK, 0.0).astype(BF16)
        km = _dot(avg, k_ref[...])
        r_head = lax.broadcasted_iota(I32, (gl, TILE), 0) & (N_HEADS - 1)
        c_head = jnp.right_shift(lax.broadcasted_iota(I32, (gl, TILE), 1), _LOG2_HEAD_DIM)
        km = jnp.where(r_head == c_head, km, 0.0)
        hi = km.astype(BF16)
        kmh_ref[...] = hi
        kml_ref[...] = (km - hi.astype(F32)).astype(BF16)

    shifted = shift_ref[0] <= MAX_SOFTMAX_SHIFT
    shift = jnp.where(shifted, shift_ref[0], 0.0)
    qt = qt_ref[...]
    gate = _dot(kmh_ref[...], qt) + _dot(kml_ref[...], qt)
    row = lax.broadcasted_iota(I32, (gl, tq), 0)
    gate = jnp.where(jnp.right_shift(row, _LOG2_N_HEADS) < qi, gate, NEG_INF)
    g = [gate[n * N_HEADS:(n + 1) * N_HEADS] for n in range(MOBA_SLOTS)]
    biases = []
    for n in range(MOBA_SLOTS):
        beaten = jnp.where(n < qi, 0, topb) + jnp.zeros((N_HEADS, tq), I32)
        for n2 in range(MOBA_SLOTS):
            if n2 != n:
                beaten = beaten + jnp.where((g[n2] >= g[n]) if n2 < n else (g[n2] > g[n]), 1, 0)
        biases.append(jnp.where(beaten < topb, -shift, NEG_INF))
    selb = jnp.concatenate(biases, axis=0)

    qaug_ref[...] = jnp.zeros(qaug_ref.shape, qaug_ref.dtype)
    for h in range(N_HEADS):
        r0 = (h % 2) * HEAD_DIM
        qaug_ref[h, r0:r0 + HEAD_DIM, :] = qt_ref[h * HEAD_DIM:(h + 1) * HEAD_DIM, :]
        qaug_ref[h, LANES:, :] = jnp.where((row & (N_HEADS - 1)) == h, selb, 0.0).astype(BF16)
    _flash_init(m_ref, l_ref, acc_ref)
    lane_blk = jnp.right_shift(lax.broadcasted_iota(I32, (MOBA_BLOCK, LANES), 1), _LOG2_N_HEADS)

    def vt_of_block(n):
        c0 = pl.multiple_of(n * MOBA_BLOCK, MOBA_BLOCK)
        return lambda h: vt_ref[h * HEAD_DIM:(h + 1) * HEAD_DIM, pl.ds(c0, MOBA_BLOCK)]

    def past_score(n):
        k = k_ref[pl.ds(pl.multiple_of(n * MOBA_BLOCK, MOBA_BLOCK), MOBA_BLOCK), :]
        onehot = jnp.where(lane_blk == n, 1.0, 0.0).astype(BF16)

        def score(h):
            k_aug = jnp.concatenate([k[:, (h // 2) * LANES:(h // 2 + 1) * LANES], onehot], axis=1)
            return _dot(k_aug, qaug_ref[h])

        return score

    def own_score():
        k = k_ref[pl.ds(pl.multiple_of(qi * MOBA_BLOCK, MOBA_BLOCK), MOBA_BLOCK), :]
        causal = jnp.where(lax.broadcasted_iota(I32, (MOBA_BLOCK, tq), 0)
                           <= lax.broadcasted_iota(I32, (MOBA_BLOCK, tq), 1), -shift, NEG_INF)
        return lambda h: _dot(k[:, (h // 2) * LANES:(h // 2 + 1) * LANES], qaug_ref[h, :LANES, :]) + causal

    @pl.when(shifted)
    def _():
        past = lambda n: (past_score(n), vt_of_block(n))
        own = lambda: (own_score(), vt_of_block(qi))

        def group(n0, count):
            _flash_tiles_shifted([past(n0 + t) for t in range(count)], p_ref, l_ref, acc_ref)

        _grouped_loop(qi, group, smallest=2)

        @pl.when((qi & 1) == 1)
        def _():
            _flash_tiles_shifted([past(qi - 1), own()], p_ref, l_ref, acc_ref)

        @pl.when((qi & 1) == 0)
        def _():
            _flash_tiles_shifted([own()], p_ref, l_ref, acc_ref)

    @pl.when(jnp.logical_not(shifted))
    def _():
        def att_body(n, carry):
            _flash_tile_step(past_score(n), vt_of_block(n), s_ref, mn_ref, m_ref, l_ref, acc_ref)
            return carry

        lax.fori_loop(0, qi, att_body, 0)
        _flash_tile_step(own_score(), vt_of_block(qi), s_ref, mn_ref, m_ref, l_ref, acc_ref)

    _flash_finish(o_ref, l_ref, acc_ref)


def _moba(P4, TT, shift, B, S):
    tq = MOBA_BLOCK
    nblk = S // MOBA_BLOCK
    assert S % MOBA_BLOCK == 0 and nblk <= MOBA_SLOTS
    topb = min(MOBA_TOPK, nblk - 1)
    gl = N_HEADS * MOBA_SLOTS
    return pl.pallas_call(
        functools.partial(_moba_kernel, tq=tq, topb=topb),
        grid=(B, S // tq),
        in_specs=[pl.BlockSpec(memory_space=pltpu.SMEM),
                  pl.BlockSpec((None, TILE, tq), lambda b, i: (b, TT_QC, i)),
                  pl.BlockSpec((None, None, S, TILE), lambda b, i: (T_KC, b, 0, 0)),
                  pl.BlockSpec((None, TILE, S), lambda b, i: (b, TT_VC, 0))],
        out_specs=pl.BlockSpec((None, tq, TILE), lambda b, i: (b, i, 0)),
        out_shape=jax.ShapeDtypeStruct((B, S, TILE), BF16),
        scratch_shapes=[pltpu.VMEM((gl, TILE), BF16), pltpu.VMEM((gl, TILE), BF16),
                        pltpu.VMEM((N_HEADS, 2 * LANES, tq), BF16)]
                       + _flash_scratch(MOBA_BLOCK, tq),
        compiler_params=_params("arbitrary", "arbitrary"),
        name="moba",
    )(shift, TT, P4, TT)


def _post_kernel(x_ref, oa0, oa1, oa2, la0, la1, la2, ob_ref, oc_ref, z0, z1, z2, g0, g1, g2,
                 wbr_ref, wout_ref, expand_ref, out_ref):
    l0, l1, l2 = la0[...], la1[...], la2[...]
    mx = jnp.maximum(jnp.maximum(l0, l1), l2)
    e0, e1, e2 = jnp.exp2(l0 - mx), jnp.exp2(l1 - mx), jnp.exp2(l2 - mx)
    den = e0 + e1 + e2

    def spread(w):
        hi = w.astype(BF16)
        lo = (w - hi.astype(F32)).astype(BF16)
        return _dot(hi, expand_ref[...]) + _dot(lo, expand_ref[...])

    o_a = (spread(e0 / den) * oa0[...].astype(F32) + spread(e1 / den) * oa1[...].astype(F32)
           + spread(e2 / den) * oa2[...].astype(F32))
    branches = (o_a, ob_ref[...].astype(F32), oc_ref[...].astype(F32))
    merged = jnp.zeros(out_ref.shape, F32)
    for n, (o, z, g) in enumerate(zip(branches, (z0, z1, z2), (g0, g1, g2))):
        y = _dot((o * z[0].astype(F32)).astype(BF16), wbr_ref[n])
        gate = jnp.concatenate([g[t] for t in range(g.shape[0])], axis=1)
        merged = merged + gate.astype(F32) * y
    out_ref[...] = x_ref[...] + _dot(merged.astype(BF16), wout_ref[...])


def _post(x2, oa, la, ob, oc, P, wbr, wout, tm):
    T = x2.shape[0]
    row = lambda width, t: pl.BlockSpec((tm, width), lambda i: (i, t))
    tiles = lambda count, first: pl.BlockSpec((count, tm, TILE), lambda i: (first // count, i, 0))
    per_g = D_MODEL // TILE
    assert T_G % per_g == 0
    head_of = np.arange(TILE) // HEAD_DIM
    expand = jnp.asarray((np.arange(LANES)[:, None] == head_of[None, :]).astype(np.float32), BF16)
    in_specs = ([row(D_MODEL, 0)] + [row(TILE, 0)] * 3 + [row(LANES, 0)] * 3 + [row(TILE, 0)] * 2
                + [tiles(1, T_Z + n) for n in range(N_BRANCH)]
                + [tiles(per_g, T_G + per_g * n) for n in range(N_BRANCH)]
                + [pl.BlockSpec((N_BRANCH, BRANCH_WIDTH, D_MODEL), lambda i: (0, 0, 0)),
                   pl.BlockSpec((D_MODEL, D_MODEL), lambda i: (0, 0)),
                   pl.BlockSpec((LANES, TILE), lambda i: (0, 0))])
    return pl.pallas_call(
        _post_kernel,
        grid=(T // tm,),
        in_specs=in_specs,
        out_specs=row(D_MODEL, 0),
        out_shape=jax.ShapeDtypeStruct((T, D_MODEL), F32),
        compiler_params=_params("arbitrary"),
        name="post",
    )(x2, oa[0], oa[1], oa[2], la[0], la[1], la[2], ob, oc, P, P, P, P, P, P, wbr, wout, expand)


def _rearrange_w_in(w):
    bw = BRANCH_WIDTH
    a_q, a_k, a_v = w[:, 0:3 * bw], w[:, 3 * bw:6 * bw], w[:, 6 * bw:9 * bw]
    off = 9 * bw
    b_q, b_k, b_v = (w[:, off + i * bw:off + (i + 1) * bw] for i in range(3)); off += 3 * bw
    iq = w[:, off:off + IDX_HEADS * IDX_DIM]; off += IDX_HEADS * IDX_DIM
    ik = w[:, off:off + IDX_DIM]; off += IDX_DIM
    iw = w[:, off:off + IDX_HEADS]; off += IDX_HEADS
    c_q, c_k, c_v = (w[:, off + i * bw:off + (i + 1) * bw] for i in range(3)); off += 3 * bw
    z = w[:, off:off + 3 * bw]; off += 3 * bw
    g = w[:, off:off + 3 * D_MODEL]; off += 3 * D_MODEL
    assert off == w.shape[1]
    cols = []
    for grp in (1, 2, 0):
        sl = slice(grp * bw, (grp + 1) * bw)
        cols += [a_q[:, sl], a_k[:, sl], a_v[:, sl]]
    pad = jnp.zeros((w.shape[0], TILE - IDX_DIM - IDX_HEADS), w.dtype)
    cols += [b_k, ik, iw, pad, c_k, g, z]
    w_nat = jnp.concatenate(cols, axis=1).astype(BF16)
    assert w_nat.shape[1] == J_T * TILE
    w_nat = w_nat.reshape(-1, J_T, TILE).transpose(1, 0, 2)
    w_t = jnp.concatenate([b_q, b_v, iq, c_q, c_v], axis=1).T.astype(BF16)
    assert w_t.shape[0] == N_T_TILES * TILE
    return w_nat, w_t


def _tile_tables(qk_g):
    kinds = np.zeros((N_TILES,), np.int32)
    scale = np.array([HEAD_DIM ** -0.5 * LOG2_E, 1.0], np.float32)
    rows = jnp.tile(qk_g.astype(F32) * scale[None, :, None], (1, 1, N_HEADS)).reshape(2 * N_BRANCH, TILE)
    rows = jnp.concatenate([rows, jnp.full((1, TILE), IDX_DIM ** -0.5, F32), jnp.ones((1, TILE), F32)])
    row_iq, row_one = 2 * N_BRANCH, 2 * N_BRANCH + 1
    which = np.full((N_TILES,), row_one, np.int32)

    def qk(tile, mixer):
        kinds[tile] = kinds[tile + 1] = KIND_NORM_ROPE
        which[tile], which[tile + 1] = 2 * mixer, 2 * mixer + 1

    qk(0, 0)
    qk(N_FOLD_TILES // 2, 0)
    qk(J_NAT + T_A, 0)
    for tile, mixer in ((T_KB, 1), (T_KC, 2)):
        kinds[J_NAT + tile] = KIND_NORM_ROPE
        which[J_NAT + tile] = 2 * mixer + 1
    kinds[J_NAT + T_IKW] = KIND_ROPE_LOW
    kinds[J_NAT + T_Z:J_NAT + T_Z + 3] = KIND_SILU
    kinds[J_NAT + T_G:J_NAT + T_G + 6] = KIND_SIGMOID
    for tile, mixer in ((TT_QB, 1), (TT_QC, 2)):
        kinds[J_T + tile] = KIND_NORM_ROPE
        which[J_T + tile] = 2 * mixer
    kinds[J_T + TT_IQ] = KIND_ROPE
    which[J_T + TT_IQ] = row_iq
    gains = rows[which[:J_T]][:, None, :]
    gains_t = jnp.broadcast_to(rows[which[J_T:]][:, :, None], (N_T_TILES, TILE, LANES))
    half = N_FOLD_TILES // 2
    dest = np.array([DEST_F1] * half + [DEST_F2] * half + [DEST_NAT] * N_NAT_TILES + [DEST_T] * N_T_TILES)
    assert all((k, d) in _ROUTES for k, d in zip(kinds.tolist(), dest.tolist()))
    routes = _route_code(kinds, dest).astype(np.int32)
    return jnp.asarray(routes), gains, gains_t


def _softmax_shift(qk_gain):
    bound = HEAD_DIM * jnp.max(jnp.abs(qk_gain[0])) * jnp.max(jnp.abs(qk_gain[1]))
    return (SHIFT_SLACK * HEAD_DIM ** -0.5 * LOG2_E * bound).reshape(1).astype(BF16).astype(F32)


def _rope_tables(positions):
    inv = ROPE_THETA ** (-jnp.arange(0, ROT_DIM, 2, dtype=F32) / ROT_DIM)
    ang = positions.astype(F32).reshape(-1)[:, None] * inv
    cos, sin = jnp.cos(ang), jnp.sin(ang)
    T = cos.shape[0]
    z8 = jnp.zeros((T, ROT_HALF), F32)
    rest1 = jnp.ones((T, HEAD_DIM - ROT_DIM), F32)
    rest0 = jnp.zeros((T, HEAD_DIM - ROT_DIM), F32)
    c = jnp.concatenate([cos, cos, rest1], axis=1)
    s1 = jnp.concatenate([-sin, z8, rest0], axis=1)
    s2 = jnp.concatenate([z8, sin, rest0], axis=1)
    two = lambda t: jnp.concatenate([t, t], axis=1)
    return (two(c), two(s1), two(s2)), (c.T, s1.T, s2.T)


def _block_diag_mean():
    h = np.arange(MXU_DIM) // HEAD_DIM
    return jnp.asarray((h[:, None] == h[None, :]).astype(np.float32) / HEAD_DIM, BF16)


def _layer(x2, B, S, tabs, tabs_t, bd, norm_g, w_in, qk_g, w_br, w_out, tm_in, tm_post):
    routes, gains, gains_t = _tile_tables(qk_g)
    w_nat, w_t = _rearrange_w_in(w_in)
    P, f1, f2, TT = _inproj(x2, norm_g[None, :], w_nat, w_t, routes, gains, gains_t, tabs, tabs_t, bd,
                            tm_in, B, S)
    P4 = P.reshape(N_NAT_TILES, B, S, TILE)
    oa, la = [], []
    for g, qkv in enumerate((P4.reshape(N_NAT_TILES, B, 1, S, TILE), f1, f2)):
        o, lse = _attn_a(qkv, _softmax_shift(qk_g[0]), g, DIL_PATTERNS[g][1], B, S)
        oa.append(o.reshape(B * S, TILE))
        la.append(lse.reshape(B * S, LANES))
    ob = _dsa(P4, TT, _softmax_shift(qk_g[1]), B, S).reshape(B * S, TILE)
    oc = _moba(P4, TT, _softmax_shift(qk_g[2]), B, S).reshape(B * S, TILE)
    return _post(x2, oa, la, ob, oc, P, w_br.astype(BF16), w_out.astype(BF16), tm_post)


def _forward(x, positions, norm_g, w_in, qk_g, w_br, w_out, tm_in=IN_ROW_TILE, tm_post=POST_ROW_TILE):
    B, S, D = x.shape
    tabs, tabs_t = _rope_tables(positions)
    bd = _block_diag_mean()
    x2 = x.reshape(B * S, D)
    for layer in range(norm_g.shape[0]):
        x2 = _layer(x2, B, S, tabs, tabs_t, bd, norm_g[layer], w_in[layer], qk_g[layer],
                    w_br[layer], w_out[layer], tm_in, tm_post)
    return x2.reshape(B, S, D)


def kernel(x, positions, norm_g, w_in, qk_g, w_br, w_out):
    return _forward(x, positions, norm_g, w_in, qk_g, w_br, w_out)
```

```python
import functools

import jax
import jax.numpy as jnp
import numpy as np
from jax import lax
from jax.experimental import pallas as pl
from jax.experimental.pallas import tpu as pltpu

F32 = jnp.float32
BF16 = jnp.bfloat16
I32 = jnp.int32
I16 = jnp.int16

D_MODEL = 1024
HEAD_DIM = 64
ROT_DIM = HEAD_DIM // 4
ROT_HALF = ROT_DIM // 2
ROPE_THETA = 500000.0
NORM_EPS = 1e-6
NEG_INF = -1e30
LOG2_E = 1.4426950408889634
MAX_SOFTMAX_SHIFT = 60.0
SHIFT_SLACK = 1.05
N_HEADS = 8
BRANCH_WIDTH = N_HEADS * HEAD_DIM
N_BRANCH = 3
DIL_PATTERNS = ((128, 1), (512, 4), (2048, 16))
BAND = 128
IDX_HEADS = 8
IDX_DIM = 64
IDX_TOPK_MAX = 256
MOBA_BLOCK = 256
MOBA_TOPK = 3
MOBA_SLOTS = 16

LANES = 128
SUBLANES = 8
MXU_DIM = 256
VMEM_LIMIT_BYTES = 56 * 1024 * 1024

TILE = BRANCH_WIDTH
N_FOLD_TILES = 6
T_A = 0
T_KB = 3
T_IKW = 4
T_KC = 5
T_G = 6
T_Z = 12
N_NAT_TILES = 15
TT_QB, TT_VB, TT_IQ, TT_QC, TT_VC = range(5)
N_T_TILES = 5
J_NAT = N_FOLD_TILES
J_T = N_FOLD_TILES + N_NAT_TILES
N_TILES = J_T + N_T_TILES
IN_ROW_TILE = 2048
IN_CHUNK = 256
A_STEP_TOKENS = 2048
GROUP = 4
SPARSE_GROUP = 8
SPARSE_TILE = 256
POST_ROW_TILE = 512

KIND_PLAIN, KIND_NORM_ROPE, KIND_ROPE, KIND_ROPE_LOW, KIND_SILU, KIND_SIGMOID = range(6)
DEST_NAT, DEST_F1, DEST_F2, DEST_T = range(4)
_ROUTES = ((KIND_PLAIN, DEST_NAT), (KIND_PLAIN, DEST_F1), (KIND_PLAIN, DEST_F2),
           (KIND_NORM_ROPE, DEST_NAT), (KIND_NORM_ROPE, DEST_F1), (KIND_NORM_ROPE, DEST_F2),
           (KIND_ROPE_LOW, DEST_NAT), (KIND_SILU, DEST_NAT), (KIND_SIGMOID, DEST_NAT),
           (KIND_PLAIN, DEST_T), (KIND_NORM_ROPE, DEST_T), (KIND_ROPE, DEST_T))


def _route_code(kind, dest):
    return kind * 4 + dest

_FLOOR_KEY = int(np.array(-5e29, np.float32).view(np.int32)) ^ 0x7FFFFFFF
_INT_MIN = -2 ** 31
_HALF16 = 32768
_PACK16 = 16
_LOG2_MOBA_BLOCK = MOBA_BLOCK.bit_length() - 1
_LOG2_N_HEADS = N_HEADS.bit_length() - 1
_LOG2_HEAD_DIM = HEAD_DIM.bit_length() - 1


def _dot(a, b):
    return jnp.dot(a, b, preferred_element_type=F32)


def _dot_nt(a, b):
    return lax.dot_general(a, b, (((1,), (1,)), ((), ())), preferred_element_type=F32)


def _params(*sem):
    return pltpu.CompilerParams(dimension_semantics=sem, vmem_limit_bytes=VMEM_LIMIT_BYTES)


def _inproj_kernel(route_ref, x_ref, ng_ref, w_ref, wt_ref, gain_ref, gain_t_ref,
                   cos_ref, s1_ref, s2_ref, cos_t_ref, s1_t_ref, s2_t_ref, bd_ref,
                   o_ref, f1_ref, f2_ref, tt_ref, h_ref, ht_ref, stage_ref):
    j = pl.program_id(1)
    tm = x_ref.shape[0]
    n_ch = TILE // LANES

    @pl.when(j == 0)
    def _():
        x = x_ref[...]
        ms = jnp.mean(x * x, axis=-1, keepdims=True)
        h = x * lax.rsqrt(ms + NORM_EPS) * ng_ref[...]
        h_ref[...] = h.astype(BF16)
        ht_ref[...] = jnp.transpose(h).astype(BF16)

    def fold(val, ref, dil, tok):
        rows = (tok.stop - tok.start) // dil
        for c in range(n_ch):
            stage_ref[c, tok, :] = val[:, c * LANES:(c + 1) * LANES]
        for r in range(dil):
            parts = [stage_ref[c, pl.ds(tok.start + r, rows, stride=dil), :] for c in range(n_ch)]
            ref[r, tok.start // dil:tok.stop // dil, :] = jnp.concatenate(parts, axis=1).astype(ref.dtype)

    def rope(v, low_only, tok):
        c = jnp.tile(cos_ref[tok, :], (1, n_ch))
        s1 = jnp.tile(s1_ref[tok, :], (1, n_ch))
        s2 = jnp.tile(s2_ref[tok, :], (1, n_ch))
        if low_only:
            low = lax.broadcasted_iota(I32, v.shape, 1) < HEAD_DIM
            c = jnp.where(low, c, 1.0)
            s1 = jnp.where(low, s1, 0.0)
            s2 = jnp.where(low, s2, 0.0)
        return v * c + pltpu.roll(v, TILE - ROT_HALF, 1) * s1 + pltpu.roll(v, ROT_HALF, 1) * s2

    def rope_t(v, tok):
        c = jnp.tile(cos_t_ref[:, tok], (N_HEADS, 1))
        s1 = jnp.tile(s1_t_ref[:, tok], (N_HEADS, 1))
        s2 = jnp.tile(s2_t_ref[:, tok], (N_HEADS, 1))
        return v * c + pltpu.roll(v, TILE - ROT_HALF, 0) * s1 + pltpu.roll(v, ROT_HALF, 0) * s2

    def epilogue(kind, y, tok):
        if kind == KIND_PLAIN:
            return y
        if kind == KIND_NORM_ROPE:
            y2 = (y * y).astype(BF16)
            ms = jnp.concatenate([_dot(y2[:, c * MXU_DIM:(c + 1) * MXU_DIM], bd_ref[...])
                                  for c in range(TILE // MXU_DIM)], axis=1)
            return rope(y * lax.rsqrt(ms + NORM_EPS) * gain_ref[0], False, tok)
        if kind == KIND_ROPE_LOW:
            return rope(y, True, tok)
        if kind == KIND_SILU:
            return y / (1.0 + jnp.exp(-y))
        assert kind == KIND_SIGMOID
        return 1.0 / (1.0 + jnp.exp(-y))

    def epilogue_t(kind, yt, tok):
        if kind == KIND_PLAIN:
            return yt
        gain = jnp.tile(gain_t_ref[0], (1, yt.shape[1] // LANES))
        if kind == KIND_NORM_ROPE:
            y2 = (yt * yt).astype(BF16)
            ms = jnp.concatenate([_dot(bd_ref[...], y2[c * MXU_DIM:(c + 1) * MXU_DIM, :])
                                  for c in range(TILE // MXU_DIM)], axis=0)
            return rope_t(yt * lax.rsqrt(ms + NORM_EPS) * gain, tok)
        assert kind == KIND_ROPE
        return rope_t(yt * gain, tok)

    route = route_ref[j]
    for kind, dest in _ROUTES:
        @pl.when(route == _route_code(kind, dest))
        def _(kind=kind, dest=dest):
            if dest == DEST_T:
                tok = slice(0, tm)
                tt_ref[...] = epilogue_t(kind, _dot(wt_ref[...], ht_ref[...]), tok).astype(tt_ref.dtype)
                return
            chunk = max(IN_CHUNK, _PACK16 * DIL_PATTERNS[2][1]) if dest == DEST_F2 else IN_CHUNK
            for m in range(tm // chunk):
                tok = slice(m * chunk, (m + 1) * chunk)
                val = epilogue(kind, _dot(h_ref[tok, :], w_ref[...]), tok)
                if dest == DEST_NAT:
                    o_ref[tok, :] = val.astype(o_ref.dtype)
                elif dest == DEST_F1:
                    fold(val, f1_ref, DIL_PATTERNS[1][1], tok)
                else:
                    fold(val, f2_ref, DIL_PATTERNS[2][1], tok)


def _inproj(x2, ng, w_nat, w_t, routes, gains, gains_t, tabs, tabs_t, bd, tm, B, S):
    T = x2.shape[0]
    grid = (T // tm, N_TILES)
    per_b = S // tm
    d1, d2 = DIL_PATTERNS[1][1], DIL_PATTERNS[2][1]
    half = N_FOLD_TILES // 2
    assert S % tm == 0 and tm % (_PACK16 * d2) == 0 and tm % IN_CHUNK == 0 and IN_CHUNK % (_PACK16 * d1) == 0
    n_w = J_T

    def fold_spec(dil, first):
        return pl.BlockSpec(
            (None, None, dil, tm // dil, TILE),
            lambda i, j, k: (jnp.clip(j - first, 0, half - 1), i // per_b, 0, i % per_b, 0))

    once_per_row_tile = pl.Buffered(1)
    row_tab = pl.BlockSpec((tm, LANES), lambda i, j, k: (i, 0), pipeline_mode=once_per_row_tile)
    col_tab = pl.BlockSpec((HEAD_DIM, tm), lambda i, j, k: (0, i), pipeline_mode=once_per_row_tile)
    t_idx = lambda j: jnp.maximum(j - J_T, 0)
    return pl.pallas_call(
        _inproj_kernel,
        grid_spec=pltpu.PrefetchScalarGridSpec(
            num_scalar_prefetch=1,
            grid=grid,
            in_specs=[
                pl.BlockSpec((tm, D_MODEL), lambda i, j, k: (i, 0), pipeline_mode=once_per_row_tile),
                pl.BlockSpec((1, D_MODEL), lambda i, j, k: (0, 0)),
                pl.BlockSpec((None, D_MODEL, TILE), lambda i, j, k: (jnp.minimum(j, n_w - 1), 0, 0)),
                pl.BlockSpec((TILE, D_MODEL), lambda i, j, k: (t_idx(j), 0)),
                pl.BlockSpec((1, 1, TILE), lambda i, j, k: (jnp.minimum(j, n_w - 1), 0, 0)),
                pl.BlockSpec((1, TILE, LANES), lambda i, j, k: (t_idx(j), 0, 0)),
                row_tab, row_tab, row_tab, col_tab, col_tab, col_tab,
                pl.BlockSpec((MXU_DIM, MXU_DIM), lambda i, j, k: (0, 0)),
            ],
            out_specs=[
                pl.BlockSpec((None, tm, TILE), lambda i, j, k: (jnp.clip(j - J_NAT, 0, N_NAT_TILES - 1), i, 0)),
                fold_spec(d1, 0),
                fold_spec(d2, half),
                pl.BlockSpec((None, TILE, tm), lambda i, j, k: (i // per_b, t_idx(j), i % per_b)),
            ],
            scratch_shapes=[pltpu.VMEM((tm, D_MODEL), BF16), pltpu.VMEM((D_MODEL, tm), BF16),
                            pltpu.VMEM((TILE // LANES, tm, LANES), F32)],
        ),
        out_shape=[jax.ShapeDtypeStruct((N_NAT_TILES, T, TILE), BF16),
                   jax.ShapeDtypeStruct((half, B, d1, S // d1, TILE), BF16),
                   jax.ShapeDtypeStruct((half, B, d2, S // d2, TILE), BF16),
                   jax.ShapeDtypeStruct((B, N_T_TILES * TILE, S), BF16)],
        compiler_params=_params("arbitrary", "arbitrary"),
        name="inproj",
    )(routes, x2, ng, w_nat, w_t, gains, gains_t, *tabs, *tabs_t, bd)


def _attn_a_kernel(shift_ref, q_ref, kc_ref, kp_ref, vc_ref, vp_ref, o_ref, lse_ref, kcat, vtcat, ost, lst,
                   s_ref, p_ref, *, tq, dil):
    m = pl.program_id(1)
    nsub = tq // BAND
    n_ch = TILE // LANES
    nk = 2 * BAND
    key = lax.broadcasted_iota(I32, (nk, BAND), 0)
    qry = lax.broadcasted_iota(I32, (nk, BAND), 1) + BAND
    in_band = jnp.logical_and(key <= qry, key >= qry - BAND)
    lane = lax.broadcasted_iota(I32, (BAND, LANES), 1)
    half_mask = (jnp.where(lane < HEAD_DIM, 1.0, 0.0).astype(BF16),
                 jnp.where(lane < HEAD_DIM, 0.0, 1.0).astype(BF16))
    ones = jnp.ones((_PACK16, nk), BF16)
    shifted = shift_ref[0] <= MAX_SOFTMAX_SHIFT
    shift = jnp.where(shifted, shift_ref[0], 0.0)

    def load_residue(r, slot):
        kcat[slot, 0:BAND, :] = kp_ref[r]
        kcat[slot, BAND:, :] = kc_ref[r]
        vtcat[slot, :, 0:BAND] = jnp.transpose(vp_ref[r].astype(F32)).astype(BF16)
        for c in range(nsub):
            chunk = vc_ref[r, c * BAND:(c + 1) * BAND, :].astype(F32)
            vtcat[slot, :, (c + 1) * BAND:(c + 2) * BAND] = jnp.transpose(chunk).astype(BF16)

    def block(r, u, use_shift, slot, par):
        r0 = pl.multiple_of(u * BAND, BAND)
        q = q_ref[r, pl.ds(r0, BAND), :]
        k2 = kcat[slot, pl.ds(r0, nk), :]
        valid = jnp.logical_and(in_band, key >= jnp.where(m * nsub + u > 0, 0, BAND))
        bias = jnp.where(valid, -shift, NEG_INF)

        def score(h):
            pair = slice((h // 2) * LANES, (h // 2 + 1) * LANES)
            q_h = q[:, pair] * half_mask[h % 2]
            return _dot_nt(k2[:, pair], q_h) + bias

        def pv_of(h, p):
            vt_h = vtcat[slot, h * HEAD_DIM:(h + 1) * HEAD_DIM, pl.ds(r0, nk)]
            pv = _dot(jnp.concatenate([vt_h, ones], axis=0), p)
            return pv[:HEAD_DIM], pv[HEAD_DIM:HEAD_DIM + 1]

        outs, lses = [], []
        if use_shift:
            for h in range(N_HEADS):
                p_ref[par, h] = jnp.exp2(score(h)).astype(BF16)
            for h in range(N_HEADS):
                acc, den = pv_of(h, p_ref[par, h])
                outs.append(acc / den)
                lses.append(shift + jnp.log2(den))
        else:
            maxes = []
            for h in range(N_HEADS):
                s = score(h)
                s_ref[par, h] = s
                maxes.append(jnp.max(s, axis=0, keepdims=True))
            for h in range(N_HEADS):
                acc, den = pv_of(h, jnp.exp2(s_ref[par, h] - maxes[h]).astype(BF16))
                outs.append(acc / den)
                lses.append(maxes[h] + jnp.log2(den))
        o = jnp.transpose(jnp.concatenate(outs, axis=0))
        lse8 = jnp.concatenate(lses, axis=0)
        lse = jnp.transpose(jnp.tile(lse8, (LANES // N_HEADS, 1)))
        rows = pl.ds(r + u * (BAND * dil), BAND, stride=dil) if dil > 1 else pl.ds(r0, BAND)
        for c in range(n_ch):
            ost[c, rows, :] = o[:, c * LANES:(c + 1) * LANES]
        lst[rows, :] = lse

    def both_paths(blocks):
        @pl.when(shifted)
        def _():
            for r, u, slot, par in blocks:
                block(r, u, True, slot, par)

        @pl.when(jnp.logical_not(shifted))
        def _():
            for r, u, slot, par in blocks:
                block(r, u, False, slot, par)

    if nsub % GROUP == 0:
        def residue(r, carry):
            load_residue(r, 0)

            def body(i, carry):
                both_paths([(r, GROUP * i + t, 0, t) for t in range(GROUP)])
                return carry

            return lax.fori_loop(0, nsub // GROUP, body, carry)

        lax.fori_loop(0, dil, residue, 0)
    else:
        assert nsub == 1 and dil % GROUP == 0

        def residue_group(i, carry):
            for t in range(GROUP):
                load_residue(GROUP * i + t, t)
            both_paths([(GROUP * i + t, 0, t, t) for t in range(GROUP)])
            return carry

        lax.fori_loop(0, dil // GROUP, residue_group, 0)
    o_ref[0] = jnp.concatenate([ost[c] for c in range(n_ch)], axis=1).astype(o_ref.dtype)
    lse_ref[0] = lst[...]


def _attn_a(qkv, shift, g, dil, B, S):
    ts = min(A_STEP_TOKENS, S)
    tq = ts // dil
    assert S % ts == 0 and tq % BAND == 0
    sub = tq // BAND

    def spec_cur(t):
        return pl.BlockSpec((None, None, dil, tq, TILE), lambda b, m: (t, b, 0, m, 0))

    def spec_prev(t):
        return pl.BlockSpec((None, None, dil, BAND, TILE),
                            lambda b, m: (t, b, 0, jnp.maximum(m * sub - 1, 0), 0))

    n_ch = TILE // LANES
    kv_slots = 1 if sub % GROUP == 0 else GROUP
    return pl.pallas_call(
        functools.partial(_attn_a_kernel, tq=tq, dil=dil),
        grid=(B, S // ts),
        in_specs=[pl.BlockSpec(memory_space=pltpu.SMEM),
                  spec_cur(0), spec_cur(1), spec_prev(1), spec_cur(2), spec_prev(2)],
        out_specs=[pl.BlockSpec((1, ts, TILE), lambda b, m: (b, m, 0)),
                   pl.BlockSpec((1, ts, LANES), lambda b, m: (b, m, 0))],
        out_shape=[jax.ShapeDtypeStruct((B, S, TILE), BF16), jax.ShapeDtypeStruct((B, S, LANES), F32)],
        scratch_shapes=[pltpu.VMEM((kv_slots, tq + BAND, TILE), BF16), pltpu.VMEM((kv_slots, TILE, tq + BAND), BF16),
                        pltpu.VMEM((n_ch, ts, LANES), F32), pltpu.VMEM((ts, LANES), F32),
                        pltpu.VMEM((GROUP, N_HEADS, 2 * BAND, BAND), F32),
                        pltpu.VMEM((GROUP, N_HEADS, 2 * BAND, BAND), BF16)],
        compiler_params=_params("arbitrary", "arbitrary"),
        name=f"attn_a{g}",
    )(shift, qkv, qkv, qkv, qkv, qkv)


def _pad_q(qt_ref, qpad_ref):
    qpad_ref[...] = jnp.zeros(qpad_ref.shape, qpad_ref.dtype)
    for h in range(N_HEADS):
        r0 = h * LANES + (h % 2) * HEAD_DIM
        qpad_ref[r0:r0 + HEAD_DIM, :] = qt_ref[h * HEAD_DIM:(h + 1) * HEAD_DIM, :]


def _flash_init(m_ref, l_ref, acc_ref):
    m_ref[...] = jnp.full(m_ref.shape, NEG_INF, F32)
    l_ref[...] = jnp.zeros(l_ref.shape, F32)
    acc_ref[...] = jnp.zeros(acc_ref.shape, F32)


def _flash_tile_step(score_of_head, vt_of_head, s_ref, mn_ref, m_ref, l_ref, acc_ref):
    tk = s_ref.shape[1]
    for h in range(N_HEADS):
        s = score_of_head(h)
        s_ref[h] = s
        mn_ref[h] = jnp.maximum(m_ref[h], jnp.max(s, axis=0, keepdims=True))
    ones = jnp.ones((_PACK16, tk), BF16)
    for h in range(N_HEADS):
        m_prev, m_next = m_ref[h], mn_ref[h]
        p = jnp.exp2(s_ref[h] - jnp.tile(m_next, (tk // SUBLANES, 1)))
        alpha = jnp.exp2(m_prev - m_next)
        pv = _dot(jnp.concatenate([vt_of_head(h), ones], axis=0), p.astype(BF16))
        l_ref[h] = alpha * l_ref[h] + pv[HEAD_DIM:HEAD_DIM + SUBLANES]
        acc_ref[h] = acc_ref[h] * jnp.tile(alpha, (HEAD_DIM // SUBLANES, 1)) + pv[:HEAD_DIM]
        m_ref[h] = m_next


def _flash_tiles_shifted(tiles, p_ref, l_ref, acc_ref):
    for t, (score_of_head, _) in enumerate(tiles):
        for h in range(N_HEADS):
            p_ref[t, h] = jnp.exp2(score_of_head(h)).astype(BF16)
    ones = jnp.ones((_PACK16, p_ref.shape[2]), BF16)
    for t, (_, vt_of_head) in enumerate(tiles):
        for h in range(N_HEADS):
            pv = _dot(jnp.concatenate([vt_of_head(h), ones], axis=0), p_ref[t, h])
            l_ref[h] = l_ref[h] + pv[HEAD_DIM:HEAD_DIM + SUBLANES]
            acc_ref[h] = acc_ref[h] + pv[:HEAD_DIM]


def _grouped_loop(n, group, smallest=1):
    def body(i, carry):
        group(i * SPARSE_GROUP, SPARSE_GROUP)
        return carry

    lax.fori_loop(0, lax.shift_right_logical(n, SPARSE_GROUP.bit_length() - 1), body, 0)
    size = SPARSE_GROUP // 2
    while size >= smallest:
        @pl.when((n & size) != 0)
        def _(size=size):
            group((n >> size.bit_length()) << size.bit_length(), size)
        size //= 2


def _flash_finish(o_ref, l_ref, acc_ref):
    outs = [acc_ref[h] / jnp.tile(l_ref[h], (HEAD_DIM // SUBLANES, 1)) for h in range(N_HEADS)]
    o_ref[...] = jnp.transpose(jnp.concatenate(outs, axis=0)).astype(o_ref.dtype)


def _dsa_kernel(shift_ref, qt_ref, k_ref, vt_ref, iqt_ref, ikw_q_ref, ikw_ref, o_ref,
                key_ref, hi_ref, lo_ref, qpad_ref, bias_ref, p_ref, s_ref, mn_ref, m_ref, l_ref, acc_ref,
                *, tq, tk, topk):
    qi = pl.program_id(1)
    n_kt = qi + 1
    n_acc = 4
    rows8 = tk // SUBLANES

    w8 = jnp.transpose(ikw_q_ref[...].astype(F32))[HEAD_DIM:HEAD_DIM + IDX_HEADS, :] * (IDX_HEADS ** -0.5)
    krow = lax.broadcasted_iota(I32, (tk, tq), 0)
    qcol = lax.broadcasted_iota(I32, (tk, tq), 1)

    def score_tile(c, diagonal):
        c0 = pl.multiple_of(c * tk, tk)
        kx = ikw_ref[pl.ds(c0, tk), :][:, :IDX_DIM]
        sc = jnp.zeros((tk, tq), F32)
        for h in range(IDX_HEADS):
            lg = _dot(kx, iqt_ref[h * IDX_DIM:(h + 1) * IDX_DIM, :])
            sc = sc + w8[h:h + 1, :] * jnp.maximum(lg, 0.0)
        sc = jnp.where(sc == 0.0, 0.0, sc)
        if diagonal:
            sc = jnp.where(krow <= qcol, sc, NEG_INF)
        bits = pltpu.bitcast(sc, I32)
        key = jnp.where(bits < 0, bits ^ 0x7FFFFFFF, bits)
        key_ref[pl.ds(c0, tk), :] = key
        hi_ref[pl.ds(c0, tk), :] = jnp.right_shift(key, 16).astype(I16)
        lo_ref[pl.ds(c0, tk), :] = ((key & 0xFFFF) - _HALF16).astype(I16)

    _grouped_loop(qi, lambda c0, count: [score_tile(c0 + t, False) for t in range(count)], smallest=2)

    @pl.when((qi & 1) == 1)
    def _():
        score_tile(qi - 1, False)
        score_tile(qi, True)

    @pl.when((qi & 1) == 0)
    def _():
        score_tile(qi, True)

    rows16 = tk // _PACK16
    one16 = jnp.ones((_PACK16, tq), I16)
    zero16 = jnp.zeros((_PACK16, tq), I16)

    def count16(ref, pred):
        def cbody(c, accs):
            accs = list(accs)
            c0 = pl.multiple_of(c * tk, tk)
            t = ref[pl.ds(c0, tk), :]
            for g in range(rows16):
                hit = jnp.where(pred(t[g * _PACK16:(g + 1) * _PACK16]), one16, zero16)
                accs[g % n_acc] = accs[g % n_acc] + hit
            return tuple(accs)
        accs = lax.fori_loop(0, n_kt, cbody, tuple(zero16 for _ in range(n_acc)))
        return jnp.sum(functools.reduce(lambda a, b: a + b, accs).astype(I32), axis=0, keepdims=True)

    def as16(v):
        return jnp.broadcast_to(v, (_PACK16, tq)).astype(I16)

    def select16(ref, need, cge0):
        def bit_body(it, carry):
            ans, cge = carry
            cand_u = ans | lax.shift_left(jnp.int32(1), 15 - it)
            cand = as16(cand_u - _HALF16)
            cnt = count16(ref, lambda t: t >= cand)
            ok = cnt >= need
            return jnp.where(ok, cand_u, ans), jnp.where(ok, cnt, cge)
        return lax.fori_loop(0, 16, bit_body, (jnp.zeros((1, tq), I32), cge0))

    n_all = jnp.zeros((1, tq), I32) + n_kt * tk
    p_u, cge_hi = select16(hi_ref, topk, n_all)
    p16 = as16(p_u - _HALF16)
    c_gt = count16(hi_ref, lambda t: t > p16)

    def bucket_body(c, carry):
        c0 = pl.multiple_of(c * tk, tk)
        lo_ref[pl.ds(c0, tk), :] = jnp.where(hi_ref[pl.ds(c0, tk), :] == jnp.tile(p16, (rows16, 1)),
                                             lo_ref[pl.ds(c0, tk), :], jnp.int16(-_HALF16))
        return carry

    lax.fori_loop(0, n_kt, bucket_body, 0)
    l_u, cge_lo = select16(lo_ref, topk - c_gt, cge_hi - c_gt)
    cge = c_gt + cge_lo
    thr = jnp.maximum((p_u - _HALF16) * 65536 + l_u, _FLOOR_KEY)

    def count_rows(fn, n_out):
        def cbody(c, accs):
            accs = [list(a) for a in accs]
            c0 = pl.multiple_of(c * tk, tk)
            kt = key_ref[pl.ds(c0, tk), :]
            for g in range(rows8):
                r0 = c0 + g * SUBLANES
                vals = fn(kt[g * SUBLANES:(g + 1) * SUBLANES], r0)
                for o in range(n_out):
                    accs[o][g % n_acc] = accs[o][g % n_acc] + vals[o]
            return tuple(tuple(a) for a in accs)
        z = jnp.zeros((SUBLANES, tq), I32)
        accs = lax.fori_loop(0, n_kt, cbody, tuple(tuple(z for _ in range(n_acc)) for _ in range(n_out)))
        return [jnp.sum(functools.reduce(lambda a, b: a + b, a), axis=0, keepdims=True) for a in accs]

    tie = jnp.logical_and(cge > topk, thr > _FLOOR_KEY)
    any_tie = jnp.max(jnp.where(tie, 1, 0)) > 0
    thr8 = jnp.broadcast_to(thr, (SUBLANES, tq))

    @pl.when(any_tie)
    def _():
        n_bits = int(np.log2(key_ref.shape[0]))
        sub = lax.broadcasted_iota(I32, (SUBLANES, tq), 0)

        def jbody(it, lo):
            cand = lo + lax.shift_left(jnp.int32(1), n_bits - 1 - it)
            pos = jnp.broadcast_to(cand - 1, (SUBLANES, tq))

            def f(kk, r0):
                eq = jnp.logical_and(kk == thr8, sub + r0 <= pos)
                return jnp.where(kk > thr8, 1, 0), jnp.where(eq, 1, 0)

            gt, eq = count_rows(f, 2)
            return jnp.where(gt + eq >= topk, lo, cand)

        jrow = jnp.broadcast_to(lax.fori_loop(0, n_bits, jbody, jnp.zeros((1, tq), I32)), (SUBLANES, tq))
        tie8 = jnp.broadcast_to(jnp.where(tie, 1, 0), (SUBLANES, tq)) > 0

        def fix(c, carry):
            for g in range(rows8):
                r0 = pl.multiple_of(c * tk + g * SUBLANES, SUBLANES)
                kk = key_ref[pl.ds(r0, SUBLANES), :]
                drop = jnp.logical_and(tie8, jnp.logical_and(kk == thr8, sub + r0 > jrow))
                key_ref[pl.ds(r0, SUBLANES), :] = jnp.where(drop, _INT_MIN, kk)
            return carry

        lax.fori_loop(0, n_kt, fix, 0)

    _pad_q(qt_ref, qpad_ref)
    _flash_init(m_ref, l_ref, acc_ref)
    shift = shift_ref[0]

    def tile(j, t, shifted):
        c0 = pl.multiple_of(j * tk, tk)
        sel = key_ref[pl.ds(c0, tk), :] >= thr
        bias_ref[t] = jnp.where(sel, -shift if shifted else 0.0, NEG_INF)
        k = k_ref[pl.ds(c0, tk), :]

        def score(h):
            k_pair = k[:, (h // 2) * LANES:(h // 2 + 1) * LANES]
            return _dot(k_pair, qpad_ref[h * LANES:(h + 1) * LANES, :]) + bias_ref[t]

        return score, lambda h: vt_ref[h * HEAD_DIM:(h + 1) * HEAD_DIM, pl.ds(c0, tk)]

    @pl.when(shift <= MAX_SOFTMAX_SHIFT)
    def _():
        def group(j0, count):
            _flash_tiles_shifted([tile(j0 + t, t, True) for t in range(count)], p_ref, l_ref, acc_ref)

        _grouped_loop(n_kt, group)

    @pl.when(shift > MAX_SOFTMAX_SHIFT)
    def _():
        def att_body(j, carry):
            _flash_tile_step(*tile(j, 0, False), s_ref, mn_ref, m_ref, l_ref, acc_ref)
            return carry

        lax.fori_loop(0, n_kt, att_body, 0)

    _flash_finish(o_ref, l_ref, acc_ref)


def _flash_scratch(tk, tq):
    stat = pltpu.VMEM((N_HEADS, SUBLANES, tq), F32)
    return [pltpu.VMEM((SPARSE_GROUP, N_HEADS, tk, tq), BF16), pltpu.VMEM((N_HEADS, tk, tq), F32), stat, stat, stat,
            pltpu.VMEM((N_HEADS, HEAD_DIM, tq), F32)]


def _dsa(P4, TT, shift, B, S):
    tq = tk = SPARSE_TILE
    topk = min(IDX_TOPK_MAX, S // 4)
    assert S % tq == 0 and topk <= tk
    def tt_q(t):
        return pl.BlockSpec((None, TILE, tq), lambda b, i: (b, t, i))

    return pl.pallas_call(
        functools.partial(_dsa_kernel, tq=tq, tk=tk, topk=topk),
        grid=(B, S // tq),
        in_specs=[pl.BlockSpec(memory_space=pltpu.SMEM),
                  tt_q(TT_QB),
                  pl.BlockSpec((None, None, S, TILE), lambda b, i: (T_KB, b, 0, 0)),
                  pl.BlockSpec((None, TILE, S), lambda b, i: (b, TT_VB, 0)),
                  tt_q(TT_IQ),
                  pl.BlockSpec((None, None, tq, LANES), lambda b, i: (T_IKW, b, i, 0)),
                  pl.BlockSpec((None, None, S, LANES), lambda b, i: (T_IKW, b, 0, 0))],
        out_specs=pl.BlockSpec((None, tq, TILE), lambda b, i: (b, i, 0)),
        out_shape=jax.ShapeDtypeStruct((B, S, TILE), BF16),
        scratch_shapes=[pltpu.VMEM((S, tq), I32), pltpu.VMEM((S, tq), I16), pltpu.VMEM((S, tq), I16),
                        pltpu.VMEM((N_HEADS * LANES, tq), BF16),
                        pltpu.VMEM((SPARSE_GROUP, tk, tq), F32)] + _flash_scratch(tk, tq),
        compiler_params=_params("arbitrary", "arbitrary"),
        name="dsa",
    )(shift, TT, P4, TT, TT, P4, P4)


def _moba_kernel(shift_ref, qt_ref, k_ref, vt_ref, o_ref, kmh_ref, kml_ref, qaug_ref,
                 p_ref, s_ref, mn_ref, m_ref, l_ref, acc_ref, *, tq, topb):
    qi = pl.program_id(1)
    S = k_ref.shape[0]
    gl = N_HEADS * MOBA_SLOTS

    @pl.when(qi == 0)
    def _():
        blk_row = jnp.right_shift(lax.broadcasted_iota(I32, (gl, S), 0), _LOG2_N_HEADS)
        blk_col = jnp.right_shift(lax.broadcasted_iota(I32, (gl, S), 1), _LOG2_MOBA_BLOCK)
        avg = jnp.where(blk_row == blk_col, 1.0 / MOBA_BLOCK, 0.0).astype(BF16)
        km = _dot(avg, k_ref[...])
        r_head = lax.broadcasted_iota(I32, (gl, TILE), 0) & (N_HEADS - 1)
        c_head = jnp.right_shift(lax.broadcasted_iota(I32, (gl, TILE), 1), _LOG2_HEAD_DIM)
        km = jnp.where(r_head == c_head, km, 0.0)
        hi = km.astype(BF16)
        kmh_ref[...] = hi
        kml_ref[...] = (km - hi.astype(F32)).astype(BF16)

    shifted = shift_ref[0] <= MAX_SOFTMAX_SHIFT
    shift = jnp.where(shifted, shift_ref[0], 0.0)
    qt = qt_ref[...]
    gate = _dot(kmh_ref[...], qt) + _dot(kml_ref[...], qt)
    row = lax.broadcasted_iota(I32, (gl, tq), 0)
    gate = jnp.where(jnp.right_shift(row, _LOG2_N_HEADS) < qi, gate, NEG_INF)
    g = [gate[n * N_HEADS:(n + 1) * N_HEADS] for n in range(MOBA_SLOTS)]
    biases = []
    for n in range(MOBA_SLOTS):
        beaten = jnp.where(n < qi, 0, topb) + jnp.zeros((N_HEADS, tq), I32)
        for n2 in range(MOBA_SLOTS):
            if n2 != n:
                beaten = beaten + jnp.where((g[n2] >= g[n]) if n2 < n else (g[n2] > g[n]), 1, 0)
        biases.append(jnp.where(beaten < topb, -shift, NEG_INF))
    selb = jnp.concatenate(biases, axis=0)

    qaug_ref[...] = jnp.zeros(qaug_ref.shape, qaug_ref.dtype)
    for h in range(N_HEADS):
        r0 = (h % 2) * HEAD_DIM
        qaug_ref[h, r0:r0 + HEAD_DIM, :] = qt_ref[h * HEAD_DIM:(h + 1) * HEAD_DIM, :]
        qaug_ref[h, LANES:, :] = jnp.where((row & (N_HEADS - 1)) == h, selb, 0.0).astype(BF16)
    _flash_init(m_ref, l_ref, acc_ref)
    lane_blk = jnp.right_shift(lax.broadcasted_iota(I32, (MOBA_BLOCK, LANES), 1), _LOG2_N_HEADS)

    def vt_of_block(n):
        c0 = pl.multiple_of(n * MOBA_BLOCK, MOBA_BLOCK)
        return lambda h: vt_ref[h * HEAD_DIM:(h + 1) * HEAD_DIM, pl.ds(c0, MOBA_BLOCK)]

    def past_score(n):
        k = k_ref[pl.ds(pl.multiple_of(n * MOBA_BLOCK, MOBA_BLOCK), MOBA_BLOCK), :]
        onehot = jnp.where(lane_blk == n, 1.0, 0.0).astype(BF16)

        def score(h):
            k_aug = jnp.concatenate([k[:, (h // 2) * LANES:(h // 2 + 1) * LANES], onehot], axis=1)
            return _dot(k_aug, qaug_ref[h])

        return score

    def own_score():
        k = k_ref[pl.ds(pl.multiple_of(qi * MOBA_BLOCK, MOBA_BLOCK), MOBA_BLOCK), :]
        causal = jnp.where(lax.broadcasted_iota(I32, (MOBA_BLOCK, tq), 0)
                           <= lax.broadcasted_iota(I32, (MOBA_BLOCK, tq), 1), -shift, NEG_INF)
        return lambda h: _dot(k[:, (h // 2) * LANES:(h // 2 + 1) * LANES], qaug_ref[h, :LANES, :]) + causal

    @pl.when(shifted)
    def _():
        past = lambda n: (past_score(n), vt_of_block(n))
        own = lambda: (own_score(), vt_of_block(qi))

        def group(n0, count):
            _flash_tiles_shifted([past(n0 + t) for t in range(count)], p_ref, l_ref, acc_ref)

        _grouped_loop(qi, group, smallest=2)

        @pl.when((qi & 1) == 1)
        def _():
            _flash_tiles_shifted([past(qi - 1), own()], p_ref, l_ref, acc_ref)

        @pl.when((qi & 1) == 0)
        def _():
            _flash_tiles_shifted([own()], p_ref, l_ref, acc_ref)

    @pl.when(jnp.logical_not(shifted))
    def _():
        def att_body(n, carry):
            _flash_tile_step(past_score(n), vt_of_block(n), s_ref, mn_ref, m_ref, l_ref, acc_ref)
            return carry

        lax.fori_loop(0, qi, att_body, 0)
        _flash_tile_step(own_score(), vt_of_block(qi), s_ref, mn_ref, m_ref, l_ref, acc_ref)

    _flash_finish(o_ref, l_ref, acc_ref)


def _moba(P4, TT, shift, B, S):
    tq = MOBA_BLOCK
    nblk = S // MOBA_BLOCK
    assert S % MOBA_BLOCK == 0 and nblk <= MOBA_SLOTS
    topb = min(MOBA_TOPK, nblk - 1)
    gl = N_HEADS * MOBA_SLOTS
    return pl.pallas_call(
        functools.partial(_moba_kernel, tq=tq, topb=topb),
        grid=(B, S // tq),
        in_specs=[pl.BlockSpec(memory_space=pltpu.SMEM),
                  pl.BlockSpec((None, TILE, tq), lambda b, i: (b, TT_QC, i)),
                  pl.BlockSpec((None, None, S, TILE), lambda b, i: (T_KC, b, 0, 0)),
                  pl.BlockSpec((None, TILE, S), lambda b, i: (b, TT_VC, 0))],
        out_specs=pl.BlockSpec((None, tq, TILE), lambda b, i: (b, i, 0)),
        out_shape=jax.ShapeDtypeStruct((B, S, TILE), BF16),
        scratch_shapes=[pltpu.VMEM((gl, TILE), BF16), pltpu.VMEM((gl, TILE), BF16),
                        pltpu.VMEM((N_HEADS, 2 * LANES, tq), BF16)]
                       + _flash_scratch(MOBA_BLOCK, tq),
        compiler_params=_params("arbitrary", "arbitrary"),
        name="moba",
    )(shift, TT, P4, TT)


def _post_kernel(x_ref, oa0, oa1, oa2, la0, la1, la2, ob_ref, oc_ref, z0, z1, z2, g0, g1, g2,
                 wbr_ref, wout_ref, expand_ref, out_ref):
    l0, l1, l2 = la0[...], la1[...], la2[...]
    mx = jnp.maximum(jnp.maximum(l0, l1), l2)
    e0, e1, e2 = jnp.exp2(l0 - mx), jnp.exp2(l1 - mx), jnp.exp2(l2 - mx)
    den = e0 + e1 + e2

    def spread(w):
        hi = w.astype(BF16)
        lo = (w - hi.astype(F32)).astype(BF16)
        return _dot(hi, expand_ref[...]) + _dot(lo, expand_ref[...])

    o_a = (spread(e0 / den) * oa0[...].astype(F32) + spread(e1 / den) * oa1[...].astype(F32)
           + spread(e2 / den) * oa2[...].astype(F32))
    branches = (o_a, ob_ref[...].astype(F32), oc_ref[...].astype(F32))
    merged = jnp.zeros(out_ref.shape, F32)
    for n, (o, z, g) in enumerate(zip(branches, (z0, z1, z2), (g0, g1, g2))):
        y = _dot((o * z[0].astype(F32)).astype(BF16), wbr_ref[n])
        gate = jnp.concatenate([g[t] for t in range(g.shape[0])], axis=1)
        merged = merged + gate.astype(F32) * y
    out_ref[...] = x_ref[...] + _dot(merged.astype(BF16), wout_ref[...])


def _post(x2, oa, la, ob, oc, P, wbr, wout, tm):
    T = x2.shape[0]
    row = lambda width, t: pl.BlockSpec((tm, width), lambda i: (i, t))
    tiles = lambda count, first: pl.BlockSpec((count, tm, TILE), lambda i: (first // count, i, 0))
    per_g = D_MODEL // TILE
    assert T_G % per_g == 0
    head_of = np.arange(TILE) // HEAD_DIM
    expand = jnp.asarray((np.arange(LANES)[:, None] == head_of[None, :]).astype(np.float32), BF16)
    in_specs = ([row(D_MODEL, 0)] + [row(TILE, 0)] * 3 + [row(LANES, 0)] * 3 + [row(TILE, 0)] * 2
                + [tiles(1, T_Z + n) for n in range(N_BRANCH)]
                + [tiles(per_g, T_G + per_g * n) for n in range(N_BRANCH)]
                + [pl.BlockSpec((N_BRANCH, BRANCH_WIDTH, D_MODEL), lambda i: (0, 0, 0)),
                   pl.BlockSpec((D_MODEL, D_MODEL), lambda i: (0, 0)),
                   pl.BlockSpec((LANES, TILE), lambda i: (0, 0))])
    return pl.pallas_call(
        _post_kernel,
        grid=(T // tm,),
        in_specs=in_specs,
        out_specs=row(D_MODEL, 0),
        out_shape=jax.ShapeDtypeStruct((T, D_MODEL), F32),
        compiler_params=_params("arbitrary"),
        name="post",
    )(x2, oa[0], oa[1], oa[2], la[0], la[1], la[2], ob, oc, P, P, P, P, P, P, wbr, wout, expand)


def _rearrange_w_in(w):
    bw = BRANCH_WIDTH
    a_q, a_k, a_v = w[:, 0:3 * bw], w[:, 3 * bw:6 * bw], w[:, 6 * bw:9 * bw]
    off = 9 * bw
    b_q, b_k, b_v = (w[:, off + i * bw:off + (i + 1) * bw] for i in range(3)); off += 3 * bw
    iq = w[:, off:off + IDX_HEADS * IDX_DIM]; off += IDX_HEADS * IDX_DIM
    ik = w[:, off:off + IDX_DIM]; off += IDX_DIM
    iw = w[:, off:off + IDX_HEADS]; off += IDX_HEADS
    c_q, c_k, c_v = (w[:, off + i * bw:off + (i + 1) * bw] for i in range(3)); off += 3 * bw
    z = w[:, off:off + 3 * bw]; off += 3 * bw
    g = w[:, off:off + 3 * D_MODEL]; off += 3 * D_MODEL
    assert off == w.shape[1]
    cols = []
    for grp in (1, 2, 0):
        sl = slice(grp * bw, (grp + 1) * bw)
        cols += [a_q[:, sl], a_k[:, sl], a_v[:, sl]]
    pad = jnp.zeros((w.shape[0], TILE - IDX_DIM - IDX_HEADS), w.dtype)
    cols += [b_k, ik, iw, pad, c_k, g, z]
    w_nat = jnp.concatenate(cols, axis=1).astype(BF16)
    assert w_nat.shape[1] == J_T * TILE
    w_nat = w_nat.reshape(-1, J_T, TILE).transpose(1, 0, 2)
    w_t = jnp.concatenate([b_q, b_v, iq, c_q, c_v], axis=1).T.astype(BF16)
    assert w_t.shape[0] == N_T_TILES * TILE
    return w_nat, w_t


def _tile_tables(qk_g):
    kinds = np.zeros((N_TILES,), np.int32)
    scale = np.array([HEAD_DIM ** -0.5 * LOG2_E, 1.0], np.float32)
    rows = jnp.tile(qk_g.astype(F32) * scale[None, :, None], (1, 1, N_HEADS)).reshape(2 * N_BRANCH, TILE)
    rows = jnp.concatenate([rows, jnp.full((1, TILE), IDX_DIM ** -0.5, F32), jnp.ones((1, TILE), F32)])
    row_iq, row_one = 2 * N_BRANCH, 2 * N_BRANCH + 1
    which = np.full((N_TILES,), row_one, np.int32)

    def qk(tile, mixer):
        kinds[tile] = kinds[tile + 1] = KIND_NORM_ROPE
        which[tile], which[tile + 1] = 2 * mixer, 2 * mixer + 1

    qk(0, 0)
    qk(N_FOLD_TILES // 2, 0)
    qk(J_NAT + T_A, 0)
    for tile, mixer in ((T_KB, 1), (T_KC, 2)):
        kinds[J_NAT + tile] = KIND_NORM_ROPE
        which[J_NAT + tile] = 2 * mixer + 1
    kinds[J_NAT + T_IKW] = KIND_ROPE_LOW
    kinds[J_NAT + T_Z:J_NAT + T_Z + 3] = KIND_SILU
    kinds[J_NAT + T_G:J_NAT + T_G + 6] = KIND_SIGMOID
    for tile, mixer in ((TT_QB, 1), (TT_QC, 2)):
        kinds[J_T + tile] = KIND_NORM_ROPE
        which[J_T + tile] = 2 * mixer
    kinds[J_T + TT_IQ] = KIND_ROPE
    which[J_T + TT_IQ] = row_iq
    gains = rows[which[:J_T]][:, None, :]
    gains_t = jnp.broadcast_to(rows[which[J_T:]][:, :, None], (N_T_TILES, TILE, LANES))
    half = N_FOLD_TILES // 2
    dest = np.array([DEST_F1] * half + [DEST_F2] * half + [DEST_NAT] * N_NAT_TILES + [DEST_T] * N_T_TILES)
    assert all((k, d) in _ROUTES for k, d in zip(kinds.tolist(), dest.tolist()))
    routes = _route_code(kinds, dest).astype(np.int32)
    return jnp.asarray(routes), gains, gains_t


def _softmax_shift(qk_gain):
    bound = HEAD_DIM * jnp.max(jnp.abs(qk_gain[0])) * jnp.max(jnp.abs(qk_gain[1]))
    return (SHIFT_SLACK * HEAD_DIM ** -0.5 * LOG2_E * bound).reshape(1).astype(BF16).astype(F32)


def _rope_tables(positions):
    inv = ROPE_THETA ** (-jnp.arange(0, ROT_DIM, 2, dtype=F32) / ROT_DIM)
    ang = positions.astype(F32).reshape(-1)[:, None] * inv
    cos, sin = jnp.cos(ang), jnp.sin(ang)
    T = cos.shape[0]
    z8 = jnp.zeros((T, ROT_HALF), F32)
    rest1 = jnp.ones((T, HEAD_DIM - ROT_DIM), F32)
    rest0 = jnp.zeros((T, HEAD_DIM - ROT_DIM), F32)
    c = jnp.concatenate([cos, cos, rest1], axis=1)
    s1 = jnp.concatenate([-sin, z8, rest0], axis=1)
    s2 = jnp.concatenate([z8, sin, rest0], axis=1)
    two = lambda t: jnp.concatenate([t, t], axis=1)
    return (two(c), two(s1), two(s2)), (c.T, s1.T, s2.T)


def _block_diag_mean():
    h = np.arange(MXU_DIM) // HEAD_DIM
    return jnp.asarray((h[:, None] == h[None, :]).astype(np.float32) / HEAD_DIM, BF16)


def _layer(x2, B, S, tabs, tabs_t, bd, norm_g, w_in, qk_g, w_br, w_out, tm_in, tm_post):
    routes, gains, gains_t = _tile_tables(qk_g)
    w_nat, w_t = _rearrange_w_in(w_in)
    P, f1, f2, TT = _inproj(x2, norm_g[None, :], w_nat, w_t, routes, gains, gains_t, tabs, tabs_t, bd,
                            tm_in, B, S)
    P4 = P.reshape(N_NAT_TILES, B, S, TILE)
    oa, la = [], []
    for g, qkv in enumerate((P4.reshape(N_NAT_TILES, B, 1, S, TILE), f1, f2)):
        o, lse = _attn_a(qkv, _softmax_shift(qk_g[0]), g, DIL_PATTERNS[g][1], B, S)
        oa.append(o.reshape(B * S, TILE))
        la.append(lse.reshape(B * S, LANES))
    ob = _dsa(P4, TT, _softmax_shift(qk_g[1]), B, S).reshape(B * S, TILE)
    oc = _moba(P4, TT, _softmax_shift(qk_g[2]), B, S).reshape(B * S, TILE)
    return _post(x2, oa, la, ob, oc, P, w_br.astype(BF16), w_out.astype(BF16), tm_post)


def _forward(x, positions, norm_g, w_in, qk_g, w_br, w_out, tm_in=IN_ROW_TILE, tm_post=POST_ROW_TILE):
    B, S, D = x.shape
    tabs, tabs_t = _rope_tables(positions)
    bd = _block_diag_mean()
    x2 = x.reshape(B * S, D)
    for layer in range(norm_g.shape[0]):
        x2 = _layer(x2, B, S, tabs, tabs_t, bd, norm_g[layer], w_in[layer], qk_g[layer],
                    w_br[layer], w_out[layer], tm_in, tm_post)
    return x2.reshape(B, S, D)


def kernel(x, positions, norm_g, w_in, qk_g, w_br, w_out):
    return _forward(x, positions, norm_g, w_in, qk_g, w_br, w_out)
```

```python
import functools

import jax
import jax.numpy as jnp
import numpy as np
from jax import lax
from jax.experimental import pallas as pl
from jax.experimental.pallas import tpu as pltpu

F32 = jnp.float32
BF16 = jnp.bfloat16
I32 = jnp.int32
I16 = jnp.int16

D_MODEL = 1024
HEAD_DIM = 64
ROT_DIM = HEAD_DIM // 4
ROT_HALF = ROT_DIM // 2
ROPE_THETA = 500000.0
NORM_EPS = 1e-6
NEG_INF = -1e30
LOG2_E = 1.4426950408889634
MAX_SOFTMAX_SHIFT = 60.0
SHIFT_SLACK = 1.05
N_HEADS = 8
BRANCH_WIDTH = N_HEADS * HEAD_DIM
N_BRANCH = 3
DIL_PATTERNS = ((128, 1), (512, 4), (2048, 16))
BAND = 128
IDX_HEADS = 8
IDX_DIM = 64
IDX_TOPK_MAX = 256
MOBA_BLOCK = 256
MOBA_TOPK = 3
MOBA_SLOTS = 16

LANES = 128
SUBLANES = 8
MXU_DIM = 256
VMEM_LIMIT_BYTES = 56 * 1024 * 1024

TILE = BRANCH_WIDTH
N_FOLD_TILES = 6
T_A = 0
T_KB = 3
T_IKW = 4
T_KC = 5
T_G = 6
T_Z = 12
N_NAT_TILES = 15
TT_QB, TT_VB, TT_IQ, TT_QC, TT_VC = range(5)
N_T_TILES = 5
J_NAT = N_FOLD_TILES
J_T = N_FOLD_TILES + N_NAT_TILES
N_TILES = J_T + N_T_TILES
IN_ROW_TILE = 2048
IN_CHUNK = 256
A_STEP_TOKENS = 2048
GROUP = 4
SPARSE_GROUP = 8
SPARSE_TILE = 256
POST_ROW_TILE = 512

KIND_PLAIN, KIND_NORM_ROPE, KIND_ROPE, KIND_ROPE_LOW, KIND_SILU, KIND_SIGMOID = range(6)
DEST_NAT, DEST_F1, DEST_F2, DEST_T = range(4)
_ROUTES = ((KIND_PLAIN, DEST_NAT), (KIND_PLAIN, DEST_F1), (KIND_PLAIN, DEST_F2),
           (KIND_NORM_ROPE, DEST_NAT), (KIND_NORM_ROPE, DEST_F1), (KIND_NORM_ROPE, DEST_F2),
           (KIND_ROPE_LOW, DEST_NAT), (KIND_SILU, DEST_NAT), (KIND_SIGMOID, DEST_NAT),
           (KIND_PLAIN, DEST_T), (KIND_NORM_ROPE, DEST_T), (KIND_ROPE, DEST_T))


def _route_code(kind, dest):
    return kind * 4 + dest

_FLOOR_KEY = int(np.array(-5e29, np.float32).view(np.int32)) ^ 0x7FFFFFFF
_INT_MIN = -2 ** 31
_HALF16 = 32768
_PACK16 = 16
_LOG2_MOBA_BLOCK = MOBA_BLOCK.bit_length() - 1
_LOG2_N_HEADS = N_HEADS.bit_length() - 1
_LOG2_HEAD_DIM = HEAD_DIM.bit_length() - 1


def _dot(a, b):
    return jnp.dot(a, b, preferred_element_type=F32)


def _dot_nt(a, b):
    return lax.dot_general(a, b, (((1,), (1,)), ((), ())), preferred_element_type=F32)


def _params(*sem):
    return pltpu.CompilerParams(dimension_semantics=sem, vmem_limit_bytes=VMEM_LIMIT_BYTES)


def _inproj_kernel(route_ref, x_ref, ng_ref, w_ref, wt_ref, gain_ref, gain_t_ref,
                   cos_ref, s1_ref, s2_ref, cos_t_ref, s1_t_ref, s2_t_ref, bd_ref,
                   o_ref, f1_ref, f2_ref, tt_ref, h_ref, stage_ref):
    j = pl.program_id(1)
    tm = x_ref.shape[0]
    n_ch = TILE // LANES

    @pl.when(j == 0)
    def _():
        x = x_ref[...]
        ms = jnp.mean(x * x, axis=-1, keepdims=True)
        h = x * lax.rsqrt(ms + NORM_EPS) * ng_ref[...]
        h_ref[...] = h.astype(BF16)

    def fold(val, ref, dil, tok):
        rows = (tok.stop - tok.start) // dil
        for c in range(n_ch):
            stage_ref[c, tok, :] = val[:, c * LANES:(c + 1) * LANES]
        for r in range(dil):
            parts = [stage_ref[c, pl.ds(tok.start + r, rows, stride=dil), :] for c in range(n_ch)]
            ref[r, tok.start // dil:tok.stop // dil, :] = jnp.concatenate(parts, axis=1).astype(ref.dtype)

    def rope(v, low_only, tok):
        c = jnp.tile(cos_ref[tok, :], (1, n_ch))
        s1 = jnp.tile(s1_ref[tok, :], (1, n_ch))
        s2 = jnp.tile(s2_ref[tok, :], (1, n_ch))
        if low_only:
            low = lax.broadcasted_iota(I32, v.shape, 1) < HEAD_DIM
            c = jnp.where(low, c, 1.0)
            s1 = jnp.where(low, s1, 0.0)
            s2 = jnp.where(low, s2, 0.0)
        return v * c + pltpu.roll(v, TILE - ROT_HALF, 1) * s1 + pltpu.roll(v, ROT_HALF, 1) * s2

    def rope_t(v, tok):
        c = jnp.tile(cos_t_ref[:, tok], (N_HEADS, 1))
        s1 = jnp.tile(s1_t_ref[:, tok], (N_HEADS, 1))
        s2 = jnp.tile(s2_t_ref[:, tok], (N_HEADS, 1))
        return v * c + pltpu.roll(v, TILE - ROT_HALF, 0) * s1 + pltpu.roll(v, ROT_HALF, 0) * s2

    def epilogue(kind, y, tok):
        if kind == KIND_PLAIN:
            return y
        if kind == KIND_NORM_ROPE:
            y2 = (y * y).astype(BF16)
            ms = jnp.concatenate([_dot(y2[:, c * MXU_DIM:(c + 1) * MXU_DIM], bd_ref[...])
                                  for c in range(TILE // MXU_DIM)], axis=1)
            return rope(y * lax.rsqrt(ms + NORM_EPS) * gain_ref[0], False, tok)
        if kind == KIND_ROPE_LOW:
            return rope(y, True, tok)
        if kind == KIND_SILU:
            return y / (1.0 + jnp.exp(-y))
        assert kind == KIND_SIGMOID
        return 1.0 / (1.0 + jnp.exp(-y))

    def epilogue_t(kind, yt, tok):
        if kind == KIND_PLAIN:
            return yt
        gain = jnp.tile(gain_t_ref[0], (1, yt.shape[1] // LANES))
        if kind == KIND_NORM_ROPE:
            y2 = (yt * yt).astype(BF16)
            ms = jnp.concatenate([_dot(bd_ref[...], y2[c * MXU_DIM:(c + 1) * MXU_DIM, :])
                                  for c in range(TILE // MXU_DIM)], axis=0)
            return rope_t(yt * lax.rsqrt(ms + NORM_EPS) * gain, tok)
        assert kind == KIND_ROPE
        return rope_t(yt * gain, tok)

    route = route_ref[j]
    for kind, dest in _ROUTES:
        @pl.when(route == _route_code(kind, dest))
        def _(kind=kind, dest=dest):
            if dest == DEST_T:
                tok = slice(0, tm)
                tt_ref[...] = epilogue_t(kind, _dot_nt(wt_ref[...], h_ref[...]), tok).astype(tt_ref.dtype)
                return
            chunk = max(IN_CHUNK, _PACK16 * DIL_PATTERNS[2][1]) if dest == DEST_F2 else IN_CHUNK
            for m in range(tm // chunk):
                tok = slice(m * chunk, (m + 1) * chunk)
                val = epilogue(kind, _dot(h_ref[tok, :], w_ref[...]), tok)
                if dest == DEST_NAT:
                    o_ref[tok, :] = val.astype(o_ref.dtype)
                elif dest == DEST_F1:
                    fold(val, f1_ref, DIL_PATTERNS[1][1], tok)
                else:
                    fold(val, f2_ref, DIL_PATTERNS[2][1], tok)


def _inproj(x2, ng, w_nat, w_t, routes, gains, gains_t, tabs, tabs_t, bd, tm, B, S):
    T = x2.shape[0]
    grid = (T // tm, N_TILES)
    per_b = S // tm
    d1, d2 = DIL_PATTERNS[1][1], DIL_PATTERNS[2][1]
    half = N_FOLD_TILES // 2
    assert S % tm == 0 and tm % (_PACK16 * d2) == 0 and tm % IN_CHUNK == 0 and IN_CHUNK % (_PACK16 * d1) == 0
    n_w = J_T

    def fold_spec(dil, first):
        return pl.BlockSpec(
            (None, None, dil, tm // dil, TILE),
            lambda i, j, k: (jnp.clip(j - first, 0, half - 1), i // per_b, 0, i % per_b, 0))

    once_per_row_tile = pl.Buffered(1)
    row_tab = pl.BlockSpec((tm, LANES), lambda i, j, k: (i, 0), pipeline_mode=once_per_row_tile)
    col_tab = pl.BlockSpec((HEAD_DIM, tm), lambda i, j, k: (0, i), pipeline_mode=once_per_row_tile)
    t_idx = lambda j: jnp.maximum(j - J_T, 0)
    return pl.pallas_call(
        _inproj_kernel,
        grid_spec=pltpu.PrefetchScalarGridSpec(
            num_scalar_prefetch=1,
            grid=grid,
            in_specs=[
                pl.BlockSpec((tm, D_MODEL), lambda i, j, k: (i, 0)),
                pl.BlockSpec((1, D_MODEL), lambda i, j, k: (0, 0)),
                pl.BlockSpec((None, D_MODEL, TILE), lambda i, j, k: (jnp.minimum(j, n_w - 1), 0, 0)),
                pl.BlockSpec((TILE, D_MODEL), lambda i, j, k: (t_idx(j), 0)),
                pl.BlockSpec((1, 1, TILE), lambda i, j, k: (jnp.minimum(j, n_w - 1), 0, 0)),
                pl.BlockSpec((1, TILE, LANES), lambda i, j, k: (t_idx(j), 0, 0)),
                row_tab, row_tab, row_tab, col_tab, col_tab, col_tab,
                pl.BlockSpec((MXU_DIM, MXU_DIM), lambda i, j, k: (0, 0)),
            ],
            out_specs=[
                pl.BlockSpec((None, tm, TILE), lambda i, j, k: (jnp.clip(j - J_NAT, 0, N_NAT_TILES - 1), i, 0)),
                fold_spec(d1, 0),
                fold_spec(d2, half),
                pl.BlockSpec((None, TILE, tm), lambda i, j, k: (i // per_b, t_idx(j), i % per_b)),
            ],
            scratch_shapes=[pltpu.VMEM((tm, D_MODEL), BF16),
                            pltpu.VMEM((TILE // LANES, tm, LANES), F32)],
        ),
        out_shape=[jax.ShapeDtypeStruct((N_NAT_TILES, T, TILE), BF16),
                   jax.ShapeDtypeStruct((half, B, d1, S // d1, TILE), BF16),
                   jax.ShapeDtypeStruct((half, B, d2, S // d2, TILE), BF16),
                   jax.ShapeDtypeStruct((B, N_T_TILES * TILE, S), BF16)],
        compiler_params=_params("arbitrary", "arbitrary"),
        name="inproj",
    )(routes, x2, ng, w_nat, w_t, gains, gains_t, *tabs, *tabs_t, bd)


def _attn_a_kernel(shift_ref, q_ref, kc_ref, kp_ref, vc_ref, vp_ref, o_ref, lse_ref, kcat, vtcat, ost, lst,
                   s_ref, p_ref, *, tq, dil):
    m = pl.program_id(1)
    nsub = tq // BAND
    n_ch = TILE // LANES
    nk = 2 * BAND
    key = lax.broadcasted_iota(I32, (nk, BAND), 0)
    qry = lax.broadcasted_iota(I32, (nk, BAND), 1) + BAND
    in_band = jnp.logical_and(key <= qry, key >= qry - BAND)
    lane = lax.broadcasted_iota(I32, (BAND, LANES), 1)
    half_mask = (jnp.where(lane < HEAD_DIM, 1.0, 0.0).astype(BF16),
                 jnp.where(lane < HEAD_DIM, 0.0, 1.0).astype(BF16))
    ones = jnp.ones((_PACK16, nk), BF16)
    shifted = shift_ref[0] <= MAX_SOFTMAX_SHIFT
    shift = jnp.where(shifted, shift_ref[0], 0.0)

    def load_residue(r, slot):
        kcat[slot, 0:BAND, :] = kp_ref[r]
        kcat[slot, BAND:, :] = kc_ref[r]
        vtcat[slot, :, 0:BAND] = jnp.transpose(vp_ref[r].astype(F32)).astype(BF16)
        for c in range(nsub):
            chunk = vc_ref[r, c * BAND:(c + 1) * BAND, :].astype(F32)
            vtcat[slot, :, (c + 1) * BAND:(c + 2) * BAND] = jnp.transpose(chunk).astype(BF16)

    def block(r, u, use_shift, slot, par):
        r0 = pl.multiple_of(u * BAND, BAND)
        q = q_ref[r, pl.ds(r0, BAND), :]
        k2 = kcat[slot, pl.ds(r0, nk), :]
        valid = jnp.logical_and(in_band, key >= jnp.where(m * nsub + u > 0, 0, BAND))
        bias = jnp.where(valid, -shift, NEG_INF)

        def score(h):
            pair = slice((h // 2) * LANES, (h // 2 + 1) * LANES)
            q_h = q[:, pair] * half_mask[h % 2]
            return _dot_nt(k2[:, pair], q_h) + bias

        def pv_of(h, p):
            vt_h = vtcat[slot, h * HEAD_DIM:(h + 1) * HEAD_DIM, pl.ds(r0, nk)]
            pv = _dot(jnp.concatenate([vt_h, ones], axis=0), p)
            return pv[:HEAD_DIM], pv[HEAD_DIM:HEAD_DIM + 1]

        outs, lses = [], []
        if use_shift:
            for h in range(N_HEADS):
                p_ref[par, h] = jnp.exp2(score(h)).astype(BF16)
            for h in range(N_HEADS):
                acc, den = pv_of(h, p_ref[par, h])
                outs.append(acc / den)
                lses.append(shift + jnp.log2(den))
        else:
            maxes = []
            for h in range(N_HEADS):
                s = score(h)
                s_ref[par, h] = s
                maxes.append(jnp.max(s, axis=0, keepdims=True))
            for h in range(N_HEADS):
                acc, den = pv_of(h, jnp.exp2(s_ref[par, h] - maxes[h]).astype(BF16))
                outs.append(acc / den)
                lses.append(maxes[h] + jnp.log2(den))
        o = jnp.transpose(jnp.concatenate(outs, axis=0))
        lse8 = jnp.concatenate(lses, axis=0)
        lse = jnp.transpose(jnp.tile(lse8, (LANES // N_HEADS, 1)))
        rows = pl.ds(r + u * (BAND * dil), BAND, stride=dil) if dil > 1 else pl.ds(r0, BAND)
        for c in range(n_ch):
            ost[c, rows, :] = o[:, c * LANES:(c + 1) * LANES]
        lst[rows, :] = lse

    def both_paths(blocks):
        @pl.when(shifted)
        def _():
            for r, u, slot, par in blocks:
                block(r, u, True, slot, par)

        @pl.when(jnp.logical_not(shifted))
        def _():
            for r, u, slot, par in blocks:
                block(r, u, False, slot, par)

    if nsub % GROUP == 0:
        def residue(r, carry):
            load_residue(r, 0)

            def body(i, carry):
                both_paths([(r, GROUP * i + t, 0, t) for t in range(GROUP)])
                return carry

            return lax.fori_loop(0, nsub // GROUP, body, carry)

        lax.fori_loop(0, dil, residue, 0)
    else:
        assert nsub == 1 and dil % GROUP == 0

        def residue_group(i, carry):
            for t in range(GROUP):
                load_residue(GROUP * i + t, t)
            both_paths([(GROUP * i + t, 0, t, t) for t in range(GROUP)])
            return carry

        lax.fori_loop(0, dil // GROUP, residue_group, 0)
    o_ref[0] = jnp.concatenate([ost[c] for c in range(n_ch)], axis=1).astype(o_ref.dtype)
    lse_ref[0] = lst[...]


def _attn_a(qkv, shift, g, dil, B, S):
    ts = min(A_STEP_TOKENS, S)
    tq = ts // dil
    assert S % ts == 0 and tq % BAND == 0
    sub = tq // BAND

    def spec_cur(t):
        return pl.BlockSpec((None, None, dil, tq, TILE), lambda b, m: (t, b, 0, m, 0))

    def spec_prev(t):
        return pl.BlockSpec((None, None, dil, BAND, TILE),
                            lambda b, m: (t, b, 0, jnp.maximum(m * sub - 1, 0), 0))

    n_ch = TILE // LANES
    kv_slots = 1 if sub % GROUP == 0 else GROUP
    return pl.pallas_call(
        functools.partial(_attn_a_kernel, tq=tq, dil=dil),
        grid=(B, S // ts),
        in_specs=[pl.BlockSpec(memory_space=pltpu.SMEM),
                  spec_cur(0), spec_cur(1), spec_prev(1), spec_cur(2), spec_prev(2)],
        out_specs=[pl.BlockSpec((1, ts, TILE), lambda b, m: (b, m, 0)),
                   pl.BlockSpec((1, ts, LANES), lambda b, m: (b, m, 0))],
        out_shape=[jax.ShapeDtypeStruct((B, S, TILE), BF16), jax.ShapeDtypeStruct((B, S, LANES), F32)],
        scratch_shapes=[pltpu.VMEM((kv_slots, tq + BAND, TILE), BF16), pltpu.VMEM((kv_slots, TILE, tq + BAND), BF16),
                        pltpu.VMEM((n_ch, ts, LANES), F32), pltpu.VMEM((ts, LANES), F32),
                        pltpu.VMEM((GROUP, N_HEADS, 2 * BAND, BAND), F32),
                        pltpu.VMEM((GROUP, N_HEADS, 2 * BAND, BAND), BF16)],
        compiler_params=_params("arbitrary", "arbitrary"),
        name=f"attn_a{g}",
    )(shift, qkv, qkv, qkv, qkv, qkv)


def _pad_q(qt_ref, qpad_ref):
    qpad_ref[...] = jnp.zeros(qpad_ref.shape, qpad_ref.dtype)
    for h in range(N_HEADS):
        r0 = h * LANES + (h % 2) * HEAD_DIM
        qpad_ref[r0:r0 + HEAD_DIM, :] = qt_ref[h * HEAD_DIM:(h + 1) * HEAD_DIM, :]


def _flash_init(m_ref, l_ref, acc_ref):
    m_ref[...] = jnp.full(m_ref.shape, NEG_INF, F32)
    l_ref[...] = jnp.zeros(l_ref.shape, F32)
    acc_ref[...] = jnp.zeros(acc_ref.shape, F32)


def _flash_tile_step(score_of_head, vt_of_head, s_ref, mn_ref, m_ref, l_ref, acc_ref):
    tk = s_ref.shape[1]
    for h in range(N_HEADS):
        s = score_of_head(h)
        s_ref[h] = s
        mn_ref[h] = jnp.maximum(m_ref[h], jnp.max(s, axis=0, keepdims=True))
    ones = jnp.ones((_PACK16, tk), BF16)
    for h in range(N_HEADS):
        m_prev, m_next = m_ref[h], mn_ref[h]
        p = jnp.exp2(s_ref[h] - jnp.tile(m_next, (tk // SUBLANES, 1)))
        alpha = jnp.exp2(m_prev - m_next)
        pv = _dot(jnp.concatenate([vt_of_head(h), ones], axis=0), p.astype(BF16))
        l_ref[h] = alpha * l_ref[h] + pv[HEAD_DIM:HEAD_DIM + SUBLANES]
        acc_ref[h] = acc_ref[h] * jnp.tile(alpha, (HEAD_DIM // SUBLANES, 1)) + pv[:HEAD_DIM]
        m_ref[h] = m_next


def _flash_tiles_shifted(tiles, p_ref, l_ref, acc_ref):
    for t, (score_of_head, _) in enumerate(tiles):
        for h in range(N_HEADS):
            p_ref[t, h] = jnp.exp2(score_of_head(h)).astype(BF16)
    ones = jnp.ones((_PACK16, p_ref.shape[2]), BF16)
    for t, (_, vt_of_head) in enumerate(tiles):
        for h in range(N_HEADS):
            pv = _dot(jnp.concatenate([vt_of_head(h), ones], axis=0), p_ref[t, h])
            l_ref[h] = l_ref[h] + pv[HEAD_DIM:HEAD_DIM + SUBLANES]
            acc_ref[h] = acc_ref[h] + pv[:HEAD_DIM]


def _grouped_loop(n, group, smallest=1):
    def body(i, carry):
        group(i * SPARSE_GROUP, SPARSE_GROUP)
        return carry

    lax.fori_loop(0, lax.shift_right_logical(n, SPARSE_GROUP.bit_length() - 1), body, 0)
    size = SPARSE_GROUP // 2
    while size >= smallest:
        @pl.when((n & size) != 0)
        def _(size=size):
            group((n >> size.bit_length()) << size.bit_length(), size)
        size //= 2


def _flash_finish(o_ref, l_ref, acc_ref):
    outs = [acc_ref[h] / jnp.tile(l_ref[h], (HEAD_DIM // SUBLANES, 1)) for h in range(N_HEADS)]
    o_ref[...] = jnp.transpose(jnp.concatenate(outs, axis=0)).astype(o_ref.dtype)


def _dsa_kernel(shift_ref, qt_ref, k_ref, vt_ref, iqt_ref, ikw_q_ref, ikw_ref, o_ref,
                key_ref, hi_ref, lo_ref, qpad_ref, bias_ref, p_ref, s_ref, mn_ref, m_ref, l_ref, acc_ref,
                *, tq, tk, topk):
    qi = pl.program_id(1)
    n_kt = qi + 1
    n_acc = 4
    rows8 = tk // SUBLANES

    w8 = jnp.transpose(ikw_q_ref[...].astype(F32))[HEAD_DIM:HEAD_DIM + IDX_HEADS, :] * (IDX_HEADS ** -0.5)
    krow = lax.broadcasted_iota(I32, (tk, tq), 0)
    qcol = lax.broadcasted_iota(I32, (tk, tq), 1)

    def score_tile(c, diagonal):
        c0 = pl.multiple_of(c * tk, tk)
        kx = ikw_ref[pl.ds(c0, tk), :][:, :IDX_DIM]
        sc = jnp.zeros((tk, tq), F32)
        for h in range(IDX_HEADS):
            lg = _dot(kx, iqt_ref[h * IDX_DIM:(h + 1) * IDX_DIM, :])
            sc = sc + w8[h:h + 1, :] * jnp.maximum(lg, 0.0)
        sc = jnp.where(sc == 0.0, 0.0, sc)
        if diagonal:
            sc = jnp.where(krow <= qcol, sc, NEG_INF)
        bits = pltpu.bitcast(sc, I32)
        key = jnp.where(bits < 0, bits ^ 0x7FFFFFFF, bits)
        key_ref[pl.ds(c0, tk), :] = key
        hi_ref[pl.ds(c0, tk), :] = jnp.right_shift(key, 16).astype(I16)
        lo_ref[pl.ds(c0, tk), :] = ((key & 0xFFFF) - _HALF16).astype(I16)

    _grouped_loop(qi, lambda c0, count: [score_tile(c0 + t, False) for t in range(count)], smallest=2)

    @pl.when((qi & 1) == 1)
    def _():
        score_tile(qi - 1, False)
        score_tile(qi, True)

    @pl.when((qi & 1) == 0)
    def _():
        score_tile(qi, True)

    rows16 = tk // _PACK16
    one16 = jnp.ones((_PACK16, tq), I16)
    zero16 = jnp.zeros((_PACK16, tq), I16)

    def count16(ref, pred):
        def cbody(c, accs):
            accs = list(accs)
            c0 = pl.multiple_of(c * tk, tk)
            t = ref[pl.ds(c0, tk), :]
            for g in range(rows16):
                hit = jnp.where(pred(t[g * _PACK16:(g + 1) * _PACK16]), one16, zero16)
                accs[g % n_acc] = accs[g % n_acc] + hit
            return tuple(accs)
        accs = lax.fori_loop(0, n_kt, cbody, tuple(zero16 for _ in range(n_acc)))
        return jnp.sum(functools.reduce(lambda a, b: a + b, accs).astype(I32), axis=0, keepdims=True)

    def as16(v):
        return jnp.broadcast_to(v, (_PACK16, tq)).astype(I16)

    def select16(ref, need, cge0):
        def bit_body(it, carry):
            ans, cge = carry
            cand_u = ans | lax.shift_left(jnp.int32(1), 15 - it)
            cand = as16(cand_u - _HALF16)
            cnt = count16(ref, lambda t: t >= cand)
            ok = cnt >= need
            return jnp.where(ok, cand_u, ans), jnp.where(ok, cnt, cge)
        return lax.fori_loop(0, 16, bit_body, (jnp.zeros((1, tq), I32), cge0))

    n_all = jnp.zeros((1, tq), I32) + n_kt * tk
    p_u, cge_hi = select16(hi_ref, topk, n_all)
    p16 = as16(p_u - _HALF16)
    c_gt = count16(hi_ref, lambda t: t > p16)

    def bucket_body(c, carry):
        c0 = pl.multiple_of(c * tk, tk)
        lo_ref[pl.ds(c0, tk), :] = jnp.where(hi_ref[pl.ds(c0, tk), :] == jnp.tile(p16, (rows16, 1)),
                                             lo_ref[pl.ds(c0, tk), :], jnp.int16(-_HALF16))
        return carry

    lax.fori_loop(0, n_kt, bucket_body, 0)
    l_u, cge_lo = select16(lo_ref, topk - c_gt, cge_hi - c_gt)
    cge = c_gt + cge_lo
    thr = jnp.maximum((p_u - _HALF16) * 65536 + l_u, _FLOOR_KEY)

    def count_rows(fn, n_out):
        def cbody(c, accs):
            accs = [list(a) for a in accs]
            c0 = pl.multiple_of(c * tk, tk)
            kt = key_ref[pl.ds(c0, tk), :]
            for g in range(rows8):
                r0 = c0 + g * SUBLANES
                vals = fn(kt[g * SUBLANES:(g + 1) * SUBLANES], r0)
                for o in range(n_out):
                    accs[o][g % n_acc] = accs[o][g % n_acc] + vals[o]
            return tuple(tuple(a) for a in accs)
        z = jnp.zeros((SUBLANES, tq), I32)
        accs = lax.fori_loop(0, n_kt, cbody, tuple(tuple(z for _ in range(n_acc)) for _ in range(n_out)))
        return [jnp.sum(functools.reduce(lambda a, b: a + b, a), axis=0, keepdims=True) for a in accs]

    tie = jnp.logical_and(cge > topk, thr > _FLOOR_KEY)
    any_tie = jnp.max(jnp.where(tie, 1, 0)) > 0
    thr8 = jnp.broadcast_to(thr, (SUBLANES, tq))

    @pl.when(any_tie)
    def _():
        n_bits = int(np.log2(key_ref.shape[0]))
        sub = lax.broadcasted_iota(I32, (SUBLANES, tq), 0)

        def jbody(it, lo):
            cand = lo + lax.shift_left(jnp.int32(1), n_bits - 1 - it)
            pos = jnp.broadcast_to(cand - 1, (SUBLANES, tq))

            def f(kk, r0):
                eq = jnp.logical_and(kk == thr8, sub + r0 <= pos)
                return jnp.where(kk > thr8, 1, 0), jnp.where(eq, 1, 0)

            gt, eq = count_rows(f, 2)
            return jnp.where(gt + eq >= topk, lo, cand)

        jrow = jnp.broadcast_to(lax.fori_loop(0, n_bits, jbody, jnp.zeros((1, tq), I32)), (SUBLANES, tq))
        tie8 = jnp.broadcast_to(jnp.where(tie, 1, 0), (SUBLANES, tq)) > 0

        def fix(c, carry):
            for g in range(rows8):
                r0 = pl.multiple_of(c * tk + g * SUBLANES, SUBLANES)
                kk = key_ref[pl.ds(r0, SUBLANES), :]
                drop = jnp.logical_and(tie8, jnp.logical_and(kk == thr8, sub + r0 > jrow))
                key_ref[pl.ds(r0, SUBLANES), :] = jnp.where(drop, _INT_MIN, kk)
            return carry

        lax.fori_loop(0, n_kt, fix, 0)

    _pad_q(qt_ref, qpad_ref)
    _flash_init(m_ref, l_ref, acc_ref)
    shift = shift_ref[0]

    def tile(j, t, shifted):
        c0 = pl.multiple_of(j * tk, tk)
        sel = key_ref[pl.ds(c0, tk), :] >= thr
        bias_ref[t] = jnp.where(sel, -shift if shifted else 0.0, NEG_INF)
        k = k_ref[pl.ds(c0, tk), :]

        def score(h):
            k_pair = k[:, (h // 2) * LANES:(h // 2 + 1) * LANES]
            return _dot(k_pair, qpad_ref[h * LANES:(h + 1) * LANES, :]) + bias_ref[t]

        return score, lambda h: vt_ref[h * HEAD_DIM:(h + 1) * HEAD_DIM, pl.ds(c0, tk)]

    @pl.when(shift <= MAX_SOFTMAX_SHIFT)
    def _():
        def group(j0, count):
            _flash_tiles_shifted([tile(j0 + t, t, True) for t in range(count)], p_ref, l_ref, acc_ref)

        _grouped_loop(n_kt, group)

    @pl.when(shift > MAX_SOFTMAX_SHIFT)
    def _():
        def att_body(j, carry):
            _flash_tile_step(*tile(j, 0, False), s_ref, mn_ref, m_ref, l_ref, acc_ref)
            return carry

        lax.fori_loop(0, n_kt, att_body, 0)

    _flash_finish(o_ref, l_ref, acc_ref)


def _flash_scratch(tk, tq):
    stat = pltpu.VMEM((N_HEADS, SUBLANES, tq), F32)
    return [pltpu.VMEM((SPARSE_GROUP, N_HEADS, tk, tq), BF16), pltpu.VMEM((N_HEADS, tk, tq), F32), stat, stat, stat,
            pltpu.VMEM((N_HEADS, HEAD_DIM, tq), F32)]


def _dsa(P4, TT, shift, B, S):
    tq = tk = SPARSE_TILE
    topk = min(IDX_TOPK_MAX, S // 4)
    assert S % tq == 0 and topk <= tk
    def tt_q(t):
        return pl.BlockSpec((None, TILE, tq), lambda b, i: (b, t, i))

    return pl.pallas_call(
        functools.partial(_dsa_kernel, tq=tq, tk=tk, topk=topk),
        grid=(B, S // tq),
        in_specs=[pl.BlockSpec(memory_space=pltpu.SMEM),
                  tt_q(TT_QB),
                  pl.BlockSpec((None, None, S, TILE), lambda b, i: (T_KB, b, 0, 0)),
                  pl.BlockSpec((None, TILE, S), lambda b, i: (b, TT_VB, 0)),
                  tt_q(TT_IQ),
                  pl.BlockSpec((None, None, tq, LANES), lambda b, i: (T_IKW, b, i, 0)),
                  pl.BlockSpec((None, None, S, LANES), lambda b, i: (T_IKW, b, 0, 0))],
        out_specs=pl.BlockSpec((None, tq, TILE), lambda b, i: (b, i, 0)),
        out_shape=jax.ShapeDtypeStruct((B, S, TILE), BF16),
        scratch_shapes=[pltpu.VMEM((S, tq), I32), pltpu.VMEM((S, tq), I16), pltpu.VMEM((S, tq), I16),
                        pltpu.VMEM((N_HEADS * LANES, tq), BF16),
                        pltpu.VMEM((SPARSE_GROUP, tk, tq), F32)] + _flash_scratch(tk, tq),
        compiler_params=_params("arbitrary", "arbitrary"),
        name="dsa",
    )(shift, TT, P4, TT, TT, P4, P4)


def _moba_kernel(shift_ref, qt_ref, k_ref, vt_ref, o_ref, kmh_ref, kml_ref, qaug_ref,
                 p_ref, s_ref, mn_ref, m_ref, l_ref, acc_ref, *, tq, topb):
    qi = pl.program_id(1)
    S = k_ref.shape[0]
    gl = N_HEADS * MOBA_SLOTS

    @pl.when(qi == 0)
    def _():
        blk_row = jnp.right_shift(lax.broadcasted_iota(I32, (gl, S), 0), _LOG2_N_HEADS)
        blk_col = jnp.right_shift(lax.broadcasted_iota(I32, (gl, S), 1), _LOG2_MOBA_BLOCK)
        avg = jnp.where(blk_row == blk_col, 1.0 / MOBA_BLOCK, 0.0).astype(BF16)
        km = _dot(avg, k_ref[...])
        r_head = lax.broadcasted_iota(I32, (gl, TILE), 0) & (N_HEADS - 1)
        c_head = jnp.right_shift(lax.broadcasted_iota(I32, (gl, TILE), 1), _LOG2_HEAD_DIM)
        km = jnp.where(r_head == c_head, km, 0.0)
        hi = km.astype(BF16)
        kmh_ref[...] = hi
        kml_ref[...] = (km - hi.astype(F32)).astype(BF16)

    shifted = shift_ref[0] <= MAX_SOFTMAX_SHIFT
    shift = jnp.where(shifted, shift_ref[0], 0.0)
    qt = qt_ref[...]
    gate = _dot(kmh_ref[...], qt) + _dot(kml_ref[...], qt)
    row = lax.broadcasted_iota(I32, (gl, tq), 0)
    gate = jnp.where(jnp.right_shift(row, _LOG2_N_HEADS) < qi, gate, NEG_INF)
    g = [gate[n * N_HEADS:(n + 1) * N_HEADS] for n in range(MOBA_SLOTS)]
    biases = []
    for n in range(MOBA_SLOTS):
        beaten = jnp.where(n < qi, 0, topb) + jnp.zeros((N_HEADS, tq), I32)
        for n2 in range(MOBA_SLOTS):
            if n2 != n:
                beaten = beaten + jnp.where((g[n2] >= g[n]) if n2 < n else (g[n2] > g[n]), 1, 0)
        biases.append(jnp.where(beaten < topb, -shift, NEG_INF))
    selb = jnp.concatenate(biases, axis=0)

    qaug_ref[...] = jnp.zeros(qaug_ref.shape, qaug_ref.dtype)
    for h in range(N_HEADS):
        r0 = (h % 2) * HEAD_DIM
        qaug_ref[h, r0:r0 + HEAD_DIM, :] = qt_ref[h * HEAD_DIM:(h + 1) * HEAD_DIM, :]
        qaug_ref[h, LANES:, :] = jnp.where((row & (N_HEADS - 1)) == h, selb, 0.0).astype(BF16)
    _flash_init(m_ref, l_ref, acc_ref)
    lane_blk = jnp.right_shift(lax.broadcasted_iota(I32, (MOBA_BLOCK, LANES), 1), _LOG2_N_HEADS)

    def vt_of_block(n):
        c0 = pl.multiple_of(n * MOBA_BLOCK, MOBA_BLOCK)
        return lambda h: vt_ref[h * HEAD_DIM:(h + 1) * HEAD_DIM, pl.ds(c0, MOBA_BLOCK)]

    def past_score(n):
        k = k_ref[pl.ds(pl.multiple_of(n * MOBA_BLOCK, MOBA_BLOCK), MOBA_BLOCK), :]
        onehot = jnp.where(lane_blk == n, 1.0, 0.0).astype(BF16)

        def score(h):
            k_aug = jnp.concatenate([k[:, (h // 2) * LANES:(h // 2 + 1) * LANES], onehot], axis=1)
            return _dot(k_aug, qaug_ref[h])

        return score

    def own_score():
        k = k_ref[pl.ds(pl.multiple_of(qi * MOBA_BLOCK, MOBA_BLOCK), MOBA_BLOCK), :]
        causal = jnp.where(lax.broadcasted_iota(I32, (MOBA_BLOCK, tq), 0)
                           <= lax.broadcasted_iota(I32, (MOBA_BLOCK, tq), 1), -shift, NEG_INF)
        return lambda h: _dot(k[:, (h // 2) * LANES:(h // 2 + 1) * LANES], qaug_ref[h, :LANES, :]) + causal

    @pl.when(shifted)
    def _():
        past = lambda n: (past_score(n), vt_of_block(n))
        own = lambda: (own_score(), vt_of_block(qi))

        def group(n0, count):
            _flash_tiles_shifted([past(n0 + t) for t in range(count)], p_ref, l_ref, acc_ref)

        _grouped_loop(qi, group, smallest=2)

        @pl.when((qi & 1) == 1)
        def _():
            _flash_tiles_shifted([past(qi - 1), own()], p_ref, l_ref, acc_ref)

        @pl.when((qi & 1) == 0)
        def _():
            _flash_tiles_shifted([own()], p_ref, l_ref, acc_ref)

    @pl.when(jnp.logical_not(shifted))
    def _():
        def att_body(n, carry):
            _flash_tile_step(past_score(n), vt_of_block(n), s_ref, mn_ref, m_ref, l_ref, acc_ref)
            return carry

        lax.fori_loop(0, qi, att_body, 0)
        _flash_tile_step(own_score(), vt_of_block(qi), s_ref, mn_ref, m_ref, l_ref, acc_ref)

    _flash_finish(o_ref, l_ref, acc_ref)


def _moba(P4, TT, shift, B, S):
    tq = MOBA_BLOCK
    nblk = S // MOBA_BLOCK
    assert S % MOBA_BLOCK == 0 and nblk <= MOBA_SLOTS
    topb = min(MOBA_TOPK, nblk - 1)
    gl = N_HEADS * MOBA_SLOTS
    return pl.pallas_call(
        functools.partial(_moba_kernel, tq=tq, topb=topb),
        grid=(B, S // tq),
        in_specs=[pl.BlockSpec(memory_space=pltpu.SMEM),
                  pl.BlockSpec((None, TILE, tq), lambda b, i: (b, TT_QC, i)),
                  pl.BlockSpec((None, None, S, TILE), lambda b, i: (T_KC, b, 0, 0)),
                  pl.BlockSpec((None, TILE, S), lambda b, i: (b, TT_VC, 0))],
        out_specs=pl.BlockSpec((None, tq, TILE), lambda b, i: (b, i, 0)),
        out_shape=jax.ShapeDtypeStruct((B, S, TILE), BF16),
        scratch_shapes=[pltpu.VMEM((gl, TILE), BF16), pltpu.VMEM((gl, TILE), BF16),
                        pltpu.VMEM((N_HEADS, 2 * LANES, tq), BF16)]
                       + _flash_scratch(MOBA_BLOCK, tq),
        compiler_params=_params("arbitrary", "arbitrary"),
        name="moba",
    )(shift, TT, P4, TT)


def _post_kernel(x_ref, oa0, oa1, oa2, la0, la1, la2, ob_ref, oc_ref, z0, z1, z2, g0, g1, g2,
                 wbr_ref, wout_ref, expand_ref, out_ref):
    l0, l1, l2 = la0[...], la1[...], la2[...]
    mx = jnp.maximum(jnp.maximum(l0, l1), l2)
    e0, e1, e2 = jnp.exp2(l0 - mx), jnp.exp2(l1 - mx), jnp.exp2(l2 - mx)
    den = e0 + e1 + e2

    def spread(w):
        hi = w.astype(BF16)
        lo = (w - hi.astype(F32)).astype(BF16)
        return _dot(hi, expand_ref[...]) + _dot(lo, expand_ref[...])

    o_a = (spread(e0 / den) * oa0[...].astype(F32) + spread(e1 / den) * oa1[...].astype(F32)
           + spread(e2 / den) * oa2[...].astype(F32))
    branches = (o_a, ob_ref[...].astype(F32), oc_ref[...].astype(F32))
    merged = jnp.zeros(out_ref.shape, F32)
    for n, (o, z, g) in enumerate(zip(branches, (z0, z1, z2), (g0, g1, g2))):
        y = _dot((o * z[0].astype(F32)).astype(BF16), wbr_ref[n])
        gate = jnp.concatenate([g[t] for t in range(g.shape[0])], axis=1)
        merged = merged + gate.astype(F32) * y
    out_ref[...] = x_ref[...] + _dot(merged.astype(BF16), wout_ref[...])


def _post(x2, oa, la, ob, oc, P, wbr, wout, tm):
    T = x2.shape[0]
    row = lambda width, t: pl.BlockSpec((tm, width), lambda i: (i, t))
    tiles = lambda count, first: pl.BlockSpec((count, tm, TILE), lambda i: (first // count, i, 0))
    per_g = D_MODEL // TILE
    assert T_G % per_g == 0
    head_of = np.arange(TILE) // HEAD_DIM
    expand = jnp.asarray((np.arange(LANES)[:, None] == head_of[None, :]).astype(np.float32), BF16)
    in_specs = ([row(D_MODEL, 0)] + [row(TILE, 0)] * 3 + [row(LANES, 0)] * 3 + [row(TILE, 0)] * 2
                + [tiles(1, T_Z + n) for n in range(N_BRANCH)]
                + [tiles(per_g, T_G + per_g * n) for n in range(N_BRANCH)]
                + [pl.BlockSpec((N_BRANCH, BRANCH_WIDTH, D_MODEL), lambda i: (0, 0, 0)),
                   pl.BlockSpec((D_MODEL, D_MODEL), lambda i: (0, 0)),
                   pl.BlockSpec((LANES, TILE), lambda i: (0, 0))])
    return pl.pallas_call(
        _post_kernel,
        grid=(T // tm,),
        in_specs=in_specs,
        out_specs=row(D_MODEL, 0),
        out_shape=jax.ShapeDtypeStruct((T, D_MODEL), F32),
        compiler_params=_params("arbitrary"),
        name="post",
    )(x2, oa[0], oa[1], oa[2], la[0], la[1], la[2], ob, oc, P, P, P, P, P, P, wbr, wout, expand)


def _rearrange_w_in(w):
    bw = BRANCH_WIDTH
    a_q, a_k, a_v = w[:, 0:3 * bw], w[:, 3 * bw:6 * bw], w[:, 6 * bw:9 * bw]
    off = 9 * bw
    b_q, b_k, b_v = (w[:, off + i * bw:off + (i + 1) * bw] for i in range(3)); off += 3 * bw
    iq = w[:, off:off + IDX_HEADS * IDX_DIM]; off += IDX_HEADS * IDX_DIM
    ik = w[:, off:off + IDX_DIM]; off += IDX_DIM
    iw = w[:, off:off + IDX_HEADS]; off += IDX_HEADS
    c_q, c_k, c_v = (w[:, off + i * bw:off + (i + 1) * bw] for i in range(3)); off += 3 * bw
    z = w[:, off:off + 3 * bw]; off += 3 * bw
    g = w[:, off:off + 3 * D_MODEL]; off += 3 * D_MODEL
    assert off == w.shape[1]
    cols = []
    for grp in (1, 2, 0):
        sl = slice(grp * bw, (grp + 1) * bw)
        cols += [a_q[:, sl], a_k[:, sl], a_v[:, sl]]
    pad = jnp.zeros((w.shape[0], TILE - IDX_DIM - IDX_HEADS), w.dtype)
    cols += [b_k, ik, iw, pad, c_k, g, z]
    w_nat = jnp.concatenate(cols, axis=1).astype(BF16)
    assert w_nat.shape[1] == J_T * TILE
    w_nat = w_nat.reshape(-1, J_T, TILE).transpose(1, 0, 2)
    w_t = jnp.concatenate([b_q, b_v, iq, c_q, c_v], axis=1).T.astype(BF16)
    assert w_t.shape[0] == N_T_TILES * TILE
    return w_nat, w_t


def _tile_tables(qk_g):
    kinds = np.zeros((N_TILES,), np.int32)
    scale = np.array([HEAD_DIM ** -0.5 * LOG2_E, 1.0], np.float32)
    rows = jnp.tile(qk_g.astype(F32) * scale[None, :, None], (1, 1, N_HEADS)).reshape(2 * N_BRANCH, TILE)
    rows = jnp.concatenate([rows, jnp.full((1, TILE), IDX_DIM ** -0.5, F32), jnp.ones((1, TILE), F32)])
    row_iq, row_one = 2 * N_BRANCH, 2 * N_BRANCH + 1
    which = np.full((N_TILES,), row_one, np.int32)

    def qk(tile, mixer):
        kinds[tile] = kinds[tile + 1] = KIND_NORM_ROPE
        which[tile], which[tile + 1] = 2 * mixer, 2 * mixer + 1

    qk(0, 0)
    qk(N_FOLD_TILES // 2, 0)
    qk(J_NAT + T_A, 0)
    for tile, mixer in ((T_KB, 1), (T_KC, 2)):
        kinds[J_NAT + tile] = KIND_NORM_ROPE
        which[J_NAT + tile] = 2 * mixer + 1
    kinds[J_NAT + T_IKW] = KIND_ROPE_LOW
    kinds[J_NAT + T_Z:J_NAT + T_Z + 3] = KIND_SILU
    kinds[J_NAT + T_G:J_NAT + T_G + 6] = KIND_SIGMOID
    for tile, mixer in ((TT_QB, 1), (TT_QC, 2)):
        kinds[J_T + tile] = KIND_NORM_ROPE
        which[J_T + tile] = 2 * mixer
    kinds[J_T + TT_IQ] = KIND_ROPE
    which[J_T + TT_IQ] = row_iq
    gains = rows[which[:J_T]][:, None, :]
    gains_t = jnp.broadcast_to(rows[which[J_T:]][:, :, None], (N_T_TILES, TILE, LANES))
    half = N_FOLD_TILES // 2
    dest = np.array([DEST_F1] * half + [DEST_F2] * half + [DEST_NAT] * N_NAT_TILES + [DEST_T] * N_T_TILES)
    assert all((k, d) in _ROUTES for k, d in zip(kinds.tolist(), dest.tolist()))
    routes = _route_code(kinds, dest).astype(np.int32)
    return jnp.asarray(routes), gains, gains_t


def _softmax_shift(qk_gain):
    bound = HEAD_DIM * jnp.max(jnp.abs(qk_gain[0])) * jnp.max(jnp.abs(qk_gain[1]))
    return (SHIFT_SLACK * HEAD_DIM ** -0.5 * LOG2_E * bound).reshape(1).astype(BF16).astype(F32)


def _rope_tables(positions):
    inv = ROPE_THETA ** (-jnp.arange(0, ROT_DIM, 2, dtype=F32) / ROT_DIM)
    ang = positions.astype(F32).reshape(-1)[:, None] * inv
    cos, sin = jnp.cos(ang), jnp.sin(ang)
    T = cos.shape[0]
    z8 = jnp.zeros((T, ROT_HALF), F32)
    rest1 = jnp.ones((T, HEAD_DIM - ROT_DIM), F32)
    rest0 = jnp.zeros((T, HEAD_DIM - ROT_DIM), F32)
    c = jnp.concatenate([cos, cos, rest1], axis=1)
    s1 = jnp.concatenate([-sin, z8, rest0], axis=1)
    s2 = jnp.concatenate([z8, sin, rest0], axis=1)
    two = lambda t: jnp.concatenate([t, t], axis=1)
    return (two(c), two(s1), two(s2)), (c.T, s1.T, s2.T)


def _block_diag_mean():
    h = np.arange(MXU_DIM) // HEAD_DIM
    return jnp.asarray((h[:, None] == h[None, :]).astype(np.float32) / HEAD_DIM, BF16)


def _layer(x2, B, S, tabs, tabs_t, bd, norm_g, w_in, qk_g, w_br, w_out, tm_in, tm_post):
    routes, gains, gains_t = _tile_tables(qk_g)
    w_nat, w_t = _rearrange_w_in(w_in)
    P, f1, f2, TT = _inproj(x2, norm_g[None, :], w_nat, w_t, routes, gains, gains_t, tabs, tabs_t, bd,
                            tm_in, B, S)
    P4 = P.reshape(N_NAT_TILES, B, S, TILE)
    oa, la = [], []
    for g, qkv in enumerate((P4.reshape(N_NAT_TILES, B, 1, S, TILE), f1, f2)):
        o, lse = _attn_a(qkv, _softmax_shift(qk_g[0]), g, DIL_PATTERNS[g][1], B, S)
        oa.append(o.reshape(B * S, TILE))
        la.append(lse.reshape(B * S, LANES))
    ob = _dsa(P4, TT, _softmax_shift(qk_g[1]), B, S).reshape(B * S, TILE)
    oc = _moba(P4, TT, _softmax_shift(qk_g[2]), B, S).reshape(B * S, TILE)
    return _post(x2, oa, la, ob, oc, P, w_br.astype(BF16), w_out.astype(BF16), tm_post)


def _forward(x, positions, norm_g, w_in, qk_g, w_br, w_out, tm_in=IN_ROW_TILE, tm_post=POST_ROW_TILE):
    B, S, D = x.shape
    tabs, tabs_t = _rope_tables(positions)
    bd = _block_diag_mean()
    x2 = x.reshape(B * S, D)
    for layer in range(norm_g.shape[0]):
        x2 = _layer(x2, B, S, tabs, tabs_t, bd, norm_g[layer], w_in[layer], qk_g[layer],
                    w_br[layer], w_out[layer], tm_in, tm_post)
    return x2.reshape(B, S, D)


def kernel(x, positions, norm_g, w_in, qk_g, w_br, w_out):
    return _forward(x, positions, norm_g, w_in, qk_g, w_br, w_out)
```
